```python
import math
import jax, jax.numpy as jnp
from jax import lax
import numpy as np

D_MODEL = 1024
BATCH = 8
SEQ = 2048
DEPTH = 1

A_HEADS = 8
A_LAT = 128
A_HEAD_DIM = 64
IDX_HEADS = 8
IDX_DIM = 64
TOPK_MAX = 256
B_HEADS = 8
B_KV_HEADS = 2
B_HEAD_DIM = 64
WINDOW = 128
BLOCK = 128
N_BUCKETS = 32
MAX_DISTANCE = 128
N_BIAS_HEADS = A_HEADS + B_HEADS
N_GROUPS = 4
EXPERTS_PER_GROUP = 8
N_EXPERTS = N_GROUPS * EXPERTS_PER_GROUP
EXPERT_TOPK = 2
D_EXPERT = 256
ALPHA = (2 * DEPTH) ** 0.25
BETA = (8 * DEPTH) ** -0.25
LN_EPS = 1e-5
RMS_EPS = 1e-6

IN_WIDTHS = (A_HEADS * A_LAT, A_LAT, IDX_HEADS * IDX_DIM, IDX_DIM, IDX_HEADS,
             B_HEADS * B_HEAD_DIM, B_KV_HEADS * B_HEAD_DIM, B_KV_HEADS * B_HEAD_DIM, D_MODEL, D_MODEL)
IN_WIDTH = (A_HEADS * A_LAT + A_LAT + IDX_HEADS * IDX_DIM + IDX_DIM + IDX_HEADS
            + B_HEADS * B_HEAD_DIM + 2 * B_KV_HEADS * B_HEAD_DIM + 2 * D_MODEL)

kernel_name = "hybrid_dsa_swa_sink_hmoe_deepnorm"


def _split_points():
    return [int(v) for v in np.cumsum(IN_WIDTHS)[:-1]]


def layer_norm(x, g, b):
    xf = x.astype(jnp.float32)
    mu = xf.mean(-1, keepdims=True)
    var = jnp.square(xf - mu).mean(-1, keepdims=True)
    return ((xf - mu) * lax.rsqrt(var + LN_EPS) * g.astype(jnp.float32) + b.astype(jnp.float32)).astype(x.dtype)


def rms_norm(x, g):
    xf = x.astype(jnp.float32)
    ms = jnp.square(xf).mean(-1, keepdims=True)
    return (xf * lax.rsqrt(ms + RMS_EPS) * g.astype(jnp.float32)).astype(x.dtype)


def t5_bucket(dist):
    n = jnp.maximum(dist, 0)
    max_exact = N_BUCKETS // 2
    nf = jnp.maximum(n, 1).astype(jnp.float32)
    large = max_exact + (jnp.log(nf / max_exact) / math.log(MAX_DISTANCE / max_exact)
                         * (N_BUCKETS - max_exact)).astype(jnp.int32)
    large = jnp.minimum(large, N_BUCKETS - 1)
    return jnp.where(n < max_exact, n, large).astype(jnp.int32)


def dsa_branch(q_lat, c_kv, q_idx, k_idx, w_idx, w_uv, bias_tab):
    bsz, L = c_kv.shape[:2]
    k_sel = min(TOPK_MAX, L // 4)
    n_blk = L // BLOCK
    key_pos = jnp.arange(L, dtype=jnp.int32)
    idx_scale = IDX_DIM ** -0.5
    w_scale = IDX_HEADS ** -0.5
    att_scale = A_LAT ** -0.5
    bias_a = bias_tab[:, :A_HEADS]
    gather = jax.vmap(lambda kv_b, i_b: kv_b[i_b])

    def to_blocks(a):
        return jnp.moveaxis(a.reshape((bsz, n_blk, BLOCK) + a.shape[2:]), 1, 0)

    def one_block(args):
        blk, qa, qi, wi = args
        q_pos = blk * BLOCK + jnp.arange(BLOCK, dtype=jnp.int32)
        s = jnp.einsum('bthd,bsd->bths', qi, k_idx) * idx_scale
        score = jnp.einsum('bths,bth->bts', jax.nn.relu(s), wi * w_scale).astype(jnp.float32)
        causal = key_pos[None, :] <= q_pos[:, None]
        score = jnp.where(causal[None], score, -jnp.inf)
        _, sel = lax.top_k(score, k_sel)
        kv = gather(c_kv, sel)
        dist = q_pos[None, :, None] - sel
        valid = dist >= 0
        bias = jnp.moveaxis(bias_a[t5_bucket(dist)], -1, 2).astype(jnp.float32)
        logits = jnp.einsum('bthc,btkc->bthk', qa, kv).astype(jnp.float32) * att_scale + bias
        logits = jnp.where(valid[:, :, None, :], logits, -jnp.inf)
        p = jax.nn.softmax(logits, axis=-1).astype(kv.dtype)
        o = jnp.einsum('bthk,btkc->bthc', p, kv)
        return jnp.einsum('bthc,hcd->bthd', o, w_uv)

    out = lax.map(one_block, (jnp.arange(n_blk, dtype=jnp.int32), to_blocks(q_lat),
                              to_blocks(q_idx), to_blocks(w_idx)))
    return jnp.moveaxis(out, 0, 1).reshape(bsz, L, A_HEADS * A_HEAD_DIM)


def swa_branch(q, k, v, sinks, bias_tab):
    bsz, L = q.shape[:2]
    n_blk = L // BLOCK
    grp = B_HEADS // B_KV_HEADS
    qb = q.reshape(bsz, n_blk, BLOCK, B_KV_HEADS, grp, B_HEAD_DIM)

    def band(a):
        a = jnp.pad(a, ((0, 0), (BLOCK, 0), (0, 0), (0, 0)))
        a = a.reshape(bsz, n_blk + 1, BLOCK, B_KV_HEADS, B_HEAD_DIM)
        return jnp.concatenate([a[:, :-1], a[:, 1:]], axis=2)

    kb, vb = band(k), band(v)
    t_loc = jnp.arange(BLOCK, dtype=jnp.int32)[:, None]
    s_loc = jnp.arange(2 * BLOCK, dtype=jnp.int32)[None, :]
    dist = t_loc + BLOCK - s_loc
    in_window = (dist >= 0) & (dist < WINDOW)
    blk_ids = jnp.arange(n_blk, dtype=jnp.int32)[:, None, None]
    mask = in_window[None] & ((blk_ids > 0) | (s_loc >= BLOCK)[None])
    bias = bias_tab[:, A_HEADS:][t5_bucket(dist)].astype(jnp.float32)
    bias = jnp.transpose(bias, (2, 0, 1)).reshape(B_KV_HEADS, grp, BLOCK, 2 * BLOCK)
    logits = jnp.einsum('bntkgd,bnskd->bnkgts', qb, kb).astype(jnp.float32) * (B_HEAD_DIM ** -0.5) + bias
    logits = jnp.where(mask[None, :, None, None], logits, -jnp.inf)
    sink = sinks.astype(jnp.float32).reshape(1, 1, B_KV_HEADS, grp, 1, 1)
    m = jnp.maximum(logits.max(-1, keepdims=True), sink)
    p = jnp.exp(logits - m)
    p = (p / (p.sum(-1, keepdims=True) + jnp.exp(sink - m))).astype(v.dtype)
    o = jnp.einsum('bnkgts,bnskd->bntkgd', p, vb)
    return o.reshape(bsz, L, B_HEADS * B_HEAD_DIM)


def hier_moe(x, w_group, b_group, w_router, b_router, w_gate, w_up, w_down):
    bsz, L, d = x.shape
    xt = x.reshape(-1, d)
    g_logits = (xt @ w_group).astype(jnp.float32) + b_group.astype(jnp.float32)
    g_prob = jax.nn.softmax(g_logits, axis=-1)
    g_sel = jnp.argmax(g_logits, axis=-1).astype(jnp.int32)
    g_w = jnp.take_along_axis(g_prob, g_sel[:, None], axis=-1)
    e_logits = ((xt @ w_router).astype(jnp.float32) + b_router.astype(jnp.float32))
    e_logits = e_logits.reshape(-1, N_GROUPS, EXPERTS_PER_GROUP)
    e_logits = jnp.take_along_axis(e_logits, g_sel[:, None, None], axis=1)[:, 0]
    e_prob = jax.nn.softmax(e_logits, axis=-1)
    top_p, top_i = lax.top_k(e_prob, EXPERT_TOPK)
    top_w = g_w * top_p / top_p.sum(-1, keepdims=True)
    expert_id = g_sel[:, None] * EXPERTS_PER_GROUP + top_i
    combine = (jax.nn.one_hot(expert_id, N_EXPERTS, dtype=jnp.float32) * top_w[..., None]).sum(1)
    combine = combine.astype(x.dtype)
    out = jnp.zeros_like(xt)
    for e in range(N_EXPERTS):
        h = jax.nn.silu(xt @ w_gate[e]) * (xt @ w_up[e])
        out = out + combine[:, e:e + 1] * (h @ w_down[e])
    return out.reshape(bsz, L, d)


def setup_inputs(seed: int = 0) -> dict:
    key = jax.random.key(seed)
    ks = jax.random.split(key, 20)
    f32 = jnp.float32

    def nrm(k, shape, scale):
        return jax.random.normal(k, shape, f32) * scale

    a_width = A_HEADS * A_HEAD_DIM
    b_width = B_HEADS * B_HEAD_DIM
    return {
        "x": nrm(ks[0], (BATCH, SEQ, D_MODEL), 1.0),
        "w_in": nrm(ks[1], (DEPTH, D_MODEL, IN_WIDTH), D_MODEL ** -0.5),
        "kv_norm_g": 1.0 + nrm(ks[2], (DEPTH, A_LAT), 0.02),
        "w_uv": nrm(ks[3], (DEPTH, A_HEADS, A_LAT, A_HEAD_DIM), A_LAT ** -0.5),
        "w_branch_a": nrm(ks[4], (DEPTH, a_width, D_MODEL), a_width ** -0.5),
        "sinks": nrm(ks[5], (DEPTH, B_HEADS), 0.5),
        "w_branch_b": nrm(ks[6], (DEPTH, b_width, D_MODEL), b_width ** -0.5),
        "w_out": nrm(ks[7], (DEPTH, D_MODEL, D_MODEL), BETA * D_MODEL ** -0.5),
        "rel_bias": nrm(ks[8], (N_BUCKETS, N_BIAS_HEADS), 0.5),
        "ln1_g": 1.0 + nrm(ks[9], (DEPTH, D_MODEL), 0.02),
        "ln1_b": nrm(ks[10], (DEPTH, D_MODEL), 0.02),
        "w_group": nrm(ks[11], (DEPTH, D_MODEL, N_GROUPS), D_MODEL ** -0.5),
        "b_group": nrm(ks[12], (DEPTH, N_GROUPS), 0.01),
        "w_router": nrm(ks[13], (DEPTH, D_MODEL, N_EXPERTS), D_MODEL ** -0.5),
        "b_router": nrm(ks[14], (DEPTH, N_EXPERTS), 0.01),
        "w_gate": nrm(ks[15], (DEPTH, N_EXPERTS, D_MODEL, D_EXPERT), D_MODEL ** -0.5),
        "w_up": nrm(ks[16], (DEPTH, N_EXPERTS, D_MODEL, D_EXPERT), D_MODEL ** -0.5),
        "w_down": nrm(ks[17], (DEPTH, N_EXPERTS, D_EXPERT, D_MODEL), BETA * D_EXPERT ** -0.5),
        "ln2_g": 1.0 + nrm(ks[18], (DEPTH, D_MODEL), 0.02),
        "ln2_b": nrm(ks[19], (DEPTH, D_MODEL), 0.02),
    }


def reference(x, w_in, kv_norm_g, w_uv, w_branch_a, sinks, w_branch_b, w_out, rel_bias,
              ln1_g, ln1_b, w_group, b_group, w_router, b_router, w_gate, w_up, w_down,
              ln2_g, ln2_b):
    bsz, L, _ = x.shape
    splits = _split_points()
    for layer in range(DEPTH):
        proj = x @ w_in[layer]
        (q_lat, c_kv, q_idx, k_idx, w_idx, q_b, k_b, v_b, gate_a, gate_b) = jnp.split(proj, splits, axis=-1)
        c_kv = rms_norm(c_kv, kv_norm_g[layer])
        y_a = dsa_branch(q_lat.reshape(bsz, L, A_HEADS, A_LAT), c_kv,
                         q_idx.reshape(bsz, L, IDX_HEADS, IDX_DIM), k_idx, w_idx,
                         w_uv[layer], rel_bias)
        y_b = swa_branch(q_b.reshape(bsz, L, B_HEADS, B_HEAD_DIM),
                         k_b.reshape(bsz, L, B_KV_HEADS, B_HEAD_DIM),
                         v_b.reshape(bsz, L, B_KV_HEADS, B_HEAD_DIM),
                         sinks[layer], rel_bias)
        merged = (jax.nn.sigmoid(gate_a) * (y_a @ w_branch_a[layer])
                  + jax.nn.sigmoid(gate_b) * (y_b @ w_branch_b[layer]))
        x = layer_norm(ALPHA * x + merged @ w_out[layer], ln1_g[layer], ln1_b[layer])
        ffn = hier_moe(x, w_group[layer], b_group[layer], w_router[layer], b_router[layer],
                       w_gate[layer], w_up[layer], w_down[layer])
        x = layer_norm(ALPHA * x + ffn, ln2_g[layer], ln2_b[layer])
    return x
```

```python
import functools
import math

import numpy as np
import jax
import jax.numpy as jnp
from jax import lax
from jax.experimental import pallas as pl
from jax.experimental.pallas import tpu as pltpu

D_MODEL = 1024
A_HEADS = 8
A_LAT = 128
A_HEAD_DIM = 64
IDX_HEADS = 8
IDX_DIM = 64
TOPK_MAX = 256
B_HEADS = 8
B_KV_HEADS = 2
B_HEAD_DIM = 64
WINDOW = 128
BLOCK = 128
N_BUCKETS = 32
MAX_DISTANCE = 128
N_GROUPS = 4
EXPERTS_PER_GROUP = 8
N_EXPERTS = 32
D_EXPERT = 256
DEPTH = 1
ALPHA = (2 * DEPTH) ** 0.25
LN_EPS = 1e-5
RMS_EPS = 1e-6

LANES = 128
INT_MIN = -(2 ** 31)
VMEM_LIMIT = 48 * 1024 * 1024

_NT = (((1,), (1,)), ((), ()))


def _cparams(*sem):
    return pltpu.CompilerParams(dimension_semantics=sem, vmem_limit_bytes=VMEM_LIMIT)


def _t5_bucket_np(dist):
    n = jnp.maximum(dist, 0)
    max_exact = N_BUCKETS // 2
    nf = jnp.maximum(n, 1).astype(jnp.float32)
    large = max_exact + (jnp.log(nf / max_exact) / math.log(MAX_DISTANCE / max_exact)
                         * (N_BUCKETS - max_exact)).astype(jnp.int32)
    large = jnp.minimum(large, N_BUCKETS - 1)
    return jnp.where(n < max_exact, n, large).astype(jnp.int32)


def _bias_kernel(tab_ref, bkt_ref, out_ref, *, head0):
    h = pl.program_id(0) + head0
    bkt = bkt_ref[...]
    acc = jnp.full(bkt.shape, tab_ref[h, N_BUCKETS - 1], jnp.float32)
    for b in range(N_BUCKETS - 1):
        acc = jnp.where(bkt == b, tab_ref[h, b], acc)
    out_ref[...] = acc


def _bias_tiles(tab_t, bkt, head0, n_heads, lane_major):
    g, r, c = bkt.shape
    if lane_major:
        out_shape = jax.ShapeDtypeStruct((g, r, n_heads * c), jnp.float32)
        out_spec = pl.BlockSpec((g, r, c), lambda h: (0, 0, h))
    else:
        out_shape = jax.ShapeDtypeStruct((n_heads * g, r, c), jnp.float32)
        out_spec = pl.BlockSpec((g, r, c), lambda h: (h, 0, 0))
    return pl.pallas_call(
        functools.partial(_bias_kernel, head0=head0),
        grid=(n_heads,),
        in_specs=[pl.BlockSpec(memory_space=pltpu.SMEM),
                  pl.BlockSpec((g, r, c), lambda h: (0, 0, 0))],
        out_specs=out_spec,
        out_shape=out_shape,
        compiler_params=_cparams("arbitrary"),
        name="bias_tiles",
    )(tab_t, bkt)


_C_QLAT = 0
_C_CKV = _C_QLAT + A_HEADS * A_LAT
_C_QIDX = _C_CKV + A_LAT
_C_KW = _C_QIDX + IDX_HEADS * IDX_DIM
_C_QB = _C_KW + LANES
_C_KB = _C_QB + B_HEADS * B_HEAD_DIM
_C_VB = _C_KB + B_KV_HEADS * B_HEAD_DIM
_C_GA = _C_VB + B_KV_HEADS * B_HEAD_DIM
_C_GB = _C_GA + D_MODEL
_C_END = _C_GB + D_MODEL
_RAW_KW_END = _C_KW + IDX_DIM + IDX_HEADS


def _proj_kernel(x_ref, w_ref, g_ref, q_ref, ckv_ref, qidx_ref, kidx_ref, widx_ref,
                 qb_ref, kb_ref, vb_ref, ga_ref, gb_ref):
    xb = x_ref[...].astype(jnp.bfloat16)
    tm = xb.shape[0]

    def seg(lo, hi):
        return jnp.dot(xb, w_ref[:, lo:hi], preferred_element_type=jnp.float32)

    for h in range(A_HEADS):
        qh = seg(_C_QLAT + h * A_LAT, _C_QLAT + (h + 1) * A_LAT).astype(jnp.bfloat16)
        for r in range(tm // BLOCK):
            q_ref[r, h] = qh[r * BLOCK:(r + 1) * BLOCK]
    c = seg(_C_CKV, _C_QIDX)
    ms = jnp.mean(c * c, axis=-1, keepdims=True)
    ckv_ref[...] = (c * lax.rsqrt(ms + RMS_EPS) * g_ref[...]).astype(jnp.bfloat16)
    qidx_ref[...] = seg(_C_QIDX, _C_KW).astype(jnp.bfloat16)
    kw = seg(_C_KW, _C_QB)
    kidx_ref[...] = kw[:, :IDX_DIM].astype(jnp.bfloat16)
    widx_ref[...] = kw[:, IDX_DIM:IDX_DIM + IDX_HEADS]
    qb_ref[...] = seg(_C_QB, _C_KB).astype(jnp.bfloat16)
    kb_ref[...] = seg(_C_KB, _C_VB).astype(jnp.bfloat16)
    vb_ref[...] = seg(_C_VB, _C_GA).astype(jnp.bfloat16)
    ga_ref[...] = jax.nn.sigmoid(seg(_C_GA, _C_GB))
    gb_ref[...] = jax.nn.sigmoid(seg(_C_GB, _C_END))


def _input_projection(x2, w_pack, kv_g, tm):
    n = x2.shape[0]
    bf, f32 = jnp.bfloat16, jnp.float32
    row = lambda w: pl.BlockSpec((tm, w), lambda i: (i, 0))
    out_shape = (
        jax.ShapeDtypeStruct((n // BLOCK, A_HEADS, BLOCK, A_LAT), bf),
        jax.ShapeDtypeStruct((n, A_LAT), bf),
        jax.ShapeDtypeStruct((n, IDX_HEADS * IDX_DIM), bf),
        jax.ShapeDtypeStruct((n, IDX_DIM), bf),
        jax.ShapeDtypeStruct((n, IDX_HEADS), f32),
        jax.ShapeDtypeStruct((n, B_HEADS * B_HEAD_DIM), bf),
        jax.ShapeDtypeStruct((n, B_KV_HEADS * B_HEAD_DIM), bf),
        jax.ShapeDtypeStruct((n, B_KV_HEADS * B_HEAD_DIM), bf),
        jax.ShapeDtypeStruct((n, D_MODEL), f32),
        jax.ShapeDtypeStruct((n, D_MODEL), f32),
    )
    out_specs = (
        pl.BlockSpec((tm // BLOCK, A_HEADS, BLOCK, A_LAT), lambda i: (i, 0, 0, 0)),
        row(A_LAT), row(IDX_HEADS * IDX_DIM), row(IDX_DIM), row(IDX_HEADS),
        row(B_HEADS * B_HEAD_DIM), row(B_KV_HEADS * B_HEAD_DIM), row(B_KV_HEADS * B_HEAD_DIM),
        row(D_MODEL), row(D_MODEL),
    )
    return pl.pallas_call(
        _proj_kernel,
        grid=(n // tm,),
        in_specs=[row(D_MODEL),
                  pl.BlockSpec((D_MODEL, _C_END), lambda i: (0, 0)),
                  pl.BlockSpec((1, A_LAT), lambda i: (0, 0))],
        out_specs=out_specs,
        out_shape=out_shape,
        compiler_params=_cparams("arbitrary"),
        name="input_projection",
    )(x2, w_pack, kv_g)


def _dsa_kernel(qidx_ref, kidx_ref, widx_ref, q_ref, ckv_ref, ckvt_ref, bias_ref, wuv_ref,
                tri_ref, ya_ref, keys_ref, lg_ref, *, k_sel, search_chunk):
    i = pl.program_id(1)
    n_blk = i + 1
    f32 = jnp.float32
    s_loc = lax.broadcasted_iota(jnp.int32, (BLOCK, BLOCK), 0)
    t_loc = lax.broadcasted_iota(jnp.int32, (BLOCK, BLOCK), 1)
    idx_scale = IDX_DIM ** -0.5
    w_scale = IDX_HEADS ** -0.5
    att_scale = A_LAT ** -0.5

    keys_ref[...] = jnp.full(keys_ref.shape, INT_MIN, jnp.int32)
    wts = widx_ref[0] * w_scale
    qidx = qidx_ref[...]

    def score_block(j, carry):
        ks = kidx_ref[0, pl.ds(pl.multiple_of(j * BLOCK, BLOCK), BLOCK), :]
        acc = jnp.zeros((BLOCK, BLOCK), f32)
        for h in range(IDX_HEADS):
            s = lax.dot_general(ks, qidx[:, h * IDX_DIM:(h + 1) * IDX_DIM], _NT,
                                preferred_element_type=f32) * idx_scale
            acc = acc + jnp.maximum(s, 0.0) * wts[h:h + 1, :]
        causal = (j * BLOCK + s_loc) <= (i * BLOCK + t_loc)
        bits = lax.bitcast_convert_type(acc + 0.0, jnp.int32)
        key = bits ^ ((bits >> 31) & 0x7FFFFFFF)
        keys_ref[pl.ds(pl.multiple_of(j * BLOCK, BLOCK), BLOCK), :] = jnp.where(causal, key, INT_MIN)
        return carry

    lax.fori_loop(0, n_blk, score_block, 0)

    n_chunks = (n_blk * BLOCK + search_chunk - 1) // search_chunk

    def count_ge(cand):
        def body(c, cnt):
            kc = keys_ref[pl.ds(pl.multiple_of(c * search_chunk, search_chunk), search_chunk), :]
            return cnt + jnp.sum((kc >= cand).astype(f32), axis=0, keepdims=True)
        return lax.fori_loop(0, n_chunks, body, jnp.zeros((1, BLOCK), f32))

    def search_bit(b, prefix_u):
        bit = lax.shift_left(jnp.int32(1), 31 - b)
        cand_u = prefix_u | bit
        cnt = count_ge(cand_u ^ INT_MIN)
        return jnp.where(cnt >= k_sel, cand_u, prefix_u)

    thr_u = lax.fori_loop(0, 32, search_bit, jnp.zeros((1, BLOCK), jnp.int32))
    thr = thr_u ^ INT_MIN
    c_gt = count_ge(thr + 1) * (thr != 0x7FFFFFFF).astype(f32)
    n_tie = k_sel - c_gt

    q_all = q_ref[0]
    tri = tri_ref[...]

    def logit_block(j, carry):
        m_run, tie_seen = carry
        row0 = pl.multiple_of(j * BLOCK, BLOCK)
        kc = keys_ref[pl.ds(row0, BLOCK), :]
        eq = kc == thr
        tie_rank = jnp.dot(tri, eq.astype(jnp.bfloat16), preferred_element_type=f32) + tie_seen
        causal = (j * BLOCK + s_loc) <= (i * BLOCK + t_loc)
        sel = ((kc > thr) | (eq & (tie_rank <= n_tie))) & causal
        neg = jnp.where(sel, 0.0, -jnp.inf)
        kind = jnp.clip(j - i + 2, 0, 2)
        cs = ckv_ref[0, pl.ds(row0, BLOCK), :]
        lg = lax.dot_general(cs, q_all, _NT, preferred_element_type=f32)
        lg = lg * att_scale + bias_ref[kind] + jnp.concatenate([neg] * A_HEADS, axis=1)
        lg_ref[pl.ds(row0, BLOCK), :] = lg
        m_run = jnp.maximum(m_run, jnp.max(lg, axis=0, keepdims=True))
        tie_seen = tie_seen + jnp.sum(eq.astype(f32), axis=0, keepdims=True)
        return m_run, tie_seen

    m_run, _ = lax.fori_loop(
        0, n_blk, logit_block,
        (jnp.full((1, A_HEADS * BLOCK), -jnp.inf, f32), jnp.zeros((1, BLOCK), f32)))

    def pv_block(j, carry):
        l_run, o_run = carry
        row0 = pl.multiple_of(j * BLOCK, BLOCK)
        p = jnp.exp(lg_ref[pl.ds(row0, BLOCK), :] - m_run)
        l_run = l_run + jnp.sum(p, axis=0, keepdims=True)
        ct = ckvt_ref[0, :, pl.ds(row0, BLOCK)]
        o_run = o_run + jnp.dot(ct, p.astype(jnp.bfloat16), preferred_element_type=f32)
        return l_run, o_run

    l_run, o_run = lax.fori_loop(
        0, n_blk, pv_block,
        (jnp.zeros((1, A_HEADS * BLOCK), f32), jnp.zeros((A_LAT, A_HEADS * BLOCK), f32)))
    o_t = o_run / l_run

    for h in range(A_HEADS):
        o_h = o_t[:, h * BLOCK:(h + 1) * BLOCK].T.astype(jnp.bfloat16)
        y_h = jnp.dot(o_h, wuv_ref[h], preferred_element_type=f32)
        ya_ref[:, h * A_HEAD_DIM:(h + 1) * A_HEAD_DIM] = y_h.astype(ya_ref.dtype)


def _dsa_attention(qidx, kidx, widx_t, q_blocks, ckv, ckv_t, bias_t, wuv, bsz, seq):
    n_blk = seq // BLOCK
    k_sel = min(TOPK_MAX, seq // 4)
    search_chunk = min(512, seq)
    tri = jnp.asarray(np.tril(np.ones((BLOCK, BLOCK), np.float32)), jnp.bfloat16)
    kern = functools.partial(_dsa_kernel, k_sel=float(k_sel), search_chunk=search_chunk)
    return pl.pallas_call(
        kern,
        grid=(bsz, n_blk),
        in_specs=[
            pl.BlockSpec((BLOCK, IDX_HEADS * IDX_DIM), lambda b, i: (b * n_blk + i, 0)),
            pl.BlockSpec((1, seq, IDX_DIM), lambda b, i: (b, 0, 0)),
            pl.BlockSpec((1, IDX_HEADS, BLOCK), lambda b, i: (b * n_blk + i, 0, 0)),
            pl.BlockSpec((1, A_HEADS * BLOCK, A_LAT), lambda b, i: (b * n_blk + i, 0, 0)),
            pl.BlockSpec((1, seq, A_LAT), lambda b, i: (b, 0, 0)),
            pl.BlockSpec((1, A_LAT, seq), lambda b, i: (b, 0, 0)),
            pl.BlockSpec((3, BLOCK, A_HEADS * BLOCK), lambda b, i: (0, 0, 0)),
            pl.BlockSpec((A_HEADS, A_LAT, A_HEAD_DIM), lambda b, i: (0, 0, 0)),
            pl.BlockSpec((BLOCK, BLOCK), lambda b, i: (0, 0)),
        ],
        out_specs=pl.BlockSpec((BLOCK, A_HEADS * A_HEAD_DIM), lambda b, i: (b * n_blk + i, 0)),
        out_shape=jax.ShapeDtypeStruct((bsz * seq, A_HEADS * A_HEAD_DIM), jnp.bfloat16),
        scratch_shapes=[pltpu.VMEM((seq, BLOCK), jnp.int32),
                        pltpu.VMEM((seq, A_HEADS * BLOCK), jnp.float32)],
        compiler_params=_cparams("arbitrary", "arbitrary"),
        name="dsa_attention",
    )(qidx, kidx, widx_t, q_blocks, ckv, ckv_t, bias_t, wuv, tri)


def _swa_kernel(sink_ref, qb_ref, kp_ref, kc_ref, vp_ref, vc_ref, bias_ref, yb_ref):
    i = pl.program_id(1)
    f32 = jnp.float32
    t_loc = lax.broadcasted_iota(jnp.int32, (BLOCK, BLOCK), 0)
    s_loc = lax.broadcasted_iota(jnp.int32, (BLOCK, BLOCK), 1)
    d_prev = t_loc + BLOCK - s_loc
    d_cur = t_loc - s_loc
    has_prev = jnp.zeros_like(d_prev) + jnp.minimum(i, 1)
    m_prev = (d_prev >= 0) & (d_prev < WINDOW) & (has_prev > 0)
    m_cur = (d_cur >= 0) & (d_cur < WINDOW)
    grp = B_HEADS // B_KV_HEADS
    scale = B_HEAD_DIM ** -0.5
    q = qb_ref[...]
    kp, kc, vp, vc = kp_ref[...], kc_ref[...], vp_ref[...], vc_ref[...]
    for h in range(B_HEADS):
        kv = h // grp
        qh = q[:, h * B_HEAD_DIM:(h + 1) * B_HEAD_DIM]
        sl = slice(kv * B_HEAD_DIM, (kv + 1) * B_HEAD_DIM)
        bias = bias_ref[h]
        lp = lax.dot_general(qh, kp[:, sl], _NT, preferred_element_type=f32) * scale + bias[:, :BLOCK]
        lc = lax.dot_general(qh, kc[:, sl], _NT, preferred_element_type=f32) * scale + bias[:, BLOCK:]
        lp = jnp.where(m_prev, lp, -jnp.inf)
        lc = jnp.where(m_cur, lc, -jnp.inf)
        sink = sink_ref[h]
        m = jnp.maximum(jnp.maximum(jnp.max(lp, axis=1, keepdims=True),
                                    jnp.max(lc, axis=1, keepdims=True)), sink)
        pp = jnp.exp(lp - m)
        pc = jnp.exp(lc - m)
        den = (jnp.sum(pp, axis=1, keepdims=True) + jnp.sum(pc, axis=1, keepdims=True)
               + jnp.exp(sink - m))
        o = (jnp.dot((pp / den).astype(jnp.bfloat16), vp[:, sl], preferred_element_type=f32)
             + jnp.dot((pc / den).astype(jnp.bfloat16), vc[:, sl], preferred_element_type=f32))
        yb_ref[:, h * B_HEAD_DIM:(h + 1) * B_HEAD_DIM] = o.astype(yb_ref.dtype)


def _swa_attention(sinks, qb, kb, vb, bias_b, bsz, seq):
    n_blk = seq // BLOCK
    kvw = B_KV_HEADS * B_HEAD_DIM
    cur = lambda b, i: (b * n_blk + i, 0)
    prev = lambda b, i: (b * n_blk + jnp.maximum(i - 1, 0), 0)
    return pl.pallas_call(
        _swa_kernel,
        grid=(bsz, n_blk),
        in_specs=[
            pl.BlockSpec(memory_space=pltpu.SMEM),
            pl.BlockSpec((BLOCK, B_HEADS * B_HEAD_DIM), cur),
            pl.BlockSpec((BLOCK, kvw), prev), pl.BlockSpec((BLOCK, kvw), cur),
            pl.BlockSpec((BLOCK, kvw), prev), pl.BlockSpec((BLOCK, kvw), cur),
            pl.BlockSpec((B_HEADS, BLOCK, 2 * BLOCK), lambda b, i: (0, 0, 0)),
        ],
        out_specs=pl.BlockSpec((BLOCK, B_HEADS * B_HEAD_DIM), cur),
        out_shape=jax.ShapeDtypeStruct((bsz * seq, B_HEADS * B_HEAD_DIM), jnp.bfloat16),
        compiler_params=_cparams("arbitrary", "arbitrary"),
        name="swa_attention",
    )(sinks, qb, kb, kb, vb, vb, bias_b)


def _layer_norm(h, g, b):
    mu = jnp.mean(h, axis=-1, keepdims=True)
    d = h - mu
    var = jnp.mean(d * d, axis=-1, keepdims=True)
    return d * lax.rsqrt(var + LN_EPS) * g + b


def _merge_kernel(ya_ref, yb_ref, ga_ref, gb_ref, x_ref, wa_ref, wb_ref, wo_ref, g1_ref, b1_ref,
                  wr_ref, br_ref, x1_ref, comb_ref):
    f32 = jnp.float32
    pa = jnp.dot(ya_ref[...], wa_ref[...], preferred_element_type=f32)
    pb = jnp.dot(yb_ref[...], wb_ref[...], preferred_element_type=f32)
    merged = ga_ref[...] * pa + gb_ref[...] * pb
    h = ALPHA * x_ref[...] + jnp.dot(merged.astype(jnp.bfloat16), wo_ref[...],
                                     preferred_element_type=f32)
    x1 = _layer_norm(h, g1_ref[...], b1_ref[...])
    x1_ref[...] = x1

    lg = jnp.dot(x1, wr_ref[...], preferred_element_type=f32,
                 precision=lax.Precision.HIGHEST) + br_ref[...]
    lane_i = lax.broadcasted_iota(jnp.int32, lg.shape, 1)
    lane = lane_i.astype(f32)
    big = jnp.float32(1 << 20)
    is_g = lane_i < N_GROUPS
    gl = jnp.where(is_g, lg, -jnp.inf)
    gmax = jnp.max(gl, axis=1, keepdims=True)
    g_sel = jnp.min(jnp.where(gl == gmax, lane, big), axis=1, keepdims=True)
    g_w = 1.0 / jnp.sum(jnp.where(is_g, jnp.exp(gl - gmax), 0.0), axis=1, keepdims=True)
    e_id = lane_i - N_GROUPS
    e_grp = (e_id >> 3).astype(f32)
    in_grp = (e_id >= 0) & (e_id < N_EXPERTS) & (e_grp == g_sel)
    el = jnp.where(in_grp, lg, -jnp.inf)
    emax = jnp.max(el, axis=1, keepdims=True)
    ee = jnp.where(in_grp, jnp.exp(el - emax), 0.0)
    ep = ee / jnp.sum(ee, axis=1, keepdims=True)
    epm = jnp.where(in_grp, ep, -1.0)
    p1 = jnp.max(epm, axis=1, keepdims=True)
    i1 = jnp.min(jnp.where(epm == p1, lane, big), axis=1, keepdims=True)
    epm2 = jnp.where(lane == i1, -1.0, epm)
    p2 = jnp.max(epm2, axis=1, keepdims=True)
    i2 = jnp.min(jnp.where(epm2 == p2, lane, big), axis=1, keepdims=True)
    psum = p1 + p2
    w1 = g_w * p1 / psum
    w2 = g_w * p2 / psum
    comb_ref[...] = (jnp.where(lane + N_GROUPS == i1, w1, 0.0)
                     + jnp.where(lane + N_GROUPS == i2, w2, 0.0))


def _merge_route(ya, yb, ga, gb, x2, wa, wb, wo, g1, b1, wr, br, tm):
    n = x2.shape[0]
    row = lambda w: pl.BlockSpec((tm, w), lambda i: (i, 0))
    full = lambda r, c: pl.BlockSpec((r, c), lambda i: (0, 0))
    aw, bw = A_HEADS * A_HEAD_DIM, B_HEADS * B_HEAD_DIM
    return pl.pallas_call(
        _merge_kernel,
        grid=(n // tm,),
        in_specs=[row(aw), row(bw), row(D_MODEL), row(D_MODEL), row(D_MODEL),
                  full(aw, D_MODEL), full(bw, D_MODEL), full(D_MODEL, D_MODEL),
                  full(1, D_MODEL), full(1, D_MODEL), full(D_MODEL, LANES), full(1, LANES)],
        out_specs=(row(D_MODEL), row(LANES)),
        out_shape=(jax.ShapeDtypeStruct((n, D_MODEL), jnp.float32),
                   jax.ShapeDtypeStruct((n, LANES), jnp.float32)),
        compiler_params=_cparams("arbitrary"),
        name="merge_route",
    )(ya, yb, ga, gb, x2, wa, wb, wo, g1, b1, wr, br)


def _moe_kernel(x1_ref, comb_ref, wg_ref, wu_ref, wd_ref, g2_ref, b2_ref, out_ref, xb_ref, acc_ref):
    e = pl.program_id(1)
    f32 = jnp.float32

    @pl.when(e == 0)
    def _():
        xb_ref[...] = x1_ref[...].astype(jnp.bfloat16)
        acc_ref[...] = jnp.zeros_like(acc_ref)

    xb = xb_ref[...]
    g = jnp.dot(xb, wg_ref[0], preferred_element_type=f32)
    u = jnp.dot(xb, wu_ref[0], preferred_element_type=f32)
    hmid = (g * jax.nn.sigmoid(g) * u).astype(jnp.bfloat16)
    y = jnp.dot(hmid, wd_ref[0], preferred_element_type=f32)
    comb = comb_ref[...]
    lane = lax.broadcasted_iota(jnp.int32, comb.shape, 1)
    c_e = jnp.sum(jnp.where(lane == e, comb, 0.0), axis=1, keepdims=True)
    acc_ref[...] += c_e * y

    @pl.when(e == N_EXPERTS - 1)
    def _():
        out_ref[...] = _layer_norm(ALPHA * x1_ref[...] + acc_ref[...], g2_ref[...], b2_ref[...])


def _moe_dense(x1, comb, wg, wu, wd, g2, b2, tm):
    n = x1.shape[0]
    return pl.pallas_call(
        _moe_kernel,
        grid=(n // tm, N_EXPERTS),
        in_specs=[
            pl.BlockSpec((tm, D_MODEL), lambda i, e: (i, 0)),
            pl.BlockSpec((tm, LANES), lambda i, e: (i, 0)),
            pl.BlockSpec((1, D_MODEL, D_EXPERT), lambda i, e: (e, 0, 0)),
            pl.BlockSpec((1, D_MODEL, D_EXPERT), lambda i, e: (e, 0, 0)),
            pl.BlockSpec((1, D_EXPERT, D_MODEL), lambda i, e: (e, 0, 0)),
            pl.BlockSpec((1, D_MODEL), lambda i, e: (0, 0)),
            pl.BlockSpec((1, D_MODEL), lambda i, e: (0, 0)),
        ],
        out_specs=pl.BlockSpec((tm, D_MODEL), lambda i, e: (i, 0)),
        out_shape=jax.ShapeDtypeStruct((n, D_MODEL), jnp.float32),
        scratch_shapes=[pltpu.VMEM((tm, D_MODEL), jnp.bfloat16),
                        pltpu.VMEM((tm, D_MODEL), jnp.float32)],
        compiler_params=_cparams("arbitrary", "arbitrary"),
        name="moe_dense",
    )(x1, comb, wg, wu, wd, g2, b2)


def _pick_tile(n, pref):
    t = min(pref, n)
    while n % t:
        t //= 2
    return t


def kernel(x, w_in, kv_norm_g, w_uv, w_branch_a, sinks, w_branch_b, w_out, rel_bias, ln1_g, ln1_b,
           w_group, b_group, w_router, b_router, w_gate, w_up, w_down, ln2_g, ln2_b):
    bsz, seq, d = x.shape
    assert d == D_MODEL and seq % BLOCK == 0 and w_in.shape[0] == DEPTH == 1
    n = bsz * seq
    n_blk = seq // BLOCK
    bf, f32 = jnp.bfloat16, jnp.float32
    x2 = x.reshape(n, d)

    r = jnp.arange(BLOCK, dtype=jnp.int32)
    d_far = jnp.full((BLOCK, BLOCK), 2 * BLOCK, jnp.int32)
    d_prev = r[None, :] + BLOCK - r[:, None]
    d_cur = r[None, :] - r[:, None]
    bkt_a = _t5_bucket_np(jnp.stack([d_far, d_prev, d_cur]))
    s2 = jnp.arange(2 * BLOCK, dtype=jnp.int32)
    bkt_b = _t5_bucket_np(r[:, None] + BLOCK - s2[None, :])[None]
    tab_t = rel_bias.astype(f32).T
    bias_a = _bias_tiles(tab_t, bkt_a, 0, A_HEADS, True)
    bias_b = _bias_tiles(tab_t, bkt_b, A_HEADS, B_HEADS, False)

    w0 = w_in[0]
    w_pack = jnp.concatenate(
        [w0[:, :_RAW_KW_END], jnp.zeros((d, _C_QB - _RAW_KW_END), w0.dtype), w0[:, _RAW_KW_END:]],
        axis=1).astype(bf)
    (q_blocks, ckv, qidx, kidx, widx, qb, kb, vb, ga, gb) = _input_projection(
        x2, w_pack, kv_norm_g[0].reshape(1, A_LAT).astype(f32), _pick_tile(n, 512))

    ckv3 = ckv.reshape(bsz, seq, A_LAT)
    ckv_t = jnp.swapaxes(ckv3, 1, 2)
    widx_t = jnp.swapaxes(widx.reshape(bsz * n_blk, BLOCK, IDX_HEADS), 1, 2)
    ya = _dsa_attention(qidx, kidx.reshape(bsz, seq, IDX_DIM), widx_t,
                        q_blocks.reshape(bsz * n_blk, A_HEADS * BLOCK, A_LAT),
                        ckv3, ckv_t, bias_a, w_uv[0].astype(bf), bsz, seq)

    yb = _swa_attention(sinks[0].astype(f32), qb, kb, vb, bias_b, bsz, seq)

    w_route = jnp.concatenate(
        [w_group[0], w_router[0], jnp.zeros((d, LANES - N_GROUPS - N_EXPERTS), f32)], axis=1).astype(f32)
    b_route = jnp.concatenate(
        [b_group[0], b_router[0], jnp.zeros((LANES - N_GROUPS - N_EXPERTS,), f32)]).reshape(1, LANES).astype(f32)
    x1, comb = _merge_route(
        ya, yb, ga, gb, x2, w_branch_a[0].astype(bf), w_branch_b[0].astype(bf), w_out[0].astype(bf),
        ln1_g[0].reshape(1, d).astype(f32), ln1_b[0].reshape(1, d).astype(f32), w_route, b_route,
        _pick_tile(n, 256))

    out = _moe_dense(x1, comb, w_gate[0].astype(bf), w_up[0].astype(bf), w_down[0].astype(bf),
                     ln2_g[0].reshape(1, d).astype(f32), ln2_b[0].reshape(1, d).astype(f32),
                     _pick_tile(n, 1024))
    return out.reshape(bsz, seq, d)
```

```python
import functools
import math

import numpy as np
import jax
import jax.numpy as jnp
from jax import lax
from jax.experimental import pallas as pl
from jax.experimental.pallas import tpu as pltpu

D_MODEL = 1024
A_HEADS = 8
A_LAT = 128
A_HEAD_DIM = 64
IDX_HEADS = 8
IDX_DIM = 64
TOPK_MAX = 256
B_HEADS = 8
B_KV_HEADS = 2
B_HEAD_DIM = 64
WINDOW = 128
BLOCK = 128
N_BUCKETS = 32
MAX_DISTANCE = 128
N_GROUPS = 4
EXPERTS_PER_GROUP = 8
N_EXPERTS = 32
D_EXPERT = 256
DEPTH = 1
ALPHA = (2 * DEPTH) ** 0.25
LN_EPS = 1e-5
RMS_EPS = 1e-6

LANES = 128
INT_MIN = -(2 ** 31)
VMEM_LIMIT = 48 * 1024 * 1024

_NT = (((1,), (1,)), ((), ()))


def _cparams(*sem):
    return pltpu.CompilerParams(dimension_semantics=sem, vmem_limit_bytes=VMEM_LIMIT)


def _t5_bucket_np(dist):
    n = jnp.maximum(dist, 0)
    max_exact = N_BUCKETS // 2
    nf = jnp.maximum(n, 1).astype(jnp.float32)
    large = max_exact + (jnp.log(nf / max_exact) / math.log(MAX_DISTANCE / max_exact)
                         * (N_BUCKETS - max_exact)).astype(jnp.int32)
    large = jnp.minimum(large, N_BUCKETS - 1)
    return jnp.where(n < max_exact, n, large).astype(jnp.int32)


def _bias_kernel(tab_ref, bkt_ref, out_ref, *, head0):
    h = pl.program_id(0) + head0
    bkt = bkt_ref[...]
    acc = jnp.full(bkt.shape, tab_ref[h, N_BUCKETS - 1], jnp.float32)
    for b in range(N_BUCKETS - 1):
        acc = jnp.where(bkt == b, tab_ref[h, b], acc)
    out_ref[...] = acc


def _bias_tiles(tab_t, bkt, head0, n_heads, lane_major):
    g, r, c = bkt.shape
    if lane_major:
        out_shape = jax.ShapeDtypeStruct((g, r, n_heads * c), jnp.float32)
        out_spec = pl.BlockSpec((g, r, c), lambda h: (0, 0, h))
    else:
        out_shape = jax.ShapeDtypeStruct((n_heads * g, r, c), jnp.float32)
        out_spec = pl.BlockSpec((g, r, c), lambda h: (h, 0, 0))
    return pl.pallas_call(
        functools.partial(_bias_kernel, head0=head0),
        grid=(n_heads,),
        in_specs=[pl.BlockSpec(memory_space=pltpu.SMEM),
                  pl.BlockSpec((g, r, c), lambda h: (0, 0, 0))],
        out_specs=out_spec,
        out_shape=out_shape,
        compiler_params=_cparams("arbitrary"),
        name="bias_tiles",
    )(tab_t, bkt)


_C_QLAT = 0
_C_CKV = _C_QLAT + A_HEADS * A_LAT
_C_QIDX = _C_CKV + A_LAT
_C_KW = _C_QIDX + IDX_HEADS * IDX_DIM
_C_QB = _C_KW + LANES
_C_KB = _C_QB + B_HEADS * B_HEAD_DIM
_C_VB = _C_KB + B_KV_HEADS * B_HEAD_DIM
_C_GA = _C_VB + B_KV_HEADS * B_HEAD_DIM
_C_GB = _C_GA + D_MODEL
_C_END = _C_GB + D_MODEL
_RAW_KW_END = _C_KW + IDX_DIM + IDX_HEADS


def _proj_kernel(x_ref, w_ref, g_ref, q_ref, ckv_ref, qidx_ref, kidx_ref, widx_ref,
                 qb_ref, kb_ref, vb_ref, ga_ref, gb_ref):
    xb = x_ref[...].astype(jnp.bfloat16)
    tm = xb.shape[0]

    def seg(lo, hi):
        return jnp.dot(xb, w_ref[:, lo:hi], preferred_element_type=jnp.float32)

    for h in range(A_HEADS):
        qh = seg(_C_QLAT + h * A_LAT, _C_QLAT + (h + 1) * A_LAT).astype(jnp.bfloat16)
        for r in range(tm // BLOCK):
            q_ref[r, h] = qh[r * BLOCK:(r + 1) * BLOCK]
    c = seg(_C_CKV, _C_QIDX)
    ms = jnp.mean(c * c, axis=-1, keepdims=True)
    ckv_ref[...] = (c * lax.rsqrt(ms + RMS_EPS) * g_ref[...]).astype(jnp.bfloat16)
    qidx_ref[...] = seg(_C_QIDX, _C_KW).astype(jnp.bfloat16)
    kw = seg(_C_KW, _C_QB)
    kidx_ref[...] = kw[:, :IDX_DIM].astype(jnp.bfloat16)
    widx_ref[...] = kw[:, IDX_DIM:IDX_DIM + IDX_HEADS]
    qbf = seg(_C_QB, _C_KB).astype(jnp.bfloat16)
    for h in range(B_HEADS):
        for r in range(tm // BLOCK):
            qb_ref[r, h] = qbf[r * BLOCK:(r + 1) * BLOCK, h * B_HEAD_DIM:(h + 1) * B_HEAD_DIM]
    kbf = seg(_C_KB, _C_VB).astype(jnp.bfloat16)
    vbf = seg(_C_VB, _C_GA).astype(jnp.bfloat16)
    for g in range(B_KV_HEADS):
        kb_ref[g] = kbf[:, g * B_HEAD_DIM:(g + 1) * B_HEAD_DIM]
        vb_ref[g, :, :B_HEAD_DIM] = vbf[:, g * B_HEAD_DIM:(g + 1) * B_HEAD_DIM]
        vb_ref[g, :, B_HEAD_DIM:] = jnp.ones((tm, B_HEAD_DIM), jnp.bfloat16)
    ga_ref[...] = jax.nn.sigmoid(seg(_C_GA, _C_GB))
    gb_ref[...] = jax.nn.sigmoid(seg(_C_GB, _C_END))


def _input_projection(x2, w_pack, kv_g, tm):
    n = x2.shape[0]
    bf, f32 = jnp.bfloat16, jnp.float32
    row = lambda w: pl.BlockSpec((tm, w), lambda i: (i, 0))
    out_shape = (
        jax.ShapeDtypeStruct((n // BLOCK, A_HEADS, BLOCK, A_LAT), bf),
        jax.ShapeDtypeStruct((n, A_LAT), bf),
        jax.ShapeDtypeStruct((n, IDX_HEADS * IDX_DIM), bf),
        jax.ShapeDtypeStruct((n, IDX_DIM), bf),
        jax.ShapeDtypeStruct((n, IDX_HEADS), f32),
        jax.ShapeDtypeStruct((n // BLOCK, B_HEADS, BLOCK, B_HEAD_DIM), bf),
        jax.ShapeDtypeStruct((B_KV_HEADS, n, B_HEAD_DIM), bf),
        jax.ShapeDtypeStruct((B_KV_HEADS, n, 2 * B_HEAD_DIM), bf),
        jax.ShapeDtypeStruct((n, D_MODEL), f32),
        jax.ShapeDtypeStruct((n, D_MODEL), f32),
    )
    out_specs = (
        pl.BlockSpec((tm // BLOCK, A_HEADS, BLOCK, A_LAT), lambda i: (i, 0, 0, 0)),
        row(A_LAT), row(IDX_HEADS * IDX_DIM), row(IDX_DIM), row(IDX_HEADS),
        pl.BlockSpec((tm // BLOCK, B_HEADS, BLOCK, B_HEAD_DIM), lambda i: (i, 0, 0, 0)),
        pl.BlockSpec((B_KV_HEADS, tm, B_HEAD_DIM), lambda i: (0, i, 0)),
        pl.BlockSpec((B_KV_HEADS, tm, 2 * B_HEAD_DIM), lambda i: (0, i, 0)),
        row(D_MODEL), row(D_MODEL),
    )
    return pl.pallas_call(
        _proj_kernel,
        grid=(n // tm,),
        in_specs=[row(D_MODEL),
                  pl.BlockSpec((D_MODEL, _C_END), lambda i: (0, 0)),
                  pl.BlockSpec((1, A_LAT), lambda i: (0, 0))],
        out_specs=out_specs,
        out_shape=out_shape,
        compiler_params=_cparams("arbitrary"),
        name="input_projection",
    )(x2, w_pack, kv_g)


def _dsa_kernel(qidx_ref, kidx_ref, widx_ref, q_ref, ckv_ref, ckvt_ref, bias_ref, wuv_ref,
                tri_ref, ya_ref, keys_ref, lg_ref, o_ref, *, k_sel, search_chunk):
    i = pl.program_id(1)
    n_blk = i + 1
    f32 = jnp.float32
    s_loc = lax.broadcasted_iota(jnp.int32, (BLOCK, BLOCK), 0)
    t_loc = lax.broadcasted_iota(jnp.int32, (BLOCK, BLOCK), 1)
    idx_scale = IDX_DIM ** -0.5
    w_scale = IDX_HEADS ** -0.5
    att_scale = A_LAT ** -0.5

    keys_ref[...] = jnp.full(keys_ref.shape, INT_MIN, jnp.int32)
    wts = (widx_ref[0] * w_scale) * idx_scale
    qidx = qidx_ref[...]

    def score_block(j, carry):
        ks = kidx_ref[0, pl.ds(pl.multiple_of(j * BLOCK, BLOCK), BLOCK), :]
        acc = jnp.zeros((BLOCK, BLOCK), f32)
        for h in range(IDX_HEADS):
            s = lax.dot_general(ks, qidx[:, h * IDX_DIM:(h + 1) * IDX_DIM], _NT,
                                preferred_element_type=f32)
            acc = acc + jnp.maximum(s, 0.0) * wts[h:h + 1, :]
        causal = (j * BLOCK + s_loc) <= (i * BLOCK + t_loc)
        bits = lax.bitcast_convert_type(acc + 0.0, jnp.int32)
        key = bits ^ ((bits >> 31) & 0x7FFFFFFF)
        keys_ref[pl.ds(pl.multiple_of(j * BLOCK, BLOCK), BLOCK), :] = jnp.where(causal, key, INT_MIN)
        return carry

    lax.fori_loop(0, n_blk, score_block, 0)

    n_chunks = (n_blk * BLOCK + search_chunk - 1) // search_chunk

    n_acc = 64

    def count_ge(cand):
        def body(c, cnt):
            base = pl.multiple_of(c * search_chunk, search_chunk)
            for r in range(search_chunk // n_acc):
                kc = keys_ref[pl.ds(base + r * n_acc, n_acc), :]
                cnt = cnt + (kc >= cand).astype(f32)
            return cnt
        cnt = lax.fori_loop(0, n_chunks, body, jnp.zeros((n_acc, BLOCK), f32))
        return jnp.sum(cnt, axis=0, keepdims=True)

    def search_bit(b, prefix_u):
        bit = lax.shift_left(jnp.int32(1), 31 - b)
        cand_u = prefix_u | bit
        cnt = count_ge(cand_u ^ INT_MIN)
        return jnp.where(cnt >= k_sel, cand_u, prefix_u)

    thr_u = lax.fori_loop(0, 32, search_bit, jnp.zeros((1, BLOCK), jnp.int32))
    thr = thr_u ^ INT_MIN
    c_gt = count_ge(thr + 1) * (thr != 0x7FFFFFFF).astype(f32)
    n_tie = k_sel - c_gt

    q_all = q_ref[0]
    tri = tri_ref[...]

    def logit_block(j, carry):
        m_run, tie_seen = carry
        row0 = pl.multiple_of(j * BLOCK, BLOCK)
        kc = keys_ref[pl.ds(row0, BLOCK), :]
        eq = kc == thr
        tie_rank = jnp.dot(tri, eq.astype(jnp.bfloat16), preferred_element_type=f32) + tie_seen
        causal = (j * BLOCK + s_loc) <= (i * BLOCK + t_loc)
        sel = ((kc > thr) | (eq & (tie_rank <= n_tie))) & causal
        neg = jnp.where(sel, 0.0, -jnp.inf)
        kind = jnp.clip(j - i + 2, 0, 2)
        cs = ckv_ref[0, pl.ds(row0, BLOCK), :]
        lg = lax.dot_general(cs, q_all, _NT, preferred_element_type=f32)
        lg = lg * att_scale + bias_ref[kind] + jnp.concatenate([neg] * A_HEADS, axis=1)
        lg_ref[pl.ds(row0, BLOCK), :] = lg
        m_run = jnp.maximum(m_run, jnp.max(lg, axis=0, keepdims=True))
        tie_seen = tie_seen + jnp.sum(eq.astype(f32), axis=0, keepdims=True)
        return m_run, tie_seen

    m_run, _ = lax.fori_loop(
        0, n_blk, logit_block,
        (jnp.full((1, A_HEADS * BLOCK), -jnp.inf, f32), jnp.zeros((1, BLOCK), f32)))

    o_ref[...] = jnp.zeros_like(o_ref)

    def pv_block(j, l_run):
        row0 = pl.multiple_of(j * BLOCK, BLOCK)
        p = jnp.exp(lg_ref[pl.ds(row0, BLOCK), :] - m_run)
        ct = ckvt_ref[0, :, pl.ds(row0, BLOCK)]
        o_ref[...] += jnp.dot(ct, p.astype(jnp.bfloat16), preferred_element_type=f32)
        return l_run + jnp.sum(p, axis=0, keepdims=True)

    l_run = lax.fori_loop(0, n_blk, pv_block, jnp.zeros((1, A_HEADS * BLOCK), f32))
    o_t = o_ref[...] / l_run

    for h in range(A_HEADS):
        o_h = o_t[:, h * BLOCK:(h + 1) * BLOCK].T.astype(jnp.bfloat16)
        y_h = jnp.dot(o_h, wuv_ref[h], preferred_element_type=f32)
        ya_ref[:, h * A_HEAD_DIM:(h + 1) * A_HEAD_DIM] = y_h.astype(ya_ref.dtype)


def _dsa_attention(qidx, kidx, widx_t, q_blocks, ckv, ckv_t, bias_t, wuv, bsz, seq):
    n_blk = seq // BLOCK
    k_sel = min(TOPK_MAX, seq // 4)
    search_chunk = min(512, seq)
    tri = jnp.asarray(np.tril(np.ones((BLOCK, BLOCK), np.float32)), jnp.bfloat16)
    kern = functools.partial(_dsa_kernel, k_sel=float(k_sel), search_chunk=search_chunk)
    return pl.pallas_call(
        kern,
        grid=(bsz, n_blk),
        in_specs=[
            pl.BlockSpec((BLOCK, IDX_HEADS * IDX_DIM), lambda b, i: (b * n_blk + i, 0)),
            pl.BlockSpec((1, seq, IDX_DIM), lambda b, i: (b, 0, 0)),
            pl.BlockSpec((1, IDX_HEADS, BLOCK), lambda b, i: (b * n_blk + i, 0, 0)),
            pl.BlockSpec((1, A_HEADS * BLOCK, A_LAT), lambda b, i: (b * n_blk + i, 0, 0)),
            pl.BlockSpec((1, seq, A_LAT), lambda b, i: (b, 0, 0)),
            pl.BlockSpec((1, A_LAT, seq), lambda b, i: (b, 0, 0)),
            pl.BlockSpec((3, BLOCK, A_HEADS * BLOCK), lambda b, i: (0, 0, 0)),
            pl.BlockSpec((A_HEADS, A_LAT, A_HEAD_DIM), lambda b, i: (0, 0, 0)),
            pl.BlockSpec((BLOCK, BLOCK), lambda b, i: (0, 0)),
        ],
        out_specs=pl.BlockSpec((BLOCK, A_HEADS * A_HEAD_DIM), lambda b, i: (b * n_blk + i, 0)),
        out_shape=jax.ShapeDtypeStruct((bsz * seq, A_HEADS * A_HEAD_DIM), jnp.bfloat16),
        scratch_shapes=[pltpu.VMEM((seq, BLOCK), jnp.int32),
                        pltpu.VMEM((seq, A_HEADS * BLOCK), jnp.float32),
                        pltpu.VMEM((A_LAT, A_HEADS * BLOCK), jnp.float32)],
        compiler_params=_cparams("arbitrary", "arbitrary"),
        name="dsa_attention",
    )(qidx, kidx, widx_t, q_blocks, ckv, ckv_t, bias_t, wuv, tri)


SWA_QB = 4


def _swa_kernel(sink_ref, q_ref, kp_ref, kc_ref, vp_ref, vc_ref, bias_ref, yb_ref, *, qb):
    i = pl.program_id(1)
    f32 = jnp.float32
    grp = B_HEADS // B_KV_HEADS
    rows = grp * BLOCK
    row = lax.broadcasted_iota(jnp.int32, (rows, BLOCK), 0)
    t_loc = row & (BLOCK - 1)
    s_loc = lax.broadcasted_iota(jnp.int32, (rows, BLOCK), 1)
    d_prev = t_loc + BLOCK - s_loc
    d_cur = t_loc - s_loc
    m_cur = (d_cur >= 0) & (d_cur < WINDOW)
    head_of_row = lax.broadcasted_iota(jnp.int32, (rows, 1), 0) >> (BLOCK.bit_length() - 1)
    scale = B_HEAD_DIM ** -0.5
    for g in range(B_KV_HEADS):
        sink = jnp.zeros((rows, 1), f32)
        for hh in range(grp):
            sink = jnp.where(head_of_row == hh, sink_ref[g * grp + hh], sink)
        bias = bias_ref[g * grp:(g + 1) * grp].reshape(rows, 2 * BLOCK)
        for r in range(qb):
            has_prev = jnp.zeros_like(d_prev) + jnp.minimum(i * qb + r, 1)
            m_prev = (d_prev >= 0) & (d_prev < WINDOW) & (has_prev > 0)
            q = q_ref[r, g * grp:(g + 1) * grp].reshape(rows, B_HEAD_DIM)
            if r == 0:
                kp, vp = kp_ref[g], vp_ref[g]
            else:
                kp, vp = kc_ref[g, (r - 1) * BLOCK:r * BLOCK], vc_ref[g, (r - 1) * BLOCK:r * BLOCK]
            kc, vc = kc_ref[g, r * BLOCK:(r + 1) * BLOCK], vc_ref[g, r * BLOCK:(r + 1) * BLOCK]
            lp = lax.dot_general(q, kp, _NT, preferred_element_type=f32) * scale + bias[:, :BLOCK]
            lc = lax.dot_general(q, kc, _NT, preferred_element_type=f32) * scale + bias[:, BLOCK:]
            lp = jnp.where(m_prev, lp, -jnp.inf)
            lc = jnp.where(m_cur, lc, -jnp.inf)
            m = jnp.maximum(jnp.max(jnp.maximum(lp, lc), axis=1, keepdims=True), sink)
            pp = jnp.exp(lp - m).astype(jnp.bfloat16)
            pc = jnp.exp(lc - m).astype(jnp.bfloat16)
            ox = (jnp.dot(pp, vp, preferred_element_type=f32)
                  + jnp.dot(pc, vc, preferred_element_type=f32))
            den = ox[:, B_HEAD_DIM:B_HEAD_DIM + 1] + jnp.exp(sink - m)
            o = (ox[:, :B_HEAD_DIM] / den).astype(yb_ref.dtype)
            for hh in range(grp):
                h = g * grp + hh
                yb_ref[r * BLOCK:(r + 1) * BLOCK, h * B_HEAD_DIM:(h + 1) * B_HEAD_DIM] = (
                    o[hh * BLOCK:(hh + 1) * BLOCK])


def _swa_attention(sinks, q_blocks, kb, vbx, bias_b, bsz, seq):
    n_blk = seq // BLOCK
    qb = SWA_QB if n_blk % SWA_QB == 0 else 1
    n_step = n_blk // qb
    cur = lambda b, i: (0, b * n_step + i, 0)
    prev = lambda b, i: (0, b * n_blk + jnp.maximum(i * qb - 1, 0), 0)
    return pl.pallas_call(
        functools.partial(_swa_kernel, qb=qb),
        grid=(bsz, n_step),
        in_specs=[
            pl.BlockSpec(memory_space=pltpu.SMEM),
            pl.BlockSpec((qb, B_HEADS, BLOCK, B_HEAD_DIM), lambda b, i: (b * n_step + i, 0, 0, 0)),
            pl.BlockSpec((B_KV_HEADS, BLOCK, B_HEAD_DIM), prev),
            pl.BlockSpec((B_KV_HEADS, qb * BLOCK, B_HEAD_DIM), cur),
            pl.BlockSpec((B_KV_HEADS, BLOCK, 2 * B_HEAD_DIM), prev),
            pl.BlockSpec((B_KV_HEADS, qb * BLOCK, 2 * B_HEAD_DIM), cur),
            pl.BlockSpec((B_HEADS, BLOCK, 2 * BLOCK), lambda b, i: (0, 0, 0)),
        ],
        out_specs=pl.BlockSpec((qb * BLOCK, B_HEADS * B_HEAD_DIM), lambda b, i: (b * n_step + i, 0)),
        out_shape=jax.ShapeDtypeStruct((bsz * seq, B_HEADS * B_HEAD_DIM), jnp.bfloat16),
        compiler_params=_cparams("arbitrary", "arbitrary"),
        name="swa_attention",
    )(sinks, q_blocks, kb, kb, vbx, vbx, bias_b)


def _layer_norm(h, g, b):
    mu = jnp.mean(h, axis=-1, keepdims=True)
    d = h - mu
    var = jnp.mean(d * d, axis=-1, keepdims=True)
    return d * lax.rsqrt(var + LN_EPS) * g + b


def _merge_kernel(ya_ref, yb_ref, ga_ref, gb_ref, x_ref, wa_ref, wb_ref, wo_ref, g1_ref, b1_ref,
                  wrh_ref, wrl_ref, br_ref, x1_ref, comb_ref):
    f32 = jnp.float32
    pa = jnp.dot(ya_ref[...], wa_ref[...], preferred_element_type=f32)
    pb = jnp.dot(yb_ref[...], wb_ref[...], preferred_element_type=f32)
    merged = ga_ref[...] * pa + gb_ref[...] * pb
    h = ALPHA * x_ref[...] + jnp.dot(merged.astype(jnp.bfloat16), wo_ref[...],
                                     preferred_element_type=f32)
    x1 = _layer_norm(h, g1_ref[...], b1_ref[...])
    x1_ref[...] = x1

    x_hi = x1.astype(jnp.bfloat16)
    x_lo = (x1 - x_hi.astype(f32)).astype(jnp.bfloat16)
    lg = (jnp.dot(x_hi, wrh_ref[...], preferred_element_type=f32)
          + jnp.dot(x_hi, wrl_ref[...], preferred_element_type=f32)
          + jnp.dot(x_lo, wrh_ref[...], preferred_element_type=f32)) + br_ref[...]
    lane_i = lax.broadcasted_iota(jnp.int32, lg.shape, 1)
    lane = lane_i.astype(f32)
    big = jnp.float32(1 << 20)
    is_g = lane_i < N_GROUPS
    gl = jnp.where(is_g, lg, -jnp.inf)
    gmax = jnp.max(gl, axis=1, keepdims=True)
    g_sel = jnp.min(jnp.where(gl == gmax, lane, big), axis=1, keepdims=True)
    g_w = 1.0 / jnp.sum(jnp.where(is_g, jnp.exp(gl - gmax), 0.0), axis=1, keepdims=True)
    e_id = lane_i - N_GROUPS
    e_grp = (e_id >> 3).astype(f32)
    in_grp = (e_id >= 0) & (e_id < N_EXPERTS) & (e_grp == g_sel)
    el = jnp.where(in_grp, lg, -jnp.inf)
    emax = jnp.max(el, axis=1, keepdims=True)
    ee = jnp.where(in_grp, jnp.exp(el - emax), 0.0)
    ep = ee / jnp.sum(ee, axis=1, keepdims=True)
    epm = jnp.where(in_grp, ep, -1.0)
    p1 = jnp.max(epm, axis=1, keepdims=True)
    i1 = jnp.min(jnp.where(epm == p1, lane, big), axis=1, keepdims=True)
    epm2 = jnp.where(lane == i1, -1.0, epm)
    p2 = jnp.max(epm2, axis=1, keepdims=True)
    i2 = jnp.min(jnp.where(epm2 == p2, lane, big), axis=1, keepdims=True)
    psum = p1 + p2
    w1 = g_w * p1 / psum
    w2 = g_w * p2 / psum
    comb_ref[...] = (jnp.where(lane + N_GROUPS == i1, w1, 0.0)
                     + jnp.where(lane + N_GROUPS == i2, w2, 0.0))


def _merge_route(ya, yb, ga, gb, x2, wa, wb, wo, g1, b1, wr, br, tm):
    wr_hi = wr.astype(jnp.bfloat16)
    wr_lo = (wr - wr_hi.astype(jnp.float32)).astype(jnp.bfloat16)
    n = x2.shape[0]
    row = lambda w: pl.BlockSpec((tm, w), lambda i: (i, 0))
    full = lambda r, c: pl.BlockSpec((r, c), lambda i: (0, 0))
    aw, bw = A_HEADS * A_HEAD_DIM, B_HEADS * B_HEAD_DIM
    return pl.pallas_call(
        _merge_kernel,
        grid=(n // tm,),
        in_specs=[row(aw), row(bw), row(D_MODEL), row(D_MODEL), row(D_MODEL),
                  full(aw, D_MODEL), full(bw, D_MODEL), full(D_MODEL, D_MODEL),
                  full(1, D_MODEL), full(1, D_MODEL), full(D_MODEL, LANES), full(D_MODEL, LANES),
                  full(1, LANES)],
        out_specs=(row(D_MODEL), row(LANES)),
        out_shape=(jax.ShapeDtypeStruct((n, D_MODEL), jnp.float32),
                   jax.ShapeDtypeStruct((n, LANES), jnp.float32)),
        compiler_params=_cparams("arbitrary"),
        name="merge_route",
    )(ya, yb, ga, gb, x2, wa, wb, wo, g1, b1, wr_hi, wr_lo, br)


def _moe_kernel(x1_ref, comb_ref, wg_ref, wu_ref, wd_ref, g2_ref, b2_ref, out_ref, xb_ref, acc_ref):
    e = pl.program_id(1)
    f32 = jnp.float32

    @pl.when(e == 0)
    def _():
        xb_ref[...] = x1_ref[...].astype(jnp.bfloat16)
        acc_ref[...] = jnp.zeros_like(acc_ref)

    xb = xb_ref[...]
    g = jnp.dot(xb, wg_ref[0], preferred_element_type=f32)
    u = jnp.dot(xb, wu_ref[0], preferred_element_type=f32)
    hmid = (g * jax.nn.sigmoid(g) * u).astype(jnp.bfloat16)
    y = jnp.dot(hmid, wd_ref[0], preferred_element_type=f32)
    comb = comb_ref[...]
    lane = lax.broadcasted_iota(jnp.int32, comb.shape, 1)
    c_e = jnp.sum(jnp.where(lane == e, comb, 0.0), axis=1, keepdims=True)
    acc_ref[...] += c_e * y

    @pl.when(e == N_EXPERTS - 1)
    def _():
        out_ref[...] = _layer_norm(ALPHA * x1_ref[...] + acc_ref[...], g2_ref[...], b2_ref[...])


def _moe_dense(x1, comb, wg, wu, wd, g2, b2, tm):
    n = x1.shape[0]
    return pl.pallas_call(
        _moe_kernel,
        grid=(n // tm, N_EXPERTS),
        in_specs=[
            pl.BlockSpec((tm, D_MODEL), lambda i, e: (i, 0)),
            pl.BlockSpec((tm, LANES), lambda i, e: (i, 0)),
            pl.BlockSpec((1, D_MODEL, D_EXPERT), lambda i, e: (e, 0, 0)),
            pl.BlockSpec((1, D_MODEL, D_EXPERT), lambda i, e: (e, 0, 0)),
            pl.BlockSpec((1, D_EXPERT, D_MODEL), lambda i, e: (e, 0, 0)),
            pl.BlockSpec((1, D_MODEL), lambda i, e: (0, 0)),
            pl.BlockSpec((1, D_MODEL), lambda i, e: (0, 0)),
        ],
        out_specs=pl.BlockSpec((tm, D_MODEL), lambda i, e: (i, 0)),
        out_shape=jax.ShapeDtypeStruct((n, D_MODEL), jnp.float32),
        scratch_shapes=[pltpu.VMEM((tm, D_MODEL), jnp.bfloat16),
                        pltpu.VMEM((tm, D_MODEL), jnp.float32)],
        compiler_params=_cparams("arbitrary", "arbitrary"),
        name="moe_dense",
    )(x1, comb, wg, wu, wd, g2, b2)


def _pick_tile(n, pref):
    t = min(pref, n)
    while n % t:
        t //= 2
    return t


def kernel(x, w_in, kv_norm_g, w_uv, w_branch_a, sinks, w_branch_b, w_out, rel_bias, ln1_g, ln1_b,
           w_group, b_group, w_router, b_router, w_gate, w_up, w_down, ln2_g, ln2_b):
    bsz, seq, d = x.shape
    assert d == D_MODEL and seq % BLOCK == 0 and w_in.shape[0] == DEPTH == 1
    n = bsz * seq
    n_blk = seq // BLOCK
    bf, f32 = jnp.bfloat16, jnp.float32
    x2 = x.reshape(n, d)

    r = jnp.arange(BLOCK, dtype=jnp.int32)
    d_far = jnp.full((BLOCK, BLOCK), 2 * BLOCK, jnp.int32)
    d_prev = r[None, :] + BLOCK - r[:, None]
    d_cur = r[None, :] - r[:, None]
    bkt_a = _t5_bucket_np(jnp.stack([d_far, d_prev, d_cur]))
    s2 = jnp.arange(2 * BLOCK, dtype=jnp.int32)
    bkt_b = _t5_bucket_np(r[:, None] + BLOCK - s2[None, :])[None]
    tab_t = rel_bias.astype(f32).T
    bias_a = _bias_tiles(tab_t, bkt_a, 0, A_HEADS, True)
    bias_b = _bias_tiles(tab_t, bkt_b, A_HEADS, B_HEADS, False)

    w0 = w_in[0]
    w_pack = jnp.concatenate(
        [w0[:, :_RAW_KW_END], jnp.zeros((d, _C_QB - _RAW_KW_END), w0.dtype), w0[:, _RAW_KW_END:]],
        axis=1).astype(bf)
    (q_blocks, ckv, qidx, kidx, widx, qb, kb, vb, ga, gb) = _input_projection(
        x2, w_pack, kv_norm_g[0].reshape(1, A_LAT).astype(f32), _pick_tile(n, 512))

    ckv3 = ckv.reshape(bsz, seq, A_LAT)
    ckv_t = jnp.swapaxes(ckv3, 1, 2)
    widx_t = jnp.swapaxes(widx.reshape(bsz * n_blk, BLOCK, IDX_HEADS), 1, 2)
    ya = _dsa_attention(qidx, kidx.reshape(bsz, seq, IDX_DIM), widx_t,
                        q_blocks.reshape(bsz * n_blk, A_HEADS * BLOCK, A_LAT),
                        ckv3, ckv_t, bias_a, w_uv[0].astype(bf), bsz, seq)

    yb = _swa_attention(sinks[0].astype(f32), qb, kb, vb, bias_b, bsz, seq)

    w_route = jnp.concatenate(
        [w_group[0], w_router[0], jnp.zeros((d, LANES - N_GROUPS - N_EXPERTS), f32)], axis=1).astype(f32)
    b_route = jnp.concatenate(
        [b_group[0], b_router[0], jnp.zeros((LANES - N_GROUPS - N_EXPERTS,), f32)]).reshape(1, LANES).astype(f32)
    x1, comb = _merge_route(
        ya, yb, ga, gb, x2, w_branch_a[0].astype(bf), w_branch_b[0].astype(bf), w_out[0].astype(bf),
        ln1_g[0].reshape(1, d).astype(f32), ln1_b[0].reshape(1, d).astype(f32), w_route, b_route,
        _pick_tile(n, 512))

    out = _moe_dense(x1, comb, w_gate[0].astype(bf), w_up[0].astype(bf), w_down[0].astype(bf),
                     ln2_g[0].reshape(1, d).astype(f32), ln2_b[0].reshape(1, d).astype(f32),
                     _pick_tile(n, 1024))
    return out.reshape(bsz, seq, d)
```

```python
import functools
import math

import numpy as np
import jax
import jax.numpy as jnp
from jax import lax
from jax.experimental import pallas as pl
from jax.experimental.pallas import tpu as pltpu

D_MODEL = 1024
A_HEADS = 8
A_LAT = 128
A_HEAD_DIM = 64
IDX_HEADS = 8
IDX_DIM = 64
TOPK_MAX = 256
B_HEADS = 8
B_KV_HEADS = 2
B_HEAD_DIM = 64
WINDOW = 128
BLOCK = 128
N_BUCKETS = 32
MAX_DISTANCE = 128
N_GROUPS = 4
EXPERTS_PER_GROUP = 8
N_EXPERTS = 32
D_EXPERT = 256
DEPTH = 1
ALPHA = (2 * DEPTH) ** 0.25
LN_EPS = 1e-5
RMS_EPS = 1e-6

LANES = 128
INT_MIN = -(2 ** 31)
VMEM_LIMIT = 48 * 1024 * 1024

_NT = (((1,), (1,)), ((), ()))


def _cparams(*sem):
    return pltpu.CompilerParams(dimension_semantics=sem, vmem_limit_bytes=VMEM_LIMIT)


def _t5_bucket_np(dist):
    f32 = np.float32
    n = np.maximum(dist, 0)
    max_exact = N_BUCKETS // 2
    nf = np.maximum(n, 1).astype(f32)
    large = max_exact + (np.log(nf / f32(max_exact)) / f32(math.log(MAX_DISTANCE / max_exact))
                         * f32(N_BUCKETS - max_exact)).astype(np.int32)
    large = np.minimum(large, N_BUCKETS - 1)
    return np.where(n < max_exact, n, large).astype(np.int32)


def _bias_kernel(tab_ref, bkt_ref, out_ref, *, head0, minus_far):
    h = pl.program_id(0) + head0
    bkt = bkt_ref[...]
    far = tab_ref[h, N_BUCKETS - 1]
    acc = jnp.full(bkt.shape, far, jnp.float32)
    for b in range(N_BUCKETS - 1):
        acc = jnp.where(bkt == b, tab_ref[h, b], acc)
    out_ref[...] = acc - far if minus_far else acc


def _bias_tiles(tab_t, bkt, head0, n_heads, lane_major, minus_far=False):
    g, r, c = bkt.shape
    if lane_major:
        out_shape = jax.ShapeDtypeStruct((g, r, n_heads * c), jnp.float32)
        out_spec = pl.BlockSpec((g, r, c), lambda h: (0, 0, h))
    else:
        out_shape = jax.ShapeDtypeStruct((n_heads * g, r, c), jnp.float32)
        out_spec = pl.BlockSpec((g, r, c), lambda h: (h, 0, 0))
    return pl.pallas_call(
        functools.partial(_bias_kernel, head0=head0, minus_far=minus_far),
        grid=(n_heads,),
        in_specs=[pl.BlockSpec(memory_space=pltpu.SMEM),
                  pl.BlockSpec((g, r, c), lambda h: (0, 0, 0))],
        out_specs=out_spec,
        out_shape=out_shape,
        compiler_params=_cparams("arbitrary"),
        name="bias_tiles",
    )(tab_t, bkt)


_C_QLAT = 0
_C_CKV = _C_QLAT + A_HEADS * A_LAT
_C_QIDX = _C_CKV + A_LAT
_C_KW = _C_QIDX + IDX_HEADS * IDX_DIM
_C_QB = _C_KW + LANES
_C_KB = _C_QB + B_HEADS * B_HEAD_DIM
_C_VB = _C_KB + B_KV_HEADS * B_HEAD_DIM
_C_GA = _C_VB + B_KV_HEADS * B_HEAD_DIM
_C_GB = _C_GA + D_MODEL
_C_END = _C_GB + D_MODEL
_RAW_KW_END = _C_KW + IDX_DIM + IDX_HEADS


def _proj_kernel(x_ref, w_ref, g_ref, q_ref, ckv_ref, qidx_ref, kidx_ref, widx_ref,
                 qb_ref, kb_ref, vb_ref, ga_ref, gb_ref):
    xb = x_ref[...].astype(jnp.bfloat16)
    tm = xb.shape[0]

    def seg(lo, hi):
        return jnp.dot(xb, w_ref[:, lo:hi], preferred_element_type=jnp.float32)

    for h in range(A_HEADS):
        qh = (seg(_C_QLAT + h * A_LAT, _C_QLAT + (h + 1) * A_LAT) * (A_LAT ** -0.5)).astype(jnp.bfloat16)
        for r in range(tm // BLOCK):
            q_ref[r, h] = qh[r * BLOCK:(r + 1) * BLOCK]
    c = seg(_C_CKV, _C_QIDX)
    ms = jnp.mean(c * c, axis=-1, keepdims=True)
    ckv_ref[...] = (c * lax.rsqrt(ms + RMS_EPS) * g_ref[...]).astype(jnp.bfloat16)
    qif = seg(_C_QIDX, _C_KW).astype(jnp.bfloat16)
    for h in range(IDX_HEADS):
        for r in range(tm // BLOCK):
            qidx_ref[r, h] = qif[r * BLOCK:(r + 1) * BLOCK, h * IDX_DIM:(h + 1) * IDX_DIM]
    kw = seg(_C_KW, _C_QB)
    kidx_ref[...] = kw[:, :IDX_DIM].astype(jnp.bfloat16)
    widx_ref[...] = kw[:, IDX_DIM:IDX_DIM + IDX_HEADS]
    qbf = seg(_C_QB, _C_KB).astype(jnp.bfloat16)
    for h in range(B_HEADS):
        for r in range(tm // BLOCK):
            qb_ref[r, h] = qbf[r * BLOCK:(r + 1) * BLOCK, h * B_HEAD_DIM:(h + 1) * B_HEAD_DIM]
    kbf = seg(_C_KB, _C_VB).astype(jnp.bfloat16)
    vbf = seg(_C_VB, _C_GA).astype(jnp.bfloat16)
    for g in range(B_KV_HEADS):
        kb_ref[g] = kbf[:, g * B_HEAD_DIM:(g + 1) * B_HEAD_DIM]
        vb_ref[g, :, :B_HEAD_DIM] = vbf[:, g * B_HEAD_DIM:(g + 1) * B_HEAD_DIM]
        vb_ref[g, :, B_HEAD_DIM:] = jnp.ones((tm, B_HEAD_DIM), jnp.bfloat16)
    ga_ref[...] = jax.nn.sigmoid(seg(_C_GA, _C_GB))
    gb_ref[...] = jax.nn.sigmoid(seg(_C_GB, _C_END))


def _input_projection(x2, w_pack, kv_g, tm):
    n = x2.shape[0]
    bf, f32 = jnp.bfloat16, jnp.float32
    row = lambda w: pl.BlockSpec((tm, w), lambda i: (i, 0))
    out_shape = (
        jax.ShapeDtypeStruct((n // BLOCK, A_HEADS, BLOCK, A_LAT), bf),
        jax.ShapeDtypeStruct((n, A_LAT), bf),
        jax.ShapeDtypeStruct((n // BLOCK, IDX_HEADS, BLOCK, IDX_DIM), bf),
        jax.ShapeDtypeStruct((n, IDX_DIM), bf),
        jax.ShapeDtypeStruct((n, IDX_HEADS), f32),
        jax.ShapeDtypeStruct((n // BLOCK, B_HEADS, BLOCK, B_HEAD_DIM), bf),
        jax.ShapeDtypeStruct((B_KV_HEADS, n, B_HEAD_DIM), bf),
        jax.ShapeDtypeStruct((B_KV_HEADS, n, 2 * B_HEAD_DIM), bf),
        jax.ShapeDtypeStruct((n, D_MODEL), f32),
        jax.ShapeDtypeStruct((n, D_MODEL), f32),
    )
    out_specs = (
        pl.BlockSpec((tm // BLOCK, A_HEADS, BLOCK, A_LAT), lambda i: (i, 0, 0, 0)),
        row(A_LAT), pl.BlockSpec((tm // BLOCK, IDX_HEADS, BLOCK, IDX_DIM), lambda i: (i, 0, 0, 0)),
        row(IDX_DIM), row(IDX_HEADS),
        pl.BlockSpec((tm // BLOCK, B_HEADS, BLOCK, B_HEAD_DIM), lambda i: (i, 0, 0, 0)),
        pl.BlockSpec((B_KV_HEADS, tm, B_HEAD_DIM), lambda i: (0, i, 0)),
        pl.BlockSpec((B_KV_HEADS, tm, 2 * B_HEAD_DIM), lambda i: (0, i, 0)),
        row(D_MODEL), row(D_MODEL),
    )
    return pl.pallas_call(
        _proj_kernel,
        grid=(n // tm,),
        in_specs=[row(D_MODEL),
                  pl.BlockSpec((D_MODEL, _C_END), lambda i: (0, 0)),
                  pl.BlockSpec((1, A_LAT), lambda i: (0, 0))],
        out_specs=out_specs,
        out_shape=out_shape,
        compiler_params=_cparams("arbitrary"),
        name="input_projection",
    )(x2, w_pack, kv_g)


def _dsa_kernel(qidx_ref, kidx_ref, widx_ref, q_ref, ckv_ref, ckvt_ref, bias_ref, wuv_ref,
                tri_ref, ya_ref, sc_ref, neg_ref, lg_ref, o_ref, *, k_sel, search_chunk):
    i = pl.program_id(1)
    f32 = jnp.float32
    s_loc = lax.broadcasted_iota(jnp.int32, (BLOCK, BLOCK), 0)
    t_loc = lax.broadcasted_iota(jnp.int32, (BLOCK, BLOCK), 1)
    causal_diag = s_loc <= t_loc
    idx_scale = IDX_DIM ** -0.5
    w_scale = IDX_HEADS ** -0.5

    def rows(j):
        return pl.ds(pl.multiple_of(j * BLOCK, BLOCK), BLOCK)

    sc_ref[...] = jnp.full(sc_ref.shape, -jnp.inf, f32)
    wts = (widx_ref[0] * w_scale) * idx_scale
    qi = qidx_ref[0]

    def score_keys(j):
        s = lax.dot_general(kidx_ref[0, rows(j), :], qi, _NT, preferred_element_type=f32)
        r = jnp.maximum(s, 0.0) * wts
        acc = r[:, :BLOCK]
        for h in range(1, IDX_HEADS):
            acc = acc + r[:, h * BLOCK:(h + 1) * BLOCK]
        return acc

    def fold_blocks(lo, hi, fn, init, combine):
        n = hi - lo

        def pair(k, acc):
            j = lo + 2 * k
            return combine(acc, combine(fn(j), fn(j + 1)))

        acc = lax.fori_loop(0, lax.shift_right_logical(n, 1), pair, init)
        return lax.cond((n & 1) == 1, lambda: combine(acc, fn(hi - 1)), lambda: acc)

    def far_scores(j):
        sc_ref[rows(j), :] = score_keys(j)
        return jnp.int32(0)

    fold_blocks(0, i, far_scores, jnp.int32(0), lambda a, b: a)
    sc_ref[rows(i), :] = jnp.where(causal_diag, score_keys(i), -jnp.inf)

    n_chunks = ((i + 1) * BLOCK + search_chunk - 1) // search_chunk
    n_acc = 64

    def count(cmp, cand):
        def body(c, cnt):
            base = pl.multiple_of(c * search_chunk, search_chunk)
            for r in range(search_chunk // n_acc):
                cnt = cnt + cmp(sc_ref[pl.ds(base + r * n_acc, n_acc), :], cand).astype(f32)
            return cnt
        cnt = lax.fori_loop(0, n_chunks, body, jnp.zeros((n_acc, BLOCK), f32))
        return jnp.sum(cnt, axis=0, keepdims=True)

    def ordered_to_float(u):
        k = u ^ INT_MIN
        return lax.bitcast_convert_type(k ^ ((k >> 31) & 0x7FFFFFFF), f32)

    def search_bit(b, prefix_u):
        cand_u = prefix_u | lax.shift_left(jnp.int32(1), 31 - b)
        cnt = count(jnp.greater_equal, ordered_to_float(cand_u))
        return jnp.where(cnt >= k_sel, cand_u, prefix_u)

    thr_u = lax.fori_loop(0, 32, search_bit, jnp.zeros((1, BLOCK), jnp.int32))
    thr = ordered_to_float(thr_u)
    thr = jnp.where(thr != thr, -jnp.inf, thr)
    c_ge = count(jnp.greater_equal, thr)
    c_gt = count(jnp.greater, thr)
    n_tie = k_sel - c_gt
    any_tie = jnp.max(((c_ge > k_sel) & (thr > -jnp.inf)).astype(f32)) > 0.0

    @pl.when(jnp.logical_not(any_tie))
    def _():
        def body(j, carry):
            neg_ref[rows(j), :] = jnp.where(sc_ref[rows(j), :] >= thr, 0.0, -jnp.inf)
            return carry
        lax.fori_loop(0, i, body, 0)
        neg_ref[rows(i), :] = jnp.where((sc_ref[rows(i), :] >= thr) & causal_diag, 0.0, -jnp.inf)

    @pl.when(any_tie)
    def _():
        tri = tri_ref[...]

        def block(j, tie_seen):
            kc = sc_ref[rows(j), :]
            eq = kc == thr
            tie_rank = jnp.dot(tri, eq.astype(jnp.bfloat16), preferred_element_type=f32) + tie_seen
            sel = (kc > thr) | (eq & (tie_rank <= n_tie))
            return sel, tie_seen + jnp.sum(eq.astype(f32), axis=0, keepdims=True)

        def body(j, tie_seen):
            sel, tie_seen = block(j, tie_seen)
            neg_ref[rows(j), :] = jnp.where(sel, 0.0, -jnp.inf)
            return tie_seen

        tie_seen = lax.fori_loop(0, i, body, jnp.zeros((1, BLOCK), f32))
        sel, _ = block(i, tie_seen)
        neg_ref[rows(i), :] = jnp.where(sel & causal_diag, 0.0, -jnp.inf)

    q_all = q_ref[0]

    def logits(j):
        lg = lax.dot_general(ckv_ref[0, rows(j), :], q_all, _NT, preferred_element_type=f32)
        return lg + jnp.concatenate([neg_ref[rows(j), :]] * A_HEADS, axis=1)

    def far_logits(j):
        lg = logits(j)
        lg_ref[rows(j), :] = lg
        return jnp.max(lg, axis=0, keepdims=True)

    def near_logits(j):
        lg = logits(j) + bias_ref[j - i + 1]
        lg_ref[rows(j), :] = lg
        return jnp.max(lg, axis=0, keepdims=True)

    n_far = jnp.maximum(i - 1, 0)
    m_run = fold_blocks(0, n_far, far_logits, jnp.full((1, A_HEADS * BLOCK), -jnp.inf, f32), jnp.maximum)
    m_run = fold_blocks(n_far, i + 1, near_logits, m_run, jnp.maximum)

    o_ref[...] = jnp.zeros_like(o_ref)

    def pv_rows(r):
        p = jnp.exp(lg_ref[r, :] - m_run).astype(jnp.bfloat16)
        o_ref[...] += jnp.dot(ckvt_ref[0, :, r], p, preferred_element_type=f32)

    def pv_pair(k, carry):
        pv_rows(pl.ds(pl.multiple_of(k * (2 * BLOCK), 2 * BLOCK), 2 * BLOCK))
        return carry

    lax.fori_loop(0, lax.shift_right_logical(i + 1, 1), pv_pair, 0)

    @pl.when((i & 1) == 0)
    def _():
        pv_rows(rows(i))

    o_t = o_ref[:A_LAT, :] / o_ref[A_LAT:A_LAT + 1, :]

    for h in range(A_HEADS):
        o_h = o_t[:, h * BLOCK:(h + 1) * BLOCK].T.astype(jnp.bfloat16)
        y_h = jnp.dot(o_h, wuv_ref[h], preferred_element_type=f32)
        ya_ref[:, h * A_HEAD_DIM:(h + 1) * A_HEAD_DIM] = y_h.astype(ya_ref.dtype)


def _dsa_attention(qidx_blocks, kidx, widx_l, q_blocks, ckv, ckv_tx, bias_near, wuv, bsz, seq):
    n_blk = seq // BLOCK
    k_sel = min(TOPK_MAX, seq // 4)
    search_chunk = min(512, seq)
    tri = jnp.asarray(np.tril(np.ones((BLOCK, BLOCK), np.float32)), jnp.bfloat16)
    kern = functools.partial(_dsa_kernel, k_sel=float(k_sel), search_chunk=search_chunk)
    blk = lambda b, i: (b * n_blk + i, 0, 0)
    return pl.pallas_call(
        kern,
        grid=(bsz, n_blk),
        in_specs=[
            pl.BlockSpec((1, IDX_HEADS * BLOCK, IDX_DIM), blk),
            pl.BlockSpec((1, seq, IDX_DIM), lambda b, i: (b, 0, 0)),
            pl.BlockSpec((1, 1, IDX_HEADS * BLOCK), blk),
            pl.BlockSpec((1, A_HEADS * BLOCK, A_LAT), blk),
            pl.BlockSpec((1, seq, A_LAT), lambda b, i: (b, 0, 0)),
            pl.BlockSpec((1, A_LAT + 8, seq), lambda b, i: (b, 0, 0)),
            pl.BlockSpec((2, BLOCK, A_HEADS * BLOCK), lambda b, i: (0, 0, 0)),
            pl.BlockSpec((A_HEADS, A_LAT, A_HEAD_DIM), lambda b, i: (0, 0, 0)),
            pl.BlockSpec((BLOCK, BLOCK), lambda b, i: (0, 0)),
        ],
        out_specs=pl.BlockSpec((BLOCK, A_HEADS * A_HEAD_DIM), lambda b, i: (b * n_blk + i, 0)),
        out_shape=jax.ShapeDtypeStruct((bsz * seq, A_HEADS * A_HEAD_DIM), jnp.bfloat16),
        scratch_shapes=[pltpu.VMEM((seq, BLOCK), jnp.float32),
                        pltpu.VMEM((seq, BLOCK), jnp.float32),
                        pltpu.VMEM((seq, A_HEADS * BLOCK), jnp.float32),
                        pltpu.VMEM((A_LAT + 8, A_HEADS * BLOCK), jnp.float32)],
        compiler_params=_cparams("arbitrary", "arbitrary"),
        name="dsa_attention",
    )(qidx_blocks, kidx, widx_l, q_blocks, ckv, ckv_tx, bias_near, wuv, tri)


SWA_QB = 4


def _swa_kernel(sink_ref, q_ref, kp_ref, kc_ref, vp_ref, vc_ref, bias_ref, yb_ref, *, qb):
    i = pl.program_id(1)
    f32 = jnp.float32
    grp = B_HEADS // B_KV_HEADS
    rows = grp * BLOCK
    row = lax.broadcasted_iota(jnp.int32, (rows, BLOCK), 0)
    t_loc = row & (BLOCK - 1)
    s_loc = lax.broadcasted_iota(jnp.int32, (rows, BLOCK), 1)
    d_prev = t_loc + BLOCK - s_loc
    d_cur = t_loc - s_loc
    m_cur = (d_cur >= 0) & (d_cur < WINDOW)
    head_of_row = lax.broadcasted_iota(jnp.int32, (rows, 1), 0) >> (BLOCK.bit_length() - 1)
    scale = B_HEAD_DIM ** -0.5
    for g in range(B_KV_HEADS):
        sink = jnp.zeros((rows, 1), f32)
        for hh in range(grp):
            sink = jnp.where(head_of_row == hh, sink_ref[g * grp + hh], sink)
        bias = bias_ref[g * grp:(g + 1) * grp].reshape(rows, 2 * BLOCK)
        for r in range(qb):
            has_prev = jnp.zeros_like(d_prev) + jnp.minimum(i * qb + r, 1)
            m_prev = (d_prev >= 0) & (d_prev < WINDOW) & (has_prev > 0)
            q = q_ref[r, g * grp:(g + 1) * grp].reshape(rows, B_HEAD_DIM)
            if r == 0:
                kp, vp = kp_ref[g], vp_ref[g]
            else:
                kp, vp = kc_ref[g, (r - 1) * BLOCK:r * BLOCK], vc_ref[g, (r - 1) * BLOCK:r * BLOCK]
            kc, vc = kc_ref[g, r * BLOCK:(r + 1) * BLOCK], vc_ref[g, r * BLOCK:(r + 1) * BLOCK]
            lp = lax.dot_general(q, kp, _NT, preferred_element_type=f32) * scale + bias[:, :BLOCK]
            lc = lax.dot_general(q, kc, _NT, preferred_element_type=f32) * scale + bias[:, BLOCK:]
            lp = jnp.where(m_prev, lp, -jnp.inf)
            lc = jnp.where(m_cur, lc, -jnp.inf)
            m = jnp.maximum(jnp.max(jnp.maximum(lp, lc), axis=1, keepdims=True), sink)
            pp = jnp.exp(lp - m).astype(jnp.bfloat16)
            pc = jnp.exp(lc - m).astype(jnp.bfloat16)
            ox = (jnp.dot(pp, vp, preferred_element_type=f32)
                  + jnp.dot(pc, vc, preferred_element_type=f32))
            den = ox[:, B_HEAD_DIM:B_HEAD_DIM + 1] + jnp.exp(sink - m)
            o = (ox[:, :B_HEAD_DIM] / den).astype(yb_ref.dtype)
            for hh in range(grp):
                h = g * grp + hh
                yb_ref[r * BLOCK:(r + 1) * BLOCK, h * B_HEAD_DIM:(h + 1) * B_HEAD_DIM] = (
                    o[hh * BLOCK:(hh + 1) * BLOCK])


def _swa_attention(sinks, q_blocks, kb, vbx, bias_b, bsz, seq):
    n_blk = seq // BLOCK
    qb = SWA_QB if n_blk % SWA_QB == 0 else 1
    n_step = n_blk // qb
    cur = lambda b, i: (0, b * n_step + i, 0)
    prev = lambda b, i: (0, b * n_blk + jnp.maximum(i * qb - 1, 0), 0)
    return pl.pallas_call(
        functools.partial(_swa_kernel, qb=qb),
        grid=(bsz, n_step),
        in_specs=[
            pl.BlockSpec(memory_space=pltpu.SMEM),
            pl.BlockSpec((qb, B_HEADS, BLOCK, B_HEAD_DIM), lambda b, i: (b * n_step + i, 0, 0, 0)),
            pl.BlockSpec((B_KV_HEADS, BLOCK, B_HEAD_DIM), prev),
            pl.BlockSpec((B_KV_HEADS, qb * BLOCK, B_HEAD_DIM), cur),
            pl.BlockSpec((B_KV_HEADS, BLOCK, 2 * B_HEAD_DIM), prev),
            pl.BlockSpec((B_KV_HEADS, qb * BLOCK, 2 * B_HEAD_DIM), cur),
            pl.BlockSpec((B_HEADS, BLOCK, 2 * BLOCK), lambda b, i: (0, 0, 0)),
        ],
        out_specs=pl.BlockSpec((qb * BLOCK, B_HEADS * B_HEAD_DIM), lambda b, i: (b * n_step + i, 0)),
        out_shape=jax.ShapeDtypeStruct((bsz * seq, B_HEADS * B_HEAD_DIM), jnp.bfloat16),
        compiler_params=_cparams("arbitrary", "arbitrary"),
        name="swa_attention",
    )(sinks, q_blocks, kb, kb, vbx, vbx, bias_b)


def _layer_norm(h, g, b):
    mu = jnp.mean(h, axis=-1, keepdims=True)
    d = h - mu
    var = jnp.mean(d * d, axis=-1, keepdims=True)
    return d * lax.rsqrt(var + LN_EPS) * g + b


def _merge_kernel(ya_ref, yb_ref, ga_ref, gb_ref, x_ref, wa_ref, wb_ref, wo_ref, g1_ref, b1_ref,
                  wrh_ref, wrl_ref, br_ref, x1_ref, comb_ref):
    f32 = jnp.float32
    pa = jnp.dot(ya_ref[...], wa_ref[...], preferred_element_type=f32)
    pb = jnp.dot(yb_ref[...], wb_ref[...], preferred_element_type=f32)
    merged = ga_ref[...] * pa + gb_ref[...] * pb
    h = ALPHA * x_ref[...] + jnp.dot(merged.astype(jnp.bfloat16), wo_ref[...],
                                     preferred_element_type=f32)
    x1 = _layer_norm(h, g1_ref[...], b1_ref[...])
    x1_ref[...] = x1

    x_hi = x1.astype(jnp.bfloat16)
    x_lo = (x1 - x_hi.astype(f32)).astype(jnp.bfloat16)
    lg = (jnp.dot(x_hi, wrh_ref[...], preferred_element_type=f32)
          + jnp.dot(x_hi, wrl_ref[...], preferred_element_type=f32)
          + jnp.dot(x_lo, wrh_ref[...], preferred_element_type=f32)) + br_ref[...]
    lane_i = lax.broadcasted_iota(jnp.int32, lg.shape, 1)
    lane = lane_i.astype(f32)
    big = jnp.float32(1 << 20)
    is_g = lane_i < N_GROUPS
    gl = jnp.where(is_g, lg, -jnp.inf)
    gmax = jnp.max(gl, axis=1, keepdims=True)
    g_sel = jnp.min(jnp.where(gl == gmax, lane, big), axis=1, keepdims=True)
    g_w = 1.0 / jnp.sum(jnp.where(is_g, jnp.exp(gl - gmax), 0.0), axis=1, keepdims=True)
    e_id = lane_i - N_GROUPS
    e_grp = (e_id >> 3).astype(f32)
    in_grp = (e_id >= 0) & (e_id < N_EXPERTS) & (e_grp == g_sel)
    el = jnp.where(in_grp, lg, -jnp.inf)
    emax = jnp.max(el, axis=1, keepdims=True)
    ee = jnp.where(in_grp, jnp.exp(el - emax), 0.0)
    ep = ee / jnp.sum(ee, axis=1, keepdims=True)
    epm = jnp.where(in_grp, ep, -1.0)
    p1 = jnp.max(epm, axis=1, keepdims=True)
    i1 = jnp.min(jnp.where(epm == p1, lane, big), axis=1, keepdims=True)
    epm2 = jnp.where(lane == i1, -1.0, epm)
    p2 = jnp.max(epm2, axis=1, keepdims=True)
    i2 = jnp.min(jnp.where(epm2 == p2, lane, big), axis=1, keepdims=True)
    psum = p1 + p2
    w1 = g_w * p1 / psum
    w2 = g_w * p2 / psum
    comb_ref[...] = (jnp.where(lane + N_GROUPS == i1, w1, 0.0)
                     + jnp.where(lane + N_GROUPS == i2, w2, 0.0))


def _merge_route(ya, yb, ga, gb, x2, wa, wb, wo, g1, b1, wr, br, tm):
    wr_hi = wr.astype(jnp.bfloat16)
    wr_lo = (wr - wr_hi.astype(jnp.float32)).astype(jnp.bfloat16)
    n = x2.shape[0]
    row = lambda w: pl.BlockSpec((tm, w), lambda i: (i, 0))
    full = lambda r, c: pl.BlockSpec((r, c), lambda i: (0, 0))
    aw, bw = A_HEADS * A_HEAD_DIM, B_HEADS * B_HEAD_DIM
    return pl.pallas_call(
        _merge_kernel,
        grid=(n // tm,),
        in_specs=[row(aw), row(bw), row(D_MODEL), row(D_MODEL), row(D_MODEL),
                  full(aw, D_MODEL), full(bw, D_MODEL), full(D_MODEL, D_MODEL),
                  full(1, D_MODEL), full(1, D_MODEL), full(D_MODEL, LANES), full(D_MODEL, LANES),
                  full(1, LANES)],
        out_specs=(row(D_MODEL), row(LANES)),
        out_shape=(jax.ShapeDtypeStruct((n, D_MODEL), jnp.float32),
                   jax.ShapeDtypeStruct((n, LANES), jnp.float32)),
        compiler_params=_cparams("arbitrary"),
        name="merge_route",
    )(ya, yb, ga, gb, x2, wa, wb, wo, g1, b1, wr_hi, wr_lo, br)


def _moe_kernel(x1_ref, comb_ref, wg_ref, wu_ref, wd_ref, g2_ref, b2_ref, out_ref, xb_ref, acc_ref):
    e = pl.program_id(1)
    f32 = jnp.float32

    @pl.when(e == 0)
    def _():
        xb_ref[...] = x1_ref[...].astype(jnp.bfloat16)
        acc_ref[...] = jnp.zeros_like(acc_ref)

    xb = xb_ref[...]
    g = jnp.dot(xb, wg_ref[0], preferred_element_type=f32)
    u = jnp.dot(xb, wu_ref[0], preferred_element_type=f32)
    hmid = (g * jax.nn.sigmoid(g) * u).astype(jnp.bfloat16)
    y = jnp.dot(hmid, wd_ref[0], preferred_element_type=f32)
    comb = comb_ref[...]
    lane = lax.broadcasted_iota(jnp.int32, comb.shape, 1)
    c_e = jnp.sum(jnp.where(lane == e, comb, 0.0), axis=1, keepdims=True)
    acc_ref[...] += c_e * y

    @pl.when(e == N_EXPERTS - 1)
    def _():
        out_ref[...] = _layer_norm(ALPHA * x1_ref[...] + acc_ref[...], g2_ref[...], b2_ref[...])


def _moe_dense(x1, comb, wg, wu, wd, g2, b2, tm):
    n = x1.shape[0]
    return pl.pallas_call(
        _moe_kernel,
        grid=(n // tm, N_EXPERTS),
        in_specs=[
            pl.BlockSpec((tm, D_MODEL), lambda i, e: (i, 0)),
            pl.BlockSpec((tm, LANES), lambda i, e: (i, 0)),
            pl.BlockSpec((1, D_MODEL, D_EXPERT), lambda i, e: (e, 0, 0)),
            pl.BlockSpec((1, D_MODEL, D_EXPERT), lambda i, e: (e, 0, 0)),
            pl.BlockSpec((1, D_EXPERT, D_MODEL), lambda i, e: (e, 0, 0)),
            pl.BlockSpec((1, D_MODEL), lambda i, e: (0, 0)),
            pl.BlockSpec((1, D_MODEL), lambda i, e: (0, 0)),
        ],
        out_specs=pl.BlockSpec((tm, D_MODEL), lambda i, e: (i, 0)),
        out_shape=jax.ShapeDtypeStruct((n, D_MODEL), jnp.float32),
        scratch_shapes=[pltpu.VMEM((tm, D_MODEL), jnp.bfloat16),
                        pltpu.VMEM((tm, D_MODEL), jnp.float32)],
        compiler_params=_cparams("arbitrary", "arbitrary"),
        name="moe_dense",
    )(x1, comb, wg, wu, wd, g2, b2)


def _pick_tile(n, pref):
    t = min(pref, n)
    while n % t:
        t //= 2
    return t


def kernel(x, w_in, kv_norm_g, w_uv, w_branch_a, sinks, w_branch_b, w_out, rel_bias, ln1_g, ln1_b,
           w_group, b_group, w_router, b_router, w_gate, w_up, w_down, ln2_g, ln2_b):
    bsz, seq, d = x.shape
    assert d == D_MODEL and seq % BLOCK == 0 and w_in.shape[0] == DEPTH == 1
    n = bsz * seq
    n_blk = seq // BLOCK
    bf, f32 = jnp.bfloat16, jnp.float32
    x2 = x.reshape(n, d)

    r = np.arange(BLOCK, dtype=np.int32)
    d_prev = r[None, :] + BLOCK - r[:, None]
    d_cur = r[None, :] - r[:, None]
    bkt_a = jnp.asarray(_t5_bucket_np(np.stack([d_prev, d_cur])))
    s2 = np.arange(2 * BLOCK, dtype=np.int32)
    bkt_b = jnp.asarray(_t5_bucket_np(r[:, None] + BLOCK - s2[None, :])[None])
    assert int(_t5_bucket_np(np.arange(BLOCK + 1, max(seq, BLOCK + 2))).min()) == N_BUCKETS - 1
    tab_t = rel_bias.astype(f32).T
    bias_a = _bias_tiles(tab_t, bkt_a, 0, A_HEADS, True, minus_far=True)
    bias_b = _bias_tiles(tab_t, bkt_b, A_HEADS, B_HEADS, False)

    w0 = w_in[0]
    w_pack = jnp.concatenate(
        [w0[:, :_RAW_KW_END], jnp.zeros((d, _C_QB - _RAW_KW_END), w0.dtype), w0[:, _RAW_KW_END:]],
        axis=1).astype(bf)
    (q_blocks, ckv, qidx_blocks, kidx, widx, qb_blocks, kb, vbx, ga, gb) = _input_projection(
        x2, w_pack, kv_norm_g[0].reshape(1, A_LAT).astype(f32), _pick_tile(n, 512))

    ckv3 = ckv.reshape(bsz, seq, A_LAT)
    ckv_tx = jnp.concatenate([jnp.swapaxes(ckv3, 1, 2), jnp.ones((bsz, 8, seq), bf)], axis=1)
    widx_l = jnp.swapaxes(widx.reshape(bsz * n_blk, BLOCK, IDX_HEADS), 1, 2).reshape(
        bsz * n_blk, 1, IDX_HEADS * BLOCK)
    ya = _dsa_attention(qidx_blocks.reshape(bsz * n_blk, IDX_HEADS * BLOCK, IDX_DIM),
                        kidx.reshape(bsz, seq, IDX_DIM), widx_l,
                        q_blocks.reshape(bsz * n_blk, A_HEADS * BLOCK, A_LAT),
                        ckv3, ckv_tx, bias_a, w_uv[0].astype(bf), bsz, seq)

    yb = _swa_attention(sinks[0].astype(f32), qb_blocks, kb, vbx, bias_b, bsz, seq)

    w_route = jnp.concatenate(
        [w_group[0], w_router[0], jnp.zeros((d, LANES - N_GROUPS - N_EXPERTS), f32)], axis=1).astype(f32)
    b_route = jnp.concatenate(
        [b_group[0], b_router[0], jnp.zeros((LANES - N_GROUPS - N_EXPERTS,), f32)]).reshape(1, LANES).astype(f32)
    x1, comb = _merge_route(
        ya, yb, ga, gb, x2, w_branch_a[0].astype(bf), w_branch_b[0].astype(bf), w_out[0].astype(bf),
        ln1_g[0].reshape(1, d).astype(f32), ln1_b[0].reshape(1, d).astype(f32), w_route, b_route,
        _pick_tile(n, 512))

    out = _moe_dense(x1, comb, w_gate[0].astype(bf), w_up[0].astype(bf), w_down[0].astype(bf),
                     ln2_g[0].reshape(1, d).astype(f32), ln2_b[0].reshape(1, d).astype(f32),
                     _pick_tile(n, 1024))
    return out.reshape(bsz, seq, d)
```

```python
import functools
import math

import numpy as np
import jax
import jax.numpy as jnp
from jax import lax
from jax.experimental import pallas as pl
from jax.experimental.pallas import tpu as pltpu

D_MODEL = 1024
A_HEADS = 8
A_LAT = 128
A_HEAD_DIM = 64
IDX_HEADS = 8
IDX_DIM = 64
TOPK_MAX = 256
B_HEADS = 8
B_KV_HEADS = 2
B_HEAD_DIM = 64
WINDOW = 128
BLOCK = 128
N_BUCKETS = 32
MAX_DISTANCE = 128
N_GROUPS = 4
EXPERTS_PER_GROUP = 8
N_EXPERTS = 32
D_EXPERT = 256
DEPTH = 1
ALPHA = (2 * DEPTH) ** 0.25
LN_EPS = 1e-5
RMS_EPS = 1e-6

LANES = 128
INT_MIN = -(2 ** 31)
VMEM_LIMIT = 48 * 1024 * 1024

_NT = (((1,), (1,)), ((), ()))


def _cparams(*sem):
    return pltpu.CompilerParams(dimension_semantics=sem, vmem_limit_bytes=VMEM_LIMIT)


def _t5_bucket_np(dist):
    f32 = np.float32
    n = np.maximum(dist, 0)
    max_exact = N_BUCKETS // 2
    nf = np.maximum(n, 1).astype(f32)
    large = max_exact + (np.log(nf / f32(max_exact)) / f32(math.log(MAX_DISTANCE / max_exact))
                         * f32(N_BUCKETS - max_exact)).astype(np.int32)
    large = np.minimum(large, N_BUCKETS - 1)
    return np.where(n < max_exact, n, large).astype(np.int32)


def _bias_kernel(tab_ref, bkt_ref, out_ref, *, head0, minus_far):
    h = pl.program_id(0) + head0
    bkt = bkt_ref[...]
    far = tab_ref[h, N_BUCKETS - 1]
    acc = jnp.full(bkt.shape, far, jnp.float32)
    for b in range(N_BUCKETS - 1):
        acc = jnp.where(bkt == b, tab_ref[h, b], acc)
    out_ref[...] = acc - far if minus_far else acc


def _bias_tiles(tab_t, bkt, head0, n_heads, lane_major, minus_far=False):
    g, r, c = bkt.shape
    if lane_major:
        out_shape = jax.ShapeDtypeStruct((g, r, n_heads * c), jnp.float32)
        out_spec = pl.BlockSpec((g, r, c), lambda h: (0, 0, h))
    else:
        out_shape = jax.ShapeDtypeStruct((n_heads * g, r, c), jnp.float32)
        out_spec = pl.BlockSpec((g, r, c), lambda h: (h, 0, 0))
    return pl.pallas_call(
        functools.partial(_bias_kernel, head0=head0, minus_far=minus_far),
        grid=(n_heads,),
        in_specs=[pl.BlockSpec(memory_space=pltpu.SMEM),
                  pl.BlockSpec((g, r, c), lambda h: (0, 0, 0))],
        out_specs=out_spec,
        out_shape=out_shape,
        compiler_params=_cparams("arbitrary"),
        name="bias_tiles",
    )(tab_t, bkt)


_C_QLAT = 0
_C_CKV = _C_QLAT + A_HEADS * A_LAT
_C_QIDX = _C_CKV + A_LAT
_C_KW = _C_QIDX + IDX_HEADS * IDX_DIM
_C_QB = _C_KW + LANES
_C_KB = _C_QB + B_HEADS * B_HEAD_DIM
_C_VB = _C_KB + B_KV_HEADS * B_HEAD_DIM
_C_GA = _C_VB + B_KV_HEADS * B_HEAD_DIM
_C_GB = _C_GA + D_MODEL
_C_END = _C_GB + D_MODEL
_RAW_KW_END = _C_KW + IDX_DIM + IDX_HEADS


def _proj_kernel(x_ref, w_ref, g_ref, q_ref, ckv_ref, qidx_ref, kidx_ref, widx_ref,
                 qb_ref, kb_ref, vb_ref, ga_ref, gb_ref):
    xb = x_ref[...].astype(jnp.bfloat16)
    tm = xb.shape[0]

    def seg(lo, hi):
        return jnp.dot(xb, w_ref[:, lo:hi], preferred_element_type=jnp.float32)

    for h in range(A_HEADS):
        qh = (seg(_C_QLAT + h * A_LAT, _C_QLAT + (h + 1) * A_LAT) * (A_LAT ** -0.5)).astype(jnp.bfloat16)
        for r in range(tm // BLOCK):
            q_ref[r, h] = qh[r * BLOCK:(r + 1) * BLOCK]
    c = seg(_C_CKV, _C_QIDX)
    ms = jnp.mean(c * c, axis=-1, keepdims=True)
    ckv_ref[...] = (c * lax.rsqrt(ms + RMS_EPS) * g_ref[...]).astype(jnp.bfloat16)
    qif = seg(_C_QIDX, _C_KW).astype(jnp.bfloat16)
    for h in range(IDX_HEADS):
        for r in range(tm // BLOCK):
            qidx_ref[r, h] = qif[r * BLOCK:(r + 1) * BLOCK, h * IDX_DIM:(h + 1) * IDX_DIM]
    kw = seg(_C_KW, _C_QB)
    kidx_ref[...] = kw[:, :IDX_DIM].astype(jnp.bfloat16)
    widx_ref[...] = kw[:, IDX_DIM:IDX_DIM + IDX_HEADS]
    qbf = seg(_C_QB, _C_KB).astype(jnp.bfloat16)
    for h in range(B_HEADS):
        for r in range(tm // BLOCK):
            qb_ref[r, h] = qbf[r * BLOCK:(r + 1) * BLOCK, h * B_HEAD_DIM:(h + 1) * B_HEAD_DIM]
    kbf = seg(_C_KB, _C_VB).astype(jnp.bfloat16)
    vbf = seg(_C_VB, _C_GA).astype(jnp.bfloat16)
    for g in range(B_KV_HEADS):
        kb_ref[g] = kbf[:, g * B_HEAD_DIM:(g + 1) * B_HEAD_DIM]
        vb_ref[g, :, :B_HEAD_DIM] = vbf[:, g * B_HEAD_DIM:(g + 1) * B_HEAD_DIM]
        vb_ref[g, :, B_HEAD_DIM:] = jnp.ones((tm, B_HEAD_DIM), jnp.bfloat16)
    ga_ref[...] = jax.nn.sigmoid(seg(_C_GA, _C_GB))
    gb_ref[...] = jax.nn.sigmoid(seg(_C_GB, _C_END))


def _input_projection(x2, w_pack, kv_g, tm):
    n = x2.shape[0]
    bf, f32 = jnp.bfloat16, jnp.float32
    row = lambda w: pl.BlockSpec((tm, w), lambda i: (i, 0))
    out_shape = (
        jax.ShapeDtypeStruct((n // BLOCK, A_HEADS, BLOCK, A_LAT), bf),
        jax.ShapeDtypeStruct((n, A_LAT), bf),
        jax.ShapeDtypeStruct((n // BLOCK, IDX_HEADS, BLOCK, IDX_DIM), bf),
        jax.ShapeDtypeStruct((n, IDX_DIM), bf),
        jax.ShapeDtypeStruct((n, IDX_HEADS), f32),
        jax.ShapeDtypeStruct((n // BLOCK, B_HEADS, BLOCK, B_HEAD_DIM), bf),
        jax.ShapeDtypeStruct((B_KV_HEADS, n, B_HEAD_DIM), bf),
        jax.ShapeDtypeStruct((B_KV_HEADS, n, 2 * B_HEAD_DIM), bf),
        jax.ShapeDtypeStruct((n, D_MODEL), f32),
        jax.ShapeDtypeStruct((n, D_MODEL), f32),
    )
    out_specs = (
        pl.BlockSpec((tm // BLOCK, A_HEADS, BLOCK, A_LAT), lambda i: (i, 0, 0, 0)),
        row(A_LAT), pl.BlockSpec((tm // BLOCK, IDX_HEADS, BLOCK, IDX_DIM), lambda i: (i, 0, 0, 0)),
        row(IDX_DIM), row(IDX_HEADS),
        pl.BlockSpec((tm // BLOCK, B_HEADS, BLOCK, B_HEAD_DIM), lambda i: (i, 0, 0, 0)),
        pl.BlockSpec((B_KV_HEADS, tm, B_HEAD_DIM), lambda i: (0, i, 0)),
        pl.BlockSpec((B_KV_HEADS, tm, 2 * B_HEAD_DIM), lambda i: (0, i, 0)),
        row(D_MODEL), row(D_MODEL),
    )
    return pl.pallas_call(
        _proj_kernel,
        grid=(n // tm,),
        in_specs=[row(D_MODEL),
                  pl.BlockSpec((D_MODEL, _C_END), lambda i: (0, 0)),
                  pl.BlockSpec((1, A_LAT), lambda i: (0, 0))],
        out_specs=out_specs,
        out_shape=out_shape,
        compiler_params=_cparams("arbitrary"),
        name="input_projection",
    )(x2, w_pack, kv_g)


def _dsa_kernel(qidx_ref, kidx_ref, widx_ref, q_ref, ckv_ref, ckvt_ref, bias_ref, wuv_ref,
                tri_ref, ya_ref, sc_ref, neg_ref, lg_ref, o_ref, *, k_sel, search_chunk):
    i = pl.program_id(1)
    f32 = jnp.float32
    s_loc = lax.broadcasted_iota(jnp.int32, (BLOCK, BLOCK), 0)
    t_loc = lax.broadcasted_iota(jnp.int32, (BLOCK, BLOCK), 1)
    causal_diag = s_loc <= t_loc
    idx_scale = IDX_DIM ** -0.5
    w_scale = IDX_HEADS ** -0.5

    def rows(j):
        return pl.ds(pl.multiple_of(j * BLOCK, BLOCK), BLOCK)

    sc_ref[...] = jnp.full(sc_ref.shape, -jnp.inf, f32)
    wts = (widx_ref[0] * w_scale) * idx_scale
    qi = qidx_ref[0]

    def score_keys(j):
        s = lax.dot_general(kidx_ref[0, rows(j), :], qi, _NT, preferred_element_type=f32)
        r = jnp.maximum(s, 0.0) * wts
        acc = r[:, :BLOCK]
        for h in range(1, IDX_HEADS):
            acc = acc + r[:, h * BLOCK:(h + 1) * BLOCK]
        return acc

    def fold_blocks(lo, hi, fn, init, combine):
        n = hi - lo

        def pair(k, acc):
            j = lo + 2 * k
            return combine(acc, combine(fn(j), fn(j + 1)))

        acc = lax.fori_loop(0, lax.shift_right_logical(n, 1), pair, init)
        return lax.cond((n & 1) == 1, lambda: combine(acc, fn(hi - 1)), lambda: acc)

    def far_scores(j):
        sc_ref[rows(j), :] = score_keys(j)
        return jnp.int32(0)

    fold_blocks(0, i, far_scores, jnp.int32(0), lambda a, b: a)
    sc_ref[rows(i), :] = jnp.where(causal_diag, score_keys(i), -jnp.inf)

    n_chunks = ((i + 1) * BLOCK + search_chunk - 1) // search_chunk
    n_acc = 64

    def count(cmp, cand):
        def body(c, cnt):
            base = pl.multiple_of(c * search_chunk, search_chunk)
            for r in range(search_chunk // n_acc):
                cnt = cnt + cmp(sc_ref[pl.ds(base + r * n_acc, n_acc), :], cand).astype(f32)
            return cnt
        cnt = lax.fori_loop(0, n_chunks, body, jnp.zeros((n_acc, BLOCK), f32))
        return jnp.sum(cnt, axis=0, keepdims=True)

    def ordered_to_float(u):
        k = u ^ INT_MIN
        return lax.bitcast_convert_type(k ^ ((k >> 31) & 0x7FFFFFFF), f32)

    def search_bit(b, prefix_u):
        cand_u = prefix_u | lax.shift_left(jnp.int32(1), 31 - b)
        cnt = count(jnp.greater_equal, ordered_to_float(cand_u))
        return jnp.where(cnt >= k_sel, cand_u, prefix_u)

    thr_u = lax.fori_loop(0, 32, search_bit, jnp.zeros((1, BLOCK), jnp.int32))
    thr = ordered_to_float(thr_u)
    thr = jnp.where(thr != thr, -jnp.inf, thr)
    c_ge = count(jnp.greater_equal, thr)
    c_gt = count(jnp.greater, thr)
    n_tie = k_sel - c_gt
    any_tie = jnp.max(((c_ge > k_sel) & (thr > -jnp.inf)).astype(f32)) > 0.0

    @pl.when(jnp.logical_not(any_tie))
    def _():
        def body(j, carry):
            neg_ref[rows(j), :] = jnp.where(sc_ref[rows(j), :] >= thr, 0.0, -jnp.inf)
            return carry
        lax.fori_loop(0, i, body, 0)
        neg_ref[rows(i), :] = jnp.where((sc_ref[rows(i), :] >= thr) & causal_diag, 0.0, -jnp.inf)

    @pl.when(any_tie)
    def _():
        tri = tri_ref[...]

        def block(j, tie_seen):
            kc = sc_ref[rows(j), :]
            eq = kc == thr
            tie_rank = jnp.dot(tri, eq.astype(jnp.bfloat16), preferred_element_type=f32) + tie_seen
            sel = (kc > thr) | (eq & (tie_rank <= n_tie))
            return sel, tie_seen + jnp.sum(eq.astype(f32), axis=0, keepdims=True)

        def body(j, tie_seen):
            sel, tie_seen = block(j, tie_seen)
            neg_ref[rows(j), :] = jnp.where(sel, 0.0, -jnp.inf)
            return tie_seen

        tie_seen = lax.fori_loop(0, i, body, jnp.zeros((1, BLOCK), f32))
        sel, _ = block(i, tie_seen)
        neg_ref[rows(i), :] = jnp.where(sel & causal_diag, 0.0, -jnp.inf)

    q_all = q_ref[0]

    def logits(j):
        lg = lax.dot_general(ckv_ref[0, rows(j), :], q_all, _NT, preferred_element_type=f32)
        return lg + jnp.concatenate([neg_ref[rows(j), :]] * A_HEADS, axis=1)

    def far_logits(j):
        lg = logits(j)
        lg_ref[rows(j), :] = lg
        return jnp.max(lg, axis=0, keepdims=True)

    def near_logits(j):
        lg = logits(j) + bias_ref[j - i + 1]
        lg_ref[rows(j), :] = lg
        return jnp.max(lg, axis=0, keepdims=True)

    n_far = jnp.maximum(i - 1, 0)
    m_run = fold_blocks(0, n_far, far_logits, jnp.full((1, A_HEADS * BLOCK), -jnp.inf, f32), jnp.maximum)
    m_run = fold_blocks(n_far, i + 1, near_logits, m_run, jnp.maximum)

    o_ref[...] = jnp.zeros_like(o_ref)

    def pv_rows(r):
        p = jnp.exp(lg_ref[r, :] - m_run).astype(jnp.bfloat16)
        o_ref[...] += jnp.dot(ckvt_ref[0, :, r], p, preferred_element_type=f32)

    def pv_pair(k, carry):
        pv_rows(pl.ds(pl.multiple_of(k * (2 * BLOCK), 2 * BLOCK), 2 * BLOCK))
        return carry

    lax.fori_loop(0, lax.shift_right_logical(i + 1, 1), pv_pair, 0)

    @pl.when((i & 1) == 0)
    def _():
        pv_rows(rows(i))

    o_t = o_ref[:A_LAT, :] / o_ref[A_LAT:A_LAT + 1, :]

    for h in range(A_HEADS):
        o_h = o_t[:, h * BLOCK:(h + 1) * BLOCK].T.astype(jnp.bfloat16)
        y_h = jnp.dot(o_h, wuv_ref[h], preferred_element_type=f32)
        ya_ref[:, h * A_HEAD_DIM:(h + 1) * A_HEAD_DIM] = y_h.astype(ya_ref.dtype)


def _dsa_attention(qidx_blocks, kidx, widx_l, q_blocks, ckv, ckv_tx, bias_near, wuv, bsz, seq):
    n_blk = seq // BLOCK
    k_sel = min(TOPK_MAX, seq // 4)
    search_chunk = min(512, seq)
    tri = jnp.asarray(np.tril(np.ones((BLOCK, BLOCK), np.float32)), jnp.bfloat16)
    kern = functools.partial(_dsa_kernel, k_sel=float(k_sel), search_chunk=search_chunk)
    blk = lambda b, i: (b * n_blk + i, 0, 0)
    return pl.pallas_call(
        kern,
        grid=(bsz, n_blk),
        in_specs=[
            pl.BlockSpec((1, IDX_HEADS * BLOCK, IDX_DIM), blk),
            pl.BlockSpec((1, seq, IDX_DIM), lambda b, i: (b, 0, 0)),
            pl.BlockSpec((1, 1, IDX_HEADS * BLOCK), blk),
            pl.BlockSpec((1, A_HEADS * BLOCK, A_LAT), blk),
            pl.BlockSpec((1, seq, A_LAT), lambda b, i: (b, 0, 0)),
            pl.BlockSpec((1, A_LAT + 8, seq), lambda b, i: (b, 0, 0)),
            pl.BlockSpec((2, BLOCK, A_HEADS * BLOCK), lambda b, i: (0, 0, 0)),
            pl.BlockSpec((A_HEADS, A_LAT, A_HEAD_DIM), lambda b, i: (0, 0, 0)),
            pl.BlockSpec((BLOCK, BLOCK), lambda b, i: (0, 0)),
        ],
        out_specs=pl.BlockSpec((BLOCK, A_HEADS * A_HEAD_DIM), lambda b, i: (b * n_blk + i, 0)),
        out_shape=jax.ShapeDtypeStruct((bsz * seq, A_HEADS * A_HEAD_DIM), jnp.bfloat16),
        scratch_shapes=[pltpu.VMEM((seq, BLOCK), jnp.float32),
                        pltpu.VMEM((seq, BLOCK), jnp.float32),
                        pltpu.VMEM((seq, A_HEADS * BLOCK), jnp.float32),
                        pltpu.VMEM((A_LAT + 8, A_HEADS * BLOCK), jnp.float32)],
        compiler_params=_cparams("arbitrary", "arbitrary"),
        name="dsa_attention",
    )(qidx_blocks, kidx, widx_l, q_blocks, ckv, ckv_tx, bias_near, wuv, tri)


SWA_QB = 4


def _swa_kernel(sink_ref, q_ref, kp_ref, kc_ref, vp_ref, vc_ref, bias_ref, yb_ref, *, qb):
    i = pl.program_id(1)
    f32 = jnp.float32
    grp = B_HEADS // B_KV_HEADS
    rows = grp * BLOCK
    row = lax.broadcasted_iota(jnp.int32, (rows, BLOCK), 0)
    t_loc = row & (BLOCK - 1)
    s_loc = lax.broadcasted_iota(jnp.int32, (rows, BLOCK), 1)
    d_prev = t_loc + BLOCK - s_loc
    d_cur = t_loc - s_loc
    m_cur = (d_cur >= 0) & (d_cur < WINDOW)
    head_of_row = lax.broadcasted_iota(jnp.int32, (rows, 1), 0) >> (BLOCK.bit_length() - 1)
    scale = B_HEAD_DIM ** -0.5
    for g in range(B_KV_HEADS):
        sink = jnp.zeros((rows, 1), f32)
        for hh in range(grp):
            sink = jnp.where(head_of_row == hh, sink_ref[g * grp + hh], sink)
        bias = bias_ref[g * grp:(g + 1) * grp].reshape(rows, 2 * BLOCK)
        for r in range(qb):
            has_prev = jnp.zeros_like(d_prev) + jnp.minimum(i * qb + r, 1)
            m_prev = (d_prev >= 0) & (d_prev < WINDOW) & (has_prev > 0)
            q = q_ref[r, g * grp:(g + 1) * grp].reshape(rows, B_HEAD_DIM)
            if r == 0:
                kp, vp = kp_ref[g], vp_ref[g]
            else:
                kp, vp = kc_ref[g, (r - 1) * BLOCK:r * BLOCK], vc_ref[g, (r - 1) * BLOCK:r * BLOCK]
            kc, vc = kc_ref[g, r * BLOCK:(r + 1) * BLOCK], vc_ref[g, r * BLOCK:(r + 1) * BLOCK]
            lp = lax.dot_general(q, kp, _NT, preferred_element_type=f32) * scale + bias[:, :BLOCK]
            lc = lax.dot_general(q, kc, _NT, preferred_element_type=f32) * scale + bias[:, BLOCK:]
            lp = jnp.where(m_prev, lp, -jnp.inf)
            lc = jnp.where(m_cur, lc, -jnp.inf)
            m = jnp.maximum(jnp.max(jnp.maximum(lp, lc), axis=1, keepdims=True), sink)
            pp = jnp.exp(lp - m).astype(jnp.bfloat16)
            pc = jnp.exp(lc - m).astype(jnp.bfloat16)
            ox = (jnp.dot(pp, vp, preferred_element_type=f32)
                  + jnp.dot(pc, vc, preferred_element_type=f32))
            den = ox[:, B_HEAD_DIM:B_HEAD_DIM + 1] + jnp.exp(sink - m)
            o = (ox[:, :B_HEAD_DIM] / den).astype(yb_ref.dtype)
            for hh in range(grp):
                h = g * grp + hh
                yb_ref[r * BLOCK:(r + 1) * BLOCK, h * B_HEAD_DIM:(h + 1) * B_HEAD_DIM] = (
                    o[hh * BLOCK:(hh + 1) * BLOCK])


def _swa_attention(sinks, q_blocks, kb, vbx, bias_b, bsz, seq):
    n_blk = seq // BLOCK
    qb = SWA_QB if n_blk % SWA_QB == 0 else 1
    n_step = n_blk // qb
    cur = lambda b, i: (0, b * n_step + i, 0)
    prev = lambda b, i: (0, b * n_blk + jnp.maximum(i * qb - 1, 0), 0)
    return pl.pallas_call(
        functools.partial(_swa_kernel, qb=qb),
        grid=(bsz, n_step),
        in_specs=[
            pl.BlockSpec(memory_space=pltpu.SMEM),
            pl.BlockSpec((qb, B_HEADS, BLOCK, B_HEAD_DIM), lambda b, i: (b * n_step + i, 0, 0, 0)),
            pl.BlockSpec((B_KV_HEADS, BLOCK, B_HEAD_DIM), prev),
            pl.BlockSpec((B_KV_HEADS, qb * BLOCK, B_HEAD_DIM), cur),
            pl.BlockSpec((B_KV_HEADS, BLOCK, 2 * B_HEAD_DIM), prev),
            pl.BlockSpec((B_KV_HEADS, qb * BLOCK, 2 * B_HEAD_DIM), cur),
            pl.BlockSpec((B_HEADS, BLOCK, 2 * BLOCK), lambda b, i: (0, 0, 0)),
        ],
        out_specs=pl.BlockSpec((qb * BLOCK, B_HEADS * B_HEAD_DIM), lambda b, i: (b * n_step + i, 0)),
        out_shape=jax.ShapeDtypeStruct((bsz * seq, B_HEADS * B_HEAD_DIM), jnp.bfloat16),
        compiler_params=_cparams("arbitrary", "arbitrary"),
        name="swa_attention",
    )(sinks, q_blocks, kb, kb, vbx, vbx, bias_b)


def _layer_norm(h, g, b):
    mu = jnp.mean(h, axis=-1, keepdims=True)
    d = h - mu
    var = jnp.mean(d * d, axis=-1, keepdims=True)
    return d * lax.rsqrt(var + LN_EPS) * g + b


def _merge_kernel(ya_ref, yb_ref, ga_ref, gb_ref, x_ref, wa_ref, wb_ref, wo_ref, g1_ref, b1_ref,
                  wrh_ref, wrl_ref, br_ref, tri_ref, x1_ref, rt_ref, cnt_ref, carry_ref):
    f32 = jnp.float32
    pa = jnp.dot(ya_ref[...], wa_ref[...], preferred_element_type=f32)
    pb = jnp.dot(yb_ref[...], wb_ref[...], preferred_element_type=f32)
    merged = ga_ref[...] * pa + gb_ref[...] * pb
    h = ALPHA * x_ref[...] + jnp.dot(merged.astype(jnp.bfloat16), wo_ref[...],
                                     preferred_element_type=f32)
    x1 = _layer_norm(h, g1_ref[...], b1_ref[...])
    _store_tile_rows(x1_ref, x1)

    x_hi = x1.astype(jnp.bfloat16)
    x_lo = (x1 - x_hi.astype(f32)).astype(jnp.bfloat16)
    lg = (jnp.dot(x_hi, wrh_ref[...], preferred_element_type=f32)
          + jnp.dot(x_hi, wrl_ref[...], preferred_element_type=f32)
          + jnp.dot(x_lo, wrh_ref[...], preferred_element_type=f32)) + br_ref[...]
    lane_i = lax.broadcasted_iota(jnp.int32, lg.shape, 1)
    lane = lane_i.astype(f32)
    big = jnp.float32(1 << 20)
    is_g = lane_i < N_GROUPS
    gl = jnp.where(is_g, lg, -jnp.inf)
    gmax = jnp.max(gl, axis=1, keepdims=True)
    g_sel = jnp.min(jnp.where(gl == gmax, lane, big), axis=1, keepdims=True)
    g_w = 1.0 / jnp.sum(jnp.where(is_g, jnp.exp(gl - gmax), 0.0), axis=1, keepdims=True)
    e_id = lane_i - N_GROUPS
    e_grp = (e_id >> 3).astype(f32)
    in_grp = (e_id >= 0) & (e_id < N_EXPERTS) & (e_grp == g_sel)
    el = jnp.where(in_grp, lg, -jnp.inf)
    emax = jnp.max(el, axis=1, keepdims=True)
    ee = jnp.where(in_grp, jnp.exp(el - emax), 0.0)
    ep = ee / jnp.sum(ee, axis=1, keepdims=True)
    epm = jnp.where(in_grp, ep, -1.0)
    p1 = jnp.max(epm, axis=1, keepdims=True)
    i1 = jnp.min(jnp.where(epm == p1, lane, big), axis=1, keepdims=True)
    epm2 = jnp.where(lane == i1, -1.0, epm)
    p2 = jnp.max(epm2, axis=1, keepdims=True)
    i2 = jnp.min(jnp.where(epm2 == p2, lane, big), axis=1, keepdims=True)
    psum = p1 + p2
    w1 = g_w * p1 / psum
    w2 = g_w * p2 / psum
    @pl.when(pl.program_id(0) == 0)
    def _():
        carry_ref[...] = jnp.zeros_like(carry_ref)

    oh1 = lane + N_GROUPS == i1
    oh2 = lane + N_GROUPS == i2
    oh = jnp.where(oh1 | oh2, 1.0, 0.0)
    prefix = jnp.dot(tri_ref[...], oh.astype(jnp.bfloat16), preferred_element_type=f32) + carry_ref[...]
    rank1 = jnp.sum(jnp.where(oh1, prefix, 0.0), axis=1, keepdims=True)
    rank2 = jnp.sum(jnp.where(oh2, prefix, 0.0), axis=1, keepdims=True)
    carry_ref[...] += jnp.sum(oh, axis=0, keepdims=True)
    cnt_ref[...] = carry_ref[...]
    rec = (i1 - N_GROUPS, i2 - N_GROUPS, w1, w2, rank1, rank2)
    rt = jnp.zeros(lg.shape, f32)
    for k, v in enumerate(rec):
        rt = jnp.where(lane_i == k, v, rt)
    rt_ref[...] = rt


def _merge_route(ya, yb, ga, gb, x2, wa, wb, wo, g1, b1, wr, br, tm):
    wr_hi = wr.astype(jnp.bfloat16)
    wr_lo = (wr - wr_hi.astype(jnp.float32)).astype(jnp.bfloat16)
    n = x2.shape[0]
    row = lambda w: pl.BlockSpec((tm, w), lambda i: (i, 0))
    full = lambda r, c: pl.BlockSpec((r, c), lambda i: (0, 0))
    aw, bw = A_HEADS * A_HEAD_DIM, B_HEADS * B_HEAD_DIM
    tri = jnp.asarray(np.tril(np.ones((tm, tm), np.float32), -1), jnp.bfloat16)
    return pl.pallas_call(
        _merge_kernel,
        grid=(n // tm,),
        in_specs=[row(aw), row(bw), row(D_MODEL), row(D_MODEL), row(D_MODEL),
                  full(aw, D_MODEL), full(bw, D_MODEL), full(D_MODEL, D_MODEL),
                  full(1, D_MODEL), full(1, D_MODEL), full(D_MODEL, LANES), full(D_MODEL, LANES),
                  full(1, LANES), full(tm, tm)],
        out_specs=(pl.BlockSpec((tm * ROW_TILE, LANES), lambda i: (i, 0)), row(LANES), full(1, LANES)),
        out_shape=(jax.ShapeDtypeStruct((n * ROW_TILE, LANES), jnp.float32),
                   jax.ShapeDtypeStruct((n, LANES), jnp.float32),
                   jax.ShapeDtypeStruct((1, LANES), jnp.float32)),
        scratch_shapes=[pltpu.VMEM((1, LANES), jnp.float32)],
        compiler_params=_cparams("arbitrary"),
        name="merge_route",
    )(ya, yb, ga, gb, x2, wa, wb, wo, g1, b1, wr_hi, wr_lo, br, tri)


ROW_TILE = D_MODEL // LANES


def _store_tile_rows(ref, val):
    rows = val.shape[0]
    for j in range(ROW_TILE):
        ref[pl.ds(j, rows, stride=ROW_TILE), :] = val[:, j * LANES:(j + 1) * LANES]


def _load_tile_rows(ref, rows):
    return jnp.concatenate([ref[pl.ds(j, rows, stride=ROW_TILE), :] for j in range(ROW_TILE)], axis=1)


DMA_UNROLL = 8


def _dispatch_kernel(pos_ref, x_ref, xs_hbm, sem, *, tm):
    def issue(k, carry):
        for u in range(DMA_UNROLL):
            r = k * DMA_UNROLL + u
            src = x_ref.at[pl.ds(pl.multiple_of(r * ROW_TILE, ROW_TILE), ROW_TILE)]
            for slot in range(2):
                p = pos_ref[0, 0, 2 * r + slot]
                dst = xs_hbm.at[pl.ds(pl.multiple_of(p * ROW_TILE, ROW_TILE), ROW_TILE)]
                pltpu.make_async_copy(src, dst, sem).start(priority=slot)
        return carry

    lax.fori_loop(0, tm // DMA_UNROLL, issue, 0)
    for slot in range(2):
        pltpu.make_async_copy(x_ref, xs_hbm.at[pl.ds(0, tm * ROW_TILE)], sem).wait()


def _moe_dispatch(pos, x1r, n_rows, tm):
    n = x1r.shape[0] // ROW_TILE
    return pl.pallas_call(
        functools.partial(_dispatch_kernel, tm=tm),
        grid=(n // tm,),
        in_specs=[pl.BlockSpec((1, 1, 2 * tm), lambda i: (i, 0, 0), memory_space=pltpu.SMEM),
                  pl.BlockSpec((tm * ROW_TILE, LANES), lambda i: (i, 0))],
        out_specs=pl.BlockSpec(memory_space=pl.ANY),
        out_shape=jax.ShapeDtypeStruct((n_rows * ROW_TILE, LANES), jnp.float32),
        scratch_shapes=[pltpu.SemaphoreType.DMA],
        compiler_params=_cparams("arbitrary"),
        name="moe_dispatch",
    )(pos, x1r)


def _expert_kernel(tile_ref, exp_ref, lo_ref, hi_ref, nit_ref, xs_ref, wg_ref, wu_ref, wd_ref, ys_ref,
                   *, tm):
    f32 = jnp.float32
    w = pl.program_id(0)

    @pl.when(w < nit_ref[0])
    def _():
        xb = _load_tile_rows(xs_ref, tm).astype(jnp.bfloat16)
        g = jnp.dot(xb, wg_ref[0], preferred_element_type=f32)
        u = jnp.dot(xb, wu_ref[0], preferred_element_type=f32)
        hmid = (g * jax.nn.sigmoid(g) * u).astype(jnp.bfloat16)
        y = jnp.dot(hmid, wd_ref[0], preferred_element_type=f32)
        row = tile_ref[w] * tm + lax.broadcasted_iota(jnp.int32, (tm, 1), 0)
        mine = (row >= lo_ref[w]) & (row < hi_ref[w])
        first_visit = jnp.logical_or(w == 0, tile_ref[jnp.maximum(w - 1, 0)] != tile_ref[w])

        @pl.when(first_visit)
        def _():
            _store_tile_rows(ys_ref, jnp.where(mine, y, 0.0))

        @pl.when(jnp.logical_not(first_visit))
        def _():
            _store_tile_rows(ys_ref, jnp.where(mine, y, _load_tile_rows(ys_ref, tm)))


def _moe_experts(items, xs, wg, wu, wd, tm):
    n_items = items[0].shape[0]
    grid_spec = pltpu.PrefetchScalarGridSpec(
        num_scalar_prefetch=5,
        grid=(n_items,),
        in_specs=[
            pl.BlockSpec((tm * ROW_TILE, LANES), lambda w, t, e, lo, hi, n: (t[w], 0)),
            pl.BlockSpec((1, D_MODEL, D_EXPERT), lambda w, t, e, lo, hi, n: (e[w], 0, 0)),
            pl.BlockSpec((1, D_MODEL, D_EXPERT), lambda w, t, e, lo, hi, n: (e[w], 0, 0)),
            pl.BlockSpec((1, D_EXPERT, D_MODEL), lambda w, t, e, lo, hi, n: (e[w], 0, 0)),
        ],
        out_specs=pl.BlockSpec((tm * ROW_TILE, LANES), lambda w, t, e, lo, hi, n: (t[w], 0)),
    )
    return pl.pallas_call(
        functools.partial(_expert_kernel, tm=tm),
        grid_spec=grid_spec,
        out_shape=jax.ShapeDtypeStruct(xs.shape, jnp.float32),
        compiler_params=_cparams("arbitrary"),
        name="moe_experts",
    )(*items, xs, wg, wu, wd)


def _combine_kernel(pos_ref, rt_ref, x_ref, ys_hbm, g2_ref, b2_ref, out_ref, buf_a, buf_b, sem, *, tm):
    bufs = (buf_a, buf_b)

    def issue(k, carry):
        for u in range(DMA_UNROLL):
            r = k * DMA_UNROLL + u
            for slot in range(2):
                p = pos_ref[0, 0, 2 * r + slot]
                src = ys_hbm.at[pl.ds(pl.multiple_of(p * ROW_TILE, ROW_TILE), ROW_TILE)]
                dst = bufs[slot].at[pl.ds(pl.multiple_of(r * ROW_TILE, ROW_TILE), ROW_TILE)]
                pltpu.make_async_copy(src, dst, sem).start(priority=slot)
        return carry

    lax.fori_loop(0, tm // DMA_UNROLL, issue, 0)
    for slot in range(2):
        pltpu.make_async_copy(ys_hbm.at[pl.ds(0, tm * ROW_TILE)], bufs[slot], sem).wait()
    rt = rt_ref[...]
    w1 = rt[:, 2:3]
    w2 = rt[:, 3:4]
    ffn = w1 * _load_tile_rows(buf_a, tm) + w2 * _load_tile_rows(buf_b, tm)
    out_ref[...] = _layer_norm(ALPHA * _load_tile_rows(x_ref, tm) + ffn, g2_ref[...], b2_ref[...])


def _moe_combine(pos, rt, x1r, ys, g2, b2, tm):
    n = rt.shape[0]
    return pl.pallas_call(
        functools.partial(_combine_kernel, tm=tm),
        grid=(n // tm,),
        in_specs=[pl.BlockSpec((1, 1, 2 * tm), lambda i: (i, 0, 0), memory_space=pltpu.SMEM),
                  pl.BlockSpec((tm, LANES), lambda i: (i, 0)),
                  pl.BlockSpec((tm * ROW_TILE, LANES), lambda i: (i, 0)),
                  pl.BlockSpec(memory_space=pl.ANY),
                  pl.BlockSpec((1, D_MODEL), lambda i: (0, 0)),
                  pl.BlockSpec((1, D_MODEL), lambda i: (0, 0))],
        out_specs=pl.BlockSpec((tm, D_MODEL), lambda i: (i, 0)),
        out_shape=jax.ShapeDtypeStruct((n, D_MODEL), jnp.float32),
        scratch_shapes=[pltpu.VMEM((tm * ROW_TILE, LANES), jnp.float32),
                        pltpu.VMEM((tm * ROW_TILE, LANES), jnp.float32),
                        pltpu.SemaphoreType.DMA],
        compiler_params=_cparams("arbitrary"),
        name="moe_combine",
    )(pos, rt, x1r, ys, g2, b2)


MOE_TM = 256


def _moe_plan(rt, counts, n_tiles):
    i32 = jnp.int32
    cnt = counts[0, :N_EXPERTS].astype(i32)
    end = jnp.cumsum(cnt)
    start = end - cnt
    e12 = rt[:, 0:2].astype(i32)
    pos = start[e12] + rt[:, 4:6].astype(i32)
    first_t = start // MOE_TM
    items_e = jnp.where(cnt > 0, (end - 1) // MOE_TM - first_t + 1, 0)
    item_end = jnp.cumsum(items_e)
    n_items = item_end[-1]
    w = jnp.minimum(jnp.arange(n_tiles + N_EXPERTS - 1, dtype=i32), n_items - 1)
    e_w = jnp.minimum(jnp.searchsorted(item_end, w, side="right"), N_EXPERTS - 1).astype(i32)
    t_w = first_t[e_w] + (w - (item_end - items_e)[e_w])
    lo_w = jnp.maximum(start[e_w], t_w * MOE_TM)
    hi_w = jnp.minimum(end[e_w], (t_w + 1) * MOE_TM)
    return pos, (t_w.astype(i32), e_w, lo_w.astype(i32), hi_w.astype(i32), n_items.reshape(1).astype(i32))


def _pick_tile(n, pref):
    t = min(pref, n)
    while n % t:
        t //= 2
    return t


def kernel(x, w_in, kv_norm_g, w_uv, w_branch_a, sinks, w_branch_b, w_out, rel_bias, ln1_g, ln1_b,
           w_group, b_group, w_router, b_router, w_gate, w_up, w_down, ln2_g, ln2_b):
    bsz, seq, d = x.shape
    assert d == D_MODEL and seq % BLOCK == 0 and w_in.shape[0] == DEPTH == 1
    n = bsz * seq
    n_blk = seq // BLOCK
    bf, f32 = jnp.bfloat16, jnp.float32
    x2 = x.reshape(n, d)

    r = np.arange(BLOCK, dtype=np.int32)
    d_prev = r[None, :] + BLOCK - r[:, None]
    d_cur = r[None, :] - r[:, None]
    bkt_a = jnp.asarray(_t5_bucket_np(np.stack([d_prev, d_cur])))
    s2 = np.arange(2 * BLOCK, dtype=np.int32)
    bkt_b = jnp.asarray(_t5_bucket_np(r[:, None] + BLOCK - s2[None, :])[None])
    assert int(_t5_bucket_np(np.arange(BLOCK + 1, max(seq, BLOCK + 2))).min()) == N_BUCKETS - 1
    tab_t = rel_bias.astype(f32).T
    bias_a = _bias_tiles(tab_t, bkt_a, 0, A_HEADS, True, minus_far=True)
    bias_b = _bias_tiles(tab_t, bkt_b, A_HEADS, B_HEADS, False)

    w0 = w_in[0]
    w_pack = jnp.concatenate(
        [w0[:, :_RAW_KW_END], jnp.zeros((d, _C_QB - _RAW_KW_END), w0.dtype), w0[:, _RAW_KW_END:]],
        axis=1).astype(bf)
    (q_blocks, ckv, qidx_blocks, kidx, widx, qb_blocks, kb, vbx, ga, gb) = _input_projection(
        x2, w_pack, kv_norm_g[0].reshape(1, A_LAT).astype(f32), _pick_tile(n, 512))

    ckv3 = ckv.reshape(bsz, seq, A_LAT)
    ckv_tx = jnp.concatenate([jnp.swapaxes(ckv3, 1, 2), jnp.ones((bsz, 8, seq), bf)], axis=1)
    widx_l = jnp.swapaxes(widx.reshape(bsz * n_blk, BLOCK, IDX_HEADS), 1, 2).reshape(
        bsz * n_blk, 1, IDX_HEADS * BLOCK)
    ya = _dsa_attention(qidx_blocks.reshape(bsz * n_blk, IDX_HEADS * BLOCK, IDX_DIM),
                        kidx.reshape(bsz, seq, IDX_DIM), widx_l,
                        q_blocks.reshape(bsz * n_blk, A_HEADS * BLOCK, A_LAT),
                        ckv3, ckv_tx, bias_a, w_uv[0].astype(bf), bsz, seq)

    yb = _swa_attention(sinks[0].astype(f32), qb_blocks, kb, vbx, bias_b, bsz, seq)

    w_route = jnp.concatenate(
        [w_group[0], w_router[0], jnp.zeros((d, LANES - N_GROUPS - N_EXPERTS), f32)], axis=1).astype(f32)
    b_route = jnp.concatenate(
        [b_group[0], b_router[0], jnp.zeros((LANES - N_GROUPS - N_EXPERTS,), f32)]).reshape(1, LANES).astype(f32)
    x1r, rt, counts = _merge_route(
        ya, yb, ga, gb, x2, w_branch_a[0].astype(bf), w_branch_b[0].astype(bf), w_out[0].astype(bf),
        ln1_g[0].reshape(1, d).astype(f32), ln1_b[0].reshape(1, d).astype(f32), w_route, b_route,
        _pick_tile(n, 512))

    assert (2 * n) % MOE_TM == 0
    pos, items = _moe_plan(rt, counts, 2 * n // MOE_TM)
    tm_io = _pick_tile(n, 512)
    pos_blocks = pos.reshape(n // tm_io, 1, 2 * tm_io)
    xs = _moe_dispatch(pos_blocks, x1r, 2 * n, tm_io)
    ys = _moe_experts(items, xs, w_gate[0].astype(bf), w_up[0].astype(bf), w_down[0].astype(bf), MOE_TM)
    out = _moe_combine(pos_blocks, rt, x1r, ys, ln2_g[0].reshape(1, d).astype(f32),
                       ln2_b[0].reshape(1, d).astype(f32), tm_io)
    return out.reshape(bsz, seq, d)
```

```python
import functools
import math

import numpy as np
import jax
import jax.numpy as jnp
from jax import lax
from jax.experimental import pallas as pl
from jax.experimental.pallas import tpu as pltpu

D_MODEL = 1024
A_HEADS = 8
A_LAT = 128
A_HEAD_DIM = 64
IDX_HEADS = 8
IDX_DIM = 64
TOPK_MAX = 256
B_HEADS = 8
B_KV_HEADS = 2
B_HEAD_DIM = 64
WINDOW = 128
BLOCK = 128
N_BUCKETS = 32
MAX_DISTANCE = 128
N_GROUPS = 4
EXPERTS_PER_GROUP = 8
N_EXPERTS = 32
D_EXPERT = 256
DEPTH = 1
ALPHA = (2 * DEPTH) ** 0.25
LN_EPS = 1e-5
RMS_EPS = 1e-6

LANES = 128
INT_MIN = -(2 ** 31)
VMEM_LIMIT = 48 * 1024 * 1024

_NT = (((1,), (1,)), ((), ()))


def _cparams(*sem):
    return pltpu.CompilerParams(dimension_semantics=sem, vmem_limit_bytes=VMEM_LIMIT)


def _t5_bucket_np(dist):
    f32 = np.float32
    n = np.maximum(dist, 0)
    max_exact = N_BUCKETS // 2
    nf = np.maximum(n, 1).astype(f32)
    large = max_exact + (np.log(nf / f32(max_exact)) / f32(math.log(MAX_DISTANCE / max_exact))
                         * f32(N_BUCKETS - max_exact)).astype(np.int32)
    large = np.minimum(large, N_BUCKETS - 1)
    return np.where(n < max_exact, n, large).astype(np.int32)


def _bias_kernel(tab_ref, bkt_ref, out_ref, *, head0, minus_far):
    h = pl.program_id(0) + head0
    bkt = bkt_ref[...]
    far = tab_ref[h, N_BUCKETS - 1]
    acc = jnp.full(bkt.shape, far, jnp.float32)
    for b in range(N_BUCKETS - 1):
        acc = jnp.where(bkt == b, tab_ref[h, b], acc)
    out_ref[...] = acc - far if minus_far else acc


def _bias_tiles(tab_t, bkt, head0, n_heads, lane_major, minus_far=False):
    g, r, c = bkt.shape
    if lane_major:
        out_shape = jax.ShapeDtypeStruct((g, r, n_heads * c), jnp.float32)
        out_spec = pl.BlockSpec((g, r, c), lambda h: (0, 0, h))
    else:
        out_shape = jax.ShapeDtypeStruct((n_heads * g, r, c), jnp.float32)
        out_spec = pl.BlockSpec((g, r, c), lambda h: (h, 0, 0))
    return pl.pallas_call(
        functools.partial(_bias_kernel, head0=head0, minus_far=minus_far),
        grid=(n_heads,),
        in_specs=[pl.BlockSpec(memory_space=pltpu.SMEM),
                  pl.BlockSpec((g, r, c), lambda h: (0, 0, 0))],
        out_specs=out_spec,
        out_shape=out_shape,
        compiler_params=_cparams("arbitrary"),
        name="bias_tiles",
    )(tab_t, bkt)


_C_QLAT = 0
_C_CKV = _C_QLAT + A_HEADS * A_LAT
_C_QIDX = _C_CKV + A_LAT
_C_KW = _C_QIDX + IDX_HEADS * IDX_DIM
_C_QB = _C_KW + LANES
_C_KB = _C_QB + B_HEADS * B_HEAD_DIM
_C_VB = _C_KB + B_KV_HEADS * B_HEAD_DIM
_C_GA = _C_VB + B_KV_HEADS * B_HEAD_DIM
_C_GB = _C_GA + D_MODEL
_C_END = _C_GB + D_MODEL
_RAW_KW_END = _C_KW + IDX_DIM + IDX_HEADS


def _proj_kernel(x_ref, w_ref, g_ref, q_ref, ckv_ref, qidx_ref, kidx_ref, widx_ref,
                 qb_ref, kb_ref, vb_ref, ga_ref, gb_ref):
    xb = x_ref[...].astype(jnp.bfloat16)
    tm = xb.shape[0]

    def seg(lo, hi):
        return jnp.dot(xb, w_ref[:, lo:hi], preferred_element_type=jnp.float32)

    for h in range(A_HEADS):
        qh = (seg(_C_QLAT + h * A_LAT, _C_QLAT + (h + 1) * A_LAT) * (A_LAT ** -0.5)).astype(jnp.bfloat16)
        for r in range(tm // BLOCK):
            q_ref[r, h] = qh[r * BLOCK:(r + 1) * BLOCK]
    c = seg(_C_CKV, _C_QIDX)
    ms = jnp.mean(c * c, axis=-1, keepdims=True)
    ckv_ref[...] = (c * lax.rsqrt(ms + RMS_EPS) * g_ref[...]).astype(jnp.bfloat16)
    qif = seg(_C_QIDX, _C_KW).astype(jnp.bfloat16)
    for h in range(IDX_HEADS):
        for r in range(tm // BLOCK):
            qidx_ref[r, h] = qif[r * BLOCK:(r + 1) * BLOCK, h * IDX_DIM:(h + 1) * IDX_DIM]
    kw = seg(_C_KW, _C_QB)
    kidx_ref[...] = kw[:, :IDX_DIM].astype(jnp.bfloat16)
    widx_ref[...] = kw[:, IDX_DIM:IDX_DIM + IDX_HEADS]
    qbf = seg(_C_QB, _C_KB).astype(jnp.bfloat16)
    for h in range(B_HEADS):
        for r in range(tm // BLOCK):
            qb_ref[r, h] = qbf[r * BLOCK:(r + 1) * BLOCK, h * B_HEAD_DIM:(h + 1) * B_HEAD_DIM]
    kbf = seg(_C_KB, _C_VB).astype(jnp.bfloat16)
    vbf = seg(_C_VB, _C_GA).astype(jnp.bfloat16)
    for g in range(B_KV_HEADS):
        kb_ref[g] = kbf[:, g * B_HEAD_DIM:(g + 1) * B_HEAD_DIM]
        vb_ref[g, :, :B_HEAD_DIM] = vbf[:, g * B_HEAD_DIM:(g + 1) * B_HEAD_DIM]
        vb_ref[g, :, B_HEAD_DIM:] = jnp.ones((tm, B_HEAD_DIM), jnp.bfloat16)
    ga_ref[...] = jax.nn.sigmoid(seg(_C_GA, _C_GB))
    gb_ref[...] = jax.nn.sigmoid(seg(_C_GB, _C_END))


def _input_projection(x2, w_pack, kv_g, tm):
    n = x2.shape[0]
    bf, f32 = jnp.bfloat16, jnp.float32
    row = lambda w: pl.BlockSpec((tm, w), lambda i: (i, 0))
    out_shape = (
        jax.ShapeDtypeStruct((n // BLOCK, A_HEADS, BLOCK, A_LAT), bf),
        jax.ShapeDtypeStruct((n, A_LAT), bf),
        jax.ShapeDtypeStruct((n // BLOCK, IDX_HEADS, BLOCK, IDX_DIM), bf),
        jax.ShapeDtypeStruct((n, IDX_DIM), bf),
        jax.ShapeDtypeStruct((n, IDX_HEADS), f32),
        jax.ShapeDtypeStruct((n // BLOCK, B_HEADS, BLOCK, B_HEAD_DIM), bf),
        jax.ShapeDtypeStruct((B_KV_HEADS, n, B_HEAD_DIM), bf),
        jax.ShapeDtypeStruct((B_KV_HEADS, n, 2 * B_HEAD_DIM), bf),
        jax.ShapeDtypeStruct((n, D_MODEL), f32),
        jax.ShapeDtypeStruct((n, D_MODEL), f32),
    )
    out_specs = (
        pl.BlockSpec((tm // BLOCK, A_HEADS, BLOCK, A_LAT), lambda i: (i, 0, 0, 0)),
        row(A_LAT), pl.BlockSpec((tm // BLOCK, IDX_HEADS, BLOCK, IDX_DIM), lambda i: (i, 0, 0, 0)),
        row(IDX_DIM), row(IDX_HEADS),
        pl.BlockSpec((tm // BLOCK, B_HEADS, BLOCK, B_HEAD_DIM), lambda i: (i, 0, 0, 0)),
        pl.BlockSpec((B_KV_HEADS, tm, B_HEAD_DIM), lambda i: (0, i, 0)),
        pl.BlockSpec((B_KV_HEADS, tm, 2 * B_HEAD_DIM), lambda i: (0, i, 0)),
        row(D_MODEL), row(D_MODEL),
    )
    return pl.pallas_call(
        _proj_kernel,
        grid=(n // tm,),
        in_specs=[row(D_MODEL),
                  pl.BlockSpec((D_MODEL, _C_END), lambda i: (0, 0)),
                  pl.BlockSpec((1, A_LAT), lambda i: (0, 0))],
        out_specs=out_specs,
        out_shape=out_shape,
        compiler_params=_cparams("arbitrary"),
        name="input_projection",
    )(x2, w_pack, kv_g)


DSA_UNROLL = 4


def _dsa_kernel(qidx_ref, kidx_ref, widx_ref, q_ref, ckv_ref, ckvt_ref, bias_ref, wuv_ref,
                tri_ref, ya_ref, sc_ref, neg_ref, lg_ref, o_ref, *, k_sel, search_chunk):
    i = pl.program_id(1)
    f32 = jnp.float32
    s_loc = lax.broadcasted_iota(jnp.int32, (BLOCK, BLOCK), 0)
    t_loc = lax.broadcasted_iota(jnp.int32, (BLOCK, BLOCK), 1)
    causal_diag = s_loc <= t_loc
    idx_scale = IDX_DIM ** -0.5
    w_scale = IDX_HEADS ** -0.5

    def rows(j):
        return pl.ds(pl.multiple_of(j * BLOCK, BLOCK), BLOCK)

    sc_ref[...] = jnp.full(sc_ref.shape, -jnp.inf, f32)
    wts = (widx_ref[0] * w_scale) * idx_scale
    qi = qidx_ref[0]

    def score_keys(j):
        s = lax.dot_general(kidx_ref[0, rows(j), :], qi, _NT, preferred_element_type=f32)
        r = jnp.maximum(s, 0.0) * wts
        acc = r[:, :BLOCK]
        for h in range(1, IDX_HEADS):
            acc = acc + r[:, h * BLOCK:(h + 1) * BLOCK]
        return acc

    def fold_blocks(lo, hi, fn, init, combine):
        n = hi - lo
        n_grp = lax.shift_right_logical(n, DSA_UNROLL.bit_length() - 1)

        def group(k, acc):
            j = lo + DSA_UNROLL * k
            vals = [fn(j + u) for u in range(DSA_UNROLL)]
            while len(vals) > 1:
                vals = [combine(vals[a], vals[a + 1]) for a in range(0, len(vals), 2)]
            return combine(acc, vals[0])

        acc = lax.fori_loop(0, n_grp, group, init)
        return lax.fori_loop(lo + DSA_UNROLL * n_grp, hi, lambda j, a: combine(a, fn(j)), acc)

    def far_scores(j):
        sc_ref[rows(j), :] = score_keys(j)
        return jnp.int32(0)

    fold_blocks(0, i, far_scores, jnp.int32(0), lambda a, b: a)
    sc_ref[rows(i), :] = jnp.where(causal_diag, score_keys(i), -jnp.inf)

    n_chunks = ((i + 1) * BLOCK + search_chunk - 1) // search_chunk
    n_acc = 64

    def count(cmp, cand):
        def body(c, cnt):
            base = pl.multiple_of(c * search_chunk, search_chunk)
            for r in range(search_chunk // n_acc):
                cnt = cnt + cmp(sc_ref[pl.ds(base + r * n_acc, n_acc), :], cand).astype(f32)
            return cnt
        cnt = lax.fori_loop(0, n_chunks, body, jnp.zeros((n_acc, BLOCK), f32))
        return jnp.sum(cnt, axis=0, keepdims=True)

    def ordered_to_float(u):
        k = u ^ INT_MIN
        return lax.bitcast_convert_type(k ^ ((k >> 31) & 0x7FFFFFFF), f32)

    def search_bit(b, prefix_u):
        cand_u = prefix_u | lax.shift_left(jnp.int32(1), 31 - b)
        cnt = count(jnp.greater_equal, ordered_to_float(cand_u))
        return jnp.where(cnt >= k_sel, cand_u, prefix_u)

    thr_u = lax.fori_loop(0, 32, search_bit, jnp.zeros((1, BLOCK), jnp.int32))
    thr = ordered_to_float(thr_u)
    thr = jnp.where(thr != thr, -jnp.inf, thr)
    c_ge = count(jnp.greater_equal, thr)
    c_gt = count(jnp.greater, thr)
    n_tie = k_sel - c_gt
    any_tie = jnp.max(((c_ge > k_sel) & (thr > -jnp.inf)).astype(f32)) > 0.0

    @pl.when(jnp.logical_not(any_tie))
    def _():
        def body(j, carry):
            neg_ref[rows(j), :] = jnp.where(sc_ref[rows(j), :] >= thr, 0.0, -jnp.inf)
            return carry
        lax.fori_loop(0, i, body, 0)
        neg_ref[rows(i), :] = jnp.where((sc_ref[rows(i), :] >= thr) & causal_diag, 0.0, -jnp.inf)

    @pl.when(any_tie)
    def _():
        tri = tri_ref[...]

        def block(j, tie_seen):
            kc = sc_ref[rows(j), :]
            eq = kc == thr
            tie_rank = jnp.dot(tri, eq.astype(jnp.bfloat16), preferred_element_type=f32) + tie_seen
            sel = (kc > thr) | (eq & (tie_rank <= n_tie))
            return sel, tie_seen + jnp.sum(eq.astype(f32), axis=0, keepdims=True)

        def body(j, tie_seen):
            sel, tie_seen = block(j, tie_seen)
            neg_ref[rows(j), :] = jnp.where(sel, 0.0, -jnp.inf)
            return tie_seen

        tie_seen = lax.fori_loop(0, i, body, jnp.zeros((1, BLOCK), f32))
        sel, _ = block(i, tie_seen)
        neg_ref[rows(i), :] = jnp.where(sel & causal_diag, 0.0, -jnp.inf)

    q_all = q_ref[0]

    def logits(j):
        lg = lax.dot_general(ckv_ref[0, rows(j), :], q_all, _NT, preferred_element_type=f32)
        return lg + jnp.concatenate([neg_ref[rows(j), :]] * A_HEADS, axis=1)

    def far_logits(j):
        lg = logits(j)
        lg_ref[rows(j), :] = lg
        return jnp.max(lg, axis=0, keepdims=True)

    def near_logits(j):
        lg = logits(j) + bias_ref[j - i + 1]
        lg_ref[rows(j), :] = lg
        return jnp.max(lg, axis=0, keepdims=True)

    n_far = jnp.maximum(i - 1, 0)
    m_run = fold_blocks(0, n_far, far_logits, jnp.full((1, A_HEADS * BLOCK), -jnp.inf, f32), jnp.maximum)
    m_run = fold_blocks(n_far, i + 1, near_logits, m_run, jnp.maximum)

    o_ref[...] = jnp.zeros_like(o_ref)

    def pv_rows(r):
        p = jnp.exp(lg_ref[r, :] - m_run).astype(jnp.bfloat16)
        o_ref[...] += jnp.dot(ckvt_ref[0, :, r], p, preferred_element_type=f32)

    def pv_group(k, carry):
        span = DSA_UNROLL * BLOCK
        pv_rows(pl.ds(pl.multiple_of(k * span, span), span))
        return carry

    def pv_single(j, carry):
        pv_rows(rows(j))
        return carry

    n_grp = lax.shift_right_logical(i + 1, DSA_UNROLL.bit_length() - 1)
    lax.fori_loop(0, n_grp, pv_group, 0)
    lax.fori_loop(DSA_UNROLL * n_grp, i + 1, pv_single, 0)

    o_t = o_ref[:A_LAT, :] / o_ref[A_LAT:A_LAT + 1, :]

    for h in range(A_HEADS):
        o_h = o_t[:, h * BLOCK:(h + 1) * BLOCK].T.astype(jnp.bfloat16)
        y_h = jnp.dot(o_h, wuv_ref[h], preferred_element_type=f32)
        ya_ref[:, h * A_HEAD_DIM:(h + 1) * A_HEAD_DIM] = y_h.astype(ya_ref.dtype)


def _dsa_attention(qidx_blocks, kidx, widx_l, q_blocks, ckv, ckv_tx, bias_near, wuv, bsz, seq):
    n_blk = seq // BLOCK
    k_sel = min(TOPK_MAX, seq // 4)
    search_chunk = min(512, seq)
    tri = jnp.asarray(np.tril(np.ones((BLOCK, BLOCK), np.float32)), jnp.bfloat16)
    kern = functools.partial(_dsa_kernel, k_sel=float(k_sel), search_chunk=search_chunk)
    blk = lambda b, i: (b * n_blk + i, 0, 0)
    return pl.pallas_call(
        kern,
        grid=(bsz, n_blk),
        in_specs=[
            pl.BlockSpec((1, IDX_HEADS * BLOCK, IDX_DIM), blk),
            pl.BlockSpec((1, seq, IDX_DIM), lambda b, i: (b, 0, 0)),
            pl.BlockSpec((1, 1, IDX_HEADS * BLOCK), blk),
            pl.BlockSpec((1, A_HEADS * BLOCK, A_LAT), blk),
            pl.BlockSpec((1, seq, A_LAT), lambda b, i: (b, 0, 0)),
            pl.BlockSpec((1, A_LAT + 8, seq), lambda b, i: (b, 0, 0)),
            pl.BlockSpec((2, BLOCK, A_HEADS * BLOCK), lambda b, i: (0, 0, 0)),
            pl.BlockSpec((A_HEADS, A_LAT, A_HEAD_DIM), lambda b, i: (0, 0, 0)),
            pl.BlockSpec((BLOCK, BLOCK), lambda b, i: (0, 0)),
        ],
        out_specs=pl.BlockSpec((BLOCK, A_HEADS * A_HEAD_DIM), lambda b, i: (b * n_blk + i, 0)),
        out_shape=jax.ShapeDtypeStruct((bsz * seq, A_HEADS * A_HEAD_DIM), jnp.bfloat16),
        scratch_shapes=[pltpu.VMEM((seq, BLOCK), jnp.float32),
                        pltpu.VMEM((seq, BLOCK), jnp.float32),
                        pltpu.VMEM((seq, A_HEADS * BLOCK), jnp.float32),
                        pltpu.VMEM((A_LAT + 8, A_HEADS * BLOCK), jnp.float32)],
        compiler_params=_cparams("arbitrary", "arbitrary"),
        name="dsa_attention",
    )(qidx_blocks, kidx, widx_l, q_blocks, ckv, ckv_tx, bias_near, wuv, tri)


SWA_QB = 4


def _swa_kernel(sink_ref, q_ref, kp_ref, kc_ref, vp_ref, vc_ref, bias_ref, yb_ref, *, qb):
    i = pl.program_id(1)
    f32 = jnp.float32
    grp = B_HEADS // B_KV_HEADS
    rows = grp * BLOCK
    row = lax.broadcasted_iota(jnp.int32, (rows, BLOCK), 0)
    t_loc = row & (BLOCK - 1)
    s_loc = lax.broadcasted_iota(jnp.int32, (rows, BLOCK), 1)
    d_prev = t_loc + BLOCK - s_loc
    d_cur = t_loc - s_loc
    m_cur = (d_cur >= 0) & (d_cur < WINDOW)
    head_of_row = lax.broadcasted_iota(jnp.int32, (rows, 1), 0) >> (BLOCK.bit_length() - 1)
    scale = B_HEAD_DIM ** -0.5
    for g in range(B_KV_HEADS):
        sink = jnp.zeros((rows, 1), f32)
        for hh in range(grp):
            sink = jnp.where(head_of_row == hh, sink_ref[g * grp + hh], sink)
        bias = bias_ref[g * grp:(g + 1) * grp].reshape(rows, 2 * BLOCK)
        for r in range(qb):
            has_prev = jnp.zeros_like(d_prev) + jnp.minimum(i * qb + r, 1)
            m_prev = (d_prev >= 0) & (d_prev < WINDOW) & (has_prev > 0)
            q = q_ref[r, g * grp:(g + 1) * grp].reshape(rows, B_HEAD_DIM)
            if r == 0:
                kp, vp = kp_ref[g], vp_ref[g]
            else:
                kp, vp = kc_ref[g, (r - 1) * BLOCK:r * BLOCK], vc_ref[g, (r - 1) * BLOCK:r * BLOCK]
            kc, vc = kc_ref[g, r * BLOCK:(r + 1) * BLOCK], vc_ref[g, r * BLOCK:(r + 1) * BLOCK]
            lp = lax.dot_general(q, kp, _NT, preferred_element_type=f32) * scale + bias[:, :BLOCK]
            lc = lax.dot_general(q, kc, _NT, preferred_element_type=f32) * scale + bias[:, BLOCK:]
            lp = jnp.where(m_prev, lp, -jnp.inf)
            lc = jnp.where(m_cur, lc, -jnp.inf)
            m = jnp.maximum(jnp.max(jnp.maximum(lp, lc), axis=1, keepdims=True), sink)
            pp = jnp.exp(lp - m).astype(jnp.bfloat16)
            pc = jnp.exp(lc - m).astype(jnp.bfloat16)
            ox = (jnp.dot(pp, vp, preferred_element_type=f32)
                  + jnp.dot(pc, vc, preferred_element_type=f32))
            den = ox[:, B_HEAD_DIM:B_HEAD_DIM + 1] + jnp.exp(sink - m)
            o = (ox[:, :B_HEAD_DIM] / den).astype(yb_ref.dtype)
            for hh in range(grp):
                h = g * grp + hh
                yb_ref[r * BLOCK:(r + 1) * BLOCK, h * B_HEAD_DIM:(h + 1) * B_HEAD_DIM] = (
                    o[hh * BLOCK:(hh + 1) * BLOCK])


def _swa_attention(sinks, q_blocks, kb, vbx, bias_b, bsz, seq):
    n_blk = seq // BLOCK
    qb = SWA_QB if n_blk % SWA_QB == 0 else 1
    n_step = n_blk // qb
    cur = lambda b, i: (0, b * n_step + i, 0)
    prev = lambda b, i: (0, b * n_blk + jnp.maximum(i * qb - 1, 0), 0)
    return pl.pallas_call(
        functools.partial(_swa_kernel, qb=qb),
        grid=(bsz, n_step),
        in_specs=[
            pl.BlockSpec(memory_space=pltpu.SMEM),
            pl.BlockSpec((qb, B_HEADS, BLOCK, B_HEAD_DIM), lambda b, i: (b * n_step + i, 0, 0, 0)),
            pl.BlockSpec((B_KV_HEADS, BLOCK, B_HEAD_DIM), prev),
            pl.BlockSpec((B_KV_HEADS, qb * BLOCK, B_HEAD_DIM), cur),
            pl.BlockSpec((B_KV_HEADS, BLOCK, 2 * B_HEAD_DIM), prev),
            pl.BlockSpec((B_KV_HEADS, qb * BLOCK, 2 * B_HEAD_DIM), cur),
            pl.BlockSpec((B_HEADS, BLOCK, 2 * BLOCK), lambda b, i: (0, 0, 0)),
        ],
        out_specs=pl.BlockSpec((qb * BLOCK, B_HEADS * B_HEAD_DIM), lambda b, i: (b * n_step + i, 0)),
        out_shape=jax.ShapeDtypeStruct((bsz * seq, B_HEADS * B_HEAD_DIM), jnp.bfloat16),
        compiler_params=_cparams("arbitrary", "arbitrary"),
        name="swa_attention",
    )(sinks, q_blocks, kb, kb, vbx, vbx, bias_b)


def _layer_norm(h, g, b):
    mu = jnp.mean(h, axis=-1, keepdims=True)
    d = h - mu
    var = jnp.mean(d * d, axis=-1, keepdims=True)
    return d * lax.rsqrt(var + LN_EPS) * g + b


def _merge_kernel(ya_ref, yb_ref, ga_ref, gb_ref, x_ref, wa_ref, wb_ref, wo_ref, g1_ref, b1_ref,
                  wrh_ref, wrl_ref, br_ref, tri_ref, x1_ref, rt_ref, cnt_ref, carry_ref):
    f32 = jnp.float32
    pa = jnp.dot(ya_ref[...], wa_ref[...], preferred_element_type=f32)
    pb = jnp.dot(yb_ref[...], wb_ref[...], preferred_element_type=f32)
    merged = ga_ref[...] * pa + gb_ref[...] * pb
    h = ALPHA * x_ref[...] + jnp.dot(merged.astype(jnp.bfloat16), wo_ref[...],
                                     preferred_element_type=f32)
    x1 = _layer_norm(h, g1_ref[...], b1_ref[...])
    _store_tile_rows(x1_ref, x1)

    x_hi = x1.astype(jnp.bfloat16)
    x_lo = (x1 - x_hi.astype(f32)).astype(jnp.bfloat16)
    lg = (jnp.dot(x_hi, wrh_ref[...], preferred_element_type=f32)
          + jnp.dot(x_hi, wrl_ref[...], preferred_element_type=f32)
          + jnp.dot(x_lo, wrh_ref[...], preferred_element_type=f32)) + br_ref[...]
    lane_i = lax.broadcasted_iota(jnp.int32, lg.shape, 1)
    lane = lane_i.astype(f32)
    big = jnp.float32(1 << 20)
    is_g = lane_i < N_GROUPS
    gl = jnp.where(is_g, lg, -jnp.inf)
    gmax = jnp.max(gl, axis=1, keepdims=True)
    g_sel = jnp.min(jnp.where(gl == gmax, lane, big), axis=1, keepdims=True)
    g_w = 1.0 / jnp.sum(jnp.where(is_g, jnp.exp(gl - gmax), 0.0), axis=1, keepdims=True)
    e_id = lane_i - N_GROUPS
    e_grp = (e_id >> 3).astype(f32)
    in_grp = (e_id >= 0) & (e_id < N_EXPERTS) & (e_grp == g_sel)
    el = jnp.where(in_grp, lg, -jnp.inf)
    emax = jnp.max(el, axis=1, keepdims=True)
    ee = jnp.where(in_grp, jnp.exp(el - emax), 0.0)
    ep = ee / jnp.sum(ee, axis=1, keepdims=True)
    epm = jnp.where(in_grp, ep, -1.0)
    p1 = jnp.max(epm, axis=1, keepdims=True)
    i1 = jnp.min(jnp.where(epm == p1, lane, big), axis=1, keepdims=True)
    epm2 = jnp.where(lane == i1, -1.0, epm)
    p2 = jnp.max(epm2, axis=1, keepdims=True)
    i2 = jnp.min(jnp.where(epm2 == p2, lane, big), axis=1, keepdims=True)
    psum = p1 + p2
    w1 = g_w * p1 / psum
    w2 = g_w * p2 / psum
    @pl.when(pl.program_id(0) == 0)
    def _():
        carry_ref[...] = jnp.zeros_like(carry_ref)

    oh1 = lane + N_GROUPS == i1
    oh2 = lane + N_GROUPS == i2
    oh = jnp.where(oh1 | oh2, 1.0, 0.0)
    prefix = jnp.dot(tri_ref[...], oh.astype(jnp.bfloat16), preferred_element_type=f32) + carry_ref[...]
    rank1 = jnp.sum(jnp.where(oh1, prefix, 0.0), axis=1, keepdims=True)
    rank2 = jnp.sum(jnp.where(oh2, prefix, 0.0), axis=1, keepdims=True)
    carry_ref[...] += jnp.sum(oh, axis=0, keepdims=True)
    cnt_ref[...] = carry_ref[...]
    rec = (i1 - N_GROUPS, i2 - N_GROUPS, w1, w2, rank1, rank2)
    rt = jnp.zeros(lg.shape, f32)
    for k, v in enumerate(rec):
        rt = jnp.where(lane_i == k, v, rt)
    rt_ref[...] = rt


def _merge_route(ya, yb, ga, gb, x2, wa, wb, wo, g1, b1, wr, br, tm):
    wr_hi = wr.astype(jnp.bfloat16)
    wr_lo = (wr - wr_hi.astype(jnp.float32)).astype(jnp.bfloat16)
    n = x2.shape[0]
    row = lambda w: pl.BlockSpec((tm, w), lambda i: (i, 0))
    full = lambda r, c: pl.BlockSpec((r, c), lambda i: (0, 0))
    aw, bw = A_HEADS * A_HEAD_DIM, B_HEADS * B_HEAD_DIM
    tri = jnp.asarray(np.tril(np.ones((tm, tm), np.float32), -1), jnp.bfloat16)
    return pl.pallas_call(
        _merge_kernel,
        grid=(n // tm,),
        in_specs=[row(aw), row(bw), row(D_MODEL), row(D_MODEL), row(D_MODEL),
                  full(aw, D_MODEL), full(bw, D_MODEL), full(D_MODEL, D_MODEL),
                  full(1, D_MODEL), full(1, D_MODEL), full(D_MODEL, LANES), full(D_MODEL, LANES),
                  full(1, LANES), full(tm, tm)],
        out_specs=(pl.BlockSpec((tm * ROW_TILE, LANES), lambda i: (i, 0)), row(LANES), full(1, LANES)),
        out_shape=(jax.ShapeDtypeStruct((n * ROW_TILE, LANES), jnp.float32),
                   jax.ShapeDtypeStruct((n, LANES), jnp.float32),
                   jax.ShapeDtypeStruct((1, LANES), jnp.float32)),
        scratch_shapes=[pltpu.VMEM((1, LANES), jnp.float32)],
        compiler_params=_cparams("arbitrary"),
        name="merge_route",
    )(ya, yb, ga, gb, x2, wa, wb, wo, g1, b1, wr_hi, wr_lo, br, tri)


ROW_TILE = D_MODEL // LANES


def _store_tile_rows(ref, val):
    rows = val.shape[0]
    for j in range(ROW_TILE):
        ref[pl.ds(j, rows, stride=ROW_TILE), :] = val[:, j * LANES:(j + 1) * LANES]


def _load_tile_rows(ref, rows):
    return jnp.concatenate([ref[pl.ds(j, rows, stride=ROW_TILE), :] for j in range(ROW_TILE)], axis=1)


DMA_UNROLL = 8


def _dispatch_kernel(pos_ref, x_ref, xs_hbm, sem, *, tm):
    def issue(k, carry):
        for u in range(DMA_UNROLL):
            r = k * DMA_UNROLL + u
            src = x_ref.at[pl.ds(pl.multiple_of(r * ROW_TILE, ROW_TILE), ROW_TILE)]
            for slot in range(2):
                p = pos_ref[0, 0, 2 * r + slot]
                dst = xs_hbm.at[pl.ds(pl.multiple_of(p * ROW_TILE, ROW_TILE), ROW_TILE)]
                pltpu.make_async_copy(src, dst, sem).start(priority=slot)
        return carry

    lax.fori_loop(0, tm // DMA_UNROLL, issue, 0)
    for slot in range(2):
        pltpu.make_async_copy(x_ref, xs_hbm.at[pl.ds(0, tm * ROW_TILE)], sem).wait()


def _moe_dispatch(pos, x1r, n_rows, tm):
    n = x1r.shape[0] // ROW_TILE
    return pl.pallas_call(
        functools.partial(_dispatch_kernel, tm=tm),
        grid=(n // tm,),
        in_specs=[pl.BlockSpec((1, 1, 2 * tm), lambda i: (i, 0, 0), memory_space=pltpu.SMEM),
                  pl.BlockSpec((tm * ROW_TILE, LANES), lambda i: (i, 0))],
        out_specs=pl.BlockSpec(memory_space=pl.ANY),
        out_shape=jax.ShapeDtypeStruct((n_rows * ROW_TILE, LANES), jnp.float32),
        scratch_shapes=[pltpu.SemaphoreType.DMA],
        compiler_params=_cparams("arbitrary"),
        name="moe_dispatch",
    )(pos, x1r)


def _expert_kernel(tile_ref, exp_ref, lo_ref, hi_ref, nit_ref, xs_ref, wg_ref, wu_ref, wd_ref, ys_ref,
                   *, tm):
    f32 = jnp.float32
    w = pl.program_id(0)

    @pl.when(w < nit_ref[0])
    def _():
        xb = _load_tile_rows(xs_ref, tm).astype(jnp.bfloat16)
        g = jnp.dot(xb, wg_ref[0], preferred_element_type=f32)
        u = jnp.dot(xb, wu_ref[0], preferred_element_type=f32)
        hmid = (g * jax.nn.sigmoid(g) * u).astype(jnp.bfloat16)
        y = jnp.dot(hmid, wd_ref[0], preferred_element_type=f32)
        row = tile_ref[w] * tm + lax.broadcasted_iota(jnp.int32, (tm, 1), 0)
        mine = (row >= lo_ref[w]) & (row < hi_ref[w])
        first_visit = jnp.logical_or(w == 0, tile_ref[jnp.maximum(w - 1, 0)] != tile_ref[w])

        @pl.when(first_visit)
        def _():
            _store_tile_rows(ys_ref, jnp.where(mine, y, 0.0))

        @pl.when(jnp.logical_not(first_visit))
        def _():
            _store_tile_rows(ys_ref, jnp.where(mine, y, _load_tile_rows(ys_ref, tm)))


def _moe_experts(items, xs, wg, wu, wd, tm):
    n_items = items[0].shape[0]
    grid_spec = pltpu.PrefetchScalarGridSpec(
        num_scalar_prefetch=5,
        grid=(n_items,),
        in_specs=[
            pl.BlockSpec((tm * ROW_TILE, LANES), lambda w, t, e, lo, hi, n: (t[w], 0)),
            pl.BlockSpec((1, D_MODEL, D_EXPERT), lambda w, t, e, lo, hi, n: (e[w], 0, 0)),
            pl.BlockSpec((1, D_MODEL, D_EXPERT), lambda w, t, e, lo, hi, n: (e[w], 0, 0)),
            pl.BlockSpec((1, D_EXPERT, D_MODEL), lambda w, t, e, lo, hi, n: (e[w], 0, 0)),
        ],
        out_specs=pl.BlockSpec((tm * ROW_TILE, LANES), lambda w, t, e, lo, hi, n: (t[w], 0)),
    )
    return pl.pallas_call(
        functools.partial(_expert_kernel, tm=tm),
        grid_spec=grid_spec,
        out_shape=jax.ShapeDtypeStruct(xs.shape, jnp.float32),
        compiler_params=_cparams("arbitrary"),
        name="moe_experts",
    )(*items, xs, wg, wu, wd)


def _combine_kernel(pos_ref, rt_ref, x_ref, ys_hbm, g2_ref, b2_ref, out_ref, buf_a, buf_b, sem, *, tm):
    bufs = (buf_a, buf_b)

    def issue(k, carry):
        for u in range(DMA_UNROLL):
            r = k * DMA_UNROLL + u
            for slot in range(2):
                p = pos_ref[0, 0, 2 * r + slot]
                src = ys_hbm.at[pl.ds(pl.multiple_of(p * ROW_TILE, ROW_TILE), ROW_TILE)]
                dst = bufs[slot].at[pl.ds(pl.multiple_of(r * ROW_TILE, ROW_TILE), ROW_TILE)]
                pltpu.make_async_copy(src, dst, sem).start(priority=slot)
        return carry

    lax.fori_loop(0, tm // DMA_UNROLL, issue, 0)
    for slot in range(2):
        pltpu.make_async_copy(ys_hbm.at[pl.ds(0, tm * ROW_TILE)], bufs[slot], sem).wait()
    rt = rt_ref[...]
    w1 = rt[:, 2:3]
    w2 = rt[:, 3:4]
    ffn = w1 * _load_tile_rows(buf_a, tm) + w2 * _load_tile_rows(buf_b, tm)
    out_ref[...] = _layer_norm(ALPHA * _load_tile_rows(x_ref, tm) + ffn, g2_ref[...], b2_ref[...])


def _moe_combine(pos, rt, x1r, ys, g2, b2, tm):
    n = rt.shape[0]
    return pl.pallas_call(
        functools.partial(_combine_kernel, tm=tm),
        grid=(n // tm,),
        in_specs=[pl.BlockSpec((1, 1, 2 * tm), lambda i: (i, 0, 0), memory_space=pltpu.SMEM),
                  pl.BlockSpec((tm, LANES), lambda i: (i, 0)),
                  pl.BlockSpec((tm * ROW_TILE, LANES), lambda i: (i, 0)),
                  pl.BlockSpec(memory_space=pl.ANY),
                  pl.BlockSpec((1, D_MODEL), lambda i: (0, 0)),
                  pl.BlockSpec((1, D_MODEL), lambda i: (0, 0))],
        out_specs=pl.BlockSpec((tm, D_MODEL), lambda i: (i, 0)),
        out_shape=jax.ShapeDtypeStruct((n, D_MODEL), jnp.float32),
        scratch_shapes=[pltpu.VMEM((tm * ROW_TILE, LANES), jnp.float32),
                        pltpu.VMEM((tm * ROW_TILE, LANES), jnp.float32),
                        pltpu.SemaphoreType.DMA],
        compiler_params=_cparams("arbitrary"),
        name="moe_combine",
    )(pos, rt, x1r, ys, g2, b2)


MOE_TM = 256


def _moe_plan(rt, counts, n_tiles):
    i32 = jnp.int32
    cnt = counts[0, :N_EXPERTS].astype(i32)
    end = jnp.cumsum(cnt)
    start = end - cnt
    e12 = rt[:, 0:2].astype(i32)
    pos = start[e12] + rt[:, 4:6].astype(i32)
    first_t = start // MOE_TM
    items_e = jnp.where(cnt > 0, (end - 1) // MOE_TM - first_t + 1, 0)
    item_end = jnp.cumsum(items_e)
    n_items = item_end[-1]
    w = jnp.minimum(jnp.arange(n_tiles + N_EXPERTS - 1, dtype=i32), n_items - 1)
    e_w = jnp.minimum(jnp.sum((item_end[None, :] <= w[:, None]).astype(i32), axis=1), N_EXPERTS - 1)
    t_w = first_t[e_w] + (w - (item_end - items_e)[e_w])
    lo_w = jnp.maximum(start[e_w], t_w * MOE_TM)
    hi_w = jnp.minimum(end[e_w], (t_w + 1) * MOE_TM)
    return pos, (t_w.astype(i32), e_w, lo_w.astype(i32), hi_w.astype(i32), n_items.reshape(1).astype(i32))


def _pick_tile(n, pref):
    t = min(pref, n)
    while n % t:
        t //= 2
    return t


def kernel(x, w_in, kv_norm_g, w_uv, w_branch_a, sinks, w_branch_b, w_out, rel_bias, ln1_g, ln1_b,
           w_group, b_group, w_router, b_router, w_gate, w_up, w_down, ln2_g, ln2_b):
    bsz, seq, d = x.shape
    assert d == D_MODEL and seq % BLOCK == 0 and w_in.shape[0] == DEPTH == 1
    n = bsz * seq
    n_blk = seq // BLOCK
    bf, f32 = jnp.bfloat16, jnp.float32
    x2 = x.reshape(n, d)

    r = np.arange(BLOCK, dtype=np.int32)
    d_prev = r[None, :] + BLOCK - r[:, None]
    d_cur = r[None, :] - r[:, None]
    bkt_a = jnp.asarray(_t5_bucket_np(np.stack([d_prev, d_cur])))
    s2 = np.arange(2 * BLOCK, dtype=np.int32)
    bkt_b = jnp.asarray(_t5_bucket_np(r[:, None] + BLOCK - s2[None, :])[None])
    assert int(_t5_bucket_np(np.arange(BLOCK + 1, max(seq, BLOCK + 2))).min()) == N_BUCKETS - 1
    tab_t = rel_bias.astype(f32).T
    bias_a = _bias_tiles(tab_t, bkt_a, 0, A_HEADS, True, minus_far=True)
    bias_b = _bias_tiles(tab_t, bkt_b, A_HEADS, B_HEADS, False)

    w0 = w_in[0]
    w_pack = jnp.concatenate(
        [w0[:, :_RAW_KW_END], jnp.zeros((d, _C_QB - _RAW_KW_END), w0.dtype), w0[:, _RAW_KW_END:]],
        axis=1).astype(bf)
    (q_blocks, ckv, qidx_blocks, kidx, widx, qb_blocks, kb, vbx, ga, gb) = _input_projection(
        x2, w_pack, kv_norm_g[0].reshape(1, A_LAT).astype(f32), _pick_tile(n, 512))

    ckv3 = ckv.reshape(bsz, seq, A_LAT)
    ckv_tx = jnp.concatenate([jnp.swapaxes(ckv3, 1, 2), jnp.ones((bsz, 8, seq), bf)], axis=1)
    widx_l = jnp.swapaxes(widx.reshape(bsz * n_blk, BLOCK, IDX_HEADS), 1, 2).reshape(
        bsz * n_blk, 1, IDX_HEADS * BLOCK)
    ya = _dsa_attention(qidx_blocks.reshape(bsz * n_blk, IDX_HEADS * BLOCK, IDX_DIM),
                        kidx.reshape(bsz, seq, IDX_DIM), widx_l,
                        q_blocks.reshape(bsz * n_blk, A_HEADS * BLOCK, A_LAT),
                        ckv3, ckv_tx, bias_a, w_uv[0].astype(bf), bsz, seq)

    yb = _swa_attention(sinks[0].astype(f32), qb_blocks, kb, vbx, bias_b, bsz, seq)

    w_route = jnp.concatenate(
        [w_group[0], w_router[0], jnp.zeros((d, LANES - N_GROUPS - N_EXPERTS), f32)], axis=1).astype(f32)
    b_route = jnp.concatenate(
        [b_group[0], b_router[0], jnp.zeros((LANES - N_GROUPS - N_EXPERTS,), f32)]).reshape(1, LANES).astype(f32)
    x1r, rt, counts = _merge_route(
        ya, yb, ga, gb, x2, w_branch_a[0].astype(bf), w_branch_b[0].astype(bf), w_out[0].astype(bf),
        ln1_g[0].reshape(1, d).astype(f32), ln1_b[0].reshape(1, d).astype(f32), w_route, b_route,
        _pick_tile(n, 512))

    assert (2 * n) % MOE_TM == 0
    pos, items = _moe_plan(rt, counts, 2 * n // MOE_TM)
    tm_io = _pick_tile(n, 512)
    pos_blocks = pos.reshape(n // tm_io, 1, 2 * tm_io)
    xs = _moe_dispatch(pos_blocks, x1r, 2 * n, tm_io)
    ys = _moe_experts(items, xs, w_gate[0].astype(bf), w_up[0].astype(bf), w_down[0].astype(bf), MOE_TM)
    out = _moe_combine(pos_blocks, rt, x1r, ys, ln2_g[0].reshape(1, d).astype(f32),
                       ln2_b[0].reshape(1, d).astype(f32), tm_io)
    return out.reshape(bsz, seq, d)
```

```python
import functools
import math

import numpy as np
import jax
import jax.numpy as jnp
from jax import lax
from jax.experimental import pallas as pl
from jax.experimental.pallas import tpu as pltpu

D_MODEL = 1024
A_HEADS = 8
A_LAT = 128
A_HEAD_DIM = 64
IDX_HEADS = 8
IDX_DIM = 64
TOPK_MAX = 256
B_HEADS = 8
B_KV_HEADS = 2
B_HEAD_DIM = 64
WINDOW = 128
BLOCK = 128
N_BUCKETS = 32
MAX_DISTANCE = 128
N_GROUPS = 4
EXPERTS_PER_GROUP = 8
N_EXPERTS = 32
D_EXPERT = 256
DEPTH = 1
ALPHA = (2 * DEPTH) ** 0.25
LN_EPS = 1e-5
RMS_EPS = 1e-6

LANES = 128
INT_MIN = -(2 ** 31)
VMEM_LIMIT = 48 * 1024 * 1024

_NT = (((1,), (1,)), ((), ()))


def _cparams(*sem):
    return pltpu.CompilerParams(dimension_semantics=sem, vmem_limit_bytes=VMEM_LIMIT)


def _t5_bucket_np(dist):
    f32 = np.float32
    n = np.maximum(dist, 0)
    max_exact = N_BUCKETS // 2
    nf = np.maximum(n, 1).astype(f32)
    large = max_exact + (np.log(nf / f32(max_exact)) / f32(math.log(MAX_DISTANCE / max_exact))
                         * f32(N_BUCKETS - max_exact)).astype(np.int32)
    large = np.minimum(large, N_BUCKETS - 1)
    return np.where(n < max_exact, n, large).astype(np.int32)


def _bias_kernel(tab_ref, bkt_ref, out_ref, *, head0, minus_far):
    h = pl.program_id(0) + head0
    bkt = bkt_ref[...]
    far = tab_ref[h, N_BUCKETS - 1]
    acc = jnp.full(bkt.shape, far, jnp.float32)
    for b in range(N_BUCKETS - 1):
        acc = jnp.where(bkt == b, tab_ref[h, b], acc)
    out_ref[...] = acc - far if minus_far else acc


def _bias_tiles(tab_t, bkt, head0, n_heads, lane_major, minus_far=False):
    g, r, c = bkt.shape
    if lane_major:
        out_shape = jax.ShapeDtypeStruct((g, r, n_heads * c), jnp.float32)
        out_spec = pl.BlockSpec((g, r, c), lambda h: (0, 0, h))
    else:
        out_shape = jax.ShapeDtypeStruct((n_heads * g, r, c), jnp.float32)
        out_spec = pl.BlockSpec((g, r, c), lambda h: (h, 0, 0))
    return pl.pallas_call(
        functools.partial(_bias_kernel, head0=head0, minus_far=minus_far),
        grid=(n_heads,),
        in_specs=[pl.BlockSpec(memory_space=pltpu.SMEM),
                  pl.BlockSpec((g, r, c), lambda h: (0, 0, 0))],
        out_specs=out_spec,
        out_shape=out_shape,
        compiler_params=_cparams("arbitrary"),
        name="bias_tiles",
    )(tab_t, bkt)


_C_QLAT = 0
_C_CKV = _C_QLAT + A_HEADS * A_LAT
_C_QIDX = _C_CKV + A_LAT
_C_KW = _C_QIDX + IDX_HEADS * IDX_DIM
_C_QB = _C_KW + LANES
_C_KB = _C_QB + B_HEADS * B_HEAD_DIM
_C_VB = _C_KB + B_KV_HEADS * B_HEAD_DIM
_C_GA = _C_VB + B_KV_HEADS * B_HEAD_DIM
_C_GB = _C_GA + D_MODEL
_C_END = _C_GB + D_MODEL
_RAW_KW_END = _C_KW + IDX_DIM + IDX_HEADS


def _proj_kernel(x_ref, w_ref, g_ref, q_ref, ckv_ref, ckvt_ref, qidx_ref, kidx_ref, widx_ref,
                 qb_ref, kb_ref, vb_ref, ga_ref, gb_ref):
    xb = x_ref[...].astype(jnp.bfloat16)
    tm = xb.shape[0]

    def seg(lo, hi):
        return jnp.dot(xb, w_ref[:, lo:hi], preferred_element_type=jnp.float32)

    for h in range(A_HEADS):
        qh = (seg(_C_QLAT + h * A_LAT, _C_QLAT + (h + 1) * A_LAT) * (A_LAT ** -0.5)).astype(jnp.bfloat16)
        for r in range(tm // BLOCK):
            q_ref[r, h] = qh[r * BLOCK:(r + 1) * BLOCK]
    c = seg(_C_CKV, _C_QIDX)
    ms = jnp.mean(c * c, axis=-1, keepdims=True)
    cn = c * lax.rsqrt(ms + RMS_EPS) * g_ref[...]
    ckv_ref[...] = cn.astype(jnp.bfloat16)
    for r in range(tm // BLOCK):
        ckvt_ref[0, :A_LAT, r * BLOCK:(r + 1) * BLOCK] = cn[r * BLOCK:(r + 1) * BLOCK].T.astype(jnp.bfloat16)
    ckvt_ref[0, A_LAT:, :] = jnp.ones((8, tm), jnp.bfloat16)
    qif = seg(_C_QIDX, _C_KW).astype(jnp.bfloat16)
    for h in range(IDX_HEADS):
        for r in range(tm // BLOCK):
            qidx_ref[r, h] = qif[r * BLOCK:(r + 1) * BLOCK, h * IDX_DIM:(h + 1) * IDX_DIM]
    kw = seg(_C_KW, _C_QB)
    kidx_ref[...] = kw[:, :IDX_DIM].astype(jnp.bfloat16)
    widx_ref[...] = kw[:, IDX_DIM:IDX_DIM + IDX_HEADS]
    qbf = seg(_C_QB, _C_KB).astype(jnp.bfloat16)
    for h in range(B_HEADS):
        for r in range(tm // BLOCK):
            qb_ref[r, h] = qbf[r * BLOCK:(r + 1) * BLOCK, h * B_HEAD_DIM:(h + 1) * B_HEAD_DIM]
    kbf = seg(_C_KB, _C_VB).astype(jnp.bfloat16)
    vbf = seg(_C_VB, _C_GA).astype(jnp.bfloat16)
    for g in range(B_KV_HEADS):
        kb_ref[g] = kbf[:, g * B_HEAD_DIM:(g + 1) * B_HEAD_DIM]
        vb_ref[g, :, :B_HEAD_DIM] = vbf[:, g * B_HEAD_DIM:(g + 1) * B_HEAD_DIM]
        vb_ref[g, :, B_HEAD_DIM:] = jnp.ones((tm, B_HEAD_DIM), jnp.bfloat16)
    ga_ref[...] = jax.nn.sigmoid(seg(_C_GA, _C_GB))
    gb_ref[...] = jax.nn.sigmoid(seg(_C_GB, _C_END))


def _input_projection(x2, w_pack, kv_g, tm, seq):
    n = x2.shape[0]
    assert seq % tm == 0
    bf, f32 = jnp.bfloat16, jnp.float32
    row = lambda w: pl.BlockSpec((tm, w), lambda i: (i, 0))
    out_shape = (
        jax.ShapeDtypeStruct((n // BLOCK, A_HEADS, BLOCK, A_LAT), bf),
        jax.ShapeDtypeStruct((n, A_LAT), bf),
        jax.ShapeDtypeStruct((n // seq, A_LAT + 8, seq), bf),
        jax.ShapeDtypeStruct((n // BLOCK, IDX_HEADS, BLOCK, IDX_DIM), bf),
        jax.ShapeDtypeStruct((n, IDX_DIM), bf),
        jax.ShapeDtypeStruct((n, IDX_HEADS), f32),
        jax.ShapeDtypeStruct((n // BLOCK, B_HEADS, BLOCK, B_HEAD_DIM), bf),
        jax.ShapeDtypeStruct((B_KV_HEADS, n, B_HEAD_DIM), bf),
        jax.ShapeDtypeStruct((B_KV_HEADS, n, 2 * B_HEAD_DIM), bf),
        jax.ShapeDtypeStruct((n, D_MODEL), f32),
        jax.ShapeDtypeStruct((n, D_MODEL), f32),
    )
    out_specs = (
        pl.BlockSpec((tm // BLOCK, A_HEADS, BLOCK, A_LAT), lambda i: (i, 0, 0, 0)),
        row(A_LAT),
        pl.BlockSpec((1, A_LAT + 8, tm), lambda i: (i // (seq // tm), 0, i % (seq // tm))),
        pl.BlockSpec((tm // BLOCK, IDX_HEADS, BLOCK, IDX_DIM), lambda i: (i, 0, 0, 0)),
        row(IDX_DIM), row(IDX_HEADS),
        pl.BlockSpec((tm // BLOCK, B_HEADS, BLOCK, B_HEAD_DIM), lambda i: (i, 0, 0, 0)),
        pl.BlockSpec((B_KV_HEADS, tm, B_HEAD_DIM), lambda i: (0, i, 0)),
        pl.BlockSpec((B_KV_HEADS, tm, 2 * B_HEAD_DIM), lambda i: (0, i, 0)),
        row(D_MODEL), row(D_MODEL),
    )
    return pl.pallas_call(
        _proj_kernel,
        grid=(n // tm,),
        in_specs=[row(D_MODEL),
                  pl.BlockSpec((D_MODEL, _C_END), lambda i: (0, 0)),
                  pl.BlockSpec((1, A_LAT), lambda i: (0, 0))],
        out_specs=out_specs,
        out_shape=out_shape,
        compiler_params=_cparams("arbitrary"),
        name="input_projection",
    )(x2, w_pack, kv_g)


DSA_UNROLL = 4


def _dsa_kernel(qidx_ref, kidx_ref, widx_ref, q_ref, ckv_ref, ckvt_ref, bias_ref, wuv_ref,
                tri_ref, ya_ref, sc_ref, neg_ref, lg_ref, o_ref, *, k_sel, search_chunk):
    i = pl.program_id(1)
    f32 = jnp.float32
    s_loc = lax.broadcasted_iota(jnp.int32, (BLOCK, BLOCK), 0)
    t_loc = lax.broadcasted_iota(jnp.int32, (BLOCK, BLOCK), 1)
    causal_diag = s_loc <= t_loc
    idx_scale = IDX_DIM ** -0.5
    w_scale = IDX_HEADS ** -0.5

    def rows(j):
        return pl.ds(pl.multiple_of(j * BLOCK, BLOCK), BLOCK)

    sc_ref[...] = jnp.full(sc_ref.shape, -jnp.inf, f32)
    wts = (widx_ref[0] * w_scale) * idx_scale
    qi = qidx_ref[0]

    def score_keys(j):
        s = lax.dot_general(kidx_ref[0, rows(j), :], qi, _NT, preferred_element_type=f32)
        r = jnp.maximum(s, 0.0) * wts
        acc = r[:, :BLOCK]
        for h in range(1, IDX_HEADS):
            acc = acc + r[:, h * BLOCK:(h + 1) * BLOCK]
        return acc

    def fold_blocks(lo, hi, fn, init, combine):
        n = hi - lo
        n_grp = lax.shift_right_logical(n, DSA_UNROLL.bit_length() - 1)

        def group(k, acc):
            j = lo + DSA_UNROLL * k
            vals = [fn(j + u) for u in range(DSA_UNROLL)]
            while len(vals) > 1:
                vals = [combine(vals[a], vals[a + 1]) for a in range(0, len(vals), 2)]
            return combine(acc, vals[0])

        acc = lax.fori_loop(0, n_grp, group, init)
        return lax.fori_loop(lo + DSA_UNROLL * n_grp, hi, lambda j, a: combine(a, fn(j)), acc)

    def far_scores(j):
        sc_ref[rows(j), :] = score_keys(j)
        return jnp.int32(0)

    fold_blocks(0, i, far_scores, jnp.int32(0), lambda a, b: a)
    sc_ref[rows(i), :] = jnp.where(causal_diag, score_keys(i), -jnp.inf)

    n_chunks = ((i + 1) * BLOCK + search_chunk - 1) // search_chunk
    n_acc = 64

    def count(cmp, cand):
        def body(c, cnt):
            base = pl.multiple_of(c * search_chunk, search_chunk)
            for r in range(search_chunk // n_acc):
                cnt = cnt + cmp(sc_ref[pl.ds(base + r * n_acc, n_acc), :], cand).astype(f32)
            return cnt
        cnt = lax.fori_loop(0, n_chunks, body, jnp.zeros((n_acc, BLOCK), f32))
        return jnp.sum(cnt, axis=0, keepdims=True)

    def ordered_to_float(u):
        k = u ^ INT_MIN
        return lax.bitcast_convert_type(k ^ ((k >> 31) & 0x7FFFFFFF), f32)

    def search_bit(b, prefix_u):
        cand_u = prefix_u | lax.shift_left(jnp.int32(1), 31 - b)
        cnt = count(jnp.greater_equal, ordered_to_float(cand_u))
        return jnp.where(cnt >= k_sel, cand_u, prefix_u)

    thr_u = lax.fori_loop(0, 32, search_bit, jnp.zeros((1, BLOCK), jnp.int32))
    thr = ordered_to_float(thr_u)
    thr = jnp.where(thr != thr, -jnp.inf, thr)
    c_ge = count(jnp.greater_equal, thr)
    c_gt = count(jnp.greater, thr)
    n_tie = k_sel - c_gt
    any_tie = jnp.max(((c_ge > k_sel) & (thr > -jnp.inf)).astype(f32)) > 0.0

    @pl.when(jnp.logical_not(any_tie))
    def _():
        def body(j, carry):
            neg_ref[rows(j), :] = jnp.where(sc_ref[rows(j), :] >= thr, 0.0, -jnp.inf)
            return carry
        lax.fori_loop(0, i, body, 0)
        neg_ref[rows(i), :] = jnp.where((sc_ref[rows(i), :] >= thr) & causal_diag, 0.0, -jnp.inf)

    @pl.when(any_tie)
    def _():
        tri = tri_ref[...]

        def block(j, tie_seen):
            kc = sc_ref[rows(j), :]
            eq = kc == thr
            tie_rank = jnp.dot(tri, eq.astype(jnp.bfloat16), preferred_element_type=f32) + tie_seen
            sel = (kc > thr) | (eq & (tie_rank <= n_tie))
            return sel, tie_seen + jnp.sum(eq.astype(f32), axis=0, keepdims=True)

        def body(j, tie_seen):
            sel, tie_seen = block(j, tie_seen)
            neg_ref[rows(j), :] = jnp.where(sel, 0.0, -jnp.inf)
            return tie_seen

        tie_seen = lax.fori_loop(0, i, body, jnp.zeros((1, BLOCK), f32))
        sel, _ = block(i, tie_seen)
        neg_ref[rows(i), :] = jnp.where(sel & causal_diag, 0.0, -jnp.inf)

    q_all = q_ref[0]

    def logits(j):
        lg = lax.dot_general(ckv_ref[0, rows(j), :], q_all, _NT, preferred_element_type=f32)
        return lg + jnp.concatenate([neg_ref[rows(j), :]] * A_HEADS, axis=1)

    def far_logits(j):
        lg = logits(j)
        lg_ref[rows(j), :] = lg
        return jnp.max(lg, axis=0, keepdims=True)

    def near_logits(j):
        lg = logits(j) + bias_ref[j - i + 1]
        lg_ref[rows(j), :] = lg
        return jnp.max(lg, axis=0, keepdims=True)

    n_far = jnp.maximum(i - 1, 0)
    m_run = fold_blocks(0, n_far, far_logits, jnp.full((1, A_HEADS * BLOCK), -jnp.inf, f32), jnp.maximum)
    m_run = fold_blocks(n_far, i + 1, near_logits, m_run, jnp.maximum)

    o_ref[...] = jnp.zeros_like(o_ref)

    def pv_rows(r):
        p = jnp.exp(lg_ref[r, :] - m_run).astype(jnp.bfloat16)
        o_ref[...] += jnp.dot(ckvt_ref[0, :, r], p, preferred_element_type=f32)

    def pv_group(k, carry):
        span = DSA_UNROLL * BLOCK
        pv_rows(pl.ds(pl.multiple_of(k * span, span), span))
        return carry

    def pv_single(j, carry):
        pv_rows(rows(j))
        return carry

    n_grp = lax.shift_right_logical(i + 1, DSA_UNROLL.bit_length() - 1)
    lax.fori_loop(0, n_grp, pv_group, 0)
    lax.fori_loop(DSA_UNROLL * n_grp, i + 1, pv_single, 0)

    o_t = o_ref[:A_LAT, :] / o_ref[A_LAT:A_LAT + 1, :]

    for h in range(A_HEADS):
        o_h = o_t[:, h * BLOCK:(h + 1) * BLOCK].T.astype(jnp.bfloat16)
        y_h = jnp.dot(o_h, wuv_ref[h], preferred_element_type=f32)
        ya_ref[:, h * A_HEAD_DIM:(h + 1) * A_HEAD_DIM] = y_h.astype(ya_ref.dtype)


def _dsa_attention(qidx_blocks, kidx, widx_l, q_blocks, ckv, ckv_tx, bias_near, wuv, bsz, seq):
    n_blk = seq // BLOCK
    k_sel = min(TOPK_MAX, seq // 4)
    search_chunk = min(512, seq)
    tri = jnp.asarray(np.tril(np.ones((BLOCK, BLOCK), np.float32)), jnp.bfloat16)
    kern = functools.partial(_dsa_kernel, k_sel=float(k_sel), search_chunk=search_chunk)
    blk = lambda b, i: (b * n_blk + i, 0, 0)
    return pl.pallas_call(
        kern,
        grid=(bsz, n_blk),
        in_specs=[
            pl.BlockSpec((1, IDX_HEADS * BLOCK, IDX_DIM), blk),
            pl.BlockSpec((1, seq, IDX_DIM), lambda b, i: (b, 0, 0)),
            pl.BlockSpec((1, 1, IDX_HEADS * BLOCK), blk),
            pl.BlockSpec((1, A_HEADS * BLOCK, A_LAT), blk),
            pl.BlockSpec((1, seq, A_LAT), lambda b, i: (b, 0, 0)),
            pl.BlockSpec((1, A_LAT + 8, seq), lambda b, i: (b, 0, 0)),
            pl.BlockSpec((2, BLOCK, A_HEADS * BLOCK), lambda b, i: (0, 0, 0)),
            pl.BlockSpec((A_HEADS, A_LAT, A_HEAD_DIM), lambda b, i: (0, 0, 0)),
            pl.BlockSpec((BLOCK, BLOCK), lambda b, i: (0, 0)),
        ],
        out_specs=pl.BlockSpec((BLOCK, A_HEADS * A_HEAD_DIM), lambda b, i: (b * n_blk + i, 0)),
        out_shape=jax.ShapeDtypeStruct((bsz * seq, A_HEADS * A_HEAD_DIM), jnp.bfloat16),
        scratch_shapes=[pltpu.VMEM((seq, BLOCK), jnp.float32),
                        pltpu.VMEM((seq, BLOCK), jnp.float32),
                        pltpu.VMEM((seq, A_HEADS * BLOCK), jnp.float32),
                        pltpu.VMEM((A_LAT + 8, A_HEADS * BLOCK), jnp.float32)],
        compiler_params=_cparams("arbitrary", "arbitrary"),
        name="dsa_attention",
    )(qidx_blocks, kidx, widx_l, q_blocks, ckv, ckv_tx, bias_near, wuv, tri)


SWA_QB = 4


def _swa_kernel(sink_ref, q_ref, kp_ref, kc_ref, vp_ref, vc_ref, bias_ref, yb_ref, *, qb):
    i = pl.program_id(1)
    f32 = jnp.float32
    grp = B_HEADS // B_KV_HEADS
    rows = grp * BLOCK
    row = lax.broadcasted_iota(jnp.int32, (rows, BLOCK), 0)
    t_loc = row & (BLOCK - 1)
    s_loc = lax.broadcasted_iota(jnp.int32, (rows, BLOCK), 1)
    d_prev = t_loc + BLOCK - s_loc
    d_cur = t_loc - s_loc
    m_cur = (d_cur >= 0) & (d_cur < WINDOW)
    head_of_row = lax.broadcasted_iota(jnp.int32, (rows, 1), 0) >> (BLOCK.bit_length() - 1)
    scale = B_HEAD_DIM ** -0.5
    for g in range(B_KV_HEADS):
        sink = jnp.zeros((rows, 1), f32)
        for hh in range(grp):
            sink = jnp.where(head_of_row == hh, sink_ref[g * grp + hh], sink)
        bias = bias_ref[g * grp:(g + 1) * grp].reshape(rows, 2 * BLOCK)
        for r in range(qb):
            has_prev = jnp.zeros_like(d_prev) + jnp.minimum(i * qb + r, 1)
            m_prev = (d_prev >= 0) & (d_prev < WINDOW) & (has_prev > 0)
            q = q_ref[r, g * grp:(g + 1) * grp].reshape(rows, B_HEAD_DIM)
            if r == 0:
                kp, vp = kp_ref[g], vp_ref[g]
            else:
                kp, vp = kc_ref[g, (r - 1) * BLOCK:r * BLOCK], vc_ref[g, (r - 1) * BLOCK:r * BLOCK]
            kc, vc = kc_ref[g, r * BLOCK:(r + 1) * BLOCK], vc_ref[g, r * BLOCK:(r + 1) * BLOCK]
            lp = lax.dot_general(q, kp, _NT, preferred_element_type=f32) * scale + bias[:, :BLOCK]
            lc = lax.dot_general(q, kc, _NT, preferred_element_type=f32) * scale + bias[:, BLOCK:]
            lp = jnp.where(m_prev, lp, -jnp.inf)
            lc = jnp.where(m_cur, lc, -jnp.inf)
            m = jnp.maximum(jnp.max(jnp.maximum(lp, lc), axis=1, keepdims=True), sink)
            pp = jnp.exp(lp - m).astype(jnp.bfloat16)
            pc = jnp.exp(lc - m).astype(jnp.bfloat16)
            ox = (jnp.dot(pp, vp, preferred_element_type=f32)
                  + jnp.dot(pc, vc, preferred_element_type=f32))
            den = ox[:, B_HEAD_DIM:B_HEAD_DIM + 1] + jnp.exp(sink - m)
            o = (ox[:, :B_HEAD_DIM] / den).astype(yb_ref.dtype)
            for hh in range(grp):
                h = g * grp + hh
                yb_ref[r * BLOCK:(r + 1) * BLOCK, h * B_HEAD_DIM:(h + 1) * B_HEAD_DIM] = (
                    o[hh * BLOCK:(hh + 1) * BLOCK])


def _swa_attention(sinks, q_blocks, kb, vbx, bias_b, bsz, seq):
    n_blk = seq // BLOCK
    qb = SWA_QB if n_blk % SWA_QB == 0 else 1
    n_step = n_blk // qb
    cur = lambda b, i: (0, b * n_step + i, 0)
    prev = lambda b, i: (0, b * n_blk + jnp.maximum(i * qb - 1, 0), 0)
    return pl.pallas_call(
        functools.partial(_swa_kernel, qb=qb),
        grid=(bsz, n_step),
        in_specs=[
            pl.BlockSpec(memory_space=pltpu.SMEM),
            pl.BlockSpec((qb, B_HEADS, BLOCK, B_HEAD_DIM), lambda b, i: (b * n_step + i, 0, 0, 0)),
            pl.BlockSpec((B_KV_HEADS, BLOCK, B_HEAD_DIM), prev),
            pl.BlockSpec((B_KV_HEADS, qb * BLOCK, B_HEAD_DIM), cur),
            pl.BlockSpec((B_KV_HEADS, BLOCK, 2 * B_HEAD_DIM), prev),
            pl.BlockSpec((B_KV_HEADS, qb * BLOCK, 2 * B_HEAD_DIM), cur),
            pl.BlockSpec((B_HEADS, BLOCK, 2 * BLOCK), lambda b, i: (0, 0, 0)),
        ],
        out_specs=pl.BlockSpec((qb * BLOCK, B_HEADS * B_HEAD_DIM), lambda b, i: (b * n_step + i, 0)),
        out_shape=jax.ShapeDtypeStruct((bsz * seq, B_HEADS * B_HEAD_DIM), jnp.bfloat16),
        compiler_params=_cparams("arbitrary", "arbitrary"),
        name="swa_attention",
    )(sinks, q_blocks, kb, kb, vbx, vbx, bias_b)


def _layer_norm(h, g, b):
    mu = jnp.mean(h, axis=-1, keepdims=True)
    d = h - mu
    var = jnp.mean(d * d, axis=-1, keepdims=True)
    return d * lax.rsqrt(var + LN_EPS) * g + b


def _merge_kernel(ya_ref, yb_ref, ga_ref, gb_ref, x_ref, wa_ref, wb_ref, wo_ref, g1_ref, b1_ref,
                  wrh_ref, wrl_ref, br_ref, tri_ref, x1_ref, rt_ref, cnt_ref, carry_ref):
    f32 = jnp.float32
    pa = jnp.dot(ya_ref[...], wa_ref[...], preferred_element_type=f32)
    pb = jnp.dot(yb_ref[...], wb_ref[...], preferred_element_type=f32)
    merged = ga_ref[...] * pa + gb_ref[...] * pb
    h = ALPHA * x_ref[...] + jnp.dot(merged.astype(jnp.bfloat16), wo_ref[...],
                                     preferred_element_type=f32)
    x1 = _layer_norm(h, g1_ref[...], b1_ref[...])
    _store_tile_rows(x1_ref, x1)

    x_hi = x1.astype(jnp.bfloat16)
    x_lo = (x1 - x_hi.astype(f32)).astype(jnp.bfloat16)
    lg = (jnp.dot(x_hi, wrh_ref[...], preferred_element_type=f32)
          + jnp.dot(x_hi, wrl_ref[...], preferred_element_type=f32)
          + jnp.dot(x_lo, wrh_ref[...], preferred_element_type=f32)) + br_ref[...]
    lane_i = lax.broadcasted_iota(jnp.int32, lg.shape, 1)
    lane = lane_i.astype(f32)
    big = jnp.float32(1 << 20)
    is_g = lane_i < N_GROUPS
    gl = jnp.where(is_g, lg, -jnp.inf)
    gmax = jnp.max(gl, axis=1, keepdims=True)
    g_sel = jnp.min(jnp.where(gl == gmax, lane, big), axis=1, keepdims=True)
    g_w = 1.0 / jnp.sum(jnp.where(is_g, jnp.exp(gl - gmax), 0.0), axis=1, keepdims=True)
    e_id = lane_i - N_GROUPS
    e_grp = (e_id >> 3).astype(f32)
    in_grp = (e_id >= 0) & (e_id < N_EXPERTS) & (e_grp == g_sel)
    el = jnp.where(in_grp, lg, -jnp.inf)
    emax = jnp.max(el, axis=1, keepdims=True)
    ee = jnp.where(in_grp, jnp.exp(el - emax), 0.0)
    ep = ee / jnp.sum(ee, axis=1, keepdims=True)
    epm = jnp.where(in_grp, ep, -1.0)
    p1 = jnp.max(epm, axis=1, keepdims=True)
    i1 = jnp.min(jnp.where(epm == p1, lane, big), axis=1, keepdims=True)
    epm2 = jnp.where(lane == i1, -1.0, epm)
    p2 = jnp.max(epm2, axis=1, keepdims=True)
    i2 = jnp.min(jnp.where(epm2 == p2, lane, big), axis=1, keepdims=True)
    psum = p1 + p2
    w1 = g_w * p1 / psum
    w2 = g_w * p2 / psum
    @pl.when(pl.program_id(0) == 0)
    def _():
        carry_ref[...] = jnp.zeros_like(carry_ref)

    oh1 = lane + N_GROUPS == i1
    oh2 = lane + N_GROUPS == i2
    oh = jnp.where(oh1 | oh2, 1.0, 0.0)
    prefix = jnp.dot(tri_ref[...], oh.astype(jnp.bfloat16), preferred_element_type=f32) + carry_ref[...]
    rank1 = jnp.sum(jnp.where(oh1, prefix, 0.0), axis=1, keepdims=True)
    rank2 = jnp.sum(jnp.where(oh2, prefix, 0.0), axis=1, keepdims=True)
    carry_ref[...] += jnp.sum(oh, axis=0, keepdims=True)
    cnt_ref[...] = carry_ref[...]
    rec = (i1 - N_GROUPS, i2 - N_GROUPS, w1, w2, rank1, rank2)
    rt = jnp.zeros(lg.shape, f32)
    for k, v in enumerate(rec):
        rt = jnp.where(lane_i == k, v, rt)
    rt_ref[...] = rt


def _merge_route(ya, yb, ga, gb, x2, wa, wb, wo, g1, b1, wr, br, tm):
    wr_hi = wr.astype(jnp.bfloat16)
    wr_lo = (wr - wr_hi.astype(jnp.float32)).astype(jnp.bfloat16)
    n = x2.shape[0]
    row = lambda w: pl.BlockSpec((tm, w), lambda i: (i, 0))
    full = lambda r, c: pl.BlockSpec((r, c), lambda i: (0, 0))
    aw, bw = A_HEADS * A_HEAD_DIM, B_HEADS * B_HEAD_DIM
    tri = jnp.asarray(np.tril(np.ones((tm, tm), np.float32), -1), jnp.bfloat16)
    return pl.pallas_call(
        _merge_kernel,
        grid=(n // tm,),
        in_specs=[row(aw), row(bw), row(D_MODEL), row(D_MODEL), row(D_MODEL),
                  full(aw, D_MODEL), full(bw, D_MODEL), full(D_MODEL, D_MODEL),
                  full(1, D_MODEL), full(1, D_MODEL), full(D_MODEL, LANES), full(D_MODEL, LANES),
                  full(1, LANES), full(tm, tm)],
        out_specs=(pl.BlockSpec((tm * ROW_TILE, LANES), lambda i: (i, 0)), row(LANES), full(1, LANES)),
        out_shape=(jax.ShapeDtypeStruct((n * ROW_TILE, LANES), jnp.float32),
                   jax.ShapeDtypeStruct((n, LANES), jnp.float32),
                   jax.ShapeDtypeStruct((1, LANES), jnp.float32)),
        scratch_shapes=[pltpu.VMEM((1, LANES), jnp.float32)],
        compiler_params=_cparams("arbitrary"),
        name="merge_route",
    )(ya, yb, ga, gb, x2, wa, wb, wo, g1, b1, wr_hi, wr_lo, br, tri)


ROW_TILE = D_MODEL // LANES


def _store_tile_rows(ref, val):
    rows = val.shape[0]
    for j in range(ROW_TILE):
        ref[pl.ds(j, rows, stride=ROW_TILE), :] = val[:, j * LANES:(j + 1) * LANES]


def _load_tile_rows(ref, rows):
    return jnp.concatenate([ref[pl.ds(j, rows, stride=ROW_TILE), :] for j in range(ROW_TILE)], axis=1)


DMA_UNROLL = 8


def _dispatch_kernel(pos_ref, x_ref, xs_hbm, sem, *, tm):
    def issue(k, carry):
        for u in range(DMA_UNROLL):
            r = k * DMA_UNROLL + u
            src = x_ref.at[pl.ds(pl.multiple_of(r * ROW_TILE, ROW_TILE), ROW_TILE)]
            for slot in range(2):
                p = pos_ref[0, slot, r]
                dst = xs_hbm.at[pl.ds(pl.multiple_of(p * ROW_TILE, ROW_TILE), ROW_TILE)]
                pltpu.make_async_copy(src, dst, sem).start(priority=slot)
        return carry

    lax.fori_loop(0, tm // DMA_UNROLL, issue, 0)
    for slot in range(2):
        pltpu.make_async_copy(x_ref, xs_hbm.at[pl.ds(0, tm * ROW_TILE)], sem).wait()


def _moe_dispatch(pos, x1r, n_rows, tm):
    n = x1r.shape[0] // ROW_TILE
    return pl.pallas_call(
        functools.partial(_dispatch_kernel, tm=tm),
        grid=(n // tm,),
        in_specs=[pl.BlockSpec((1, 8, tm), lambda i: (i, 0, 0), memory_space=pltpu.SMEM),
                  pl.BlockSpec((tm * ROW_TILE, LANES), lambda i: (i, 0))],
        out_specs=pl.BlockSpec(memory_space=pl.ANY),
        out_shape=jax.ShapeDtypeStruct((n_rows * ROW_TILE, LANES), jnp.float32),
        scratch_shapes=[pltpu.SemaphoreType.DMA],
        compiler_params=_cparams("arbitrary"),
        name="moe_dispatch",
    )(pos, x1r)


def _expert_kernel(tile_ref, exp_ref, lo_ref, hi_ref, nit_ref, xs_ref, wg_ref, wu_ref, wd_ref, ys_ref,
                   wgb_ref, wub_ref, wdb_ref, *, tm):
    f32 = jnp.float32
    w = pl.program_id(0)
    prev = jnp.maximum(w - 1, 0)

    @pl.when(jnp.logical_or(w == 0, exp_ref[prev] != exp_ref[w]))
    def _():
        wgb_ref[...] = wg_ref[0].astype(jnp.bfloat16)
        wub_ref[...] = wu_ref[0].astype(jnp.bfloat16)
        wdb_ref[...] = wd_ref[0].astype(jnp.bfloat16)

    @pl.when(w < nit_ref[0])
    def _():
        xb = _load_tile_rows(xs_ref, tm).astype(jnp.bfloat16)
        g = jnp.dot(xb, wgb_ref[...], preferred_element_type=f32)
        u = jnp.dot(xb, wub_ref[...], preferred_element_type=f32)
        hmid = (g * jax.nn.sigmoid(g) * u).astype(jnp.bfloat16)
        y = jnp.dot(hmid, wdb_ref[...], preferred_element_type=f32)
        row = tile_ref[w] * tm + lax.broadcasted_iota(jnp.int32, (tm, 1), 0)
        mine = (row >= lo_ref[w]) & (row < hi_ref[w])
        first_visit = jnp.logical_or(w == 0, tile_ref[prev] != tile_ref[w])

        @pl.when(first_visit)
        def _():
            _store_tile_rows(ys_ref, jnp.where(mine, y, 0.0))

        @pl.when(jnp.logical_not(first_visit))
        def _():
            _store_tile_rows(ys_ref, jnp.where(mine, y, _load_tile_rows(ys_ref, tm)))


def _moe_experts(items, xs, wg, wu, wd, tm):
    n_items = items[0].shape[0]
    grid_spec = pltpu.PrefetchScalarGridSpec(
        num_scalar_prefetch=5,
        grid=(n_items,),
        in_specs=[
            pl.BlockSpec((tm * ROW_TILE, LANES), lambda w, t, e, lo, hi, n: (t[w], 0)),
            pl.BlockSpec((1, D_MODEL, D_EXPERT), lambda w, t, e, lo, hi, n: (e[w], 0, 0)),
            pl.BlockSpec((1, D_MODEL, D_EXPERT), lambda w, t, e, lo, hi, n: (e[w], 0, 0)),
            pl.BlockSpec((1, D_EXPERT, D_MODEL), lambda w, t, e, lo, hi, n: (e[w], 0, 0)),
        ],
        out_specs=pl.BlockSpec((tm * ROW_TILE, LANES), lambda w, t, e, lo, hi, n: (t[w], 0)),
        scratch_shapes=[pltpu.VMEM((D_MODEL, D_EXPERT), jnp.bfloat16),
                        pltpu.VMEM((D_MODEL, D_EXPERT), jnp.bfloat16),
                        pltpu.VMEM((D_EXPERT, D_MODEL), jnp.bfloat16)],
    )
    return pl.pallas_call(
        functools.partial(_expert_kernel, tm=tm),
        grid_spec=grid_spec,
        out_shape=jax.ShapeDtypeStruct(xs.shape, jnp.float32),
        compiler_params=_cparams("arbitrary"),
        name="moe_experts",
    )(*items, xs, wg, wu, wd)


def _combine_kernel(pos_ref, rt_ref, x_ref, ys_hbm, g2_ref, b2_ref, out_ref, buf_a, buf_b, sem, *, tm):
    bufs = (buf_a, buf_b)

    def issue(k, carry):
        for u in range(DMA_UNROLL):
            r = k * DMA_UNROLL + u
            for slot in range(2):
                p = pos_ref[0, slot, r]
                src = ys_hbm.at[pl.ds(pl.multiple_of(p * ROW_TILE, ROW_TILE), ROW_TILE)]
                dst = bufs[slot].at[pl.ds(pl.multiple_of(r * ROW_TILE, ROW_TILE), ROW_TILE)]
                pltpu.make_async_copy(src, dst, sem).start(priority=slot)
        return carry

    lax.fori_loop(0, tm // DMA_UNROLL, issue, 0)
    for slot in range(2):
        pltpu.make_async_copy(ys_hbm.at[pl.ds(0, tm * ROW_TILE)], bufs[slot], sem).wait()
    rt = rt_ref[...]
    w1 = rt[:, 2:3]
    w2 = rt[:, 3:4]
    ffn = w1 * _load_tile_rows(buf_a, tm) + w2 * _load_tile_rows(buf_b, tm)
    out_ref[...] = _layer_norm(ALPHA * _load_tile_rows(x_ref, tm) + ffn, g2_ref[...], b2_ref[...])


def _moe_combine(pos, rt, x1r, ys, g2, b2, tm):
    n = rt.shape[0]
    return pl.pallas_call(
        functools.partial(_combine_kernel, tm=tm),
        grid=(n // tm,),
        in_specs=[pl.BlockSpec((1, 8, tm), lambda i: (i, 0, 0), memory_space=pltpu.SMEM),
                  pl.BlockSpec((tm, LANES), lambda i: (i, 0)),
                  pl.BlockSpec((tm * ROW_TILE, LANES), lambda i: (i, 0)),
                  pl.BlockSpec(memory_space=pl.ANY),
                  pl.BlockSpec((1, D_MODEL), lambda i: (0, 0)),
                  pl.BlockSpec((1, D_MODEL), lambda i: (0, 0))],
        out_specs=pl.BlockSpec((tm, D_MODEL), lambda i: (i, 0)),
        out_shape=jax.ShapeDtypeStruct((n, D_MODEL), jnp.float32),
        scratch_shapes=[pltpu.VMEM((tm * ROW_TILE, LANES), jnp.float32),
                        pltpu.VMEM((tm * ROW_TILE, LANES), jnp.float32),
                        pltpu.SemaphoreType.DMA],
        compiler_params=_cparams("arbitrary"),
        name="moe_combine",
    )(pos, rt, x1r, ys, g2, b2)


MOE_TM = 256


def _pos_kernel(start_ref, rt_ref, pos_ref):
    rt = rt_ref[...]
    lane = lax.broadcasted_iota(jnp.int32, rt.shape, 1)
    lane_f = lane.astype(jnp.float32)
    start = start_ref[...]
    tile = jnp.zeros(rt.shape, jnp.float32)
    for slot in range(2):
        first_row = jnp.sum(jnp.where(lane_f == rt[:, slot:slot + 1], start, 0.0), axis=1, keepdims=True)
        tile = jnp.where(lane == slot, first_row + rt[:, 4 + slot:5 + slot], tile)
    for r in range(rt.shape[0] // BLOCK):
        t = tile[r * BLOCK:(r + 1) * BLOCK].T
        pos_ref[0, :, r * BLOCK:(r + 1) * BLOCK] = t[:8].astype(jnp.int32)


def _moe_pos(start, rt, tm):
    n = rt.shape[0]
    return pl.pallas_call(
        _pos_kernel,
        grid=(n // tm,),
        in_specs=[pl.BlockSpec((1, LANES), lambda i: (0, 0)), pl.BlockSpec((tm, LANES), lambda i: (i, 0))],
        out_specs=pl.BlockSpec((1, 8, tm), lambda i: (i, 0, 0)),
        out_shape=jax.ShapeDtypeStruct((n // tm, 8, tm), jnp.int32),
        compiler_params=_cparams("arbitrary"),
        name="moe_pos",
    )(start, rt)


def _moe_plan(counts, n_tiles):
    i32 = jnp.int32
    cnt = counts[0, :N_EXPERTS].astype(i32)
    end = jnp.cumsum(cnt)
    start = end - cnt
    start_lanes = jnp.pad(start.astype(jnp.float32), (0, LANES - N_EXPERTS)).reshape(1, LANES)
    first_t = start // MOE_TM
    items_e = jnp.where(cnt > 0, (end - 1) // MOE_TM - first_t + 1, 0)
    item_end = jnp.cumsum(items_e)
    n_items = item_end[-1]
    w = jnp.minimum(jnp.arange(n_tiles + N_EXPERTS - 1, dtype=i32), n_items - 1)
    e_w = jnp.minimum(jnp.sum((item_end[None, :] <= w[:, None]).astype(i32), axis=1), N_EXPERTS - 1)
    t_w = first_t[e_w] + (w - (item_end - items_e)[e_w])
    lo_w = jnp.maximum(start[e_w], t_w * MOE_TM)
    hi_w = jnp.minimum(end[e_w], (t_w + 1) * MOE_TM)
    return start_lanes, (t_w.astype(i32), e_w, lo_w.astype(i32), hi_w.astype(i32),
                         n_items.reshape(1).astype(i32))


def _pick_tile(n, pref):
    t = min(pref, n)
    while n % t:
        t //= 2
    return t


def kernel(x, w_in, kv_norm_g, w_uv, w_branch_a, sinks, w_branch_b, w_out, rel_bias, ln1_g, ln1_b,
           w_group, b_group, w_router, b_router, w_gate, w_up, w_down, ln2_g, ln2_b):
    bsz, seq, d = x.shape
    assert d == D_MODEL and seq % BLOCK == 0 and w_in.shape[0] == DEPTH == 1
    n = bsz * seq
    n_blk = seq // BLOCK
    bf, f32 = jnp.bfloat16, jnp.float32
    x2 = x.reshape(n, d)

    r = np.arange(BLOCK, dtype=np.int32)
    d_prev = r[None, :] + BLOCK - r[:, None]
    d_cur = r[None, :] - r[:, None]
    bkt_a = jnp.asarray(_t5_bucket_np(np.stack([d_prev, d_cur])))
    s2 = np.arange(2 * BLOCK, dtype=np.int32)
    bkt_b = jnp.asarray(_t5_bucket_np(r[:, None] + BLOCK - s2[None, :])[None])
    assert int(_t5_bucket_np(np.arange(BLOCK + 1, max(seq, BLOCK + 2))).min()) == N_BUCKETS - 1
    tab_t = rel_bias.astype(f32).T
    bias_a = _bias_tiles(tab_t, bkt_a, 0, A_HEADS, True, minus_far=True)
    bias_b = _bias_tiles(tab_t, bkt_b, A_HEADS, B_HEADS, False)

    w0 = w_in[0]
    w_pack = jnp.concatenate(
        [w0[:, :_RAW_KW_END], jnp.zeros((d, _C_QB - _RAW_KW_END), w0.dtype), w0[:, _RAW_KW_END:]],
        axis=1).astype(bf)
    (q_blocks, ckv, ckv_tx, qidx_blocks, kidx, widx, qb_blocks, kb, vbx, ga, gb) = _input_projection(
        x2, w_pack, kv_norm_g[0].reshape(1, A_LAT).astype(f32), _pick_tile(seq, 512), seq)

    ckv3 = ckv.reshape(bsz, seq, A_LAT)
    widx_l = jnp.swapaxes(widx.reshape(bsz * n_blk, BLOCK, IDX_HEADS), 1, 2).reshape(
        bsz * n_blk, 1, IDX_HEADS * BLOCK)
    ya = _dsa_attention(qidx_blocks.reshape(bsz * n_blk, IDX_HEADS * BLOCK, IDX_DIM),
                        kidx.reshape(bsz, seq, IDX_DIM), widx_l,
                        q_blocks.reshape(bsz * n_blk, A_HEADS * BLOCK, A_LAT),
                        ckv3, ckv_tx, bias_a, w_uv[0].astype(bf), bsz, seq)

    yb = _swa_attention(sinks[0].astype(f32), qb_blocks, kb, vbx, bias_b, bsz, seq)

    w_route = jnp.concatenate(
        [w_group[0], w_router[0], jnp.zeros((d, LANES - N_GROUPS - N_EXPERTS), f32)], axis=1).astype(f32)
    b_route = jnp.concatenate(
        [b_group[0], b_router[0], jnp.zeros((LANES - N_GROUPS - N_EXPERTS,), f32)]).reshape(1, LANES).astype(f32)
    x1r, rt, counts = _merge_route(
        ya, yb, ga, gb, x2, w_branch_a[0].astype(bf), w_branch_b[0].astype(bf), w_out[0].astype(bf),
        ln1_g[0].reshape(1, d).astype(f32), ln1_b[0].reshape(1, d).astype(f32), w_route, b_route,
        _pick_tile(n, 512))

    assert (2 * n) % MOE_TM == 0
    start_lanes, items = _moe_plan(counts, 2 * n // MOE_TM)
    tm_io = _pick_tile(n, 512)
    pos_blocks = _moe_pos(start_lanes, rt, tm_io)
    xs = _moe_dispatch(pos_blocks, x1r, 2 * n, tm_io)
    ys = _moe_experts(items, xs, w_gate[0], w_up[0], w_down[0], MOE_TM)
    out = _moe_combine(pos_blocks, rt, x1r, ys, ln2_g[0].reshape(1, d).astype(f32),
                       ln2_b[0].reshape(1, d).astype(f32), tm_io)
    return out.reshape(bsz, seq, d)
```

```python
import functools
import math

import numpy as np
import jax
import jax.numpy as jnp
from jax import lax
from jax.experimental import pallas as pl
from jax.experimental.pallas import tpu as pltpu

D_MODEL = 1024
A_HEADS = 8
A_LAT = 128
A_HEAD_DIM = 64
IDX_HEADS = 8
IDX_DIM = 64
TOPK_MAX = 256
B_HEADS = 8
B_KV_HEADS = 2
B_HEAD_DIM = 64
WINDOW = 128
BLOCK = 128
N_BUCKETS = 32
MAX_DISTANCE = 128
N_GROUPS = 4
EXPERTS_PER_GROUP = 8
N_EXPERTS = 32
D_EXPERT = 256
DEPTH = 1
ALPHA = (2 * DEPTH) ** 0.25
LN_EPS = 1e-5
RMS_EPS = 1e-6

LANES = 128
INT_MIN = -(2 ** 31)
VMEM_LIMIT = 48 * 1024 * 1024

_NT = (((1,), (1,)), ((), ()))


def _cparams(*sem):
    return pltpu.CompilerParams(dimension_semantics=sem, vmem_limit_bytes=VMEM_LIMIT)


def _t5_bucket_np(dist):
    f32 = np.float32
    n = np.maximum(dist, 0)
    max_exact = N_BUCKETS // 2
    nf = np.maximum(n, 1).astype(f32)
    large = max_exact + (np.log(nf / f32(max_exact)) / f32(math.log(MAX_DISTANCE / max_exact))
                         * f32(N_BUCKETS - max_exact)).astype(np.int32)
    large = np.minimum(large, N_BUCKETS - 1)
    return np.where(n < max_exact, n, large).astype(np.int32)


def _bias_kernel(tab_ref, bkt_ref, out_ref, *, head0, minus_far):
    h = pl.program_id(0) + head0
    bkt = bkt_ref[...]
    far = tab_ref[h, N_BUCKETS - 1]
    acc = jnp.full(bkt.shape, far, jnp.float32)
    for b in range(N_BUCKETS - 1):
        acc = jnp.where(bkt == b, tab_ref[h, b], acc)
    out_ref[...] = acc - far if minus_far else acc


def _bias_tiles(tab_t, bkt, head0, n_heads, lane_major, minus_far=False):
    g, r, c = bkt.shape
    if lane_major:
        out_shape = jax.ShapeDtypeStruct((g, r, n_heads * c), jnp.float32)
        out_spec = pl.BlockSpec((g, r, c), lambda h: (0, 0, h))
    else:
        out_shape = jax.ShapeDtypeStruct((n_heads * g, r, c), jnp.float32)
        out_spec = pl.BlockSpec((g, r, c), lambda h: (h, 0, 0))
    return pl.pallas_call(
        functools.partial(_bias_kernel, head0=head0, minus_far=minus_far),
        grid=(n_heads,),
        in_specs=[pl.BlockSpec(memory_space=pltpu.SMEM),
                  pl.BlockSpec((g, r, c), lambda h: (0, 0, 0))],
        out_specs=out_spec,
        out_shape=out_shape,
        compiler_params=_cparams("arbitrary"),
        name="bias_tiles",
    )(tab_t, bkt)


_C_QLAT = 0
_C_CKV = _C_QLAT + A_HEADS * A_LAT
_C_QIDX = _C_CKV + A_LAT
_C_KW = _C_QIDX + IDX_HEADS * IDX_DIM
_C_QB = _C_KW + LANES
_C_KB = _C_QB + B_HEADS * B_HEAD_DIM
_C_VB = _C_KB + B_KV_HEADS * B_HEAD_DIM
_C_GA = _C_VB + B_KV_HEADS * B_HEAD_DIM
_C_GB = _C_GA + D_MODEL
_C_END = _C_GB + D_MODEL
_RAW_KW_END = _C_KW + IDX_DIM + IDX_HEADS


def _proj_kernel(x_ref, w_ref, g_ref, q_ref, ckv_ref, ckvt_ref, qidx_ref, kidx_ref, widx_ref,
                 qb_ref, kb_ref, vb_ref, ga_ref, gb_ref):
    xb = x_ref[...].astype(jnp.bfloat16)
    tm = xb.shape[0]

    def seg(lo, hi):
        return jnp.dot(xb, w_ref[:, lo:hi], preferred_element_type=jnp.float32)

    for h in range(A_HEADS):
        qh = (seg(_C_QLAT + h * A_LAT, _C_QLAT + (h + 1) * A_LAT) * (A_LAT ** -0.5)).astype(jnp.bfloat16)
        for r in range(tm // BLOCK):
            q_ref[r, h] = qh[r * BLOCK:(r + 1) * BLOCK]
    c = seg(_C_CKV, _C_QIDX)
    ms = jnp.mean(c * c, axis=-1, keepdims=True)
    cn = c * lax.rsqrt(ms + RMS_EPS) * g_ref[...]
    ckv_ref[...] = cn.astype(jnp.bfloat16)
    for r in range(tm // BLOCK):
        ckvt_ref[0, :A_LAT, r * BLOCK:(r + 1) * BLOCK] = cn[r * BLOCK:(r + 1) * BLOCK].T.astype(jnp.bfloat16)
    ckvt_ref[0, A_LAT:, :] = jnp.ones((8, tm), jnp.bfloat16)
    qif = seg(_C_QIDX, _C_KW).astype(jnp.bfloat16)
    for h in range(IDX_HEADS):
        for r in range(tm // BLOCK):
            qidx_ref[r, h] = qif[r * BLOCK:(r + 1) * BLOCK, h * IDX_DIM:(h + 1) * IDX_DIM]
    kw = seg(_C_KW, _C_QB)
    kidx_ref[...] = kw[:, :IDX_DIM].astype(jnp.bfloat16)
    widx_ref[...] = kw[:, IDX_DIM:IDX_DIM + IDX_HEADS]
    qbf = seg(_C_QB, _C_KB).astype(jnp.bfloat16)
    for h in range(B_HEADS):
        for r in range(tm // BLOCK):
            qb_ref[r, h] = qbf[r * BLOCK:(r + 1) * BLOCK, h * B_HEAD_DIM:(h + 1) * B_HEAD_DIM]
    kbf = seg(_C_KB, _C_VB).astype(jnp.bfloat16)
    vbf = seg(_C_VB, _C_GA)
    for g in range(B_KV_HEADS):
        kb_ref[g] = kbf[:, g * B_HEAD_DIM:(g + 1) * B_HEAD_DIM]
        vb_ref[g, 0, B_HEAD_DIM:, :] = jnp.ones((SWA_ONES, tm), jnp.bfloat16)
    for r in range(tm // BLOCK):
        vt = vbf[r * BLOCK:(r + 1) * BLOCK].T.astype(jnp.bfloat16)
        for g in range(B_KV_HEADS):
            vb_ref[g, 0, :B_HEAD_DIM, r * BLOCK:(r + 1) * BLOCK] = vt[g * B_HEAD_DIM:(g + 1) * B_HEAD_DIM]
    ga_ref[...] = jax.nn.sigmoid(seg(_C_GA, _C_GB))
    gb_ref[...] = jax.nn.sigmoid(seg(_C_GB, _C_END))


def _input_projection(x2, w_pack, kv_g, tm, seq):
    n = x2.shape[0]
    assert seq % tm == 0
    bf, f32 = jnp.bfloat16, jnp.float32
    row = lambda w: pl.BlockSpec((tm, w), lambda i: (i, 0))
    out_shape = (
        jax.ShapeDtypeStruct((n // BLOCK, A_HEADS, BLOCK, A_LAT), bf),
        jax.ShapeDtypeStruct((n, A_LAT), bf),
        jax.ShapeDtypeStruct((n // seq, A_LAT + 8, seq), bf),
        jax.ShapeDtypeStruct((n // BLOCK, IDX_HEADS, BLOCK, IDX_DIM), bf),
        jax.ShapeDtypeStruct((n, IDX_DIM), bf),
        jax.ShapeDtypeStruct((n, IDX_HEADS), f32),
        jax.ShapeDtypeStruct((n // BLOCK, B_HEADS, BLOCK, B_HEAD_DIM), bf),
        jax.ShapeDtypeStruct((B_KV_HEADS, n, B_HEAD_DIM), bf),
        jax.ShapeDtypeStruct((B_KV_HEADS, n // seq, B_HEAD_DIM + SWA_ONES, seq), bf),
        jax.ShapeDtypeStruct((n, D_MODEL), f32),
        jax.ShapeDtypeStruct((n, D_MODEL), f32),
    )
    out_specs = (
        pl.BlockSpec((tm // BLOCK, A_HEADS, BLOCK, A_LAT), lambda i: (i, 0, 0, 0)),
        row(A_LAT),
        pl.BlockSpec((1, A_LAT + 8, tm), lambda i: (i // (seq // tm), 0, i % (seq // tm))),
        pl.BlockSpec((tm // BLOCK, IDX_HEADS, BLOCK, IDX_DIM), lambda i: (i, 0, 0, 0)),
        row(IDX_DIM), row(IDX_HEADS),
        pl.BlockSpec((tm // BLOCK, B_HEADS, BLOCK, B_HEAD_DIM), lambda i: (i, 0, 0, 0)),
        pl.BlockSpec((B_KV_HEADS, tm, B_HEAD_DIM), lambda i: (0, i, 0)),
        pl.BlockSpec((B_KV_HEADS, 1, B_HEAD_DIM + SWA_ONES, tm),
                     lambda i: (0, i // (seq // tm), 0, i % (seq // tm))),
        row(D_MODEL), row(D_MODEL),
    )
    return pl.pallas_call(
        _proj_kernel,
        grid=(n // tm,),
        in_specs=[row(D_MODEL),
                  pl.BlockSpec((D_MODEL, _C_END), lambda i: (0, 0)),
                  pl.BlockSpec((1, A_LAT), lambda i: (0, 0))],
        out_specs=out_specs,
        out_shape=out_shape,
        compiler_params=_cparams("arbitrary"),
        name="input_projection",
    )(x2, w_pack, kv_g)


DSA_UNROLL = 4


def _dsa_kernel(qidx_ref, kidx_ref, widx_ref, q_ref, ckv_ref, ckvt_ref, bias_ref, wuv_ref,
                tri_ref, ya_ref, sc_ref, neg_ref, lg_ref, o_ref, *, k_sel, search_chunk):
    i = pl.program_id(1)
    f32 = jnp.float32
    s_loc = lax.broadcasted_iota(jnp.int32, (BLOCK, BLOCK), 0)
    t_loc = lax.broadcasted_iota(jnp.int32, (BLOCK, BLOCK), 1)
    causal_diag = s_loc <= t_loc
    idx_scale = IDX_DIM ** -0.5
    w_scale = IDX_HEADS ** -0.5

    def rows(j):
        return pl.ds(pl.multiple_of(j * BLOCK, BLOCK), BLOCK)

    sc_ref[...] = jnp.full(sc_ref.shape, -jnp.inf, f32)
    wts = (widx_ref[0] * w_scale) * idx_scale
    qi = qidx_ref[0]

    def score_keys(j):
        s = lax.dot_general(kidx_ref[0, rows(j), :], qi, _NT, preferred_element_type=f32)
        r = jnp.maximum(s, 0.0) * wts
        acc = r[:, :BLOCK]
        for h in range(1, IDX_HEADS):
            acc = acc + r[:, h * BLOCK:(h + 1) * BLOCK]
        return acc

    def fold_blocks(lo, hi, fn, init, combine):
        n = hi - lo
        n_grp = lax.shift_right_logical(n, DSA_UNROLL.bit_length() - 1)

        def group(k, acc):
            j = lo + DSA_UNROLL * k
            vals = [fn(j + u) for u in range(DSA_UNROLL)]
            while len(vals) > 1:
                vals = [combine(vals[a], vals[a + 1]) for a in range(0, len(vals), 2)]
            return combine(acc, vals[0])

        acc = lax.fori_loop(0, n_grp, group, init)
        return lax.fori_loop(lo + DSA_UNROLL * n_grp, hi, lambda j, a: combine(a, fn(j)), acc)

    def far_scores(j):
        sc_ref[rows(j), :] = score_keys(j)
        return jnp.int32(0)

    fold_blocks(0, i, far_scores, jnp.int32(0), lambda a, b: a)
    sc_ref[rows(i), :] = jnp.where(causal_diag, score_keys(i), -jnp.inf)

    n_chunks = ((i + 1) * BLOCK + search_chunk - 1) // search_chunk
    n_acc = 64

    def count(cmp, cand):
        def body(c, cnt):
            base = pl.multiple_of(c * search_chunk, search_chunk)
            for r in range(search_chunk // n_acc):
                cnt = cnt + cmp(sc_ref[pl.ds(base + r * n_acc, n_acc), :], cand).astype(f32)
            return cnt
        cnt = lax.fori_loop(0, n_chunks, body, jnp.zeros((n_acc, BLOCK), f32))
        return jnp.sum(cnt, axis=0, keepdims=True)

    def ordered_to_float(u):
        k = u ^ INT_MIN
        return lax.bitcast_convert_type(k ^ ((k >> 31) & 0x7FFFFFFF), f32)

    def search_bit(b, prefix_u):
        cand_u = prefix_u | lax.shift_left(jnp.int32(1), 31 - b)
        cnt = count(jnp.greater_equal, ordered_to_float(cand_u))
        return jnp.where(cnt >= k_sel, cand_u, prefix_u)

    thr_u = lax.fori_loop(0, 32, search_bit, jnp.zeros((1, BLOCK), jnp.int32))
    thr = ordered_to_float(thr_u)
    thr = jnp.where(thr != thr, -jnp.inf, thr)
    c_ge = count(jnp.greater_equal, thr)
    c_gt = count(jnp.greater, thr)
    n_tie = k_sel - c_gt
    any_tie = jnp.max(((c_ge > k_sel) & (thr > -jnp.inf)).astype(f32)) > 0.0

    @pl.when(jnp.logical_not(any_tie))
    def _():
        def body(j, carry):
            neg_ref[rows(j), :] = jnp.where(sc_ref[rows(j), :] >= thr, 0.0, -jnp.inf)
            return carry
        lax.fori_loop(0, i, body, 0)
        neg_ref[rows(i), :] = jnp.where((sc_ref[rows(i), :] >= thr) & causal_diag, 0.0, -jnp.inf)

    @pl.when(any_tie)
    def _():
        tri = tri_ref[...]

        def block(j, tie_seen):
            kc = sc_ref[rows(j), :]
            eq = kc == thr
            tie_rank = jnp.dot(tri, eq.astype(jnp.bfloat16), preferred_element_type=f32) + tie_seen
            sel = (kc > thr) | (eq & (tie_rank <= n_tie))
            return sel, tie_seen + jnp.sum(eq.astype(f32), axis=0, keepdims=True)

        def body(j, tie_seen):
            sel, tie_seen = block(j, tie_seen)
            neg_ref[rows(j), :] = jnp.where(sel, 0.0, -jnp.inf)
            return tie_seen

        tie_seen = lax.fori_loop(0, i, body, jnp.zeros((1, BLOCK), f32))
        sel, _ = block(i, tie_seen)
        neg_ref[rows(i), :] = jnp.where(sel & causal_diag, 0.0, -jnp.inf)

    q_all = q_ref[0]

    def logits(j):
        lg = lax.dot_general(ckv_ref[0, rows(j), :], q_all, _NT, preferred_element_type=f32)
        return lg + jnp.concatenate([neg_ref[rows(j), :]] * A_HEADS, axis=1)

    def far_logits(j):
        lg = logits(j)
        lg_ref[rows(j), :] = lg
        return jnp.max(lg, axis=0, keepdims=True)

    def near_logits(j):
        lg = logits(j) + bias_ref[j - i + 1]
        lg_ref[rows(j), :] = lg
        return jnp.max(lg, axis=0, keepdims=True)

    n_far = jnp.maximum(i - 1, 0)
    m_run = fold_blocks(0, n_far, far_logits, jnp.full((1, A_HEADS * BLOCK), -jnp.inf, f32), jnp.maximum)
    m_run = fold_blocks(n_far, i + 1, near_logits, m_run, jnp.maximum)

    o_ref[...] = jnp.zeros_like(o_ref)

    def pv_rows(r):
        p = jnp.exp(lg_ref[r, :] - m_run).astype(jnp.bfloat16)
        o_ref[...] += jnp.dot(ckvt_ref[0, :, r], p, preferred_element_type=f32)

    def pv_group(k, carry):
        span = DSA_UNROLL * BLOCK
        pv_rows(pl.ds(pl.multiple_of(k * span, span), span))
        return carry

    def pv_single(j, carry):
        pv_rows(rows(j))
        return carry

    n_grp = lax.shift_right_logical(i + 1, DSA_UNROLL.bit_length() - 1)
    lax.fori_loop(0, n_grp, pv_group, 0)
    lax.fori_loop(DSA_UNROLL * n_grp, i + 1, pv_single, 0)

    o_t = o_ref[:A_LAT, :] / o_ref[A_LAT:A_LAT + 1, :]

    for h in range(A_HEADS):
        o_h = o_t[:, h * BLOCK:(h + 1) * BLOCK].T.astype(jnp.bfloat16)
        y_h = jnp.dot(o_h, wuv_ref[h], preferred_element_type=f32)
        ya_ref[:, h * A_HEAD_DIM:(h + 1) * A_HEAD_DIM] = y_h.astype(ya_ref.dtype)


def _dsa_attention(qidx_blocks, kidx, widx_l, q_blocks, ckv, ckv_tx, bias_near, wuv, bsz, seq):
    n_blk = seq // BLOCK
    k_sel = min(TOPK_MAX, seq // 4)
    search_chunk = min(512, seq)
    tri = jnp.asarray(np.tril(np.ones((BLOCK, BLOCK), np.float32)), jnp.bfloat16)
    kern = functools.partial(_dsa_kernel, k_sel=float(k_sel), search_chunk=search_chunk)
    blk = lambda b, i: (b * n_blk + i, 0, 0)
    return pl.pallas_call(
        kern,
        grid=(bsz, n_blk),
        in_specs=[
            pl.BlockSpec((1, IDX_HEADS * BLOCK, IDX_DIM), blk),
            pl.BlockSpec((1, seq, IDX_DIM), lambda b, i: (b, 0, 0)),
            pl.BlockSpec((1, 1, IDX_HEADS * BLOCK), blk),
            pl.BlockSpec((1, A_HEADS * BLOCK, A_LAT), blk),
            pl.BlockSpec((1, seq, A_LAT), lambda b, i: (b, 0, 0)),
            pl.BlockSpec((1, A_LAT + 8, seq), lambda b, i: (b, 0, 0)),
            pl.BlockSpec((2, BLOCK, A_HEADS * BLOCK), lambda b, i: (0, 0, 0)),
            pl.BlockSpec((A_HEADS, A_LAT, A_HEAD_DIM), lambda b, i: (0, 0, 0)),
            pl.BlockSpec((BLOCK, BLOCK), lambda b, i: (0, 0)),
        ],
        out_specs=pl.BlockSpec((BLOCK, A_HEADS * A_HEAD_DIM), lambda b, i: (b * n_blk + i, 0)),
        out_shape=jax.ShapeDtypeStruct((bsz * seq, A_HEADS * A_HEAD_DIM), jnp.bfloat16),
        scratch_shapes=[pltpu.VMEM((seq, BLOCK), jnp.float32),
                        pltpu.VMEM((seq, BLOCK), jnp.float32),
                        pltpu.VMEM((seq, A_HEADS * BLOCK), jnp.float32),
                        pltpu.VMEM((A_LAT + 8, A_HEADS * BLOCK), jnp.float32)],
        compiler_params=_cparams("arbitrary", "arbitrary"),
        name="dsa_attention",
    )(qidx_blocks, kidx, widx_l, q_blocks, ckv, ckv_tx, bias_near, wuv, tri)


SWA_QB = 4
SWA_ONES = 16


def _swa_kernel(sink_ref, q_ref, kp_ref, kc_ref, vp_ref, vc_ref, bias_ref, yb_ref, *, qb):
    i = pl.program_id(1)
    f32 = jnp.float32
    grp = B_HEADS // B_KV_HEADS
    cols = grp * BLOCK
    s_loc = lax.broadcasted_iota(jnp.int32, (BLOCK, BLOCK), 0)
    t_loc = lax.broadcasted_iota(jnp.int32, (BLOCK, BLOCK), 1)
    neg_cur = jnp.concatenate([jnp.where(s_loc <= t_loc, 0.0, -jnp.inf)] * grp, axis=1)
    in_prev = s_loc > t_loc
    head_of_col = lax.broadcasted_iota(jnp.int32, (1, cols), 1) >> (BLOCK.bit_length() - 1)
    scale = B_HEAD_DIM ** -0.5
    for g in range(B_KV_HEADS):
        sink = jnp.zeros((1, cols), f32)
        for hh in range(grp):
            sink = jnp.where(head_of_col == hh, sink_ref[g * grp + hh], sink)
        bias_p = bias_ref[:BLOCK, g * cols:(g + 1) * cols]
        bias_c = bias_ref[BLOCK:, g * cols:(g + 1) * cols]
        for r in range(qb):
            has_prev = jnp.zeros_like(s_loc) + jnp.minimum(i * qb + r, 1)
            neg_prev = jnp.concatenate([jnp.where(in_prev & (has_prev > 0), 0.0, -jnp.inf)] * grp, axis=1)
            q = q_ref[r, g * grp:(g + 1) * grp].reshape(cols, B_HEAD_DIM)
            if r == 0:
                kp, vp = kp_ref[g], vp_ref[g, 0]
            else:
                kp, vp = kc_ref[g, (r - 1) * BLOCK:r * BLOCK], vc_ref[g, 0, :, (r - 1) * BLOCK:r * BLOCK]
            kc, vc = kc_ref[g, r * BLOCK:(r + 1) * BLOCK], vc_ref[g, 0, :, r * BLOCK:(r + 1) * BLOCK]
            lp = lax.dot_general(kp, q, _NT, preferred_element_type=f32) * scale + bias_p + neg_prev
            lc = lax.dot_general(kc, q, _NT, preferred_element_type=f32) * scale + bias_c + neg_cur
            m = jnp.maximum(jnp.max(jnp.maximum(lp, lc), axis=0, keepdims=True), sink)
            pp = jnp.exp(lp - m).astype(jnp.bfloat16)
            pc = jnp.exp(lc - m).astype(jnp.bfloat16)
            ox = (jnp.dot(vp, pp, preferred_element_type=f32)
                  + jnp.dot(vc, pc, preferred_element_type=f32))
            den = ox[B_HEAD_DIM:B_HEAD_DIM + 1, :] + jnp.exp(sink - m)
            o = ox[:B_HEAD_DIM, :] / den
            for pair in range(grp // 2):
                two = jnp.concatenate([o[:, (2 * pair) * BLOCK:(2 * pair + 1) * BLOCK],
                                       o[:, (2 * pair + 1) * BLOCK:(2 * pair + 2) * BLOCK]], axis=0)
                h0 = g * grp + 2 * pair
                yb_ref[r * BLOCK:(r + 1) * BLOCK, h0 * B_HEAD_DIM:(h0 + 2) * B_HEAD_DIM] = (
                    two.T.astype(yb_ref.dtype))


def _swa_attention(sinks, q_blocks, kb, vbt, bias_b, bsz, seq):
    n_blk = seq // BLOCK
    qb = SWA_QB if n_blk % SWA_QB == 0 else 1
    n_step = n_blk // qb
    cur = lambda b, i: (0, b * n_step + i, 0)
    prev = lambda b, i: (0, b * n_blk + jnp.maximum(i * qb - 1, 0), 0)
    vrows = B_HEAD_DIM + SWA_ONES
    return pl.pallas_call(
        functools.partial(_swa_kernel, qb=qb),
        grid=(bsz, n_step),
        in_specs=[
            pl.BlockSpec(memory_space=pltpu.SMEM),
            pl.BlockSpec((qb, B_HEADS, BLOCK, B_HEAD_DIM), lambda b, i: (b * n_step + i, 0, 0, 0)),
            pl.BlockSpec((B_KV_HEADS, BLOCK, B_HEAD_DIM), prev),
            pl.BlockSpec((B_KV_HEADS, qb * BLOCK, B_HEAD_DIM), cur),
            pl.BlockSpec((B_KV_HEADS, 1, vrows, BLOCK), lambda b, i: (0, b, 0, jnp.maximum(i * qb - 1, 0))),
            pl.BlockSpec((B_KV_HEADS, 1, vrows, qb * BLOCK), lambda b, i: (0, b, 0, i)),
            pl.BlockSpec((2 * BLOCK, B_HEADS * BLOCK), lambda b, i: (0, 0)),
        ],
        out_specs=pl.BlockSpec((qb * BLOCK, B_HEADS * B_HEAD_DIM), lambda b, i: (b * n_step + i, 0)),
        out_shape=jax.ShapeDtypeStruct((bsz * seq, B_HEADS * B_HEAD_DIM), jnp.bfloat16),
        compiler_params=_cparams("arbitrary", "arbitrary"),
        name="swa_attention",
    )(sinks, q_blocks, kb, kb, vbt, vbt, bias_b)


def _layer_norm(h, g, b):
    mu = jnp.mean(h, axis=-1, keepdims=True)
    d = h - mu
    var = jnp.mean(d * d, axis=-1, keepdims=True)
    return d * lax.rsqrt(var + LN_EPS) * g + b


def _merge_kernel(ya_ref, yb_ref, ga_ref, gb_ref, x_ref, wa_ref, wb_ref, wo_ref, g1_ref, b1_ref,
                  wrh_ref, wrl_ref, br_ref, tri_ref, x1_ref, rt_ref, cnt_ref, carry_ref):
    f32 = jnp.float32
    pa = jnp.dot(ya_ref[...], wa_ref[...], preferred_element_type=f32)
    pb = jnp.dot(yb_ref[...], wb_ref[...], preferred_element_type=f32)
    merged = ga_ref[...] * pa + gb_ref[...] * pb
    h = ALPHA * x_ref[...] + jnp.dot(merged.astype(jnp.bfloat16), wo_ref[...],
                                     preferred_element_type=f32)
    x1 = _layer_norm(h, g1_ref[...], b1_ref[...])
    _store_tile_rows(x1_ref, x1)

    x_hi = x1.astype(jnp.bfloat16)
    x_lo = (x1 - x_hi.astype(f32)).astype(jnp.bfloat16)
    lg = (jnp.dot(x_hi, wrh_ref[...], preferred_element_type=f32)
          + jnp.dot(x_hi, wrl_ref[...], preferred_element_type=f32)
          + jnp.dot(x_lo, wrh_ref[...], preferred_element_type=f32)) + br_ref[...]
    lane_i = lax.broadcasted_iota(jnp.int32, lg.shape, 1)
    lane = lane_i.astype(f32)
    big = jnp.float32(1 << 20)
    is_g = lane_i < N_GROUPS
    gl = jnp.where(is_g, lg, -jnp.inf)
    gmax = jnp.max(gl, axis=1, keepdims=True)
    g_sel = jnp.min(jnp.where(gl == gmax, lane, big), axis=1, keepdims=True)
    g_w = 1.0 / jnp.sum(jnp.where(is_g, jnp.exp(gl - gmax), 0.0), axis=1, keepdims=True)
    e_id = lane_i - N_GROUPS
    e_grp = (e_id >> 3).astype(f32)
    in_grp = (e_id >= 0) & (e_id < N_EXPERTS) & (e_grp == g_sel)
    el = jnp.where(in_grp, lg, -jnp.inf)
    emax = jnp.max(el, axis=1, keepdims=True)
    ee = jnp.where(in_grp, jnp.exp(el - emax), 0.0)
    ep = ee / jnp.sum(ee, axis=1, keepdims=True)
    epm = jnp.where(in_grp, ep, -1.0)
    p1 = jnp.max(epm, axis=1, keepdims=True)
    i1 = jnp.min(jnp.where(epm == p1, lane, big), axis=1, keepdims=True)
    epm2 = jnp.where(lane == i1, -1.0, epm)
    p2 = jnp.max(epm2, axis=1, keepdims=True)
    i2 = jnp.min(jnp.where(epm2 == p2, lane, big), axis=1, keepdims=True)
    psum = p1 + p2
    w1 = g_w * p1 / psum
    w2 = g_w * p2 / psum
    @pl.when(pl.program_id(0) == 0)
    def _():
        carry_ref[...] = jnp.zeros_like(carry_ref)

    oh1 = lane + N_GROUPS == i1
    oh2 = lane + N_GROUPS == i2
    oh = jnp.where(oh1 | oh2, 1.0, 0.0)
    prefix = jnp.dot(tri_ref[...], oh.astype(jnp.bfloat16), preferred_element_type=f32) + carry_ref[...]
    rank1 = jnp.sum(jnp.where(oh1, prefix, 0.0), axis=1, keepdims=True)
    rank2 = jnp.sum(jnp.where(oh2, prefix, 0.0), axis=1, keepdims=True)
    carry_ref[...] += jnp.sum(oh, axis=0, keepdims=True)
    cnt_ref[...] = carry_ref[...]
    rec = (i1 - N_GROUPS, i2 - N_GROUPS, w1, w2, rank1, rank2)
    rt = jnp.zeros(lg.shape, f32)
    for k, v in enumerate(rec):
        rt = jnp.where(lane_i == k, v, rt)
    rt_ref[...] = rt


def _merge_route(ya, yb, ga, gb, x2, wa, wb, wo, g1, b1, wr, br, tm):
    wr_hi = wr.astype(jnp.bfloat16)
    wr_lo = (wr - wr_hi.astype(jnp.float32)).astype(jnp.bfloat16)
    n = x2.shape[0]
    row = lambda w: pl.BlockSpec((tm, w), lambda i: (i, 0))
    full = lambda r, c: pl.BlockSpec((r, c), lambda i: (0, 0))
    aw, bw = A_HEADS * A_HEAD_DIM, B_HEADS * B_HEAD_DIM
    tri = jnp.asarray(np.tril(np.ones((tm, tm), np.float32), -1), jnp.bfloat16)
    return pl.pallas_call(
        _merge_kernel,
        grid=(n // tm,),
        in_specs=[row(aw), row(bw), row(D_MODEL), row(D_MODEL), row(D_MODEL),
                  full(aw, D_MODEL), full(bw, D_MODEL), full(D_MODEL, D_MODEL),
                  full(1, D_MODEL), full(1, D_MODEL), full(D_MODEL, LANES), full(D_MODEL, LANES),
                  full(1, LANES), full(tm, tm)],
        out_specs=(pl.BlockSpec((tm * ROW_TILE, LANES), lambda i: (i, 0)), row(LANES), full(1, LANES)),
        out_shape=(jax.ShapeDtypeStruct((n * ROW_TILE, LANES), jnp.float32),
                   jax.ShapeDtypeStruct((n, LANES), jnp.float32),
                   jax.ShapeDtypeStruct((1, LANES), jnp.float32)),
        scratch_shapes=[pltpu.VMEM((1, LANES), jnp.float32)],
        compiler_params=_cparams("arbitrary"),
        name="merge_route",
    )(ya, yb, ga, gb, x2, wa, wb, wo, g1, b1, wr_hi, wr_lo, br, tri)


ROW_TILE = D_MODEL // LANES


def _store_tile_rows(ref, val):
    rows = val.shape[0]
    for j in range(ROW_TILE):
        ref[pl.ds(j, rows, stride=ROW_TILE), :] = val[:, j * LANES:(j + 1) * LANES]


def _load_tile_rows(ref, rows):
    return jnp.concatenate([ref[pl.ds(j, rows, stride=ROW_TILE), :] for j in range(ROW_TILE)], axis=1)


DMA_UNROLL = 8


def _dispatch_kernel(pos_ref, x_ref, xs_hbm, sem, *, tm):
    def issue(k, carry):
        for u in range(DMA_UNROLL):
            r = k * DMA_UNROLL + u
            src = x_ref.at[pl.ds(pl.multiple_of(r * ROW_TILE, ROW_TILE), ROW_TILE)]
            for slot in range(2):
                p = pos_ref[0, slot, r]
                dst = xs_hbm.at[pl.ds(pl.multiple_of(p * ROW_TILE, ROW_TILE), ROW_TILE)]
                pltpu.make_async_copy(src, dst, sem).start(priority=slot)
        return carry

    lax.fori_loop(0, tm // DMA_UNROLL, issue, 0)
    for slot in range(2):
        pltpu.make_async_copy(x_ref, xs_hbm.at[pl.ds(0, tm * ROW_TILE)], sem).wait()


def _moe_dispatch(pos, x1r, n_rows, tm):
    n = x1r.shape[0] // ROW_TILE
    return pl.pallas_call(
        functools.partial(_dispatch_kernel, tm=tm),
        grid=(n // tm,),
        in_specs=[pl.BlockSpec((1, 8, tm), lambda i: (i, 0, 0), memory_space=pltpu.SMEM),
                  pl.BlockSpec((tm * ROW_TILE, LANES), lambda i: (i, 0))],
        out_specs=pl.BlockSpec(memory_space=pl.ANY),
        out_shape=jax.ShapeDtypeStruct((n_rows * ROW_TILE, LANES), jnp.float32),
        scratch_shapes=[pltpu.SemaphoreType.DMA],
        compiler_params=_cparams("arbitrary"),
        name="moe_dispatch",
    )(pos, x1r)


def _expert_kernel(tile_ref, exp_ref, lo_ref, hi_ref, nit_ref, xs_ref, wg_ref, wu_ref, wd_ref, ys_ref,
                   wgb_ref, wub_ref, wdb_ref, *, tm):
    f32 = jnp.float32
    w = pl.program_id(0)
    prev = jnp.maximum(w - 1, 0)

    @pl.when(jnp.logical_or(w == 0, exp_ref[prev] != exp_ref[w]))
    def _():
        wgb_ref[...] = wg_ref[0].astype(jnp.bfloat16)
        wub_ref[...] = wu_ref[0].astype(jnp.bfloat16)
        wdb_ref[...] = wd_ref[0].astype(jnp.bfloat16)

    @pl.when(w < nit_ref[0])
    def _():
        rc = tm // MOE_CHUNKS
        chunk = lambda ref, c: ref.at[pl.ds(c * rc * ROW_TILE, rc * ROW_TILE)]
        ys, mine = [], []
        for c in range(MOE_CHUNKS):
            xb = _load_tile_rows(chunk(xs_ref, c), rc).astype(jnp.bfloat16)
            g = jnp.dot(xb, wgb_ref[...], preferred_element_type=f32)
            u = jnp.dot(xb, wub_ref[...], preferred_element_type=f32)
            hmid = (g * jax.nn.sigmoid(g) * u).astype(jnp.bfloat16)
            ys.append(jnp.dot(hmid, wdb_ref[...], preferred_element_type=f32))
            row = tile_ref[w] * tm + c * rc + lax.broadcasted_iota(jnp.int32, (rc, 1), 0)
            mine.append((row >= lo_ref[w]) & (row < hi_ref[w]))
        first_visit = jnp.logical_or(w == 0, tile_ref[prev] != tile_ref[w])

        @pl.when(first_visit)
        def _():
            for c in range(MOE_CHUNKS):
                _store_tile_rows(chunk(ys_ref, c), jnp.where(mine[c], ys[c], 0.0))

        @pl.when(jnp.logical_not(first_visit))
        def _():
            for c in range(MOE_CHUNKS):
                old = _load_tile_rows(chunk(ys_ref, c), rc)
                _store_tile_rows(chunk(ys_ref, c), jnp.where(mine[c], ys[c], old))


def _moe_experts(items, xs, wg, wu, wd, tm):
    n_items = items[0].shape[0]
    grid_spec = pltpu.PrefetchScalarGridSpec(
        num_scalar_prefetch=5,
        grid=(n_items,),
        in_specs=[
            pl.BlockSpec((tm * ROW_TILE, LANES), lambda w, t, e, lo, hi, n: (t[w], 0)),
            pl.BlockSpec((1, D_MODEL, D_EXPERT), lambda w, t, e, lo, hi, n: (e[w], 0, 0)),
            pl.BlockSpec((1, D_MODEL, D_EXPERT), lambda w, t, e, lo, hi, n: (e[w], 0, 0)),
            pl.BlockSpec((1, D_EXPERT, D_MODEL), lambda w, t, e, lo, hi, n: (e[w], 0, 0)),
        ],
        out_specs=pl.BlockSpec((tm * ROW_TILE, LANES), lambda w, t, e, lo, hi, n: (t[w], 0)),
        scratch_shapes=[pltpu.VMEM((D_MODEL, D_EXPERT), jnp.bfloat16),
                        pltpu.VMEM((D_MODEL, D_EXPERT), jnp.bfloat16),
                        pltpu.VMEM((D_EXPERT, D_MODEL), jnp.bfloat16)],
    )
    return pl.pallas_call(
        functools.partial(_expert_kernel, tm=tm),
        grid_spec=grid_spec,
        out_shape=jax.ShapeDtypeStruct(xs.shape, jnp.float32),
        compiler_params=_cparams("arbitrary"),
        name="moe_experts",
    )(*items, xs, wg, wu, wd)


def _combine_kernel(pos_ref, rt_ref, x_ref, ys_hbm, g2_ref, b2_ref, out_ref, buf_a, buf_b, sem, *, tm):
    bufs = (buf_a, buf_b)
    th = tm // 2
    half = lambda ref, h: ref.at[pl.ds(h * th * ROW_TILE, th * ROW_TILE)]

    def issue_half(h):
        def issue(k, carry):
            for u in range(DMA_UNROLL):
                r = h * th + k * DMA_UNROLL + u
                for slot in range(2):
                    p = pos_ref[0, slot, r]
                    src = ys_hbm.at[pl.ds(pl.multiple_of(p * ROW_TILE, ROW_TILE), ROW_TILE)]
                    dst = bufs[slot].at[pl.ds(pl.multiple_of(r * ROW_TILE, ROW_TILE), ROW_TILE)]
                    pltpu.make_async_copy(src, dst, sem.at[h]).start(priority=slot)
            return carry
        lax.fori_loop(0, th // DMA_UNROLL, issue, 0)

    def finish_half(h):
        for slot in range(2):
            pltpu.make_async_copy(ys_hbm.at[pl.ds(0, th * ROW_TILE)], half(bufs[slot], h), sem.at[h]).wait()
        rt = rt_ref[h * th:(h + 1) * th, :]
        ffn = (rt[:, 2:3] * _load_tile_rows(half(buf_a, h), th)
               + rt[:, 3:4] * _load_tile_rows(half(buf_b, h), th))
        out_ref[h * th:(h + 1) * th, :] = _layer_norm(
            ALPHA * _load_tile_rows(half(x_ref, h), th) + ffn, g2_ref[...], b2_ref[...])

    issue_half(0)
    issue_half(1)
    finish_half(0)
    finish_half(1)


def _moe_combine(pos, rt, x1r, ys, g2, b2, tm):
    n = rt.shape[0]
    return pl.pallas_call(
        functools.partial(_combine_kernel, tm=tm),
        grid=(n // tm,),
        in_specs=[pl.BlockSpec((1, 8, tm), lambda i: (i, 0, 0), memory_space=pltpu.SMEM),
                  pl.BlockSpec((tm, LANES), lambda i: (i, 0)),
                  pl.BlockSpec((tm * ROW_TILE, LANES), lambda i: (i, 0)),
                  pl.BlockSpec(memory_space=pl.ANY),
                  pl.BlockSpec((1, D_MODEL), lambda i: (0, 0)),
                  pl.BlockSpec((1, D_MODEL), lambda i: (0, 0))],
        out_specs=pl.BlockSpec((tm, D_MODEL), lambda i: (i, 0)),
        out_shape=jax.ShapeDtypeStruct((n, D_MODEL), jnp.float32),
        scratch_shapes=[pltpu.VMEM((tm * ROW_TILE, LANES), jnp.float32),
                        pltpu.VMEM((tm * ROW_TILE, LANES), jnp.float32),
                        pltpu.SemaphoreType.DMA((2,))],
        compiler_params=_cparams("arbitrary"),
        name="moe_combine",
    )(pos, rt, x1r, ys, g2, b2)


MOE_TM = 256
MOE_CHUNKS = 2


def _pos_kernel(start_ref, rt_ref, pos_ref):
    rt = rt_ref[...]
    lane = lax.broadcasted_iota(jnp.int32, rt.shape, 1)
    lane_f = lane.astype(jnp.float32)
    start = start_ref[...]
    tile = jnp.zeros(rt.shape, jnp.float32)
    for slot in range(2):
        first_row = jnp.sum(jnp.where(lane_f == rt[:, slot:slot + 1], start, 0.0), axis=1, keepdims=True)
        tile = jnp.where(lane == slot, first_row + rt[:, 4 + slot:5 + slot], tile)
    for r in range(rt.shape[0] // BLOCK):
        t = tile[r * BLOCK:(r + 1) * BLOCK].T
        pos_ref[0, :, r * BLOCK:(r + 1) * BLOCK] = t[:8].astype(jnp.int32)


def _moe_pos(start, rt, tm):
    n = rt.shape[0]
    return pl.pallas_call(
        _pos_kernel,
        grid=(n // tm,),
        in_specs=[pl.BlockSpec((1, LANES), lambda i: (0, 0)), pl.BlockSpec((tm, LANES), lambda i: (i, 0))],
        out_specs=pl.BlockSpec((1, 8, tm), lambda i: (i, 0, 0)),
        out_shape=jax.ShapeDtypeStruct((n // tm, 8, tm), jnp.int32),
        compiler_params=_cparams("arbitrary"),
        name="moe_pos",
    )(start, rt)


def _moe_plan(counts, n_tiles):
    i32 = jnp.int32
    cnt = counts[0, :N_EXPERTS].astype(i32)
    end = jnp.cumsum(cnt)
    start = end - cnt
    start_lanes = jnp.pad(start.astype(jnp.float32), (0, LANES - N_EXPERTS)).reshape(1, LANES)
    first_t = start // MOE_TM
    items_e = jnp.where(cnt > 0, (end - 1) // MOE_TM - first_t + 1, 0)
    item_end = jnp.cumsum(items_e)
    n_items = item_end[-1]
    w = jnp.minimum(jnp.arange(n_tiles + N_EXPERTS - 1, dtype=i32), n_items - 1)
    e_w = jnp.minimum(jnp.sum((item_end[None, :] <= w[:, None]).astype(i32), axis=1), N_EXPERTS - 1)
    t_w = first_t[e_w] + (w - (item_end - items_e)[e_w])
    lo_w = jnp.maximum(start[e_w], t_w * MOE_TM)
    hi_w = jnp.minimum(end[e_w], (t_w + 1) * MOE_TM)
    return start_lanes, (t_w.astype(i32), e_w, lo_w.astype(i32), hi_w.astype(i32),
                         n_items.reshape(1).astype(i32))


def _pick_tile(n, pref):
    t = min(pref, n)
    while n % t:
        t //= 2
    return t


def kernel(x, w_in, kv_norm_g, w_uv, w_branch_a, sinks, w_branch_b, w_out, rel_bias, ln1_g, ln1_b,
           w_group, b_group, w_router, b_router, w_gate, w_up, w_down, ln2_g, ln2_b):
    bsz, seq, d = x.shape
    assert d == D_MODEL and seq % BLOCK == 0 and w_in.shape[0] == DEPTH == 1
    n = bsz * seq
    n_blk = seq // BLOCK
    bf, f32 = jnp.bfloat16, jnp.float32
    x2 = x.reshape(n, d)

    r = np.arange(BLOCK, dtype=np.int32)
    d_prev = r[None, :] + BLOCK - r[:, None]
    d_cur = r[None, :] - r[:, None]
    bkt_a = jnp.asarray(_t5_bucket_np(np.stack([d_prev, d_cur])))
    s2 = np.arange(2 * BLOCK, dtype=np.int32)
    bkt_b = jnp.asarray(_t5_bucket_np(r[None, :] + BLOCK - s2[:, None])[None])
    assert int(_t5_bucket_np(np.arange(BLOCK + 1, max(seq, BLOCK + 2))).min()) == N_BUCKETS - 1
    tab_t = rel_bias.astype(f32).T
    bias_a = _bias_tiles(tab_t, bkt_a, 0, A_HEADS, True, minus_far=True)
    bias_b = _bias_tiles(tab_t, bkt_b, A_HEADS, B_HEADS, True)[0]

    w0 = w_in[0]
    w_pack = jnp.concatenate(
        [w0[:, :_RAW_KW_END], jnp.zeros((d, _C_QB - _RAW_KW_END), w0.dtype), w0[:, _RAW_KW_END:]],
        axis=1).astype(bf)
    (q_blocks, ckv, ckv_tx, qidx_blocks, kidx, widx, qb_blocks, kb, vbx, ga, gb) = _input_projection(
        x2, w_pack, kv_norm_g[0].reshape(1, A_LAT).astype(f32), _pick_tile(seq, 512), seq)

    ckv3 = ckv.reshape(bsz, seq, A_LAT)
    widx_l = jnp.swapaxes(widx.reshape(bsz * n_blk, BLOCK, IDX_HEADS), 1, 2).reshape(
        bsz * n_blk, 1, IDX_HEADS * BLOCK)
    ya = _dsa_attention(qidx_blocks.reshape(bsz * n_blk, IDX_HEADS * BLOCK, IDX_DIM),
                        kidx.reshape(bsz, seq, IDX_DIM), widx_l,
                        q_blocks.reshape(bsz * n_blk, A_HEADS * BLOCK, A_LAT),
                        ckv3, ckv_tx, bias_a, w_uv[0].astype(bf), bsz, seq)

    yb = _swa_attention(sinks[0].astype(f32), qb_blocks, kb, vbx, bias_b, bsz, seq)

    w_route = jnp.concatenate(
        [w_group[0], w_router[0], jnp.zeros((d, LANES - N_GROUPS - N_EXPERTS), f32)], axis=1).astype(f32)
    b_route = jnp.concatenate(
        [b_group[0], b_router[0], jnp.zeros((LANES - N_GROUPS - N_EXPERTS,), f32)]).reshape(1, LANES).astype(f32)
    x1r, rt, counts = _merge_route(
        ya, yb, ga, gb, x2, w_branch_a[0].astype(bf), w_branch_b[0].astype(bf), w_out[0].astype(bf),
        ln1_g[0].reshape(1, d).astype(f32), ln1_b[0].reshape(1, d).astype(f32), w_route, b_route,
        _pick_tile(n, 512))

    assert (2 * n) % MOE_TM == 0
    start_lanes, items = _moe_plan(counts, 2 * n // MOE_TM)
    tm_io = _pick_tile(n, 512)
    pos_blocks = _moe_pos(start_lanes, rt, tm_io)
    xs = _moe_dispatch(pos_blocks, x1r, 2 * n, tm_io)
    ys = _moe_experts(items, xs, w_gate[0], w_up[0], w_down[0], MOE_TM)
    out = _moe_combine(pos_blocks, rt, x1r, ys, ln2_g[0].reshape(1, d).astype(f32),
                       ln2_b[0].reshape(1, d).astype(f32), tm_io)
    return out.reshape(bsz, seq, d)
```

```python
import functools
import math

import numpy as np
import jax
import jax.numpy as jnp
from jax import lax
from jax.experimental import pallas as pl
from jax.experimental.pallas import tpu as pltpu

D_MODEL = 1024
A_HEADS = 8
A_LAT = 128
A_HEAD_DIM = 64
IDX_HEADS = 8
IDX_DIM = 64
TOPK_MAX = 256
B_HEADS = 8
B_KV_HEADS = 2
B_HEAD_DIM = 64
WINDOW = 128
BLOCK = 128
N_BUCKETS = 32
MAX_DISTANCE = 128
N_GROUPS = 4
EXPERTS_PER_GROUP = 8
N_EXPERTS = 32
D_EXPERT = 256
DEPTH = 1
ALPHA = (2 * DEPTH) ** 0.25
LN_EPS = 1e-5
RMS_EPS = 1e-6

LANES = 128
INT_MIN = -(2 ** 31)
VMEM_LIMIT = 48 * 1024 * 1024

_NT = (((1,), (1,)), ((), ()))


def _cparams(*sem):
    return pltpu.CompilerParams(dimension_semantics=sem, vmem_limit_bytes=VMEM_LIMIT)


def _t5_bucket_np(dist):
    f32 = np.float32
    n = np.maximum(dist, 0)
    max_exact = N_BUCKETS // 2
    nf = np.maximum(n, 1).astype(f32)
    large = max_exact + (np.log(nf / f32(max_exact)) / f32(math.log(MAX_DISTANCE / max_exact))
                         * f32(N_BUCKETS - max_exact)).astype(np.int32)
    large = np.minimum(large, N_BUCKETS - 1)
    return np.where(n < max_exact, n, large).astype(np.int32)


def _bias_kernel(tab_ref, bkt_ref, out_ref, *, head0, minus_far):
    h = pl.program_id(0) + head0
    bkt = bkt_ref[...]
    far = tab_ref[h, N_BUCKETS - 1]
    acc = jnp.full(bkt.shape, far, jnp.float32)
    for b in range(N_BUCKETS - 1):
        acc = jnp.where(bkt == b, tab_ref[h, b], acc)
    out_ref[...] = acc - far if minus_far else acc


def _bias_tiles(tab_t, bkt, head0, n_heads, lane_major, minus_far=False):
    g, r, c = bkt.shape
    if lane_major:
        out_shape = jax.ShapeDtypeStruct((g, r, n_heads * c), jnp.float32)
        out_spec = pl.BlockSpec((g, r, c), lambda h: (0, 0, h))
    else:
        out_shape = jax.ShapeDtypeStruct((n_heads * g, r, c), jnp.float32)
        out_spec = pl.BlockSpec((g, r, c), lambda h: (h, 0, 0))
    return pl.pallas_call(
        functools.partial(_bias_kernel, head0=head0, minus_far=minus_far),
        grid=(n_heads,),
        in_specs=[pl.BlockSpec(memory_space=pltpu.SMEM),
                  pl.BlockSpec((g, r, c), lambda h: (0, 0, 0))],
        out_specs=out_spec,
        out_shape=out_shape,
        compiler_params=_cparams("arbitrary"),
        name="bias_tiles",
    )(tab_t, bkt)


_C_QLAT = 0
_C_CKV = _C_QLAT + A_HEADS * A_LAT
_C_QIDX = _C_CKV + A_LAT
_C_KW = _C_QIDX + IDX_HEADS * IDX_DIM
_C_QB = _C_KW + LANES
_C_KB = _C_QB + B_HEADS * B_HEAD_DIM
_C_VB = _C_KB + B_KV_HEADS * B_HEAD_DIM
_C_GA = _C_VB + B_KV_HEADS * B_HEAD_DIM
_C_GB = _C_GA + D_MODEL
_C_END = _C_GB + D_MODEL
_RAW_KW_END = _C_KW + IDX_DIM + IDX_HEADS


def _proj_kernel(x_ref, w_ref, g_ref, q_ref, ckv_ref, ckvt_ref, qidx_ref, kidx_ref, widx_ref,
                 qb_ref, kb_ref, vb_ref, ga_ref, gb_ref):
    xb = x_ref[...].astype(jnp.bfloat16)
    tm = xb.shape[0]

    def seg(lo, hi):
        return jnp.dot(xb, w_ref[:, lo:hi], preferred_element_type=jnp.float32)

    for h in range(A_HEADS):
        qh = (seg(_C_QLAT + h * A_LAT, _C_QLAT + (h + 1) * A_LAT) * (A_LAT ** -0.5)).astype(jnp.bfloat16)
        for r in range(tm // BLOCK):
            q_ref[r, h] = qh[r * BLOCK:(r + 1) * BLOCK]
    c = seg(_C_CKV, _C_QIDX)
    ms = jnp.mean(c * c, axis=-1, keepdims=True)
    cn = c * lax.rsqrt(ms + RMS_EPS) * g_ref[...]
    ckv_ref[...] = cn.astype(jnp.bfloat16)
    for r in range(tm // BLOCK):
        ckvt_ref[0, :A_LAT, r * BLOCK:(r + 1) * BLOCK] = cn[r * BLOCK:(r + 1) * BLOCK].T.astype(jnp.bfloat16)
    ckvt_ref[0, A_LAT:, :] = jnp.ones((8, tm), jnp.bfloat16)
    qif = seg(_C_QIDX, _C_KW).astype(jnp.bfloat16)
    for h in range(IDX_HEADS):
        for r in range(tm // BLOCK):
            qidx_ref[r, h] = qif[r * BLOCK:(r + 1) * BLOCK, h * IDX_DIM:(h + 1) * IDX_DIM]
    kw = seg(_C_KW, _C_QB)
    kidx_ref[...] = kw[:, :IDX_DIM].astype(jnp.bfloat16)
    widx_ref[...] = kw[:, IDX_DIM:IDX_DIM + IDX_HEADS]
    qbf = seg(_C_QB, _C_KB).astype(jnp.bfloat16)
    for h in range(B_HEADS):
        for r in range(tm // BLOCK):
            qb_ref[r, h] = qbf[r * BLOCK:(r + 1) * BLOCK, h * B_HEAD_DIM:(h + 1) * B_HEAD_DIM]
    kbf = seg(_C_KB, _C_VB).astype(jnp.bfloat16)
    vbf = seg(_C_VB, _C_GA)
    for g in range(B_KV_HEADS):
        kb_ref[g] = kbf[:, g * B_HEAD_DIM:(g + 1) * B_HEAD_DIM]
        vb_ref[g, 0, B_HEAD_DIM:, :] = jnp.ones((SWA_ONES, tm), jnp.bfloat16)
    for r in range(tm // BLOCK):
        vt = vbf[r * BLOCK:(r + 1) * BLOCK].T.astype(jnp.bfloat16)
        for g in range(B_KV_HEADS):
            vb_ref[g, 0, :B_HEAD_DIM, r * BLOCK:(r + 1) * BLOCK] = vt[g * B_HEAD_DIM:(g + 1) * B_HEAD_DIM]
    ga_ref[...] = jax.nn.sigmoid(seg(_C_GA, _C_GB))
    gb_ref[...] = jax.nn.sigmoid(seg(_C_GB, _C_END))


def _input_projection(x2, w_pack, kv_g, tm, seq):
    n = x2.shape[0]
    assert seq % tm == 0
    bf, f32 = jnp.bfloat16, jnp.float32
    row = lambda w: pl.BlockSpec((tm, w), lambda i: (i, 0))
    out_shape = (
        jax.ShapeDtypeStruct((n // BLOCK, A_HEADS, BLOCK, A_LAT), bf),
        jax.ShapeDtypeStruct((n, A_LAT), bf),
        jax.ShapeDtypeStruct((n // seq, A_LAT + 8, seq), bf),
        jax.ShapeDtypeStruct((n // BLOCK, IDX_HEADS, BLOCK, IDX_DIM), bf),
        jax.ShapeDtypeStruct((n, IDX_DIM), bf),
        jax.ShapeDtypeStruct((n, IDX_HEADS), f32),
        jax.ShapeDtypeStruct((n // BLOCK, B_HEADS, BLOCK, B_HEAD_DIM), bf),
        jax.ShapeDtypeStruct((B_KV_HEADS, n, B_HEAD_DIM), bf),
        jax.ShapeDtypeStruct((B_KV_HEADS, n // seq, B_HEAD_DIM + SWA_ONES, seq), bf),
        jax.ShapeDtypeStruct((n, D_MODEL), f32),
        jax.ShapeDtypeStruct((n, D_MODEL), f32),
    )
    out_specs = (
        pl.BlockSpec((tm // BLOCK, A_HEADS, BLOCK, A_LAT), lambda i: (i, 0, 0, 0)),
        row(A_LAT),
        pl.BlockSpec((1, A_LAT + 8, tm), lambda i: (i // (seq // tm), 0, i % (seq // tm))),
        pl.BlockSpec((tm // BLOCK, IDX_HEADS, BLOCK, IDX_DIM), lambda i: (i, 0, 0, 0)),
        row(IDX_DIM), row(IDX_HEADS),
        pl.BlockSpec((tm // BLOCK, B_HEADS, BLOCK, B_HEAD_DIM), lambda i: (i, 0, 0, 0)),
        pl.BlockSpec((B_KV_HEADS, tm, B_HEAD_DIM), lambda i: (0, i, 0)),
        pl.BlockSpec((B_KV_HEADS, 1, B_HEAD_DIM + SWA_ONES, tm),
                     lambda i: (0, i // (seq // tm), 0, i % (seq // tm))),
        row(D_MODEL), row(D_MODEL),
    )
    return pl.pallas_call(
        _proj_kernel,
        grid=(n // tm,),
        in_specs=[row(D_MODEL),
                  pl.BlockSpec((D_MODEL, _C_END), lambda i: (0, 0)),
                  pl.BlockSpec((1, A_LAT), lambda i: (0, 0))],
        out_specs=out_specs,
        out_shape=out_shape,
        compiler_params=_cparams("arbitrary"),
        name="input_projection",
    )(x2, w_pack, kv_g)


DSA_UNROLL = 4


def _bit_transpose32(words):
    w = list(words)
    j, m = 16, 0x0000FFFF
    while j:
        mask = jnp.int32(m - (1 << 32) if m >= (1 << 31) else m)
        k = 0
        while k < 32:
            t = (w[k] ^ lax.shift_right_logical(w[k + j], jnp.int32(j))) & mask
            w[k] = w[k] ^ t
            w[k + j] = w[k + j] ^ lax.shift_left(t, jnp.int32(j))
            k = (k + j + 1) & ~j
        j >>= 1
        m = (m ^ (m << j)) & 0xFFFFFFFF
    return w


def _dsa_kernel(qidx_ref, kidx_ref, widx_ref, q_ref, ckv_ref, ckvt_ref, bias_ref, wuv_ref,
                tri_ref, ya_ref, sc_ref, neg_ref, lg_ref, o_ref, planes_ref, sel_ref, *, k_sel,
                search_chunk):
    i = pl.program_id(1)
    f32 = jnp.float32
    s_loc = lax.broadcasted_iota(jnp.int32, (BLOCK, BLOCK), 0)
    t_loc = lax.broadcasted_iota(jnp.int32, (BLOCK, BLOCK), 1)
    causal_diag = s_loc <= t_loc
    idx_scale = IDX_DIM ** -0.5
    w_scale = IDX_HEADS ** -0.5

    def rows(j):
        return pl.ds(pl.multiple_of(j * BLOCK, BLOCK), BLOCK)

    sc_ref[...] = jnp.full(sc_ref.shape, -jnp.inf, f32)
    wts = (widx_ref[0] * w_scale) * idx_scale
    qi = qidx_ref[0]

    def score_keys(j):
        s = lax.dot_general(kidx_ref[0, rows(j), :], qi, _NT, preferred_element_type=f32)
        r = jnp.maximum(s, 0.0) * wts
        acc = r[:, :BLOCK]
        for h in range(1, IDX_HEADS):
            acc = acc + r[:, h * BLOCK:(h + 1) * BLOCK]
        return acc

    def fold_blocks(lo, hi, fn, init, combine):
        n = hi - lo
        n_grp = lax.shift_right_logical(n, DSA_UNROLL.bit_length() - 1)

        def group(k, acc):
            j = lo + DSA_UNROLL * k
            vals = [fn(j + u) for u in range(DSA_UNROLL)]
            while len(vals) > 1:
                vals = [combine(vals[a], vals[a + 1]) for a in range(0, len(vals), 2)]
            return combine(acc, vals[0])

        acc = lax.fori_loop(0, n_grp, group, init)
        return lax.fori_loop(lo + DSA_UNROLL * n_grp, hi, lambda j, a: combine(a, fn(j)), acc)

    def far_scores(j):
        sc_ref[rows(j), :] = score_keys(j)
        return jnp.int32(0)

    fold_blocks(0, i, far_scores, jnp.int32(0), lambda a, b: a)
    sc_ref[rows(i), :] = jnp.where(causal_diag, score_keys(i), -jnp.inf)

    n_chunks = ((i + 1) * BLOCK + search_chunk - 1) // search_chunk
    n_acc = 64

    def count(cmp, cand):
        def body(c, cnt):
            base = pl.multiple_of(c * search_chunk, search_chunk)
            for r in range(search_chunk // n_acc):
                cnt = cnt + cmp(sc_ref[pl.ds(base + r * n_acc, n_acc), :], cand).astype(f32)
            return cnt
        cnt = lax.fori_loop(0, n_chunks, body, jnp.zeros((n_acc, BLOCK), f32))
        return jnp.sum(cnt, axis=0, keepdims=True)

    def ordered_to_float(u):
        k = u ^ INT_MIN
        return lax.bitcast_convert_type(k ^ ((k >> 31) & 0x7FFFFFFF), f32)

    n_grp_keys = 32 * 8
    n_groups = ((i + 1) * BLOCK + n_grp_keys - 1) // n_grp_keys
    max_groups = planes_ref.shape[0]

    def build_planes(g, carry):
        words = []
        for v in range(32):
            s = sc_ref[pl.ds(pl.multiple_of(g * n_grp_keys + v * 8, 8), 8), :]
            bits = lax.bitcast_convert_type(s, jnp.int32)
            words.append(bits ^ ((bits >> 31) & 0x7FFFFFFF) ^ INT_MIN)
        planes = _bit_transpose32(words)
        for p in range(32):
            planes_ref[g, p] = planes[p]
        return carry

    lax.fori_loop(0, n_groups, build_planes, 0)

    def select_bit(n, carry):
        alive, k_left, thr_u = carry
        b = 31 - n
        ones = [alive[g] & planes_ref[jnp.minimum(g, n_groups - 1), n] for g in range(max_groups)]
        cnt = lax.population_count(ones[0])
        for g in range(1, max_groups):
            cnt = cnt + lax.population_count(ones[g])
        c1 = jnp.sum(cnt.astype(f32), axis=0, keepdims=True)
        take = c1 >= k_left
        take_w = jnp.broadcast_to(take, (8, BLOCK))
        alive = [jnp.where(take_w, ones[g], alive[g] ^ ones[g]) for g in range(max_groups)]
        thr_u = jnp.where(take, thr_u | lax.shift_left(jnp.int32(1), b), thr_u)
        return alive, jnp.where(take, k_left, k_left - c1), thr_u

    alive0 = [jnp.where(g < n_groups, jnp.full((8, BLOCK), -1, jnp.int32), 0) for g in range(max_groups)]
    _, _, thr_u = lax.fori_loop(
        0, 32, select_bit, (alive0, jnp.full((1, BLOCK), k_sel, f32), jnp.zeros((1, BLOCK), jnp.int32)))
    thr = ordered_to_float(thr_u)
    c_ge = count(jnp.greater_equal, thr)
    c_gt = count(jnp.greater, thr)
    sel_ref[0:1, :] = thr
    sel_ref[1:2, :] = c_ge
    sel_ref[2:3, :] = c_gt
    consistent = jnp.min(((c_ge >= k_sel) & (c_gt < k_sel)).astype(f32)) > 0.0

    @pl.when(jnp.logical_not(consistent))
    def _():
        def search_bit(b, prefix_u):
            cand_u = prefix_u | lax.shift_left(jnp.int32(1), 31 - b)
            cnt = count(jnp.greater_equal, ordered_to_float(cand_u))
            return jnp.where(cnt >= k_sel, cand_u, prefix_u)

        t = ordered_to_float(lax.fori_loop(0, 32, search_bit, jnp.zeros((1, BLOCK), jnp.int32)))
        t = jnp.where(t != t, -jnp.inf, t)
        sel_ref[0:1, :] = t
        sel_ref[1:2, :] = count(jnp.greater_equal, t)
        sel_ref[2:3, :] = count(jnp.greater, t)

    thr = sel_ref[0:1, :]
    c_ge = sel_ref[1:2, :]
    c_gt = sel_ref[2:3, :]
    n_tie = k_sel - c_gt
    any_tie = jnp.max(((c_ge > k_sel) & (thr > -jnp.inf)).astype(f32)) > 0.0

    @pl.when(jnp.logical_not(any_tie))
    def _():
        def body(j, carry):
            neg_ref[rows(j), :] = jnp.where(sc_ref[rows(j), :] >= thr, 0.0, -jnp.inf)
            return carry
        lax.fori_loop(0, i, body, 0)
        neg_ref[rows(i), :] = jnp.where((sc_ref[rows(i), :] >= thr) & causal_diag, 0.0, -jnp.inf)

    @pl.when(any_tie)
    def _():
        tri = tri_ref[...]

        def block(j, tie_seen):
            kc = sc_ref[rows(j), :]
            eq = kc == thr
            tie_rank = jnp.dot(tri, eq.astype(jnp.bfloat16), preferred_element_type=f32) + tie_seen
            sel = (kc > thr) | (eq & (tie_rank <= n_tie))
            return sel, tie_seen + jnp.sum(eq.astype(f32), axis=0, keepdims=True)

        def body(j, tie_seen):
            sel, tie_seen = block(j, tie_seen)
            neg_ref[rows(j), :] = jnp.where(sel, 0.0, -jnp.inf)
            return tie_seen

        tie_seen = lax.fori_loop(0, i, body, jnp.zeros((1, BLOCK), f32))
        sel, _ = block(i, tie_seen)
        neg_ref[rows(i), :] = jnp.where(sel & causal_diag, 0.0, -jnp.inf)

    q_all = q_ref[0]

    def logits(j):
        lg = lax.dot_general(ckv_ref[0, rows(j), :], q_all, _NT, preferred_element_type=f32)
        return lg + jnp.concatenate([neg_ref[rows(j), :]] * A_HEADS, axis=1)

    def far_logits(j):
        lg = logits(j)
        lg_ref[rows(j), :] = lg
        return jnp.max(lg, axis=0, keepdims=True)

    def near_logits(j):
        lg = logits(j) + bias_ref[j - i + 1]
        lg_ref[rows(j), :] = lg
        return jnp.max(lg, axis=0, keepdims=True)

    n_far = jnp.maximum(i - 1, 0)
    m_run = fold_blocks(0, n_far, far_logits, jnp.full((1, A_HEADS * BLOCK), -jnp.inf, f32), jnp.maximum)
    m_run = fold_blocks(n_far, i + 1, near_logits, m_run, jnp.maximum)

    o_ref[...] = jnp.zeros_like(o_ref)

    def pv_rows(r):
        p = jnp.exp(lg_ref[r, :] - m_run).astype(jnp.bfloat16)
        o_ref[...] += jnp.dot(ckvt_ref[0, :, r], p, preferred_element_type=f32)

    def pv_group(k, carry):
        span = DSA_UNROLL * BLOCK
        pv_rows(pl.ds(pl.multiple_of(k * span, span), span))
        return carry

    def pv_single(j, carry):
        pv_rows(rows(j))
        return carry

    n_grp = lax.shift_right_logical(i + 1, DSA_UNROLL.bit_length() - 1)
    lax.fori_loop(0, n_grp, pv_group, 0)
    lax.fori_loop(DSA_UNROLL * n_grp, i + 1, pv_single, 0)

    o_t = o_ref[:A_LAT, :] / o_ref[A_LAT:A_LAT + 1, :]

    for h in range(A_HEADS):
        o_h = o_t[:, h * BLOCK:(h + 1) * BLOCK].T.astype(jnp.bfloat16)
        y_h = jnp.dot(o_h, wuv_ref[h], preferred_element_type=f32)
        ya_ref[:, h * A_HEAD_DIM:(h + 1) * A_HEAD_DIM] = y_h.astype(ya_ref.dtype)


def _dsa_attention(qidx_blocks, kidx, widx_l, q_blocks, ckv, ckv_tx, bias_near, wuv, bsz, seq):
    n_blk = seq // BLOCK
    k_sel = min(TOPK_MAX, seq // 4)
    search_chunk = min(512, seq)
    tri = jnp.asarray(np.tril(np.ones((BLOCK, BLOCK), np.float32)), jnp.bfloat16)
    kern = functools.partial(_dsa_kernel, k_sel=float(k_sel), search_chunk=search_chunk)
    blk = lambda b, i: (b * n_blk + i, 0, 0)
    return pl.pallas_call(
        kern,
        grid=(bsz, n_blk),
        in_specs=[
            pl.BlockSpec((1, IDX_HEADS * BLOCK, IDX_DIM), blk),
            pl.BlockSpec((1, seq, IDX_DIM), lambda b, i: (b, 0, 0)),
            pl.BlockSpec((1, 1, IDX_HEADS * BLOCK), blk),
            pl.BlockSpec((1, A_HEADS * BLOCK, A_LAT), blk),
            pl.BlockSpec((1, seq, A_LAT), lambda b, i: (b, 0, 0)),
            pl.BlockSpec((1, A_LAT + 8, seq), lambda b, i: (b, 0, 0)),
            pl.BlockSpec((2, BLOCK, A_HEADS * BLOCK), lambda b, i: (0, 0, 0)),
            pl.BlockSpec((A_HEADS, A_LAT, A_HEAD_DIM), lambda b, i: (0, 0, 0)),
            pl.BlockSpec((BLOCK, BLOCK), lambda b, i: (0, 0)),
        ],
        out_specs=pl.BlockSpec((BLOCK, A_HEADS * A_HEAD_DIM), lambda b, i: (b * n_blk + i, 0)),
        out_shape=jax.ShapeDtypeStruct((bsz * seq, A_HEADS * A_HEAD_DIM), jnp.bfloat16),
        scratch_shapes=[pltpu.VMEM((seq, BLOCK), jnp.float32),
                        pltpu.VMEM((seq, BLOCK), jnp.float32),
                        pltpu.VMEM((seq, A_HEADS * BLOCK), jnp.float32),
                        pltpu.VMEM((A_LAT + 8, A_HEADS * BLOCK), jnp.float32),
                        pltpu.VMEM((seq // 256, 32, 8, BLOCK), jnp.int32),
                        pltpu.VMEM((8, BLOCK), jnp.float32)],
        compiler_params=_cparams("arbitrary", "arbitrary"),
        name="dsa_attention",
    )(qidx_blocks, kidx, widx_l, q_blocks, ckv, ckv_tx, bias_near, wuv, tri)


SWA_QB = 4
SWA_ONES = 16


def _swa_kernel(sink_ref, q_ref, kp_ref, kc_ref, vp_ref, vc_ref, bias_ref, yb_ref, *, qb):
    i = pl.program_id(1)
    f32 = jnp.float32
    grp = B_HEADS // B_KV_HEADS
    cols = grp * BLOCK
    s_loc = lax.broadcasted_iota(jnp.int32, (BLOCK, BLOCK), 0)
    t_loc = lax.broadcasted_iota(jnp.int32, (BLOCK, BLOCK), 1)
    neg_cur = jnp.concatenate([jnp.where(s_loc <= t_loc, 0.0, -jnp.inf)] * grp, axis=1)
    in_prev = s_loc > t_loc
    head_of_col = lax.broadcasted_iota(jnp.int32, (1, cols), 1) >> (BLOCK.bit_length() - 1)
    scale = B_HEAD_DIM ** -0.5
    for g in range(B_KV_HEADS):
        sink = jnp.zeros((1, cols), f32)
        for hh in range(grp):
            sink = jnp.where(head_of_col == hh, sink_ref[g * grp + hh], sink)
        bias_p = bias_ref[:BLOCK, g * cols:(g + 1) * cols]
        bias_c = bias_ref[BLOCK:, g * cols:(g + 1) * cols]
        for r in range(qb):
            has_prev = jnp.zeros_like(s_loc) + jnp.minimum(i * qb + r, 1)
            neg_prev = jnp.concatenate([jnp.where(in_prev & (has_prev > 0), 0.0, -jnp.inf)] * grp, axis=1)
            q = q_ref[r, g * grp:(g + 1) * grp].reshape(cols, B_HEAD_DIM)
            if r == 0:
                kp, vp = kp_ref[g], vp_ref[g, 0]
            else:
                kp, vp = kc_ref[g, (r - 1) * BLOCK:r * BLOCK], vc_ref[g, 0, :, (r - 1) * BLOCK:r * BLOCK]
            kc, vc = kc_ref[g, r * BLOCK:(r + 1) * BLOCK], vc_ref[g, 0, :, r * BLOCK:(r + 1) * BLOCK]
            lp = lax.dot_general(kp, q, _NT, preferred_element_type=f32) * scale + bias_p + neg_prev
            lc = lax.dot_general(kc, q, _NT, preferred_element_type=f32) * scale + bias_c + neg_cur
            m = jnp.maximum(jnp.max(jnp.maximum(lp, lc), axis=0, keepdims=True), sink)
            pp = jnp.exp(lp - m).astype(jnp.bfloat16)
            pc = jnp.exp(lc - m).astype(jnp.bfloat16)
            ox = (jnp.dot(vp, pp, preferred_element_type=f32)
                  + jnp.dot(vc, pc, preferred_element_type=f32))
            den = ox[B_HEAD_DIM:B_HEAD_DIM + 1, :] + jnp.exp(sink - m)
            o = ox[:B_HEAD_DIM, :] / den
            for pair in range(grp // 2):
                two = jnp.concatenate([o[:, (2 * pair) * BLOCK:(2 * pair + 1) * BLOCK],
                                       o[:, (2 * pair + 1) * BLOCK:(2 * pair + 2) * BLOCK]], axis=0)
                h0 = g * grp + 2 * pair
                yb_ref[r * BLOCK:(r + 1) * BLOCK, h0 * B_HEAD_DIM:(h0 + 2) * B_HEAD_DIM] = (
                    two.T.astype(yb_ref.dtype))


def _swa_attention(sinks, q_blocks, kb, vbt, bias_b, bsz, seq):
    n_blk = seq // BLOCK
    qb = SWA_QB if n_blk % SWA_QB == 0 else 1
    n_step = n_blk // qb
    cur = lambda b, i: (0, b * n_step + i, 0)
    prev = lambda b, i: (0, b * n_blk + jnp.maximum(i * qb - 1, 0), 0)
    vrows = B_HEAD_DIM + SWA_ONES
    return pl.pallas_call(
        functools.partial(_swa_kernel, qb=qb),
        grid=(bsz, n_step),
        in_specs=[
            pl.BlockSpec(memory_space=pltpu.SMEM),
            pl.BlockSpec((qb, B_HEADS, BLOCK, B_HEAD_DIM), lambda b, i: (b * n_step + i, 0, 0, 0)),
            pl.BlockSpec((B_KV_HEADS, BLOCK, B_HEAD_DIM), prev),
            pl.BlockSpec((B_KV_HEADS, qb * BLOCK, B_HEAD_DIM), cur),
            pl.BlockSpec((B_KV_HEADS, 1, vrows, BLOCK), lambda b, i: (0, b, 0, jnp.maximum(i * qb - 1, 0))),
            pl.BlockSpec((B_KV_HEADS, 1, vrows, qb * BLOCK), lambda b, i: (0, b, 0, i)),
            pl.BlockSpec((2 * BLOCK, B_HEADS * BLOCK), lambda b, i: (0, 0)),
        ],
        out_specs=pl.BlockSpec((qb * BLOCK, B_HEADS * B_HEAD_DIM), lambda b, i: (b * n_step + i, 0)),
        out_shape=jax.ShapeDtypeStruct((bsz * seq, B_HEADS * B_HEAD_DIM), jnp.bfloat16),
        compiler_params=_cparams("arbitrary", "arbitrary"),
        name="swa_attention",
    )(sinks, q_blocks, kb, kb, vbt, vbt, bias_b)


def _layer_norm(h, g, b):
    mu = jnp.mean(h, axis=-1, keepdims=True)
    d = h - mu
    var = jnp.mean(d * d, axis=-1, keepdims=True)
    return d * lax.rsqrt(var + LN_EPS) * g + b


def _merge_kernel(ya_ref, yb_ref, ga_ref, gb_ref, x_ref, wa_ref, wb_ref, wo_ref, g1_ref, b1_ref,
                  wrh_ref, wrl_ref, br_ref, tri_ref, x1_ref, rt_ref, cnt_ref, carry_ref):
    f32 = jnp.float32
    pa = jnp.dot(ya_ref[...], wa_ref[...], preferred_element_type=f32)
    pb = jnp.dot(yb_ref[...], wb_ref[...], preferred_element_type=f32)
    merged = ga_ref[...] * pa + gb_ref[...] * pb
    h = ALPHA * x_ref[...] + jnp.dot(merged.astype(jnp.bfloat16), wo_ref[...],
                                     preferred_element_type=f32)
    x1 = _layer_norm(h, g1_ref[...], b1_ref[...])
    _store_tile_rows(x1_ref, x1)

    x_hi = x1.astype(jnp.bfloat16)
    x_lo = (x1 - x_hi.astype(f32)).astype(jnp.bfloat16)
    lg = (jnp.dot(x_hi, wrh_ref[...], preferred_element_type=f32)
          + jnp.dot(x_hi, wrl_ref[...], preferred_element_type=f32)
          + jnp.dot(x_lo, wrh_ref[...], preferred_element_type=f32)) + br_ref[...]
    lane_i = lax.broadcasted_iota(jnp.int32, lg.shape, 1)
    lane = lane_i.astype(f32)
    big = jnp.float32(1 << 20)
    is_g = lane_i < N_GROUPS
    gl = jnp.where(is_g, lg, -jnp.inf)
    gmax = jnp.max(gl, axis=1, keepdims=True)
    g_sel = jnp.min(jnp.where(gl == gmax, lane, big), axis=1, keepdims=True)
    g_w = 1.0 / jnp.sum(jnp.where(is_g, jnp.exp(gl - gmax), 0.0), axis=1, keepdims=True)
    e_id = lane_i - N_GROUPS
    e_grp = (e_id >> 3).astype(f32)
    in_grp = (e_id >= 0) & (e_id < N_EXPERTS) & (e_grp == g_sel)
    el = jnp.where(in_grp, lg, -jnp.inf)
    emax = jnp.max(el, axis=1, keepdims=True)
    ee = jnp.where(in_grp, jnp.exp(el - emax), 0.0)
    ep = ee / jnp.sum(ee, axis=1, keepdims=True)
    epm = jnp.where(in_grp, ep, -1.0)
    p1 = jnp.max(epm, axis=1, keepdims=True)
    i1 = jnp.min(jnp.where(epm == p1, lane, big), axis=1, keepdims=True)
    epm2 = jnp.where(lane == i1, -1.0, epm)
    p2 = jnp.max(epm2, axis=1, keepdims=True)
    i2 = jnp.min(jnp.where(epm2 == p2, lane, big), axis=1, keepdims=True)
    psum = p1 + p2
    w1 = g_w * p1 / psum
    w2 = g_w * p2 / psum
    @pl.when(pl.program_id(0) == 0)
    def _():
        carry_ref[...] = jnp.zeros_like(carry_ref)

    oh1 = lane + N_GROUPS == i1
    oh2 = lane + N_GROUPS == i2
    oh = jnp.where(oh1 | oh2, 1.0, 0.0)
    prefix = jnp.dot(tri_ref[...], oh.astype(jnp.bfloat16), preferred_element_type=f32) + carry_ref[...]
    rank1 = jnp.sum(jnp.where(oh1, prefix, 0.0), axis=1, keepdims=True)
    rank2 = jnp.sum(jnp.where(oh2, prefix, 0.0), axis=1, keepdims=True)
    carry_ref[...] += jnp.sum(oh, axis=0, keepdims=True)
    cnt_ref[...] = carry_ref[...]
    rec = (i1 - N_GROUPS, i2 - N_GROUPS, w1, w2, rank1, rank2)
    rt = jnp.zeros(lg.shape, f32)
    for k, v in enumerate(rec):
        rt = jnp.where(lane_i == k, v, rt)
    rt_ref[...] = rt


def _merge_route(ya, yb, ga, gb, x2, wa, wb, wo, g1, b1, wr, br, tm):
    wr_hi = wr.astype(jnp.bfloat16)
    wr_lo = (wr - wr_hi.astype(jnp.float32)).astype(jnp.bfloat16)
    n = x2.shape[0]
    row = lambda w: pl.BlockSpec((tm, w), lambda i: (i, 0))
    full = lambda r, c: pl.BlockSpec((r, c), lambda i: (0, 0))
    aw, bw = A_HEADS * A_HEAD_DIM, B_HEADS * B_HEAD_DIM
    tri = jnp.asarray(np.tril(np.ones((tm, tm), np.float32), -1), jnp.bfloat16)
    return pl.pallas_call(
        _merge_kernel,
        grid=(n // tm,),
        in_specs=[row(aw), row(bw), row(D_MODEL), row(D_MODEL), row(D_MODEL),
                  full(aw, D_MODEL), full(bw, D_MODEL), full(D_MODEL, D_MODEL),
                  full(1, D_MODEL), full(1, D_MODEL), full(D_MODEL, LANES), full(D_MODEL, LANES),
                  full(1, LANES), full(tm, tm)],
        out_specs=(pl.BlockSpec((tm * ROW_TILE, LANES), lambda i: (i, 0)), row(LANES), full(1, LANES)),
        out_shape=(jax.ShapeDtypeStruct((n * ROW_TILE, LANES), jnp.float32),
                   jax.ShapeDtypeStruct((n, LANES), jnp.float32),
                   jax.ShapeDtypeStruct((1, LANES), jnp.float32)),
        scratch_shapes=[pltpu.VMEM((1, LANES), jnp.float32)],
        compiler_params=_cparams("arbitrary"),
        name="merge_route",
    )(ya, yb, ga, gb, x2, wa, wb, wo, g1, b1, wr_hi, wr_lo, br, tri)


ROW_TILE = D_MODEL // LANES


def _store_tile_rows(ref, val):
    rows = val.shape[0]
    for j in range(ROW_TILE):
        ref[pl.ds(j, rows, stride=ROW_TILE), :] = val[:, j * LANES:(j + 1) * LANES]


def _load_tile_rows(ref, rows):
    return jnp.concatenate([ref[pl.ds(j, rows, stride=ROW_TILE), :] for j in range(ROW_TILE)], axis=1)


DMA_UNROLL = 8


def _dispatch_kernel(pos_ref, x_ref, xs_hbm, sem, *, tm):
    def issue(k, carry):
        for u in range(DMA_UNROLL):
            r = k * DMA_UNROLL + u
            src = x_ref.at[pl.ds(pl.multiple_of(r * ROW_TILE, ROW_TILE), ROW_TILE)]
            for slot in range(2):
                p = pos_ref[0, slot, r]
                dst = xs_hbm.at[pl.ds(pl.multiple_of(p * ROW_TILE, ROW_TILE), ROW_TILE)]
                pltpu.make_async_copy(src, dst, sem).start(priority=slot)
        return carry

    lax.fori_loop(0, tm // DMA_UNROLL, issue, 0)
    for slot in range(2):
        pltpu.make_async_copy(x_ref, xs_hbm.at[pl.ds(0, tm * ROW_TILE)], sem).wait()


def _moe_dispatch(pos, x1r, n_rows, tm):
    n = x1r.shape[0] // ROW_TILE
    return pl.pallas_call(
        functools.partial(_dispatch_kernel, tm=tm),
        grid=(n // tm,),
        in_specs=[pl.BlockSpec((1, 8, tm), lambda i: (i, 0, 0), memory_space=pltpu.SMEM),
                  pl.BlockSpec((tm * ROW_TILE, LANES), lambda i: (i, 0))],
        out_specs=pl.BlockSpec(memory_space=pl.ANY),
        out_shape=jax.ShapeDtypeStruct((n_rows * ROW_TILE, LANES), jnp.float32),
        scratch_shapes=[pltpu.SemaphoreType.DMA],
        compiler_params=_cparams("arbitrary"),
        name="moe_dispatch",
    )(pos, x1r)


def _expert_kernel(tile_ref, exp_ref, lo_ref, hi_ref, nit_ref, xs_ref, wg_ref, wu_ref, wd_ref, ys_ref,
                   wgb_ref, wub_ref, wdb_ref, *, tm):
    f32 = jnp.float32
    w = pl.program_id(0)
    prev = jnp.maximum(w - 1, 0)

    @pl.when(jnp.logical_or(w == 0, exp_ref[prev] != exp_ref[w]))
    def _():
        wgb_ref[...] = wg_ref[0].astype(jnp.bfloat16)
        wub_ref[...] = wu_ref[0].astype(jnp.bfloat16)
        wdb_ref[...] = wd_ref[0].astype(jnp.bfloat16)

    @pl.when(w < nit_ref[0])
    def _():
        xb = _load_tile_rows(xs_ref, tm).astype(jnp.bfloat16)
        g = jnp.dot(xb, wgb_ref[...], preferred_element_type=f32)
        u = jnp.dot(xb, wub_ref[...], preferred_element_type=f32)
        hmid = (g * jax.nn.sigmoid(g) * u).astype(jnp.bfloat16)
        y = jnp.dot(hmid, wdb_ref[...], preferred_element_type=f32)
        row = tile_ref[w] * tm + lax.broadcasted_iota(jnp.int32, (tm, 1), 0)
        mine = (row >= lo_ref[w]) & (row < hi_ref[w])
        first_visit = jnp.logical_or(w == 0, tile_ref[prev] != tile_ref[w])

        @pl.when(first_visit)
        def _():
            _store_tile_rows(ys_ref, jnp.where(mine, y, 0.0))

        @pl.when(jnp.logical_not(first_visit))
        def _():
            _store_tile_rows(ys_ref, jnp.where(mine, y, _load_tile_rows(ys_ref, tm)))


def _moe_experts(items, xs, wg, wu, wd, tm):
    n_items = items[0].shape[0]
    grid_spec = pltpu.PrefetchScalarGridSpec(
        num_scalar_prefetch=5,
        grid=(n_items,),
        in_specs=[
            pl.BlockSpec((tm * ROW_TILE, LANES), lambda w, t, e, lo, hi, n: (t[w], 0)),
            pl.BlockSpec((1, D_MODEL, D_EXPERT), lambda w, t, e, lo, hi, n: (e[w], 0, 0)),
            pl.BlockSpec((1, D_MODEL, D_EXPERT), lambda w, t, e, lo, hi, n: (e[w], 0, 0)),
            pl.BlockSpec((1, D_EXPERT, D_MODEL), lambda w, t, e, lo, hi, n: (e[w], 0, 0)),
        ],
        out_specs=pl.BlockSpec((tm * ROW_TILE, LANES), lambda w, t, e, lo, hi, n: (t[w], 0)),
        scratch_shapes=[pltpu.VMEM((D_MODEL, D_EXPERT), jnp.bfloat16),
                        pltpu.VMEM((D_MODEL, D_EXPERT), jnp.bfloat16),
                        pltpu.VMEM((D_EXPERT, D_MODEL), jnp.bfloat16)],
    )
    return pl.pallas_call(
        functools.partial(_expert_kernel, tm=tm),
        grid_spec=grid_spec,
        out_shape=jax.ShapeDtypeStruct(xs.shape, jnp.float32),
        compiler_params=_cparams("arbitrary"),
        name="moe_experts",
    )(*items, xs, wg, wu, wd)


def _combine_kernel(pos_ref, rt_ref, x_ref, ys_hbm, g2_ref, b2_ref, out_ref, buf_a, buf_b, sem, *, tm):
    bufs = (buf_a, buf_b)
    th = tm // 2
    half = lambda ref, h: ref.at[pl.ds(h * th * ROW_TILE, th * ROW_TILE)]

    def issue_half(h):
        def issue(k, carry):
            for u in range(DMA_UNROLL):
                r = h * th + k * DMA_UNROLL + u
                for slot in range(2):
                    p = pos_ref[0, slot, r]
                    src = ys_hbm.at[pl.ds(pl.multiple_of(p * ROW_TILE, ROW_TILE), ROW_TILE)]
                    dst = bufs[slot].at[pl.ds(pl.multiple_of(r * ROW_TILE, ROW_TILE), ROW_TILE)]
                    pltpu.make_async_copy(src, dst, sem.at[h]).start(priority=slot)
            return carry
        lax.fori_loop(0, th // DMA_UNROLL, issue, 0)

    def finish_half(h):
        for slot in range(2):
            pltpu.make_async_copy(ys_hbm.at[pl.ds(0, th * ROW_TILE)], half(bufs[slot], h), sem.at[h]).wait()
        rt = rt_ref[h * th:(h + 1) * th, :]
        ffn = (rt[:, 2:3] * _load_tile_rows(half(buf_a, h), th)
               + rt[:, 3:4] * _load_tile_rows(half(buf_b, h), th))
        out_ref[h * th:(h + 1) * th, :] = _layer_norm(
            ALPHA * _load_tile_rows(half(x_ref, h), th) + ffn, g2_ref[...], b2_ref[...])

    issue_half(0)
    issue_half(1)
    finish_half(0)
    finish_half(1)


def _moe_combine(pos, rt, x1r, ys, g2, b2, tm):
    n = rt.shape[0]
    return pl.pallas_call(
        functools.partial(_combine_kernel, tm=tm),
        grid=(n // tm,),
        in_specs=[pl.BlockSpec((1, 8, tm), lambda i: (i, 0, 0), memory_space=pltpu.SMEM),
                  pl.BlockSpec((tm, LANES), lambda i: (i, 0)),
                  pl.BlockSpec((tm * ROW_TILE, LANES), lambda i: (i, 0)),
                  pl.BlockSpec(memory_space=pl.ANY),
                  pl.BlockSpec((1, D_MODEL), lambda i: (0, 0)),
                  pl.BlockSpec((1, D_MODEL), lambda i: (0, 0))],
        out_specs=pl.BlockSpec((tm, D_MODEL), lambda i: (i, 0)),
        out_shape=jax.ShapeDtypeStruct((n, D_MODEL), jnp.float32),
        scratch_shapes=[pltpu.VMEM((tm * ROW_TILE, LANES), jnp.float32),
                        pltpu.VMEM((tm * ROW_TILE, LANES), jnp.float32),
                        pltpu.SemaphoreType.DMA((2,))],
        compiler_params=_cparams("arbitrary"),
        name="moe_combine",
    )(pos, rt, x1r, ys, g2, b2)


MOE_TM = 256


def _pos_kernel(start_ref, rt_ref, pos_ref):
    rt = rt_ref[...]
    lane = lax.broadcasted_iota(jnp.int32, rt.shape, 1)
    lane_f = lane.astype(jnp.float32)
    start = start_ref[...]
    tile = jnp.zeros(rt.shape, jnp.float32)
    for slot in range(2):
        first_row = jnp.sum(jnp.where(lane_f == rt[:, slot:slot + 1], start, 0.0), axis=1, keepdims=True)
        tile = jnp.where(lane == slot, first_row + rt[:, 4 + slot:5 + slot], tile)
    for r in range(rt.shape[0] // BLOCK):
        t = tile[r * BLOCK:(r + 1) * BLOCK].T
        pos_ref[0, :, r * BLOCK:(r + 1) * BLOCK] = t[:8].astype(jnp.int32)


def _moe_pos(start, rt, tm):
    n = rt.shape[0]
    return pl.pallas_call(
        _pos_kernel,
        grid=(n // tm,),
        in_specs=[pl.BlockSpec((1, LANES), lambda i: (0, 0)), pl.BlockSpec((tm, LANES), lambda i: (i, 0))],
        out_specs=pl.BlockSpec((1, 8, tm), lambda i: (i, 0, 0)),
        out_shape=jax.ShapeDtypeStruct((n // tm, 8, tm), jnp.int32),
        compiler_params=_cparams("arbitrary"),
        name="moe_pos",
    )(start, rt)


def _moe_plan(counts, n_tiles):
    i32 = jnp.int32
    cnt = counts[0, :N_EXPERTS].astype(i32)
    end = jnp.cumsum(cnt)
    start = end - cnt
    start_lanes = jnp.pad(start.astype(jnp.float32), (0, LANES - N_EXPERTS)).reshape(1, LANES)
    first_t = start // MOE_TM
    items_e = jnp.where(cnt > 0, (end - 1) // MOE_TM - first_t + 1, 0)
    item_end = jnp.cumsum(items_e)
    n_items = item_end[-1]
    w = jnp.minimum(jnp.arange(n_tiles + N_EXPERTS - 1, dtype=i32), n_items - 1)
    e_w = jnp.minimum(jnp.sum((item_end[None, :] <= w[:, None]).astype(i32), axis=1), N_EXPERTS - 1)
    t_w = first_t[e_w] + (w - (item_end - items_e)[e_w])
    lo_w = jnp.maximum(start[e_w], t_w * MOE_TM)
    hi_w = jnp.minimum(end[e_w], (t_w + 1) * MOE_TM)
    return start_lanes, (t_w.astype(i32), e_w, lo_w.astype(i32), hi_w.astype(i32),
                         n_items.reshape(1).astype(i32))


def _pick_tile(n, pref):
    t = min(pref, n)
    while n % t:
        t //= 2
    return t


def kernel(x, w_in, kv_norm_g, w_uv, w_branch_a, sinks, w_branch_b, w_out, rel_bias, ln1_g, ln1_b,
           w_group, b_group, w_router, b_router, w_gate, w_up, w_down, ln2_g, ln2_b):
    bsz, seq, d = x.shape
    assert d == D_MODEL and seq % BLOCK == 0 and w_in.shape[0] == DEPTH == 1
    n = bsz * seq
    n_blk = seq // BLOCK
    bf, f32 = jnp.bfloat16, jnp.float32
    x2 = x.reshape(n, d)

    r = np.arange(BLOCK, dtype=np.int32)
    d_prev = r[None, :] + BLOCK - r[:, None]
    d_cur = r[None, :] - r[:, None]
    bkt_a = jnp.asarray(_t5_bucket_np(np.stack([d_prev, d_cur])))
    s2 = np.arange(2 * BLOCK, dtype=np.int32)
    bkt_b = jnp.asarray(_t5_bucket_np(r[None, :] + BLOCK - s2[:, None])[None])
    assert int(_t5_bucket_np(np.arange(BLOCK + 1, max(seq, BLOCK + 2))).min()) == N_BUCKETS - 1
    tab_t = rel_bias.astype(f32).T
    bias_a = _bias_tiles(tab_t, bkt_a, 0, A_HEADS, True, minus_far=True)
    bias_b = _bias_tiles(tab_t, bkt_b, A_HEADS, B_HEADS, True)[0]

    w0 = w_in[0]
    w_pack = jnp.concatenate(
        [w0[:, :_RAW_KW_END], jnp.zeros((d, _C_QB - _RAW_KW_END), w0.dtype), w0[:, _RAW_KW_END:]],
        axis=1).astype(bf)
    (q_blocks, ckv, ckv_tx, qidx_blocks, kidx, widx, qb_blocks, kb, vbx, ga, gb) = _input_projection(
        x2, w_pack, kv_norm_g[0].reshape(1, A_LAT).astype(f32), _pick_tile(seq, 512), seq)

    ckv3 = ckv.reshape(bsz, seq, A_LAT)
    widx_l = jnp.swapaxes(widx.reshape(bsz * n_blk, BLOCK, IDX_HEADS), 1, 2).reshape(
        bsz * n_blk, 1, IDX_HEADS * BLOCK)
    ya = _dsa_attention(qidx_blocks.reshape(bsz * n_blk, IDX_HEADS * BLOCK, IDX_DIM),
                        kidx.reshape(bsz, seq, IDX_DIM), widx_l,
                        q_blocks.reshape(bsz * n_blk, A_HEADS * BLOCK, A_LAT),
                        ckv3, ckv_tx, bias_a, w_uv[0].astype(bf), bsz, seq)

    yb = _swa_attention(sinks[0].astype(f32), qb_blocks, kb, vbx, bias_b, bsz, seq)

    w_route = jnp.concatenate(
        [w_group[0], w_router[0], jnp.zeros((d, LANES - N_GROUPS - N_EXPERTS), f32)], axis=1).astype(f32)
    b_route = jnp.concatenate(
        [b_group[0], b_router[0], jnp.zeros((LANES - N_GROUPS - N_EXPERTS,), f32)]).reshape(1, LANES).astype(f32)
    x1r, rt, counts = _merge_route(
        ya, yb, ga, gb, x2, w_branch_a[0].astype(bf), w_branch_b[0].astype(bf), w_out[0].astype(bf),
        ln1_g[0].reshape(1, d).astype(f32), ln1_b[0].reshape(1, d).astype(f32), w_route, b_route,
        _pick_tile(n, 512))

    assert (2 * n) % MOE_TM == 0
    start_lanes, items = _moe_plan(counts, 2 * n // MOE_TM)
    tm_io = _pick_tile(n, 512)
    pos_blocks = _moe_pos(start_lanes, rt, tm_io)
    xs = _moe_dispatch(pos_blocks, x1r, 2 * n, tm_io)
    ys = _moe_experts(items, xs, w_gate[0], w_up[0], w_down[0], MOE_TM)
    out = _moe_combine(pos_blocks, rt, x1r, ys, ln2_g[0].reshape(1, d).astype(f32),
                       ln2_b[0].reshape(1, d).astype(f32), tm_io)
    return out.reshape(bsz, seq, d)
```

```python
import functools
import math

import numpy as np
import jax
import jax.numpy as jnp
from jax import lax
from jax.experimental import pallas as pl
from jax.experimental.pallas import tpu as pltpu

D_MODEL = 1024
A_HEADS = 8
A_LAT = 128
A_HEAD_DIM = 64
IDX_HEADS = 8
IDX_DIM = 64
TOPK_MAX = 256
B_HEADS = 8
B_KV_HEADS = 2
B_HEAD_DIM = 64
WINDOW = 128
BLOCK = 128
N_BUCKETS = 32
MAX_DISTANCE = 128
N_GROUPS = 4
EXPERTS_PER_GROUP = 8
N_EXPERTS = 32
D_EXPERT = 256
DEPTH = 1
ALPHA = (2 * DEPTH) ** 0.25
LN_EPS = 1e-5
RMS_EPS = 1e-6

LANES = 128
INT_MIN = -(2 ** 31)
VMEM_LIMIT = 48 * 1024 * 1024

_NT = (((1,), (1,)), ((), ()))


def _cparams(*sem):
    return pltpu.CompilerParams(dimension_semantics=sem, vmem_limit_bytes=VMEM_LIMIT)


def _t5_bucket_np(dist):
    f32 = np.float32
    n = np.maximum(dist, 0)
    max_exact = N_BUCKETS // 2
    nf = np.maximum(n, 1).astype(f32)
    large = max_exact + (np.log(nf / f32(max_exact)) / f32(math.log(MAX_DISTANCE / max_exact))
                         * f32(N_BUCKETS - max_exact)).astype(np.int32)
    large = np.minimum(large, N_BUCKETS - 1)
    return np.where(n < max_exact, n, large).astype(np.int32)


def _bias_kernel(tab_ref, bkt_ref, out_ref, *, head0, minus_far):
    h = pl.program_id(0) + head0
    bkt = bkt_ref[...]
    far = tab_ref[h, N_BUCKETS - 1]
    acc = jnp.full(bkt.shape, far, jnp.float32)
    for b in range(N_BUCKETS - 1):
        acc = jnp.where(bkt == b, tab_ref[h, b], acc)
    out_ref[...] = acc - far if minus_far else acc


def _bias_tiles(tab_t, bkt, head0, n_heads, lane_major, minus_far=False):
    g, r, c = bkt.shape
    if lane_major:
        out_shape = jax.ShapeDtypeStruct((g, r, n_heads * c), jnp.float32)
        out_spec = pl.BlockSpec((g, r, c), lambda h: (0, 0, h))
    else:
        out_shape = jax.ShapeDtypeStruct((n_heads * g, r, c), jnp.float32)
        out_spec = pl.BlockSpec((g, r, c), lambda h: (h, 0, 0))
    return pl.pallas_call(
        functools.partial(_bias_kernel, head0=head0, minus_far=minus_far),
        grid=(n_heads,),
        in_specs=[pl.BlockSpec(memory_space=pltpu.SMEM),
                  pl.BlockSpec((g, r, c), lambda h: (0, 0, 0))],
        out_specs=out_spec,
        out_shape=out_shape,
        compiler_params=_cparams("arbitrary"),
        name="bias_tiles",
    )(tab_t, bkt)


_C_QLAT = 0
_C_CKV = _C_QLAT + A_HEADS * A_LAT
_C_QIDX = _C_CKV + A_LAT
_C_KW = _C_QIDX + IDX_HEADS * IDX_DIM
_C_QB = _C_KW + LANES
_C_KB = _C_QB + B_HEADS * B_HEAD_DIM
_C_VB = _C_KB + B_KV_HEADS * B_HEAD_DIM
_C_GA = _C_VB + B_KV_HEADS * B_HEAD_DIM
_C_GB = _C_GA + D_MODEL
_C_END = _C_GB + D_MODEL
_RAW_KW_END = _C_KW + IDX_DIM + IDX_HEADS


def _proj_kernel(x_ref, w_ref, g_ref, q_ref, ckv_ref, ckvt_ref, qidx_ref, kidx_ref, widx_ref,
                 qb_ref, kb_ref, vb_ref, ga_ref, gb_ref):
    xb = x_ref[...].astype(jnp.bfloat16)
    tm = xb.shape[0]

    def seg(lo, hi):
        return jnp.dot(xb, w_ref[:, lo:hi], preferred_element_type=jnp.float32)

    for h in range(A_HEADS):
        qh = (seg(_C_QLAT + h * A_LAT, _C_QLAT + (h + 1) * A_LAT) * (A_LAT ** -0.5)).astype(jnp.bfloat16)
        for r in range(tm // BLOCK):
            q_ref[r, h] = qh[r * BLOCK:(r + 1) * BLOCK]
    c = seg(_C_CKV, _C_QIDX)
    ms = jnp.mean(c * c, axis=-1, keepdims=True)
    cn = c * lax.rsqrt(ms + RMS_EPS) * g_ref[...]
    ckv_ref[...] = cn.astype(jnp.bfloat16)
    for r in range(tm // BLOCK):
        ckvt_ref[0, :A_LAT, r * BLOCK:(r + 1) * BLOCK] = cn[r * BLOCK:(r + 1) * BLOCK].T.astype(jnp.bfloat16)
    ckvt_ref[0, A_LAT:, :] = jnp.ones((8, tm), jnp.bfloat16)
    qif = seg(_C_QIDX, _C_KW).astype(jnp.bfloat16)
    for h in range(IDX_HEADS):
        for r in range(tm // BLOCK):
            qidx_ref[r, h] = qif[r * BLOCK:(r + 1) * BLOCK, h * IDX_DIM:(h + 1) * IDX_DIM]
    kw = seg(_C_KW, _C_QB)
    kidx_ref[...] = kw[:, :IDX_DIM].astype(jnp.bfloat16)
    widx_ref[...] = kw[:, IDX_DIM:IDX_DIM + IDX_HEADS]
    qbf = seg(_C_QB, _C_KB).astype(jnp.bfloat16)
    for h in range(B_HEADS):
        for r in range(tm // BLOCK):
            qb_ref[r, h] = qbf[r * BLOCK:(r + 1) * BLOCK, h * B_HEAD_DIM:(h + 1) * B_HEAD_DIM]
    kbf = seg(_C_KB, _C_VB).astype(jnp.bfloat16)
    vbf = seg(_C_VB, _C_GA)
    for g in range(B_KV_HEADS):
        kb_ref[g] = kbf[:, g * B_HEAD_DIM:(g + 1) * B_HEAD_DIM]
        vb_ref[g, 0, B_HEAD_DIM:, :] = jnp.ones((SWA_ONES, tm), jnp.bfloat16)
    for r in range(tm // BLOCK):
        vt = vbf[r * BLOCK:(r + 1) * BLOCK].T.astype(jnp.bfloat16)
        for g in range(B_KV_HEADS):
            vb_ref[g, 0, :B_HEAD_DIM, r * BLOCK:(r + 1) * BLOCK] = vt[g * B_HEAD_DIM:(g + 1) * B_HEAD_DIM]
    ga_ref[...] = jax.nn.sigmoid(seg(_C_GA, _C_GB))
    gb_ref[...] = jax.nn.sigmoid(seg(_C_GB, _C_END))


def _input_projection(x2, w_pack, kv_g, tm, seq):
    n = x2.shape[0]
    assert seq % tm == 0
    bf, f32 = jnp.bfloat16, jnp.float32
    row = lambda w: pl.BlockSpec((tm, w), lambda i: (i, 0))
    out_shape = (
        jax.ShapeDtypeStruct((n // BLOCK, A_HEADS, BLOCK, A_LAT), bf),
        jax.ShapeDtypeStruct((n, A_LAT), bf),
        jax.ShapeDtypeStruct((n // seq, A_LAT + 8, seq), bf),
        jax.ShapeDtypeStruct((n // BLOCK, IDX_HEADS, BLOCK, IDX_DIM), bf),
        jax.ShapeDtypeStruct((n, IDX_DIM), bf),
        jax.ShapeDtypeStruct((n, IDX_HEADS), f32),
        jax.ShapeDtypeStruct((n // BLOCK, B_HEADS, BLOCK, B_HEAD_DIM), bf),
        jax.ShapeDtypeStruct((B_KV_HEADS, n, B_HEAD_DIM), bf),
        jax.ShapeDtypeStruct((B_KV_HEADS, n // seq, B_HEAD_DIM + SWA_ONES, seq), bf),
        jax.ShapeDtypeStruct((n, D_MODEL), f32),
        jax.ShapeDtypeStruct((n, D_MODEL), f32),
    )
    out_specs = (
        pl.BlockSpec((tm // BLOCK, A_HEADS, BLOCK, A_LAT), lambda i: (i, 0, 0, 0)),
        row(A_LAT),
        pl.BlockSpec((1, A_LAT + 8, tm), lambda i: (i // (seq // tm), 0, i % (seq // tm))),
        pl.BlockSpec((tm // BLOCK, IDX_HEADS, BLOCK, IDX_DIM), lambda i: (i, 0, 0, 0)),
        row(IDX_DIM), row(IDX_HEADS),
        pl.BlockSpec((tm // BLOCK, B_HEADS, BLOCK, B_HEAD_DIM), lambda i: (i, 0, 0, 0)),
        pl.BlockSpec((B_KV_HEADS, tm, B_HEAD_DIM), lambda i: (0, i, 0)),
        pl.BlockSpec((B_KV_HEADS, 1, B_HEAD_DIM + SWA_ONES, tm),
                     lambda i: (0, i // (seq // tm), 0, i % (seq // tm))),
        row(D_MODEL), row(D_MODEL),
    )
    return pl.pallas_call(
        _proj_kernel,
        grid=(n // tm,),
        in_specs=[row(D_MODEL),
                  pl.BlockSpec((D_MODEL, _C_END), lambda i: (0, 0)),
                  pl.BlockSpec((1, A_LAT), lambda i: (0, 0))],
        out_specs=out_specs,
        out_shape=out_shape,
        compiler_params=_cparams("arbitrary"),
        name="input_projection",
    )(x2, w_pack, kv_g)


DSA_UNROLL = 4


def _bit_transpose32(words):
    w = list(words)
    j, m = 16, 0x0000FFFF
    while j:
        mask = jnp.int32(m - (1 << 32) if m >= (1 << 31) else m)
        k = 0
        while k < 32:
            t = (w[k] ^ lax.shift_right_logical(w[k + j], jnp.int32(j))) & mask
            w[k] = w[k] ^ t
            w[k + j] = w[k + j] ^ lax.shift_left(t, jnp.int32(j))
            k = (k + j + 1) & ~j
        j >>= 1
        m = (m ^ (m << j)) & 0xFFFFFFFF
    return w


def _dsa_kernel(qidx_ref, kidx_ref, widx_ref, q_ref, ckv_ref, ckvt_ref, bias_ref, wuvt_ref,
                tri_ref, ya_ref, sc_ref, neg_ref, lg_ref, o_ref, planes_ref, sel_ref, *, k_sel,
                search_chunk):
    i = pl.program_id(1)
    f32 = jnp.float32
    s_loc = lax.broadcasted_iota(jnp.int32, (BLOCK, BLOCK), 0)
    t_loc = lax.broadcasted_iota(jnp.int32, (BLOCK, BLOCK), 1)
    causal_diag = s_loc <= t_loc
    idx_scale = IDX_DIM ** -0.5
    w_scale = IDX_HEADS ** -0.5

    def rows(j):
        return pl.ds(pl.multiple_of(j * BLOCK, BLOCK), BLOCK)

    sc_ref[...] = jnp.full(sc_ref.shape, -jnp.inf, f32)
    wts = (widx_ref[0] * w_scale) * idx_scale
    qi = qidx_ref[0]

    def score_keys(j):
        s = lax.dot_general(kidx_ref[0, rows(j), :], qi, _NT, preferred_element_type=f32)
        r = jnp.maximum(s, 0.0) * wts
        acc = r[:, :BLOCK]
        for h in range(1, IDX_HEADS):
            acc = acc + r[:, h * BLOCK:(h + 1) * BLOCK]
        return acc

    def fold_blocks(lo, hi, fn, init, combine):
        n = hi - lo
        n_grp = lax.shift_right_logical(n, DSA_UNROLL.bit_length() - 1)

        def group(k, acc):
            j = lo + DSA_UNROLL * k
            vals = [fn(j + u) for u in range(DSA_UNROLL)]
            while len(vals) > 1:
                vals = [combine(vals[a], vals[a + 1]) for a in range(0, len(vals), 2)]
            return combine(acc, vals[0])

        acc = lax.fori_loop(0, n_grp, group, init)
        return lax.fori_loop(lo + DSA_UNROLL * n_grp, hi, lambda j, a: combine(a, fn(j)), acc)

    def far_scores(j):
        sc_ref[rows(j), :] = score_keys(j)
        return jnp.int32(0)

    fold_blocks(0, i, far_scores, jnp.int32(0), lambda a, b: a)
    sc_ref[rows(i), :] = jnp.where(causal_diag, score_keys(i), -jnp.inf)

    n_chunks = ((i + 1) * BLOCK + search_chunk - 1) // search_chunk
    n_acc = 64

    def count(cmp, cand):
        def body(c, cnt):
            base = pl.multiple_of(c * search_chunk, search_chunk)
            for r in range(search_chunk // n_acc):
                cnt = cnt + cmp(sc_ref[pl.ds(base + r * n_acc, n_acc), :], cand).astype(f32)
            return cnt
        cnt = lax.fori_loop(0, n_chunks, body, jnp.zeros((n_acc, BLOCK), f32))
        return jnp.sum(cnt, axis=0, keepdims=True)

    def ordered_to_float(u):
        k = u ^ INT_MIN
        return lax.bitcast_convert_type(k ^ ((k >> 31) & 0x7FFFFFFF), f32)

    n_grp_keys = 32 * 8
    n_groups = ((i + 1) * BLOCK + n_grp_keys - 1) // n_grp_keys
    max_groups = planes_ref.shape[0]

    def build_planes(g, carry):
        words = []
        for v in range(32):
            s = sc_ref[pl.ds(pl.multiple_of(g * n_grp_keys + v * 8, 8), 8), :]
            bits = lax.bitcast_convert_type(s, jnp.int32)
            words.append(bits ^ ((bits >> 31) & 0x7FFFFFFF) ^ INT_MIN)
        planes = _bit_transpose32(words)
        for p in range(32):
            planes_ref[g, p] = planes[p]
        return carry

    lax.fori_loop(0, n_groups, build_planes, 0)

    def select_bit(n, carry):
        alive, k_left, thr_u = carry
        b = 31 - n
        ones = [alive[g] & planes_ref[jnp.minimum(g, n_groups - 1), n] for g in range(max_groups)]
        cnt = lax.population_count(ones[0])
        for g in range(1, max_groups):
            cnt = cnt + lax.population_count(ones[g])
        c1 = jnp.sum(cnt.astype(f32), axis=0, keepdims=True)
        take = c1 >= k_left
        take_w = jnp.broadcast_to(take, (8, BLOCK))
        alive = [jnp.where(take_w, ones[g], alive[g] ^ ones[g]) for g in range(max_groups)]
        thr_u = jnp.where(take, thr_u | lax.shift_left(jnp.int32(1), b), thr_u)
        return alive, jnp.where(take, k_left, k_left - c1), thr_u

    alive0 = [jnp.where(g < n_groups, jnp.full((8, BLOCK), -1, jnp.int32), 0) for g in range(max_groups)]
    _, _, thr_u = lax.fori_loop(
        0, 32, select_bit, (alive0, jnp.full((1, BLOCK), k_sel, f32), jnp.zeros((1, BLOCK), jnp.int32)))
    thr = ordered_to_float(thr_u)
    c_ge = count(jnp.greater_equal, thr)
    c_gt = count(jnp.greater, thr)
    sel_ref[0:1, :] = thr
    sel_ref[1:2, :] = c_ge
    sel_ref[2:3, :] = c_gt
    consistent = jnp.min(((c_ge >= k_sel) & (c_gt < k_sel)).astype(f32)) > 0.0

    @pl.when(jnp.logical_not(consistent))
    def _():
        def search_bit(b, prefix_u):
            cand_u = prefix_u | lax.shift_left(jnp.int32(1), 31 - b)
            cnt = count(jnp.greater_equal, ordered_to_float(cand_u))
            return jnp.where(cnt >= k_sel, cand_u, prefix_u)

        t = ordered_to_float(lax.fori_loop(0, 32, search_bit, jnp.zeros((1, BLOCK), jnp.int32)))
        t = jnp.where(t != t, -jnp.inf, t)
        sel_ref[0:1, :] = t
        sel_ref[1:2, :] = count(jnp.greater_equal, t)
        sel_ref[2:3, :] = count(jnp.greater, t)

    thr = sel_ref[0:1, :]
    c_ge = sel_ref[1:2, :]
    c_gt = sel_ref[2:3, :]
    n_tie = k_sel - c_gt
    any_tie = jnp.max(((c_ge > k_sel) & (thr > -jnp.inf)).astype(f32)) > 0.0

    @pl.when(jnp.logical_not(any_tie))
    def _():
        def body(j, carry):
            neg_ref[rows(j), :] = jnp.where(sc_ref[rows(j), :] >= thr, 0.0, -jnp.inf)
            return carry
        lax.fori_loop(0, i, body, 0)
        neg_ref[rows(i), :] = jnp.where((sc_ref[rows(i), :] >= thr) & causal_diag, 0.0, -jnp.inf)

    @pl.when(any_tie)
    def _():
        tri = tri_ref[...]

        def block(j, tie_seen):
            kc = sc_ref[rows(j), :]
            eq = kc == thr
            tie_rank = jnp.dot(tri, eq.astype(jnp.bfloat16), preferred_element_type=f32) + tie_seen
            sel = (kc > thr) | (eq & (tie_rank <= n_tie))
            return sel, tie_seen + jnp.sum(eq.astype(f32), axis=0, keepdims=True)

        def body(j, tie_seen):
            sel, tie_seen = block(j, tie_seen)
            neg_ref[rows(j), :] = jnp.where(sel, 0.0, -jnp.inf)
            return tie_seen

        tie_seen = lax.fori_loop(0, i, body, jnp.zeros((1, BLOCK), f32))
        sel, _ = block(i, tie_seen)
        neg_ref[rows(i), :] = jnp.where(sel & causal_diag, 0.0, -jnp.inf)

    q_all = q_ref[0]

    def logits(j):
        lg = lax.dot_general(ckv_ref[0, rows(j), :], q_all, _NT, preferred_element_type=f32)
        return lg + jnp.concatenate([neg_ref[rows(j), :]] * A_HEADS, axis=1)

    def far_logits(j):
        lg = logits(j)
        lg_ref[rows(j), :] = lg
        return jnp.max(lg, axis=0, keepdims=True)

    def near_logits(j):
        lg = logits(j) + bias_ref[j - i + 1]
        lg_ref[rows(j), :] = lg
        return jnp.max(lg, axis=0, keepdims=True)

    n_far = jnp.maximum(i - 1, 0)
    m_run = fold_blocks(0, n_far, far_logits, jnp.full((1, A_HEADS * BLOCK), -jnp.inf, f32), jnp.maximum)
    m_run = fold_blocks(n_far, i + 1, near_logits, m_run, jnp.maximum)

    o_ref[...] = jnp.zeros_like(o_ref)

    def pv_rows(r):
        p = jnp.exp(lg_ref[r, :] - m_run).astype(jnp.bfloat16)
        o_ref[...] += jnp.dot(ckvt_ref[0, :, r], p, preferred_element_type=f32)

    def pv_group(k, carry):
        span = DSA_UNROLL * BLOCK
        pv_rows(pl.ds(pl.multiple_of(k * span, span), span))
        return carry

    def pv_single(j, carry):
        pv_rows(rows(j))
        return carry

    n_grp = lax.shift_right_logical(i + 1, DSA_UNROLL.bit_length() - 1)
    lax.fori_loop(0, n_grp, pv_group, 0)
    lax.fori_loop(DSA_UNROLL * n_grp, i + 1, pv_single, 0)

    inv_l = 1.0 / o_ref[A_LAT:A_LAT + 1, :]
    for pair in range(A_HEADS // 2):
        halves = []
        for h in (2 * pair, 2 * pair + 1):
            cols = slice(h * BLOCK, (h + 1) * BLOCK)
            y_t = jnp.dot(wuvt_ref[h], o_ref[:A_LAT, cols].astype(jnp.bfloat16),
                          preferred_element_type=f32)
            halves.append(y_t * inv_l[:, cols])
        two = jnp.concatenate(halves, axis=0)
        ya_ref[:, 2 * pair * A_HEAD_DIM:(2 * pair + 2) * A_HEAD_DIM] = two.T.astype(ya_ref.dtype)


def _dsa_attention(qidx_blocks, kidx, widx_l, q_blocks, ckv, ckv_tx, bias_near, wuv, bsz, seq):
    n_blk = seq // BLOCK
    k_sel = min(TOPK_MAX, seq // 4)
    search_chunk = min(512, seq)
    tri = jnp.asarray(np.tril(np.ones((BLOCK, BLOCK), np.float32)), jnp.bfloat16)
    kern = functools.partial(_dsa_kernel, k_sel=float(k_sel), search_chunk=search_chunk)
    blk = lambda b, i: (b * n_blk + i, 0, 0)
    return pl.pallas_call(
        kern,
        grid=(bsz, n_blk),
        in_specs=[
            pl.BlockSpec((1, IDX_HEADS * BLOCK, IDX_DIM), blk),
            pl.BlockSpec((1, seq, IDX_DIM), lambda b, i: (b, 0, 0)),
            pl.BlockSpec((1, 1, IDX_HEADS * BLOCK), blk),
            pl.BlockSpec((1, A_HEADS * BLOCK, A_LAT), blk),
            pl.BlockSpec((1, seq, A_LAT), lambda b, i: (b, 0, 0)),
            pl.BlockSpec((1, A_LAT + 8, seq), lambda b, i: (b, 0, 0)),
            pl.BlockSpec((2, BLOCK, A_HEADS * BLOCK), lambda b, i: (0, 0, 0)),
            pl.BlockSpec((A_HEADS, A_HEAD_DIM, A_LAT), lambda b, i: (0, 0, 0)),
            pl.BlockSpec((BLOCK, BLOCK), lambda b, i: (0, 0)),
        ],
        out_specs=pl.BlockSpec((BLOCK, A_HEADS * A_HEAD_DIM), lambda b, i: (b * n_blk + i, 0)),
        out_shape=jax.ShapeDtypeStruct((bsz * seq, A_HEADS * A_HEAD_DIM), jnp.bfloat16),
        scratch_shapes=[pltpu.VMEM((seq, BLOCK), jnp.float32),
                        pltpu.VMEM((seq, BLOCK), jnp.float32),
                        pltpu.VMEM((seq, A_HEADS * BLOCK), jnp.float32),
                        pltpu.VMEM((A_LAT + 8, A_HEADS * BLOCK), jnp.float32),
                        pltpu.VMEM((seq // 256, 32, 8, BLOCK), jnp.int32),
                        pltpu.VMEM((8, BLOCK), jnp.float32)],
        compiler_params=_cparams("arbitrary", "arbitrary"),
        name="dsa_attention",
    )(qidx_blocks, kidx, widx_l, q_blocks, ckv, ckv_tx, bias_near, wuv, tri)


SWA_QB = 4
SWA_ONES = 16


def _swa_kernel(sink_ref, q_ref, kp_ref, kc_ref, vp_ref, vc_ref, bias_ref, yb_ref, *, qb):
    i = pl.program_id(1)
    f32 = jnp.float32
    grp = B_HEADS // B_KV_HEADS
    cols = grp * BLOCK
    s_loc = lax.broadcasted_iota(jnp.int32, (BLOCK, BLOCK), 0)
    t_loc = lax.broadcasted_iota(jnp.int32, (BLOCK, BLOCK), 1)
    neg_cur = jnp.concatenate([jnp.where(s_loc <= t_loc, 0.0, -jnp.inf)] * grp, axis=1)
    in_prev = s_loc > t_loc
    head_of_col = lax.broadcasted_iota(jnp.int32, (1, cols), 1) >> (BLOCK.bit_length() - 1)
    scale = B_HEAD_DIM ** -0.5
    for g in range(B_KV_HEADS):
        sink = jnp.zeros((1, cols), f32)
        for hh in range(grp):
            sink = jnp.where(head_of_col == hh, sink_ref[g * grp + hh], sink)
        bias_p = bias_ref[:BLOCK, g * cols:(g + 1) * cols]
        bias_c = bias_ref[BLOCK:, g * cols:(g + 1) * cols]
        for r in range(qb):
            has_prev = jnp.zeros_like(s_loc) + jnp.minimum(i * qb + r, 1)
            neg_prev = jnp.concatenate([jnp.where(in_prev & (has_prev > 0), 0.0, -jnp.inf)] * grp, axis=1)
            q = q_ref[r, g * grp:(g + 1) * grp].reshape(cols, B_HEAD_DIM)
            if r == 0:
                kp, vp = kp_ref[g], vp_ref[g, 0]
            else:
                kp, vp = kc_ref[g, (r - 1) * BLOCK:r * BLOCK], vc_ref[g, 0, :, (r - 1) * BLOCK:r * BLOCK]
            kc, vc = kc_ref[g, r * BLOCK:(r + 1) * BLOCK], vc_ref[g, 0, :, r * BLOCK:(r + 1) * BLOCK]
            lp = lax.dot_general(kp, q, _NT, preferred_element_type=f32) * scale + bias_p + neg_prev
            lc = lax.dot_general(kc, q, _NT, preferred_element_type=f32) * scale + bias_c + neg_cur
            m = jnp.maximum(jnp.max(jnp.maximum(lp, lc), axis=0, keepdims=True), sink)
            pp = jnp.exp(lp - m).astype(jnp.bfloat16)
            pc = jnp.exp(lc - m).astype(jnp.bfloat16)
            ox = (jnp.dot(vp, pp, preferred_element_type=f32)
                  + jnp.dot(vc, pc, preferred_element_type=f32))
            den = ox[B_HEAD_DIM:B_HEAD_DIM + 1, :] + jnp.exp(sink - m)
            o = ox[:B_HEAD_DIM, :] / den
            for pair in range(grp // 2):
                two = jnp.concatenate([o[:, (2 * pair) * BLOCK:(2 * pair + 1) * BLOCK],
                                       o[:, (2 * pair + 1) * BLOCK:(2 * pair + 2) * BLOCK]], axis=0)
                h0 = g * grp + 2 * pair
                yb_ref[r * BLOCK:(r + 1) * BLOCK, h0 * B_HEAD_DIM:(h0 + 2) * B_HEAD_DIM] = (
                    two.T.astype(yb_ref.dtype))


def _swa_attention(sinks, q_blocks, kb, vbt, bias_b, bsz, seq):
    n_blk = seq // BLOCK
    qb = SWA_QB if n_blk % SWA_QB == 0 else 1
    n_step = n_blk // qb
    cur = lambda b, i: (0, b * n_step + i, 0)
    prev = lambda b, i: (0, b * n_blk + jnp.maximum(i * qb - 1, 0), 0)
    vrows = B_HEAD_DIM + SWA_ONES
    return pl.pallas_call(
        functools.partial(_swa_kernel, qb=qb),
        grid=(bsz, n_step),
        in_specs=[
            pl.BlockSpec(memory_space=pltpu.SMEM),
            pl.BlockSpec((qb, B_HEADS, BLOCK, B_HEAD_DIM), lambda b, i: (b * n_step + i, 0, 0, 0)),
            pl.BlockSpec((B_KV_HEADS, BLOCK, B_HEAD_DIM), prev),
            pl.BlockSpec((B_KV_HEADS, qb * BLOCK, B_HEAD_DIM), cur),
            pl.BlockSpec((B_KV_HEADS, 1, vrows, BLOCK), lambda b, i: (0, b, 0, jnp.maximum(i * qb - 1, 0))),
            pl.BlockSpec((B_KV_HEADS, 1, vrows, qb * BLOCK), lambda b, i: (0, b, 0, i)),
            pl.BlockSpec((2 * BLOCK, B_HEADS * BLOCK), lambda b, i: (0, 0)),
        ],
        out_specs=pl.BlockSpec((qb * BLOCK, B_HEADS * B_HEAD_DIM), lambda b, i: (b * n_step + i, 0)),
        out_shape=jax.ShapeDtypeStruct((bsz * seq, B_HEADS * B_HEAD_DIM), jnp.bfloat16),
        compiler_params=_cparams("arbitrary", "arbitrary"),
        name="swa_attention",
    )(sinks, q_blocks, kb, kb, vbt, vbt, bias_b)


def _layer_norm(h, g, b):
    mu = jnp.mean(h, axis=-1, keepdims=True)
    d = h - mu
    var = jnp.mean(d * d, axis=-1, keepdims=True)
    return d * lax.rsqrt(var + LN_EPS) * g + b


def _merge_kernel(ya_ref, yb_ref, ga_ref, gb_ref, x_ref, wa_ref, wb_ref, wo_ref, g1_ref, b1_ref,
                  wrh_ref, wrl_ref, br_ref, tri_ref, x1_ref, rt_ref, cnt_ref, carry_ref):
    f32 = jnp.float32
    pa = jnp.dot(ya_ref[...], wa_ref[...], preferred_element_type=f32)
    pb = jnp.dot(yb_ref[...], wb_ref[...], preferred_element_type=f32)
    merged = ga_ref[...] * pa + gb_ref[...] * pb
    h = ALPHA * x_ref[...] + jnp.dot(merged.astype(jnp.bfloat16), wo_ref[...],
                                     preferred_element_type=f32)
    x1 = _layer_norm(h, g1_ref[...], b1_ref[...])
    _store_tile_rows(x1_ref, x1)

    x_hi = x1.astype(jnp.bfloat16)
    x_lo = (x1 - x_hi.astype(f32)).astype(jnp.bfloat16)
    lg = (jnp.dot(x_hi, wrh_ref[...], preferred_element_type=f32)
          + jnp.dot(x_hi, wrl_ref[...], preferred_element_type=f32)
          + jnp.dot(x_lo, wrh_ref[...], preferred_element_type=f32)) + br_ref[...]
    lane_i = lax.broadcasted_iota(jnp.int32, lg.shape, 1)
    lane = lane_i.astype(f32)
    big = jnp.float32(1 << 20)
    is_g = lane_i < N_GROUPS
    gl = jnp.where(is_g, lg, -jnp.inf)
    gmax = jnp.max(gl, axis=1, keepdims=True)
    g_sel = jnp.min(jnp.where(gl == gmax, lane, big), axis=1, keepdims=True)
    g_w = 1.0 / jnp.sum(jnp.where(is_g, jnp.exp(gl - gmax), 0.0), axis=1, keepdims=True)
    e_id = lane_i - N_GROUPS
    e_grp = (e_id >> 3).astype(f32)
    in_grp = (e_id >= 0) & (e_id < N_EXPERTS) & (e_grp == g_sel)
    el = jnp.where(in_grp, lg, -jnp.inf)
    emax = jnp.max(el, axis=1, keepdims=True)
    ee = jnp.where(in_grp, jnp.exp(el - emax), 0.0)
    ep = ee / jnp.sum(ee, axis=1, keepdims=True)
    epm = jnp.where(in_grp, ep, -1.0)
    p1 = jnp.max(epm, axis=1, keepdims=True)
    i1 = jnp.min(jnp.where(epm == p1, lane, big), axis=1, keepdims=True)
    epm2 = jnp.where(lane == i1, -1.0, epm)
    p2 = jnp.max(epm2, axis=1, keepdims=True)
    i2 = jnp.min(jnp.where(epm2 == p2, lane, big), axis=1, keepdims=True)
    psum = p1 + p2
    w1 = g_w * p1 / psum
    w2 = g_w * p2 / psum
    @pl.when(pl.program_id(0) == 0)
    def _():
        carry_ref[...] = jnp.zeros_like(carry_ref)

    oh1 = lane + N_GROUPS == i1
    oh2 = lane + N_GROUPS == i2
    oh = jnp.where(oh1 | oh2, 1.0, 0.0)
    prefix = jnp.dot(tri_ref[...], oh.astype(jnp.bfloat16), preferred_element_type=f32) + carry_ref[...]
    rank1 = jnp.sum(jnp.where(oh1, prefix, 0.0), axis=1, keepdims=True)
    rank2 = jnp.sum(jnp.where(oh2, prefix, 0.0), axis=1, keepdims=True)
    carry_ref[...] += jnp.sum(oh, axis=0, keepdims=True)
    cnt_ref[...] = carry_ref[...]
    rec = (i1 - N_GROUPS, i2 - N_GROUPS, w1, w2, rank1, rank2)
    rt = jnp.zeros(lg.shape, f32)
    for k, v in enumerate(rec):
        rt = jnp.where(lane_i == k, v, rt)
    rt_ref[...] = rt


def _merge_route(ya, yb, ga, gb, x2, wa, wb, wo, g1, b1, wr, br, tm):
    wr_hi = wr.astype(jnp.bfloat16)
    wr_lo = (wr - wr_hi.astype(jnp.float32)).astype(jnp.bfloat16)
    n = x2.shape[0]
    row = lambda w: pl.BlockSpec((tm, w), lambda i: (i, 0))
    full = lambda r, c: pl.BlockSpec((r, c), lambda i: (0, 0))
    aw, bw = A_HEADS * A_HEAD_DIM, B_HEADS * B_HEAD_DIM
    tri = jnp.asarray(np.tril(np.ones((tm, tm), np.float32), -1), jnp.bfloat16)
    return pl.pallas_call(
        _merge_kernel,
        grid=(n // tm,),
        in_specs=[row(aw), row(bw), row(D_MODEL), row(D_MODEL), row(D_MODEL),
                  full(aw, D_MODEL), full(bw, D_MODEL), full(D_MODEL, D_MODEL),
                  full(1, D_MODEL), full(1, D_MODEL), full(D_MODEL, LANES), full(D_MODEL, LANES),
                  full(1, LANES), full(tm, tm)],
        out_specs=(pl.BlockSpec((tm * ROW_TILE, LANES), lambda i: (i, 0)), row(LANES), full(1, LANES)),
        out_shape=(jax.ShapeDtypeStruct((n * ROW_TILE, LANES), jnp.float32),
                   jax.ShapeDtypeStruct((n, LANES), jnp.float32),
                   jax.ShapeDtypeStruct((1, LANES), jnp.float32)),
        scratch_shapes=[pltpu.VMEM((1, LANES), jnp.float32)],
        compiler_params=_cparams("arbitrary"),
        name="merge_route",
    )(ya, yb, ga, gb, x2, wa, wb, wo, g1, b1, wr_hi, wr_lo, br, tri)


ROW_TILE = D_MODEL // LANES


def _store_tile_rows(ref, val):
    rows = val.shape[0]
    for j in range(ROW_TILE):
        ref[pl.ds(j, rows, stride=ROW_TILE), :] = val[:, j * LANES:(j + 1) * LANES]


def _load_tile_rows(ref, rows):
    return jnp.concatenate([ref[pl.ds(j, rows, stride=ROW_TILE), :] for j in range(ROW_TILE)], axis=1)


DMA_UNROLL = 8


def _dispatch_kernel(pos_ref, x_ref, xs_hbm, sem, *, tm):
    def issue(k, carry):
        for u in range(DMA_UNROLL):
            r = k * DMA_UNROLL + u
            src = x_ref.at[pl.ds(pl.multiple_of(r * ROW_TILE, ROW_TILE), ROW_TILE)]
            for slot in range(2):
                p = pos_ref[0, slot, r]
                dst = xs_hbm.at[pl.ds(pl.multiple_of(p * ROW_TILE, ROW_TILE), ROW_TILE)]
                pltpu.make_async_copy(src, dst, sem).start(priority=slot)
        return carry

    lax.fori_loop(0, tm // DMA_UNROLL, issue, 0)
    for slot in range(2):
        pltpu.make_async_copy(x_ref, xs_hbm.at[pl.ds(0, tm * ROW_TILE)], sem).wait()


def _moe_dispatch(pos, x1r, n_rows, tm):
    n = x1r.shape[0] // ROW_TILE
    return pl.pallas_call(
        functools.partial(_dispatch_kernel, tm=tm),
        grid=(n // tm,),
        in_specs=[pl.BlockSpec((1, 8, tm), lambda i: (i, 0, 0), memory_space=pltpu.SMEM),
                  pl.BlockSpec((tm * ROW_TILE, LANES), lambda i: (i, 0))],
        out_specs=pl.BlockSpec(memory_space=pl.ANY),
        out_shape=jax.ShapeDtypeStruct((n_rows * ROW_TILE, LANES), jnp.float32),
        scratch_shapes=[pltpu.SemaphoreType.DMA],
        compiler_params=_cparams("arbitrary"),
        name="moe_dispatch",
    )(pos, x1r)


def _expert_kernel(tile_ref, exp_ref, lo_ref, hi_ref, nit_ref, xs_ref, wg_ref, wu_ref, wd_ref, ys_ref,
                   wgb_ref, wub_ref, wdb_ref, *, tm):
    f32 = jnp.float32
    w = pl.program_id(0)
    prev = jnp.maximum(w - 1, 0)

    @pl.when(jnp.logical_or(w == 0, exp_ref[prev] != exp_ref[w]))
    def _():
        wgb_ref[...] = wg_ref[0].astype(jnp.bfloat16)
        wub_ref[...] = wu_ref[0].astype(jnp.bfloat16)
        wdb_ref[...] = wd_ref[0].astype(jnp.bfloat16)

    @pl.when(w < nit_ref[0])
    def _():
        xb = _load_tile_rows(xs_ref, tm).astype(jnp.bfloat16)
        g = jnp.dot(xb, wgb_ref[...], preferred_element_type=f32)
        u = jnp.dot(xb, wub_ref[...], preferred_element_type=f32)
        hmid = (g * jax.nn.sigmoid(g) * u).astype(jnp.bfloat16)
        y = jnp.dot(hmid, wdb_ref[...], preferred_element_type=f32)
        row = tile_ref[w] * tm + lax.broadcasted_iota(jnp.int32, (tm, 1), 0)
        mine = (row >= lo_ref[w]) & (row < hi_ref[w])
        first_visit = jnp.logical_or(w == 0, tile_ref[prev] != tile_ref[w])

        @pl.when(first_visit)
        def _():
            _store_tile_rows(ys_ref, jnp.where(mine, y, 0.0))

        @pl.when(jnp.logical_not(first_visit))
        def _():
            _store_tile_rows(ys_ref, jnp.where(mine, y, _load_tile_rows(ys_ref, tm)))


def _moe_experts(items, xs, wg, wu, wd, tm):
    n_items = items[0].shape[0]
    grid_spec = pltpu.PrefetchScalarGridSpec(
        num_scalar_prefetch=5,
        grid=(n_items,),
        in_specs=[
            pl.BlockSpec((tm * ROW_TILE, LANES), lambda w, t, e, lo, hi, n: (t[w], 0)),
            pl.BlockSpec((1, D_MODEL, D_EXPERT), lambda w, t, e, lo, hi, n: (e[w], 0, 0)),
            pl.BlockSpec((1, D_MODEL, D_EXPERT), lambda w, t, e, lo, hi, n: (e[w], 0, 0)),
            pl.BlockSpec((1, D_EXPERT, D_MODEL), lambda w, t, e, lo, hi, n: (e[w], 0, 0)),
        ],
        out_specs=pl.BlockSpec((tm * ROW_TILE, LANES), lambda w, t, e, lo, hi, n: (t[w], 0)),
        scratch_shapes=[pltpu.VMEM((D_MODEL, D_EXPERT), jnp.bfloat16),
                        pltpu.VMEM((D_MODEL, D_EXPERT), jnp.bfloat16),
                        pltpu.VMEM((D_EXPERT, D_MODEL), jnp.bfloat16)],
    )
    return pl.pallas_call(
        functools.partial(_expert_kernel, tm=tm),
        grid_spec=grid_spec,
        out_shape=jax.ShapeDtypeStruct(xs.shape, jnp.float32),
        compiler_params=_cparams("arbitrary"),
        name="moe_experts",
    )(*items, xs, wg, wu, wd)


def _combine_kernel(pos_ref, rt_ref, x_ref, ys_hbm, g2_ref, b2_ref, out_ref, buf_a, buf_b, sem, *, tm):
    bufs = (buf_a, buf_b)
    th = tm // 2
    half = lambda ref, h: ref.at[pl.ds(h * th * ROW_TILE, th * ROW_TILE)]

    def issue_half(h):
        def issue(k, carry):
            for u in range(DMA_UNROLL):
                r = h * th + k * DMA_UNROLL + u
                for slot in range(2):
                    p = pos_ref[0, slot, r]
                    src = ys_hbm.at[pl.ds(pl.multiple_of(p * ROW_TILE, ROW_TILE), ROW_TILE)]
                    dst = bufs[slot].at[pl.ds(pl.multiple_of(r * ROW_TILE, ROW_TILE), ROW_TILE)]
                    pltpu.make_async_copy(src, dst, sem.at[h]).start(priority=slot)
            return carry
        lax.fori_loop(0, th // DMA_UNROLL, issue, 0)

    def finish_half(h):
        for slot in range(2):
            pltpu.make_async_copy(ys_hbm.at[pl.ds(0, th * ROW_TILE)], half(bufs[slot], h), sem.at[h]).wait()
        rt = rt_ref[h * th:(h + 1) * th, :]
        ffn = (rt[:, 2:3] * _load_tile_rows(half(buf_a, h), th)
               + rt[:, 3:4] * _load_tile_rows(half(buf_b, h), th))
        out_ref[h * th:(h + 1) * th, :] = _layer_norm(
            ALPHA * _load_tile_rows(half(x_ref, h), th) + ffn, g2_ref[...], b2_ref[...])

    issue_half(0)
    issue_half(1)
    finish_half(0)
    finish_half(1)


def _moe_combine(pos, rt, x1r, ys, g2, b2, tm):
    n = rt.shape[0]
    return pl.pallas_call(
        functools.partial(_combine_kernel, tm=tm),
        grid=(n // tm,),
        in_specs=[pl.BlockSpec((1, 8, tm), lambda i: (i, 0, 0), memory_space=pltpu.SMEM),
                  pl.BlockSpec((tm, LANES), lambda i: (i, 0)),
                  pl.BlockSpec((tm * ROW_TILE, LANES), lambda i: (i, 0)),
                  pl.BlockSpec(memory_space=pl.ANY),
                  pl.BlockSpec((1, D_MODEL), lambda i: (0, 0)),
                  pl.BlockSpec((1, D_MODEL), lambda i: (0, 0))],
        out_specs=pl.BlockSpec((tm, D_MODEL), lambda i: (i, 0)),
        out_shape=jax.ShapeDtypeStruct((n, D_MODEL), jnp.float32),
        scratch_shapes=[pltpu.VMEM((tm * ROW_TILE, LANES), jnp.float32),
                        pltpu.VMEM((tm * ROW_TILE, LANES), jnp.float32),
                        pltpu.SemaphoreType.DMA((2,))],
        compiler_params=_cparams("arbitrary"),
        name="moe_combine",
    )(pos, rt, x1r, ys, g2, b2)


MOE_TM = 512


def _pos_kernel(start_ref, rt_ref, pos_ref):
    rt = rt_ref[...]
    lane = lax.broadcasted_iota(jnp.int32, rt.shape, 1)
    lane_f = lane.astype(jnp.float32)
    start = start_ref[...]
    tile = jnp.zeros(rt.shape, jnp.float32)
    for slot in range(2):
        first_row = jnp.sum(jnp.where(lane_f == rt[:, slot:slot + 1], start, 0.0), axis=1, keepdims=True)
        tile = jnp.where(lane == slot, first_row + rt[:, 4 + slot:5 + slot], tile)
    tm = pos_ref.shape[2]
    for r in range(rt.shape[0] // BLOCK):
        t = tile[r * BLOCK:(r + 1) * BLOCK].T
        c = (r * BLOCK) % tm
        pos_ref[(r * BLOCK) // tm, :, c:c + BLOCK] = t[:8].astype(jnp.int32)


def _moe_pos(start, rt, tm):
    n = rt.shape[0]
    per_step = _pick_tile(n // tm, 4)
    return pl.pallas_call(
        _pos_kernel,
        grid=(n // (tm * per_step),),
        in_specs=[pl.BlockSpec((1, LANES), lambda i: (0, 0)),
                  pl.BlockSpec((tm * per_step, LANES), lambda i: (i, 0))],
        out_specs=pl.BlockSpec((per_step, 8, tm), lambda i: (i, 0, 0)),
        out_shape=jax.ShapeDtypeStruct((n // tm, 8, tm), jnp.int32),
        compiler_params=_cparams("arbitrary"),
        name="moe_pos",
    )(start, rt)


def _moe_plan(counts, n_tiles):
    i32 = jnp.int32
    cnt = counts[0, :N_EXPERTS].astype(i32)
    end = jnp.cumsum(cnt)
    start = end - cnt
    start_lanes = jnp.pad(start.astype(jnp.float32), (0, LANES - N_EXPERTS)).reshape(1, LANES)
    first_t = start // MOE_TM
    items_e = jnp.where(cnt > 0, (end - 1) // MOE_TM - first_t + 1, 0)
    item_end = jnp.cumsum(items_e)
    n_items = item_end[-1]
    w = jnp.minimum(jnp.arange(n_tiles + N_EXPERTS - 1, dtype=i32), n_items - 1)
    e_w = jnp.minimum(jnp.sum((item_end[None, :] <= w[:, None]).astype(i32), axis=1), N_EXPERTS - 1)
    t_w = first_t[e_w] + (w - (item_end - items_e)[e_w])
    lo_w = jnp.maximum(start[e_w], t_w * MOE_TM)
    hi_w = jnp.minimum(end[e_w], (t_w + 1) * MOE_TM)
    return start_lanes, (t_w.astype(i32), e_w, lo_w.astype(i32), hi_w.astype(i32),
                         n_items.reshape(1).astype(i32))


def _pick_tile(n, pref):
    t = min(pref, n)
    while n % t:
        t //= 2
    return t


def kernel(x, w_in, kv_norm_g, w_uv, w_branch_a, sinks, w_branch_b, w_out, rel_bias, ln1_g, ln1_b,
           w_group, b_group, w_router, b_router, w_gate, w_up, w_down, ln2_g, ln2_b):
    bsz, seq, d = x.shape
    assert d == D_MODEL and seq % BLOCK == 0 and w_in.shape[0] == DEPTH == 1
    n = bsz * seq
    n_blk = seq // BLOCK
    bf, f32 = jnp.bfloat16, jnp.float32
    x2 = x.reshape(n, d)

    r = np.arange(BLOCK, dtype=np.int32)
    d_prev = r[None, :] + BLOCK - r[:, None]
    d_cur = r[None, :] - r[:, None]
    bkt_a = jnp.asarray(_t5_bucket_np(np.stack([d_prev, d_cur])))
    s2 = np.arange(2 * BLOCK, dtype=np.int32)
    bkt_b = jnp.asarray(_t5_bucket_np(r[None, :] + BLOCK - s2[:, None])[None])
    assert int(_t5_bucket_np(np.arange(BLOCK + 1, max(seq, BLOCK + 2))).min()) == N_BUCKETS - 1
    tab_t = rel_bias.astype(f32).T
    bias_a = _bias_tiles(tab_t, bkt_a, 0, A_HEADS, True, minus_far=True)
    bias_b = _bias_tiles(tab_t, bkt_b, A_HEADS, B_HEADS, True)[0]

    w0 = w_in[0]
    w_pack = jnp.concatenate(
        [w0[:, :_RAW_KW_END], jnp.zeros((d, _C_QB - _RAW_KW_END), w0.dtype), w0[:, _RAW_KW_END:]],
        axis=1).astype(bf)
    (q_blocks, ckv, ckv_tx, qidx_blocks, kidx, widx, qb_blocks, kb, vbx, ga, gb) = _input_projection(
        x2, w_pack, kv_norm_g[0].reshape(1, A_LAT).astype(f32), _pick_tile(seq, 512), seq)

    ckv3 = ckv.reshape(bsz, seq, A_LAT)
    widx_l = jnp.swapaxes(widx.reshape(bsz * n_blk, BLOCK, IDX_HEADS), 1, 2).reshape(
        bsz * n_blk, 1, IDX_HEADS * BLOCK)
    ya = _dsa_attention(qidx_blocks.reshape(bsz * n_blk, IDX_HEADS * BLOCK, IDX_DIM),
                        kidx.reshape(bsz, seq, IDX_DIM), widx_l,
                        q_blocks.reshape(bsz * n_blk, A_HEADS * BLOCK, A_LAT),
                        ckv3, ckv_tx, bias_a, jnp.swapaxes(w_uv[0], 1, 2).astype(bf), bsz, seq)

    yb = _swa_attention(sinks[0].astype(f32), qb_blocks, kb, vbx, bias_b, bsz, seq)

    w_route = jnp.concatenate(
        [w_group[0], w_router[0], jnp.zeros((d, LANES - N_GROUPS - N_EXPERTS), f32)], axis=1).astype(f32)
    b_route = jnp.concatenate(
        [b_group[0], b_router[0], jnp.zeros((LANES - N_GROUPS - N_EXPERTS,), f32)]).reshape(1, LANES).astype(f32)
    x1r, rt, counts = _merge_route(
        ya, yb, ga, gb, x2, w_branch_a[0].astype(bf), w_branch_b[0].astype(bf), w_out[0].astype(bf),
        ln1_g[0].reshape(1, d).astype(f32), ln1_b[0].reshape(1, d).astype(f32), w_route, b_route,
        _pick_tile(n, 512))

    assert (2 * n) % MOE_TM == 0
    start_lanes, items = _moe_plan(counts, 2 * n // MOE_TM)
    tm_io = _pick_tile(n, 512)
    pos_blocks = _moe_pos(start_lanes, rt, tm_io)
    xs = _moe_dispatch(pos_blocks, x1r, 2 * n, tm_io)
    ys = _moe_experts(items, xs, w_gate[0], w_up[0], w_down[0], MOE_TM)
    out = _moe_combine(pos_blocks, rt, x1r, ys, ln2_g[0].reshape(1, d).astype(f32),
                       ln2_b[0].reshape(1, d).astype(f32), tm_io)
    return out.reshape(bsz, seq, d)
```

```python
import functools
import math

import numpy as np
import jax
import jax.numpy as jnp
from jax import lax
from jax.experimental import pallas as pl
from jax.experimental.pallas import tpu as pltpu

D_MODEL = 1024
A_HEADS = 8
A_LAT = 128
A_HEAD_DIM = 64
IDX_HEADS = 8
IDX_DIM = 64
TOPK_MAX = 256
B_HEADS = 8
B_KV_HEADS = 2
B_HEAD_DIM = 64
WINDOW = 128
BLOCK = 128
N_BUCKETS = 32
MAX_DISTANCE = 128
N_GROUPS = 4
EXPERTS_PER_GROUP = 8
N_EXPERTS = 32
D_EXPERT = 256
DEPTH = 1
ALPHA = (2 * DEPTH) ** 0.25
LN_EPS = 1e-5
RMS_EPS = 1e-6

LANES = 128
INT_MIN = -(2 ** 31)
VMEM_LIMIT = 48 * 1024 * 1024

_NT = (((1,), (1,)), ((), ()))


def _cparams(*sem):
    return pltpu.CompilerParams(dimension_semantics=sem, vmem_limit_bytes=VMEM_LIMIT)


def _t5_bucket_np(dist):
    f32 = np.float32
    n = np.maximum(dist, 0)
    max_exact = N_BUCKETS // 2
    nf = np.maximum(n, 1).astype(f32)
    large = max_exact + (np.log(nf / f32(max_exact)) / f32(math.log(MAX_DISTANCE / max_exact))
                         * f32(N_BUCKETS - max_exact)).astype(np.int32)
    large = np.minimum(large, N_BUCKETS - 1)
    return np.where(n < max_exact, n, large).astype(np.int32)


def _bias_kernel(tab_ref, bkt_ref, out_ref, *, head0, minus_far):
    h = pl.program_id(0) + head0
    bkt = bkt_ref[...]
    far = tab_ref[h, N_BUCKETS - 1]
    acc = jnp.full(bkt.shape, far, jnp.float32)
    for b in range(N_BUCKETS - 1):
        acc = jnp.where(bkt == b, tab_ref[h, b], acc)
    out_ref[...] = acc - far if minus_far else acc


def _bias_tiles(tab_t, bkt, head0, n_heads, lane_major, minus_far=False):
    g, r, c = bkt.shape
    if lane_major:
        out_shape = jax.ShapeDtypeStruct((g, r, n_heads * c), jnp.float32)
        out_spec = pl.BlockSpec((g, r, c), lambda h: (0, 0, h))
    else:
        out_shape = jax.ShapeDtypeStruct((n_heads * g, r, c), jnp.float32)
        out_spec = pl.BlockSpec((g, r, c), lambda h: (h, 0, 0))
    return pl.pallas_call(
        functools.partial(_bias_kernel, head0=head0, minus_far=minus_far),
        grid=(n_heads,),
        in_specs=[pl.BlockSpec(memory_space=pltpu.SMEM),
                  pl.BlockSpec((g, r, c), lambda h: (0, 0, 0))],
        out_specs=out_spec,
        out_shape=out_shape,
        compiler_params=_cparams("arbitrary"),
        name="bias_tiles",
    )(tab_t, bkt)


_C_QLAT = 0
_C_CKV = _C_QLAT + A_HEADS * A_LAT
_C_QIDX = _C_CKV + A_LAT
_C_KW = _C_QIDX + IDX_HEADS * IDX_DIM
_C_QB = _C_KW + LANES
_C_KB = _C_QB + B_HEADS * B_HEAD_DIM
_C_VB = _C_KB + B_KV_HEADS * B_HEAD_DIM
_C_GA = _C_VB + B_KV_HEADS * B_HEAD_DIM
_C_GB = _C_GA + D_MODEL
_C_END = _C_GB + D_MODEL
_RAW_KW_END = _C_KW + IDX_DIM + IDX_HEADS


def _proj_kernel(x_ref, w_ref, g_ref, q_ref, ckv_ref, ckvt_ref, qidx_ref, kidx_ref, widx_ref,
                 qb_ref, kb_ref, vb_ref, ga_ref, gb_ref):
    xb = x_ref[...].astype(jnp.bfloat16)
    tm = xb.shape[0]

    def seg(lo, hi):
        return jnp.dot(xb, w_ref[:, lo:hi], preferred_element_type=jnp.float32)

    for h in range(A_HEADS):
        qh = (seg(_C_QLAT + h * A_LAT, _C_QLAT + (h + 1) * A_LAT) * (A_LAT ** -0.5)).astype(jnp.bfloat16)
        for r in range(tm // BLOCK):
            q_ref[r, h] = qh[r * BLOCK:(r + 1) * BLOCK]
    c = seg(_C_CKV, _C_QIDX)
    ms = jnp.mean(c * c, axis=-1, keepdims=True)
    cn = c * lax.rsqrt(ms + RMS_EPS) * g_ref[...]
    ckv_ref[...] = cn.astype(jnp.bfloat16)
    for r in range(tm // BLOCK):
        ckvt_ref[0, :A_LAT, r * BLOCK:(r + 1) * BLOCK] = cn[r * BLOCK:(r + 1) * BLOCK].T.astype(jnp.bfloat16)
    ckvt_ref[0, A_LAT:, :] = jnp.ones((8, tm), jnp.bfloat16)
    qif = seg(_C_QIDX, _C_KW).astype(jnp.bfloat16)
    for h in range(IDX_HEADS):
        for r in range(tm // BLOCK):
            qidx_ref[r, h] = qif[r * BLOCK:(r + 1) * BLOCK, h * IDX_DIM:(h + 1) * IDX_DIM]
    kw = seg(_C_KW, _C_QB)
    kidx_ref[...] = kw[:, :IDX_DIM].astype(jnp.bfloat16)
    widx_ref[...] = kw[:, IDX_DIM:IDX_DIM + IDX_HEADS]
    qbf = seg(_C_QB, _C_KB).astype(jnp.bfloat16)
    for h in range(B_HEADS):
        for r in range(tm // BLOCK):
            qb_ref[r, h] = qbf[r * BLOCK:(r + 1) * BLOCK, h * B_HEAD_DIM:(h + 1) * B_HEAD_DIM]
    kbf = seg(_C_KB, _C_VB).astype(jnp.bfloat16)
    vbf = seg(_C_VB, _C_GA)
    for g in range(B_KV_HEADS):
        kb_ref[g] = kbf[:, g * B_HEAD_DIM:(g + 1) * B_HEAD_DIM]
        vb_ref[g, 0, B_HEAD_DIM:, :] = jnp.ones((SWA_ONES, tm), jnp.bfloat16)
    for r in range(tm // BLOCK):
        vt = vbf[r * BLOCK:(r + 1) * BLOCK].T.astype(jnp.bfloat16)
        for g in range(B_KV_HEADS):
            vb_ref[g, 0, :B_HEAD_DIM, r * BLOCK:(r + 1) * BLOCK] = vt[g * B_HEAD_DIM:(g + 1) * B_HEAD_DIM]
    ga_ref[...] = jax.nn.sigmoid(seg(_C_GA, _C_GB))
    gb_ref[...] = jax.nn.sigmoid(seg(_C_GB, _C_END))


def _input_projection(x2, w_pack, kv_g, tm, seq):
    n = x2.shape[0]
    assert seq % tm == 0
    bf, f32 = jnp.bfloat16, jnp.float32
    row = lambda w: pl.BlockSpec((tm, w), lambda i: (i, 0))
    out_shape = (
        jax.ShapeDtypeStruct((n // BLOCK, A_HEADS, BLOCK, A_LAT), bf),
        jax.ShapeDtypeStruct((n, A_LAT), bf),
        jax.ShapeDtypeStruct((n // seq, A_LAT + 8, seq), bf),
        jax.ShapeDtypeStruct((n // BLOCK, IDX_HEADS, BLOCK, IDX_DIM), bf),
        jax.ShapeDtypeStruct((n, IDX_DIM), bf),
        jax.ShapeDtypeStruct((n, IDX_HEADS), f32),
        jax.ShapeDtypeStruct((n // BLOCK, B_HEADS, BLOCK, B_HEAD_DIM), bf),
        jax.ShapeDtypeStruct((B_KV_HEADS, n, B_HEAD_DIM), bf),
        jax.ShapeDtypeStruct((B_KV_HEADS, n // seq, B_HEAD_DIM + SWA_ONES, seq), bf),
        jax.ShapeDtypeStruct((n, D_MODEL), f32),
        jax.ShapeDtypeStruct((n, D_MODEL), f32),
    )
    out_specs = (
        pl.BlockSpec((tm // BLOCK, A_HEADS, BLOCK, A_LAT), lambda i: (i, 0, 0, 0)),
        row(A_LAT),
        pl.BlockSpec((1, A_LAT + 8, tm), lambda i: (i // (seq // tm), 0, i % (seq // tm))),
        pl.BlockSpec((tm // BLOCK, IDX_HEADS, BLOCK, IDX_DIM), lambda i: (i, 0, 0, 0)),
        row(IDX_DIM), row(IDX_HEADS),
        pl.BlockSpec((tm // BLOCK, B_HEADS, BLOCK, B_HEAD_DIM), lambda i: (i, 0, 0, 0)),
        pl.BlockSpec((B_KV_HEADS, tm, B_HEAD_DIM), lambda i: (0, i, 0)),
        pl.BlockSpec((B_KV_HEADS, 1, B_HEAD_DIM + SWA_ONES, tm),
                     lambda i: (0, i // (seq // tm), 0, i % (seq // tm))),
        row(D_MODEL), row(D_MODEL),
    )
    return pl.pallas_call(
        _proj_kernel,
        grid=(n // tm,),
        in_specs=[row(D_MODEL),
                  pl.BlockSpec((D_MODEL, _C_END), lambda i: (0, 0)),
                  pl.BlockSpec((1, A_LAT), lambda i: (0, 0))],
        out_specs=out_specs,
        out_shape=out_shape,
        compiler_params=_cparams("arbitrary"),
        name="input_projection",
    )(x2, w_pack, kv_g)


DSA_UNROLL = 4


def _bit_transpose32(words):
    w = list(words)
    j, m = 16, 0x0000FFFF
    while j:
        mask = jnp.int32(m - (1 << 32) if m >= (1 << 31) else m)
        k = 0
        while k < 32:
            t = (w[k] ^ lax.shift_right_logical(w[k + j], jnp.int32(j))) & mask
            w[k] = w[k] ^ t
            w[k + j] = w[k + j] ^ lax.shift_left(t, jnp.int32(j))
            k = (k + j + 1) & ~j
        j >>= 1
        m = (m ^ (m << j)) & 0xFFFFFFFF
    return w


def _dsa_kernel(qidx_ref, kidx_ref, widx_ref, q_ref, ckv_ref, ckvt_ref, bias_ref, wuvt_ref,
                tri_ref, ya_ref, sc_ref, neg_ref, lg_ref, o_ref, planes_ref, sel_ref, *, k_sel,
                search_chunk):
    i = pl.program_id(1)
    f32 = jnp.float32
    s_loc = lax.broadcasted_iota(jnp.int32, (BLOCK, BLOCK), 0)
    t_loc = lax.broadcasted_iota(jnp.int32, (BLOCK, BLOCK), 1)
    causal_diag = s_loc <= t_loc
    idx_scale = IDX_DIM ** -0.5
    w_scale = IDX_HEADS ** -0.5

    def rows(j):
        return pl.ds(pl.multiple_of(j * BLOCK, BLOCK), BLOCK)

    sc_ref[...] = jnp.full(sc_ref.shape, -jnp.inf, f32)
    wts = (widx_ref[0] * w_scale) * idx_scale
    qi = qidx_ref[0]

    def score_keys(j):
        s = lax.dot_general(kidx_ref[0, rows(j), :], qi, _NT, preferred_element_type=f32)
        r = jnp.maximum(s, 0.0) * wts
        acc = r[:, :BLOCK]
        for h in range(1, IDX_HEADS):
            acc = acc + r[:, h * BLOCK:(h + 1) * BLOCK]
        return acc

    def fold_blocks(lo, hi, fn, init, combine):
        assert DSA_UNROLL == 4
        n = hi - lo
        n_grp = lax.shift_right_logical(n, DSA_UNROLL.bit_length() - 1)

        def group(k, acc):
            j = lo + DSA_UNROLL * k
            vals = [fn(j + u) for u in range(DSA_UNROLL)]
            while len(vals) > 1:
                vals = [combine(vals[a], vals[a + 1]) for a in range(0, len(vals), 2)]
            return combine(acc, vals[0])

        acc = lax.fori_loop(0, n_grp, group, init)
        j2 = lo + DSA_UNROLL * n_grp
        left = hi - j2
        acc = lax.cond((left & 2) != 0, lambda: combine(acc, combine(fn(j2), fn(j2 + 1))), lambda: acc)
        j1 = j2 + (left & 2)
        return lax.cond((left & 1) != 0, lambda: combine(acc, fn(j1)), lambda: acc)

    def far_scores(j):
        sc_ref[rows(j), :] = score_keys(j)
        return jnp.int32(0)

    fold_blocks(0, i, far_scores, jnp.int32(0), lambda a, b: a)
    sc_ref[rows(i), :] = jnp.where(causal_diag, score_keys(i), -jnp.inf)

    n_chunks = ((i + 1) * BLOCK + search_chunk - 1) // search_chunk
    n_acc = 64

    def count(cmp, cand):
        def body(c, cnt):
            base = pl.multiple_of(c * search_chunk, search_chunk)
            for r in range(search_chunk // n_acc):
                cnt = cnt + cmp(sc_ref[pl.ds(base + r * n_acc, n_acc), :], cand).astype(f32)
            return cnt
        cnt = lax.fori_loop(0, n_chunks, body, jnp.zeros((n_acc, BLOCK), f32))
        return jnp.sum(cnt, axis=0, keepdims=True)

    def ordered_to_float(u):
        k = u ^ INT_MIN
        return lax.bitcast_convert_type(k ^ ((k >> 31) & 0x7FFFFFFF), f32)

    n_grp_keys = 32 * 8
    n_groups = ((i + 1) * BLOCK + n_grp_keys - 1) // n_grp_keys
    max_groups = planes_ref.shape[0]

    def build_planes(g, carry):
        words = []
        for v in range(32):
            s = sc_ref[pl.ds(pl.multiple_of(g * n_grp_keys + v * 8, 8), 8), :]
            bits = lax.bitcast_convert_type(s, jnp.int32)
            words.append(bits ^ ((bits >> 31) & 0x7FFFFFFF) ^ INT_MIN)
        planes = _bit_transpose32(words)
        for p in range(32):
            planes_ref[g, p] = planes[p]
        return carry

    lax.fori_loop(0, n_groups, build_planes, 0)

    def select_bit(n, carry):
        alive, k_left, thr_u = carry
        b = 31 - n
        ones = [alive[g] & planes_ref[jnp.minimum(g, n_groups - 1), n] for g in range(max_groups)]
        cnt = lax.population_count(ones[0])
        for g in range(1, max_groups):
            cnt = cnt + lax.population_count(ones[g])
        c1 = jnp.sum(cnt.astype(f32), axis=0, keepdims=True)
        take = c1 >= k_left
        take_w = jnp.broadcast_to(take, (8, BLOCK))
        alive = [jnp.where(take_w, ones[g], alive[g] ^ ones[g]) for g in range(max_groups)]
        thr_u = jnp.where(take, thr_u | lax.shift_left(jnp.int32(1), b), thr_u)
        return alive, jnp.where(take, k_left, k_left - c1), thr_u

    alive0 = [jnp.where(g < n_groups, jnp.full((8, BLOCK), -1, jnp.int32), 0) for g in range(max_groups)]
    _, _, thr_u = lax.fori_loop(
        0, 32, select_bit, (alive0, jnp.full((1, BLOCK), k_sel, f32), jnp.zeros((1, BLOCK), jnp.int32)))
    thr = ordered_to_float(thr_u)
    c_ge = count(jnp.greater_equal, thr)
    c_gt = count(jnp.greater, thr)
    sel_ref[0:1, :] = thr
    sel_ref[1:2, :] = c_ge
    sel_ref[2:3, :] = c_gt
    consistent = jnp.min(((c_ge >= k_sel) & (c_gt < k_sel)).astype(f32)) > 0.0

    @pl.when(jnp.logical_not(consistent))
    def _():
        def search_bit(b, prefix_u):
            cand_u = prefix_u | lax.shift_left(jnp.int32(1), 31 - b)
            cnt = count(jnp.greater_equal, ordered_to_float(cand_u))
            return jnp.where(cnt >= k_sel, cand_u, prefix_u)

        t = ordered_to_float(lax.fori_loop(0, 32, search_bit, jnp.zeros((1, BLOCK), jnp.int32)))
        t = jnp.where(t != t, -jnp.inf, t)
        sel_ref[0:1, :] = t
        sel_ref[1:2, :] = count(jnp.greater_equal, t)
        sel_ref[2:3, :] = count(jnp.greater, t)

    thr = sel_ref[0:1, :]
    c_ge = sel_ref[1:2, :]
    c_gt = sel_ref[2:3, :]
    n_tie = k_sel - c_gt
    any_tie = jnp.max(((c_ge > k_sel) & (thr > -jnp.inf)).astype(f32)) > 0.0

    @pl.when(jnp.logical_not(any_tie))
    def _():
        def body(j, carry):
            neg_ref[rows(j), :] = jnp.where(sc_ref[rows(j), :] >= thr, 0.0, -jnp.inf)
            return carry
        lax.fori_loop(0, i, body, 0)
        neg_ref[rows(i), :] = jnp.where((sc_ref[rows(i), :] >= thr) & causal_diag, 0.0, -jnp.inf)

    @pl.when(any_tie)
    def _():
        tri = tri_ref[...]

        def block(j, tie_seen):
            kc = sc_ref[rows(j), :]
            eq = kc == thr
            tie_rank = jnp.dot(tri, eq.astype(jnp.bfloat16), preferred_element_type=f32) + tie_seen
            sel = (kc > thr) | (eq & (tie_rank <= n_tie))
            return sel, tie_seen + jnp.sum(eq.astype(f32), axis=0, keepdims=True)

        def body(j, tie_seen):
            sel, tie_seen = block(j, tie_seen)
            neg_ref[rows(j), :] = jnp.where(sel, 0.0, -jnp.inf)
            return tie_seen

        tie_seen = lax.fori_loop(0, i, body, jnp.zeros((1, BLOCK), f32))
        sel, _ = block(i, tie_seen)
        neg_ref[rows(i), :] = jnp.where(sel & causal_diag, 0.0, -jnp.inf)

    q_all = q_ref[0]

    def logits(j):
        lg = lax.dot_general(ckv_ref[0, rows(j), :], q_all, _NT, preferred_element_type=f32)
        return lg + jnp.concatenate([neg_ref[rows(j), :]] * A_HEADS, axis=1)

    def far_logits(j):
        lg = logits(j)
        lg_ref[rows(j), :] = lg
        return jnp.max(lg, axis=0, keepdims=True)

    def near_logits(j):
        lg = logits(j) + bias_ref[j - i + 1]
        lg_ref[rows(j), :] = lg
        return jnp.max(lg, axis=0, keepdims=True)

    n_far = jnp.maximum(i - 1, 0)
    m_run = fold_blocks(0, n_far, far_logits, jnp.full((1, A_HEADS * BLOCK), -jnp.inf, f32), jnp.maximum)
    m_run = fold_blocks(n_far, i + 1, near_logits, m_run, jnp.maximum)

    o_ref[...] = jnp.zeros_like(o_ref)

    def pv_rows(r):
        p = jnp.exp(lg_ref[r, :] - m_run).astype(jnp.bfloat16)
        o_ref[...] += jnp.dot(ckvt_ref[0, :, r], p, preferred_element_type=f32)

    def pv_group(k, carry):
        span = DSA_UNROLL * BLOCK
        pv_rows(pl.ds(pl.multiple_of(k * span, span), span))
        return carry

    n_grp = lax.shift_right_logical(i + 1, DSA_UNROLL.bit_length() - 1)
    lax.fori_loop(0, n_grp, pv_group, 0)
    j2 = DSA_UNROLL * n_grp
    left = i + 1 - j2

    @pl.when((left & 2) != 0)
    def _():
        pv_rows(pl.ds(pl.multiple_of(j2 * BLOCK, 2 * BLOCK), 2 * BLOCK))

    @pl.when((left & 1) != 0)
    def _():
        pv_rows(rows(j2 + (left & 2)))

    inv_l = 1.0 / o_ref[A_LAT:A_LAT + 1, :]
    for pair in range(A_HEADS // 2):
        halves = []
        for h in (2 * pair, 2 * pair + 1):
            cols = slice(h * BLOCK, (h + 1) * BLOCK)
            y_t = jnp.dot(wuvt_ref[h], o_ref[:A_LAT, cols].astype(jnp.bfloat16),
                          preferred_element_type=f32)
            halves.append(y_t * inv_l[:, cols])
        two = jnp.concatenate(halves, axis=0)
        ya_ref[:, 2 * pair * A_HEAD_DIM:(2 * pair + 2) * A_HEAD_DIM] = two.T.astype(ya_ref.dtype)


def _dsa_attention(qidx_blocks, kidx, widx_l, q_blocks, ckv, ckv_tx, bias_near, wuv, bsz, seq):
    n_blk = seq // BLOCK
    k_sel = min(TOPK_MAX, seq // 4)
    search_chunk = min(512, seq)
    tri = jnp.asarray(np.tril(np.ones((BLOCK, BLOCK), np.float32)), jnp.bfloat16)
    kern = functools.partial(_dsa_kernel, k_sel=float(k_sel), search_chunk=search_chunk)
    blk = lambda b, i: (b * n_blk + i, 0, 0)
    return pl.pallas_call(
        kern,
        grid=(bsz, n_blk),
        in_specs=[
            pl.BlockSpec((1, IDX_HEADS * BLOCK, IDX_DIM), blk),
            pl.BlockSpec((1, seq, IDX_DIM), lambda b, i: (b, 0, 0)),
            pl.BlockSpec((1, 1, IDX_HEADS * BLOCK), blk),
            pl.BlockSpec((1, A_HEADS * BLOCK, A_LAT), blk),
            pl.BlockSpec((1, seq, A_LAT), lambda b, i: (b, 0, 0)),
            pl.BlockSpec((1, A_LAT + 8, seq), lambda b, i: (b, 0, 0)),
            pl.BlockSpec((2, BLOCK, A_HEADS * BLOCK), lambda b, i: (0, 0, 0)),
            pl.BlockSpec((A_HEADS, A_HEAD_DIM, A_LAT), lambda b, i: (0, 0, 0)),
            pl.BlockSpec((BLOCK, BLOCK), lambda b, i: (0, 0)),
        ],
        out_specs=pl.BlockSpec((BLOCK, A_HEADS * A_HEAD_DIM), lambda b, i: (b * n_blk + i, 0)),
        out_shape=jax.ShapeDtypeStruct((bsz * seq, A_HEADS * A_HEAD_DIM), jnp.bfloat16),
        scratch_shapes=[pltpu.VMEM((seq, BLOCK), jnp.float32),
                        pltpu.VMEM((seq, BLOCK), jnp.float32),
                        pltpu.VMEM((seq, A_HEADS * BLOCK), jnp.float32),
                        pltpu.VMEM((A_LAT + 8, A_HEADS * BLOCK), jnp.float32),
                        pltpu.VMEM((seq // 256, 32, 8, BLOCK), jnp.int32),
                        pltpu.VMEM((8, BLOCK), jnp.float32)],
        compiler_params=_cparams("arbitrary", "arbitrary"),
        name="dsa_attention",
    )(qidx_blocks, kidx, widx_l, q_blocks, ckv, ckv_tx, bias_near, wuv, tri)


SWA_QB = 4
SWA_ONES = 16


def _swa_kernel(sink_ref, q_ref, kp_ref, kc_ref, vp_ref, vc_ref, bias_ref, yb_ref, *, qb):
    i = pl.program_id(1)
    f32 = jnp.float32
    grp = B_HEADS // B_KV_HEADS
    cols = grp * BLOCK
    s_loc = lax.broadcasted_iota(jnp.int32, (BLOCK, BLOCK), 0)
    t_loc = lax.broadcasted_iota(jnp.int32, (BLOCK, BLOCK), 1)
    neg_cur = jnp.concatenate([jnp.where(s_loc <= t_loc, 0.0, -jnp.inf)] * grp, axis=1)
    in_prev = s_loc > t_loc
    head_of_col = lax.broadcasted_iota(jnp.int32, (1, cols), 1) >> (BLOCK.bit_length() - 1)
    scale = B_HEAD_DIM ** -0.5
    for g in range(B_KV_HEADS):
        sink = jnp.zeros((1, cols), f32)
        for hh in range(grp):
            sink = jnp.where(head_of_col == hh, sink_ref[g * grp + hh], sink)
        bias_p = bias_ref[:BLOCK, g * cols:(g + 1) * cols]
        bias_c = bias_ref[BLOCK:, g * cols:(g + 1) * cols]
        for r in range(qb):
            has_prev = jnp.zeros_like(s_loc) + jnp.minimum(i * qb + r, 1)
            neg_prev = jnp.concatenate([jnp.where(in_prev & (has_prev > 0), 0.0, -jnp.inf)] * grp, axis=1)
            q = q_ref[r, g * grp:(g + 1) * grp].reshape(cols, B_HEAD_DIM)
            if r == 0:
                kp, vp = kp_ref[g], vp_ref[g, 0]
            else:
                kp, vp = kc_ref[g, (r - 1) * BLOCK:r * BLOCK], vc_ref[g, 0, :, (r - 1) * BLOCK:r * BLOCK]
            kc, vc = kc_ref[g, r * BLOCK:(r + 1) * BLOCK], vc_ref[g, 0, :, r * BLOCK:(r + 1) * BLOCK]
            lp = lax.dot_general(kp, q, _NT, preferred_element_type=f32) * scale + bias_p + neg_prev
            lc = lax.dot_general(kc, q, _NT, preferred_element_type=f32) * scale + bias_c + neg_cur
            m = jnp.maximum(jnp.max(jnp.maximum(lp, lc), axis=0, keepdims=True), sink)
            pp = jnp.exp(lp - m).astype(jnp.bfloat16)
            pc = jnp.exp(lc - m).astype(jnp.bfloat16)
            ox = (jnp.dot(vp, pp, preferred_element_type=f32)
                  + jnp.dot(vc, pc, preferred_element_type=f32))
            den = ox[B_HEAD_DIM:B_HEAD_DIM + 1, :] + jnp.exp(sink - m)
            o = ox[:B_HEAD_DIM, :] / den
            for pair in range(grp // 2):
                two = jnp.concatenate([o[:, (2 * pair) * BLOCK:(2 * pair + 1) * BLOCK],
                                       o[:, (2 * pair + 1) * BLOCK:(2 * pair + 2) * BLOCK]], axis=0)
                h0 = g * grp + 2 * pair
                yb_ref[r * BLOCK:(r + 1) * BLOCK, h0 * B_HEAD_DIM:(h0 + 2) * B_HEAD_DIM] = (
                    two.T.astype(yb_ref.dtype))


def _swa_attention(sinks, q_blocks, kb, vbt, bias_b, bsz, seq):
    n_blk = seq // BLOCK
    qb = SWA_QB if n_blk % SWA_QB == 0 else 1
    n_step = n_blk // qb
    cur = lambda b, i: (0, b * n_step + i, 0)
    prev = lambda b, i: (0, b * n_blk + jnp.maximum(i * qb - 1, 0), 0)
    vrows = B_HEAD_DIM + SWA_ONES
    return pl.pallas_call(
        functools.partial(_swa_kernel, qb=qb),
        grid=(bsz, n_step),
        in_specs=[
            pl.BlockSpec(memory_space=pltpu.SMEM),
            pl.BlockSpec((qb, B_HEADS, BLOCK, B_HEAD_DIM), lambda b, i: (b * n_step + i, 0, 0, 0)),
            pl.BlockSpec((B_KV_HEADS, BLOCK, B_HEAD_DIM), prev),
            pl.BlockSpec((B_KV_HEADS, qb * BLOCK, B_HEAD_DIM), cur),
            pl.BlockSpec((B_KV_HEADS, 1, vrows, BLOCK), lambda b, i: (0, b, 0, jnp.maximum(i * qb - 1, 0))),
            pl.BlockSpec((B_KV_HEADS, 1, vrows, qb * BLOCK), lambda b, i: (0, b, 0, i)),
            pl.BlockSpec((2 * BLOCK, B_HEADS * BLOCK), lambda b, i: (0, 0)),
        ],
        out_specs=pl.BlockSpec((qb * BLOCK, B_HEADS * B_HEAD_DIM), lambda b, i: (b * n_step + i, 0)),
        out_shape=jax.ShapeDtypeStruct((bsz * seq, B_HEADS * B_HEAD_DIM), jnp.bfloat16),
        compiler_params=_cparams("arbitrary", "arbitrary"),
        name="swa_attention",
    )(sinks, q_blocks, kb, kb, vbt, vbt, bias_b)


def _layer_norm(h, g, b):
    mu = jnp.mean(h, axis=-1, keepdims=True)
    d = h - mu
    var = jnp.mean(d * d, axis=-1, keepdims=True)
    return d * lax.rsqrt(var + LN_EPS) * g + b


def _merge_kernel(ya_ref, yb_ref, ga_ref, gb_ref, x_ref, wa_ref, wb_ref, wo_ref, g1_ref, b1_ref,
                  wrh_ref, wrl_ref, br_ref, tri_ref, x1_ref, rt_ref, cnt_ref, carry_ref):
    f32 = jnp.float32
    pa = jnp.dot(ya_ref[...], wa_ref[...], preferred_element_type=f32)
    pb = jnp.dot(yb_ref[...], wb_ref[...], preferred_element_type=f32)
    merged = ga_ref[...] * pa + gb_ref[...] * pb
    h = ALPHA * x_ref[...] + jnp.dot(merged.astype(jnp.bfloat16), wo_ref[...],
                                     preferred_element_type=f32)
    x1 = _layer_norm(h, g1_ref[...], b1_ref[...])
    _store_tile_rows(x1_ref, x1)

    x_hi = x1.astype(jnp.bfloat16)
    x_lo = (x1 - x_hi.astype(f32)).astype(jnp.bfloat16)
    lg = (jnp.dot(x_hi, wrh_ref[...], preferred_element_type=f32)
          + jnp.dot(x_hi, wrl_ref[...], preferred_element_type=f32)
          + jnp.dot(x_lo, wrh_ref[...], preferred_element_type=f32)) + br_ref[...]
    lane_i = lax.broadcasted_iota(jnp.int32, lg.shape, 1)
    lane = lane_i.astype(f32)
    big = jnp.float32(1 << 20)
    is_g = lane_i < N_GROUPS
    gl = jnp.where(is_g, lg, -jnp.inf)
    gmax = jnp.max(gl, axis=1, keepdims=True)
    g_sel = jnp.min(jnp.where(gl == gmax, lane, big), axis=1, keepdims=True)
    g_w = 1.0 / jnp.sum(jnp.where(is_g, jnp.exp(gl - gmax), 0.0), axis=1, keepdims=True)
    e_id = lane_i - N_GROUPS
    e_grp = (e_id >> 3).astype(f32)
    in_grp = (e_id >= 0) & (e_id < N_EXPERTS) & (e_grp == g_sel)
    el = jnp.where(in_grp, lg, -jnp.inf)
    emax = jnp.max(el, axis=1, keepdims=True)
    ee = jnp.where(in_grp, jnp.exp(el - emax), 0.0)
    ep = ee / jnp.sum(ee, axis=1, keepdims=True)
    epm = jnp.where(in_grp, ep, -1.0)
    p1 = jnp.max(epm, axis=1, keepdims=True)
    i1 = jnp.min(jnp.where(epm == p1, lane, big), axis=1, keepdims=True)
    epm2 = jnp.where(lane == i1, -1.0, epm)
    p2 = jnp.max(epm2, axis=1, keepdims=True)
    i2 = jnp.min(jnp.where(epm2 == p2, lane, big), axis=1, keepdims=True)
    psum = p1 + p2
    w1 = g_w * p1 / psum
    w2 = g_w * p2 / psum
    @pl.when(pl.program_id(0) == 0)
    def _():
        carry_ref[...] = jnp.zeros_like(carry_ref)

    oh1 = lane + N_GROUPS == i1
    oh2 = lane + N_GROUPS == i2
    oh = jnp.where(oh1 | oh2, 1.0, 0.0)
    prefix = jnp.dot(tri_ref[...], oh.astype(jnp.bfloat16), preferred_element_type=f32) + carry_ref[...]
    rank1 = jnp.sum(jnp.where(oh1, prefix, 0.0), axis=1, keepdims=True)
    rank2 = jnp.sum(jnp.where(oh2, prefix, 0.0), axis=1, keepdims=True)
    carry_ref[...] += jnp.sum(oh, axis=0, keepdims=True)
    cnt_ref[...] = carry_ref[...]
    rec = (i1 - N_GROUPS, i2 - N_GROUPS, w1, w2, rank1, rank2)
    rt = jnp.zeros(lg.shape, f32)
    for k, v in enumerate(rec):
        rt = jnp.where(lane_i == k, v, rt)
    rt_ref[...] = rt


def _route_weights_kernel(wg_ref, wr_ref, hi_ref, lo_ref):
    w = jnp.concatenate([wg_ref[...], wr_ref[...],
                         jnp.zeros((D_MODEL, LANES - N_GROUPS - N_EXPERTS), jnp.float32)], axis=1)
    hi = w.astype(jnp.bfloat16)
    hi_ref[...] = hi
    lo_ref[...] = (w - hi.astype(jnp.float32)).astype(jnp.bfloat16)


def _route_weights(w_group, w_router):
    full = lambda c: pl.BlockSpec((D_MODEL, c), lambda: (0, 0))
    return pl.pallas_call(
        _route_weights_kernel,
        in_specs=[full(N_GROUPS), full(N_EXPERTS)],
        out_specs=(full(LANES), full(LANES)),
        out_shape=(jax.ShapeDtypeStruct((D_MODEL, LANES), jnp.bfloat16),) * 2,
        name="route_weights",
    )(w_group, w_router)


def _merge_route(ya, yb, ga, gb, x2, wa, wb, wo, g1, b1, wr_hi, wr_lo, br, tm):
    n = x2.shape[0]
    row = lambda w: pl.BlockSpec((tm, w), lambda i: (i, 0))
    full = lambda r, c: pl.BlockSpec((r, c), lambda i: (0, 0))
    aw, bw = A_HEADS * A_HEAD_DIM, B_HEADS * B_HEAD_DIM
    tri = jnp.asarray(np.tril(np.ones((tm, tm), np.float32), -1), jnp.bfloat16)
    return pl.pallas_call(
        _merge_kernel,
        grid=(n // tm,),
        in_specs=[row(aw), row(bw), row(D_MODEL), row(D_MODEL), row(D_MODEL),
                  full(aw, D_MODEL), full(bw, D_MODEL), full(D_MODEL, D_MODEL),
                  full(1, D_MODEL), full(1, D_MODEL), full(D_MODEL, LANES), full(D_MODEL, LANES),
                  full(1, LANES), full(tm, tm)],
        out_specs=(pl.BlockSpec((tm * ROW_TILE, LANES), lambda i: (i, 0)), row(LANES), full(1, LANES)),
        out_shape=(jax.ShapeDtypeStruct((n * ROW_TILE, LANES), jnp.float32),
                   jax.ShapeDtypeStruct((n, LANES), jnp.float32),
                   jax.ShapeDtypeStruct((1, LANES), jnp.float32)),
        scratch_shapes=[pltpu.VMEM((1, LANES), jnp.float32)],
        compiler_params=_cparams("arbitrary"),
        name="merge_route",
    )(ya, yb, ga, gb, x2, wa, wb, wo, g1, b1, wr_hi, wr_lo, br, tri)


ROW_TILE = D_MODEL // LANES


def _store_tile_rows(ref, val):
    rows = val.shape[0]
    for j in range(ROW_TILE):
        ref[pl.ds(j, rows, stride=ROW_TILE), :] = val[:, j * LANES:(j + 1) * LANES]


def _load_tile_rows(ref, rows):
    return jnp.concatenate([ref[pl.ds(j, rows, stride=ROW_TILE), :] for j in range(ROW_TILE)], axis=1)


DMA_UNROLL = 8


def _dispatch_kernel(pos_ref, x_ref, xs_hbm, sem, *, tm):
    def issue(k, carry):
        for u in range(DMA_UNROLL):
            r = k * DMA_UNROLL + u
            src = x_ref.at[pl.ds(pl.multiple_of(r * ROW_TILE, ROW_TILE), ROW_TILE)]
            for slot in range(2):
                p = pos_ref[0, slot, r]
                dst = xs_hbm.at[pl.ds(pl.multiple_of(p * ROW_TILE, ROW_TILE), ROW_TILE)]
                pltpu.make_async_copy(src, dst, sem).start(priority=slot)
        return carry

    lax.fori_loop(0, tm // DMA_UNROLL, issue, 0)
    for slot in range(2):
        pltpu.make_async_copy(x_ref, xs_hbm.at[pl.ds(0, tm * ROW_TILE)], sem).wait()


def _moe_dispatch(pos, x1r, n_rows, tm):
    n = x1r.shape[0] // ROW_TILE
    return pl.pallas_call(
        functools.partial(_dispatch_kernel, tm=tm),
        grid=(n // tm,),
        in_specs=[pl.BlockSpec((1, 8, tm), lambda i: (i, 0, 0), memory_space=pltpu.SMEM),
                  pl.BlockSpec((tm * ROW_TILE, LANES), lambda i: (i, 0))],
        out_specs=pl.BlockSpec(memory_space=pl.ANY),
        out_shape=jax.ShapeDtypeStruct((n_rows * ROW_TILE, LANES), jnp.float32),
        scratch_shapes=[pltpu.SemaphoreType.DMA],
        compiler_params=_cparams("arbitrary"),
        name="moe_dispatch",
    )(pos, x1r)


def _expert_kernel(tile_ref, exp_ref, lo_ref, hi_ref, nit_ref, xs_ref, wg_ref, wu_ref, wd_ref, ys_ref,
                   wgb_ref, wub_ref, wdb_ref, *, tm):
    f32 = jnp.float32
    w = pl.program_id(0)
    prev = jnp.maximum(w - 1, 0)

    @pl.when(jnp.logical_or(w == 0, exp_ref[prev] != exp_ref[w]))
    def _():
        wgb_ref[...] = wg_ref[0].astype(jnp.bfloat16)
        wub_ref[...] = wu_ref[0].astype(jnp.bfloat16)
        wdb_ref[...] = wd_ref[0].astype(jnp.bfloat16)

    @pl.when(w < nit_ref[0])
    def _():
        xb = _load_tile_rows(xs_ref, tm).astype(jnp.bfloat16)
        g = jnp.dot(xb, wgb_ref[...], preferred_element_type=f32)
        u = jnp.dot(xb, wub_ref[...], preferred_element_type=f32)
        hmid = (g * jax.nn.sigmoid(g) * u).astype(jnp.bfloat16)
        y = jnp.dot(hmid, wdb_ref[...], preferred_element_type=f32)
        row = tile_ref[w] * tm + lax.broadcasted_iota(jnp.int32, (tm, 1), 0)
        mine = (row >= lo_ref[w]) & (row < hi_ref[w])
        first_visit = jnp.logical_or(w == 0, tile_ref[prev] != tile_ref[w])

        @pl.when(first_visit)
        def _():
            _store_tile_rows(ys_ref, jnp.where(mine, y, 0.0))

        @pl.when(jnp.logical_not(first_visit))
        def _():
            _store_tile_rows(ys_ref, jnp.where(mine, y, _load_tile_rows(ys_ref, tm)))


def _moe_experts(items, xs, wg, wu, wd, tm):
    n_items = items[0].shape[0]
    grid_spec = pltpu.PrefetchScalarGridSpec(
        num_scalar_prefetch=5,
        grid=(n_items,),
        in_specs=[
            pl.BlockSpec((tm * ROW_TILE, LANES), lambda w, t, e, lo, hi, n: (t[w], 0)),
            pl.BlockSpec((1, D_MODEL, D_EXPERT), lambda w, t, e, lo, hi, n: (e[w], 0, 0)),
            pl.BlockSpec((1, D_MODEL, D_EXPERT), lambda w, t, e, lo, hi, n: (e[w], 0, 0)),
            pl.BlockSpec((1, D_EXPERT, D_MODEL), lambda w, t, e, lo, hi, n: (e[w], 0, 0)),
        ],
        out_specs=pl.BlockSpec((tm * ROW_TILE, LANES), lambda w, t, e, lo, hi, n: (t[w], 0)),
        scratch_shapes=[pltpu.VMEM((D_MODEL, D_EXPERT), jnp.bfloat16),
                        pltpu.VMEM((D_MODEL, D_EXPERT), jnp.bfloat16),
                        pltpu.VMEM((D_EXPERT, D_MODEL), jnp.bfloat16)],
    )
    return pl.pallas_call(
        functools.partial(_expert_kernel, tm=tm),
        grid_spec=grid_spec,
        out_shape=jax.ShapeDtypeStruct(xs.shape, jnp.float32),
        compiler_params=_cparams("arbitrary"),
        name="moe_experts",
    )(*items, xs, wg, wu, wd)


def _combine_kernel(pos_ref, rt_ref, x_ref, ys_hbm, g2_ref, b2_ref, out_ref, buf_a, buf_b, sem, *, tm):
    bufs = (buf_a, buf_b)
    th = tm // 2
    half = lambda ref, h: ref.at[pl.ds(h * th * ROW_TILE, th * ROW_TILE)]

    def issue_half(h):
        def issue(k, carry):
            for u in range(DMA_UNROLL):
                r = h * th + k * DMA_UNROLL + u
                for slot in range(2):
                    p = pos_ref[0, slot, r]
                    src = ys_hbm.at[pl.ds(pl.multiple_of(p * ROW_TILE, ROW_TILE), ROW_TILE)]
                    dst = bufs[slot].at[pl.ds(pl.multiple_of(r * ROW_TILE, ROW_TILE), ROW_TILE)]
                    pltpu.make_async_copy(src, dst, sem.at[h]).start(priority=slot)
            return carry
        lax.fori_loop(0, th // DMA_UNROLL, issue, 0)

    def finish_half(h):
        for slot in range(2):
            pltpu.make_async_copy(ys_hbm.at[pl.ds(0, th * ROW_TILE)], half(bufs[slot], h), sem.at[h]).wait()
        rt = rt_ref[h * th:(h + 1) * th, :]
        ffn = (rt[:, 2:3] * _load_tile_rows(half(buf_a, h), th)
               + rt[:, 3:4] * _load_tile_rows(half(buf_b, h), th))
        out_ref[h * th:(h + 1) * th, :] = _layer_norm(
            ALPHA * _load_tile_rows(half(x_ref, h), th) + ffn, g2_ref[...], b2_ref[...])

    issue_half(0)
    issue_half(1)
    finish_half(0)
    finish_half(1)


def _moe_combine(pos, rt, x1r, ys, g2, b2, tm):
    n = rt.shape[0]
    return pl.pallas_call(
        functools.partial(_combine_kernel, tm=tm),
        grid=(n // tm,),
        in_specs=[pl.BlockSpec((1, 8, tm), lambda i: (i, 0, 0), memory_space=pltpu.SMEM),
                  pl.BlockSpec((tm, LANES), lambda i: (i, 0)),
                  pl.BlockSpec((tm * ROW_TILE, LANES), lambda i: (i, 0)),
                  pl.BlockSpec(memory_space=pl.ANY),
                  pl.BlockSpec((1, D_MODEL), lambda i: (0, 0)),
                  pl.BlockSpec((1, D_MODEL), lambda i: (0, 0))],
        out_specs=pl.BlockSpec((tm, D_MODEL), lambda i: (i, 0)),
        out_shape=jax.ShapeDtypeStruct((n, D_MODEL), jnp.float32),
        scratch_shapes=[pltpu.VMEM((tm * ROW_TILE, LANES), jnp.float32),
                        pltpu.VMEM((tm * ROW_TILE, LANES), jnp.float32),
                        pltpu.SemaphoreType.DMA((2,))],
        compiler_params=_cparams("arbitrary"),
        name="moe_combine",
    )(pos, rt, x1r, ys, g2, b2)


MOE_TM = 512


def _pos_kernel(start_ref, rt_ref, pos_ref):
    rt = rt_ref[...]
    lane = lax.broadcasted_iota(jnp.int32, rt.shape, 1)
    lane_f = lane.astype(jnp.float32)
    start = start_ref[...]
    tile = jnp.zeros(rt.shape, jnp.float32)
    for slot in range(2):
        first_row = jnp.sum(jnp.where(lane_f == rt[:, slot:slot + 1], start, 0.0), axis=1, keepdims=True)
        tile = jnp.where(lane == slot, first_row + rt[:, 4 + slot:5 + slot], tile)
    tm = pos_ref.shape[2]
    for r in range(rt.shape[0] // BLOCK):
        t = tile[r * BLOCK:(r + 1) * BLOCK].T
        c = (r * BLOCK) % tm
        pos_ref[(r * BLOCK) // tm, :, c:c + BLOCK] = t[:8].astype(jnp.int32)


def _moe_pos(start, rt, tm):
    n = rt.shape[0]
    per_step = _pick_tile(n // tm, 4)
    return pl.pallas_call(
        _pos_kernel,
        grid=(n // (tm * per_step),),
        in_specs=[pl.BlockSpec((1, LANES), lambda i: (0, 0)),
                  pl.BlockSpec((tm * per_step, LANES), lambda i: (i, 0))],
        out_specs=pl.BlockSpec((per_step, 8, tm), lambda i: (i, 0, 0)),
        out_shape=jax.ShapeDtypeStruct((n // tm, 8, tm), jnp.int32),
        compiler_params=_cparams("arbitrary"),
        name="moe_pos",
    )(start, rt)


def _moe_plan(counts, n_tiles):
    i32 = jnp.int32
    cnt = counts[0, :N_EXPERTS].astype(i32)
    end = jnp.cumsum(cnt)
    start = end - cnt
    start_lanes = jnp.pad(start.astype(jnp.float32), (0, LANES - N_EXPERTS)).reshape(1, LANES)
    first_t = start // MOE_TM
    items_e = jnp.where(cnt > 0, (end - 1) // MOE_TM - first_t + 1, 0)
    item_end = jnp.cumsum(items_e)
    n_items = item_end[-1]
    w = jnp.minimum(jnp.arange(n_tiles + N_EXPERTS - 1, dtype=i32), n_items - 1)
    e_w = jnp.minimum(jnp.sum((item_end[None, :] <= w[:, None]).astype(i32), axis=1), N_EXPERTS - 1)
    t_w = first_t[e_w] + (w - (item_end - items_e)[e_w])
    lo_w = jnp.maximum(start[e_w], t_w * MOE_TM)
    hi_w = jnp.minimum(end[e_w], (t_w + 1) * MOE_TM)
    return start_lanes, (t_w.astype(i32), e_w, lo_w.astype(i32), hi_w.astype(i32),
                         n_items.reshape(1).astype(i32))


def _pick_tile(n, pref):
    t = min(pref, n)
    while n % t:
        t //= 2
    return t


def kernel(x, w_in, kv_norm_g, w_uv, w_branch_a, sinks, w_branch_b, w_out, rel_bias, ln1_g, ln1_b,
           w_group, b_group, w_router, b_router, w_gate, w_up, w_down, ln2_g, ln2_b):
    bsz, seq, d = x.shape
    assert d == D_MODEL and seq % BLOCK == 0 and w_in.shape[0] == DEPTH == 1
    n = bsz * seq
    n_blk = seq // BLOCK
    bf, f32 = jnp.bfloat16, jnp.float32
    x2 = x.reshape(n, d)

    r = np.arange(BLOCK, dtype=np.int32)
    d_prev = r[None, :] + BLOCK - r[:, None]
    d_cur = r[None, :] - r[:, None]
    bkt_a = jnp.asarray(_t5_bucket_np(np.stack([d_prev, d_cur])))
    s2 = np.arange(2 * BLOCK, dtype=np.int32)
    bkt_b = jnp.asarray(_t5_bucket_np(r[None, :] + BLOCK - s2[:, None])[None])
    assert int(_t5_bucket_np(np.arange(BLOCK + 1, max(seq, BLOCK + 2))).min()) == N_BUCKETS - 1
    tab_t = rel_bias.astype(f32).T
    bias_a = _bias_tiles(tab_t, bkt_a, 0, A_HEADS, True, minus_far=True)
    bias_b = _bias_tiles(tab_t, bkt_b, A_HEADS, B_HEADS, True)[0]

    w0 = w_in[0]
    w_pack = jnp.concatenate(
        [w0[:, :_RAW_KW_END], jnp.zeros((d, _C_QB - _RAW_KW_END), w0.dtype), w0[:, _RAW_KW_END:]],
        axis=1).astype(bf)
    (q_blocks, ckv, ckv_tx, qidx_blocks, kidx, widx, qb_blocks, kb, vbx, ga, gb) = _input_projection(
        x2, w_pack, kv_norm_g[0].reshape(1, A_LAT).astype(f32), _pick_tile(seq, 512), seq)

    ckv3 = ckv.reshape(bsz, seq, A_LAT)
    widx_l = jnp.swapaxes(widx.reshape(bsz * n_blk, BLOCK, IDX_HEADS), 1, 2).reshape(
        bsz * n_blk, 1, IDX_HEADS * BLOCK)
    ya = _dsa_attention(qidx_blocks.reshape(bsz * n_blk, IDX_HEADS * BLOCK, IDX_DIM),
                        kidx.reshape(bsz, seq, IDX_DIM), widx_l,
                        q_blocks.reshape(bsz * n_blk, A_HEADS * BLOCK, A_LAT),
                        ckv3, ckv_tx, bias_a, jnp.swapaxes(w_uv[0], 1, 2).astype(bf), bsz, seq)

    yb = _swa_attention(sinks[0].astype(f32), qb_blocks, kb, vbx, bias_b, bsz, seq)

    wr_hi, wr_lo = _route_weights(w_group[0].astype(f32), w_router[0].astype(f32))
    b_route = jnp.concatenate(
        [b_group[0], b_router[0], jnp.zeros((LANES - N_GROUPS - N_EXPERTS,), f32)]).reshape(1, LANES).astype(f32)
    x1r, rt, counts = _merge_route(
        ya, yb, ga, gb, x2, w_branch_a[0].astype(bf), w_branch_b[0].astype(bf), w_out[0].astype(bf),
        ln1_g[0].reshape(1, d).astype(f32), ln1_b[0].reshape(1, d).astype(f32), wr_hi, wr_lo, b_route,
        _pick_tile(n, 512))

    assert (2 * n) % MOE_TM == 0
    start_lanes, items = _moe_plan(counts, 2 * n // MOE_TM)
    tm_io = _pick_tile(n, 512)
    pos_blocks = _moe_pos(start_lanes, rt, tm_io)
    xs = _moe_dispatch(pos_blocks, x1r, 2 * n, tm_io)
    ys = _moe_experts(items, xs, w_gate[0], w_up[0], w_down[0], MOE_TM)
    out = _moe_combine(pos_blocks, rt, x1r, ys, ln2_g[0].reshape(1, d).astype(f32),
                       ln2_b[0].reshape(1, d).astype(f32), tm_io)
    return out.reshape(bsz, seq, d)
```

```python
import functools
import math

import numpy as np
import jax
import jax.numpy as jnp
from jax import lax
from jax.experimental import pallas as pl
from jax.experimental.pallas import tpu as pltpu

D_MODEL = 1024
A_HEADS = 8
A_LAT = 128
A_HEAD_DIM = 64
IDX_HEADS = 8
IDX_DIM = 64
TOPK_MAX = 256
B_HEADS = 8
B_KV_HEADS = 2
B_HEAD_DIM = 64
WINDOW = 128
BLOCK = 128
N_BUCKETS = 32
MAX_DISTANCE = 128
N_GROUPS = 4
EXPERTS_PER_GROUP = 8
N_EXPERTS = 32
D_EXPERT = 256
DEPTH = 1
ALPHA = (2 * DEPTH) ** 0.25
LN_EPS = 1e-5
RMS_EPS = 1e-6

LANES = 128
INT_MIN = -(2 ** 31)
VMEM_LIMIT = 48 * 1024 * 1024

_NT = (((1,), (1,)), ((), ()))


def _cparams(*sem):
    return pltpu.CompilerParams(dimension_semantics=sem, vmem_limit_bytes=VMEM_LIMIT)


def _t5_bucket_np(dist):
    f32 = np.float32
    n = np.maximum(dist, 0)
    max_exact = N_BUCKETS // 2
    nf = np.maximum(n, 1).astype(f32)
    large = max_exact + (np.log(nf / f32(max_exact)) / f32(math.log(MAX_DISTANCE / max_exact))
                         * f32(N_BUCKETS - max_exact)).astype(np.int32)
    large = np.minimum(large, N_BUCKETS - 1)
    return np.where(n < max_exact, n, large).astype(np.int32)


def _bias_kernel(tab_ref, bkt_ref, out_ref, *, head0, minus_far):
    h = pl.program_id(0) + head0
    bkt = bkt_ref[...]
    far = tab_ref[h, N_BUCKETS - 1]
    acc = jnp.full(bkt.shape, far, jnp.float32)
    for b in range(N_BUCKETS - 1):
        acc = jnp.where(bkt == b, tab_ref[h, b], acc)
    out_ref[...] = acc - far if minus_far else acc


def _bias_tiles(tab_t, bkt, head0, n_heads, lane_major, minus_far=False):
    g, r, c = bkt.shape
    if lane_major:
        out_shape = jax.ShapeDtypeStruct((g, r, n_heads * c), jnp.float32)
        out_spec = pl.BlockSpec((g, r, c), lambda h: (0, 0, h))
    else:
        out_shape = jax.ShapeDtypeStruct((n_heads * g, r, c), jnp.float32)
        out_spec = pl.BlockSpec((g, r, c), lambda h: (h, 0, 0))
    return pl.pallas_call(
        functools.partial(_bias_kernel, head0=head0, minus_far=minus_far),
        grid=(n_heads,),
        in_specs=[pl.BlockSpec(memory_space=pltpu.SMEM),
                  pl.BlockSpec((g, r, c), lambda h: (0, 0, 0))],
        out_specs=out_spec,
        out_shape=out_shape,
        compiler_params=_cparams("arbitrary"),
        name="bias_tiles",
    )(tab_t, bkt)


_C_QLAT = 0
_C_CKV = _C_QLAT + A_HEADS * A_LAT
_C_QIDX = _C_CKV + A_LAT
_C_KW = _C_QIDX + IDX_HEADS * IDX_DIM
_C_QB = _C_KW + LANES
_C_KB = _C_QB + B_HEADS * B_HEAD_DIM
_C_VB = _C_KB + B_KV_HEADS * B_HEAD_DIM
_C_GA = _C_VB + B_KV_HEADS * B_HEAD_DIM
_C_GB = _C_GA + D_MODEL
_C_END = _C_GB + D_MODEL
_RAW_KW_END = _C_KW + IDX_DIM + IDX_HEADS


def _proj_kernel(x_ref, w_ref, g_ref, q_ref, ckv_ref, ckvt_ref, qidx_ref, kidx_ref, widx_ref,
                 qb_ref, kb_ref, vb_ref, ga_ref, gb_ref):
    xb = x_ref[...].astype(jnp.bfloat16)
    tm = xb.shape[0]

    def seg(lo, hi):
        return jnp.dot(xb, w_ref[:, lo:hi], preferred_element_type=jnp.float32)

    for h in range(A_HEADS):
        qh = (seg(_C_QLAT + h * A_LAT, _C_QLAT + (h + 1) * A_LAT) * (A_LAT ** -0.5)).astype(jnp.bfloat16)
        for r in range(tm // BLOCK):
            q_ref[r, h] = qh[r * BLOCK:(r + 1) * BLOCK]
    c = seg(_C_CKV, _C_QIDX)
    ms = jnp.mean(c * c, axis=-1, keepdims=True)
    cn = c * lax.rsqrt(ms + RMS_EPS) * g_ref[...]
    ckv_ref[...] = cn.astype(jnp.bfloat16)
    for r in range(tm // BLOCK):
        ckvt_ref[0, :A_LAT, r * BLOCK:(r + 1) * BLOCK] = cn[r * BLOCK:(r + 1) * BLOCK].T.astype(jnp.bfloat16)
    ckvt_ref[0, A_LAT:, :] = jnp.ones((8, tm), jnp.bfloat16)
    qif = seg(_C_QIDX, _C_KW).astype(jnp.bfloat16)
    for h in range(IDX_HEADS):
        for r in range(tm // BLOCK):
            qidx_ref[r, h] = qif[r * BLOCK:(r + 1) * BLOCK, h * IDX_DIM:(h + 1) * IDX_DIM]
    kw = seg(_C_KW, _C_QB)
    kidx_ref[...] = kw[:, :IDX_DIM].astype(jnp.bfloat16)
    widx_ref[...] = kw[:, IDX_DIM:IDX_DIM + IDX_HEADS]
    qbf = seg(_C_QB, _C_KB).astype(jnp.bfloat16)
    for h in range(B_HEADS):
        for r in range(tm // BLOCK):
            qb_ref[r, h] = qbf[r * BLOCK:(r + 1) * BLOCK, h * B_HEAD_DIM:(h + 1) * B_HEAD_DIM]
    kbf = seg(_C_KB, _C_VB).astype(jnp.bfloat16)
    vbf = seg(_C_VB, _C_GA)
    for g in range(B_KV_HEADS):
        kb_ref[g] = kbf[:, g * B_HEAD_DIM:(g + 1) * B_HEAD_DIM]
        vb_ref[g, 0, B_HEAD_DIM:, :] = jnp.ones((SWA_ONES, tm), jnp.bfloat16)
    for r in range(tm // BLOCK):
        vt = vbf[r * BLOCK:(r + 1) * BLOCK].T.astype(jnp.bfloat16)
        for g in range(B_KV_HEADS):
            vb_ref[g, 0, :B_HEAD_DIM, r * BLOCK:(r + 1) * BLOCK] = vt[g * B_HEAD_DIM:(g + 1) * B_HEAD_DIM]
    ga_ref[...] = jax.nn.sigmoid(seg(_C_GA, _C_GB))
    gb_ref[...] = jax.nn.sigmoid(seg(_C_GB, _C_END))


def _input_projection(x2, w_pack, kv_g, tm, seq):
    n = x2.shape[0]
    assert seq % tm == 0
    bf, f32 = jnp.bfloat16, jnp.float32
    row = lambda w: pl.BlockSpec((tm, w), lambda i: (i, 0))
    out_shape = (
        jax.ShapeDtypeStruct((n // BLOCK, A_HEADS, BLOCK, A_LAT), bf),
        jax.ShapeDtypeStruct((n, A_LAT), bf),
        jax.ShapeDtypeStruct((n // seq, A_LAT + 8, seq), bf),
        jax.ShapeDtypeStruct((n // BLOCK, IDX_HEADS, BLOCK, IDX_DIM), bf),
        jax.ShapeDtypeStruct((n, IDX_DIM), bf),
        jax.ShapeDtypeStruct((n, IDX_HEADS), f32),
        jax.ShapeDtypeStruct((n // BLOCK, B_HEADS, BLOCK, B_HEAD_DIM), bf),
        jax.ShapeDtypeStruct((B_KV_HEADS, n, B_HEAD_DIM), bf),
        jax.ShapeDtypeStruct((B_KV_HEADS, n // seq, B_HEAD_DIM + SWA_ONES, seq), bf),
        jax.ShapeDtypeStruct((n, D_MODEL), f32),
        jax.ShapeDtypeStruct((n, D_MODEL), f32),
    )
    out_specs = (
        pl.BlockSpec((tm // BLOCK, A_HEADS, BLOCK, A_LAT), lambda i: (i, 0, 0, 0)),
        row(A_LAT),
        pl.BlockSpec((1, A_LAT + 8, tm), lambda i: (i // (seq // tm), 0, i % (seq // tm))),
        pl.BlockSpec((tm // BLOCK, IDX_HEADS, BLOCK, IDX_DIM), lambda i: (i, 0, 0, 0)),
        row(IDX_DIM), row(IDX_HEADS),
        pl.BlockSpec((tm // BLOCK, B_HEADS, BLOCK, B_HEAD_DIM), lambda i: (i, 0, 0, 0)),
        pl.BlockSpec((B_KV_HEADS, tm, B_HEAD_DIM), lambda i: (0, i, 0)),
        pl.BlockSpec((B_KV_HEADS, 1, B_HEAD_DIM + SWA_ONES, tm),
                     lambda i: (0, i // (seq // tm), 0, i % (seq // tm))),
        row(D_MODEL), row(D_MODEL),
    )
    return pl.pallas_call(
        _proj_kernel,
        grid=(n // tm,),
        in_specs=[row(D_MODEL),
                  pl.BlockSpec((D_MODEL, _C_END), lambda i: (0, 0)),
                  pl.BlockSpec((1, A_LAT), lambda i: (0, 0))],
        out_specs=out_specs,
        out_shape=out_shape,
        compiler_params=_cparams("arbitrary"),
        name="input_projection",
    )(x2, w_pack, kv_g)


DSA_UNROLL = 8


def _bit_transpose32(words):
    w = list(words)
    j, m = 16, 0x0000FFFF
    while j:
        mask = jnp.int32(m - (1 << 32) if m >= (1 << 31) else m)
        k = 0
        while k < 32:
            t = (w[k] ^ lax.shift_right_logical(w[k + j], jnp.int32(j))) & mask
            w[k] = w[k] ^ t
            w[k + j] = w[k + j] ^ lax.shift_left(t, jnp.int32(j))
            k = (k + j + 1) & ~j
        j >>= 1
        m = (m ^ (m << j)) & 0xFFFFFFFF
    return w


def _dsa_kernel(qidx_ref, kidx_ref, widx_ref, q_ref, ckv_ref, ckvt_ref, bias_ref, wuvt_ref,
                tri_ref, ya_ref, sc_ref, neg_ref, lg_ref, o_ref, planes_ref, sel_ref, *, k_sel,
                search_chunk):
    i = pl.program_id(1)
    f32 = jnp.float32
    s_loc = lax.broadcasted_iota(jnp.int32, (BLOCK, BLOCK), 0)
    t_loc = lax.broadcasted_iota(jnp.int32, (BLOCK, BLOCK), 1)
    causal_diag = s_loc <= t_loc
    idx_scale = IDX_DIM ** -0.5
    w_scale = IDX_HEADS ** -0.5

    def rows(j):
        return pl.ds(pl.multiple_of(j * BLOCK, BLOCK), BLOCK)

    sc_ref[...] = jnp.full(sc_ref.shape, -jnp.inf, f32)
    wts = (widx_ref[0] * w_scale) * idx_scale
    qi = qidx_ref[0]

    def score_keys(j):
        s = lax.dot_general(kidx_ref[0, rows(j), :], qi, _NT, preferred_element_type=f32)
        r = jnp.maximum(s, 0.0) * wts
        acc = r[:, :BLOCK]
        for h in range(1, IDX_HEADS):
            acc = acc + r[:, h * BLOCK:(h + 1) * BLOCK]
        return acc

    def fold_blocks(lo, hi, fn, init, combine):
        n = hi - lo
        n_grp = lax.shift_right_logical(n, DSA_UNROLL.bit_length() - 1)

        def piece(acc, j, size):
            vals = [fn(j + u) for u in range(size)]
            while len(vals) > 1:
                vals = [combine(vals[a], vals[a + 1]) for a in range(0, len(vals), 2)]
            return combine(acc, vals[0])

        acc = lax.fori_loop(0, n_grp, lambda k, a: piece(a, lo + DSA_UNROLL * k, DSA_UNROLL), init)
        j = lo + DSA_UNROLL * n_grp
        left = hi - j
        size = DSA_UNROLL // 2
        while size:
            acc = lax.cond((left & size) != 0, functools.partial(piece, acc, j, size), lambda a=acc: a)
            j = j + (left & size)
            size //= 2
        return acc

    def far_scores(j):
        sc_ref[rows(j), :] = score_keys(j)
        return jnp.int32(0)

    fold_blocks(0, i, far_scores, jnp.int32(0), lambda a, b: a)
    sc_ref[rows(i), :] = jnp.where(causal_diag, score_keys(i), -jnp.inf)

    n_chunks = ((i + 1) * BLOCK + search_chunk - 1) // search_chunk
    n_acc = 64

    def count(cmp, cand):
        def body(c, cnt):
            base = pl.multiple_of(c * search_chunk, search_chunk)
            for r in range(search_chunk // n_acc):
                cnt = cnt + cmp(sc_ref[pl.ds(base + r * n_acc, n_acc), :], cand).astype(f32)
            return cnt
        cnt = lax.fori_loop(0, n_chunks, body, jnp.zeros((n_acc, BLOCK), f32))
        return jnp.sum(cnt, axis=0, keepdims=True)

    def ordered_to_float(u):
        k = u ^ INT_MIN
        return lax.bitcast_convert_type(k ^ ((k >> 31) & 0x7FFFFFFF), f32)

    n_grp_keys = 32 * 8
    n_groups = ((i + 1) * BLOCK + n_grp_keys - 1) // n_grp_keys
    max_groups = planes_ref.shape[0]

    def build_planes(g, carry):
        words = []
        for v in range(32):
            s = sc_ref[pl.ds(pl.multiple_of(g * n_grp_keys + v * 8, 8), 8), :]
            bits = lax.bitcast_convert_type(s, jnp.int32)
            words.append(bits ^ ((bits >> 31) & 0x7FFFFFFF) ^ INT_MIN)
        planes = _bit_transpose32(words)
        for p in range(32):
            planes_ref[g, p] = planes[p]
        return carry

    lax.fori_loop(0, n_groups, build_planes, 0)

    def popcount_rows(words):
        cnt = lax.population_count(words[0])
        for wd in words[1:]:
            cnt = cnt + lax.population_count(wd)
        return jnp.sum(cnt.astype(f32), axis=0, keepdims=True)

    def select_two_bits(n, carry):
        alive, k_left, thr_u = carry
        grp = [jnp.minimum(g, n_groups - 1) for g in range(max_groups)]
        hi1 = [alive[g] & planes_ref[grp[g], 2 * n] for g in range(max_groups)]
        hi0 = [alive[g] ^ hi1[g] for g in range(max_groups)]
        lo = [planes_ref[grp[g], 2 * n + 1] for g in range(max_groups)]
        c11 = [hi1[g] & lo[g] for g in range(max_groups)]
        c10 = [hi1[g] ^ c11[g] for g in range(max_groups)]
        c01 = [hi0[g] & lo[g] for g in range(max_groups)]
        c00 = [hi0[g] ^ c01[g] for g in range(max_groups)]
        s1 = popcount_rows(c11)
        s2 = s1 + popcount_rows(c10)
        s3 = s2 + popcount_rows(c01)
        t11 = s1 >= k_left
        t10 = jnp.logical_and(jnp.logical_not(t11), s2 >= k_left)
        t01 = jnp.logical_and(jnp.logical_not(t11 | t10), s3 >= k_left)
        w11, w10, w01 = (jnp.broadcast_to(t, (8, BLOCK)) for t in (t11, t10, t01))
        alive = [jnp.where(w11, c11[g], jnp.where(w10, c10[g], jnp.where(w01, c01[g], c00[g])))
                 for g in range(max_groups)]
        k_left = k_left - jnp.where(t11, 0.0, jnp.where(t10, s1, jnp.where(t01, s2, s3)))
        b_hi = 31 - 2 * n
        bits = (jnp.where(t11 | t10, lax.shift_left(jnp.int32(1), b_hi), 0)
                | jnp.where(t11 | t01, lax.shift_left(jnp.int32(1), b_hi - 1), 0))
        return alive, k_left, thr_u | bits

    alive0 = [jnp.where(g < n_groups, jnp.full((8, BLOCK), -1, jnp.int32), 0) for g in range(max_groups)]
    _, _, thr_u = lax.fori_loop(
        0, 16, select_two_bits, (alive0, jnp.full((1, BLOCK), k_sel, f32), jnp.zeros((1, BLOCK), jnp.int32)))
    thr = ordered_to_float(thr_u)
    c_ge = count(jnp.greater_equal, thr)
    c_gt = count(jnp.greater, thr)
    sel_ref[0:1, :] = thr
    sel_ref[1:2, :] = c_ge
    sel_ref[2:3, :] = c_gt
    consistent = jnp.min(((c_ge >= k_sel) & (c_gt < k_sel)).astype(f32)) > 0.0

    @pl.when(jnp.logical_not(consistent))
    def _():
        def search_bit(b, prefix_u):
            cand_u = prefix_u | lax.shift_left(jnp.int32(1), 31 - b)
            cnt = count(jnp.greater_equal, ordered_to_float(cand_u))
            return jnp.where(cnt >= k_sel, cand_u, prefix_u)

        t = ordered_to_float(lax.fori_loop(0, 32, search_bit, jnp.zeros((1, BLOCK), jnp.int32)))
        t = jnp.where(t != t, -jnp.inf, t)
        sel_ref[0:1, :] = t
        sel_ref[1:2, :] = count(jnp.greater_equal, t)
        sel_ref[2:3, :] = count(jnp.greater, t)

    thr = sel_ref[0:1, :]
    c_ge = sel_ref[1:2, :]
    c_gt = sel_ref[2:3, :]
    n_tie = k_sel - c_gt
    any_tie = jnp.max(((c_ge > k_sel) & (thr > -jnp.inf)).astype(f32)) > 0.0

    @pl.when(jnp.logical_not(any_tie))
    def _():
        def body(j, carry):
            neg_ref[rows(j), :] = jnp.where(sc_ref[rows(j), :] >= thr, 0.0, -jnp.inf)
            return carry
        lax.fori_loop(0, i, body, 0)
        neg_ref[rows(i), :] = jnp.where((sc_ref[rows(i), :] >= thr) & causal_diag, 0.0, -jnp.inf)

    @pl.when(any_tie)
    def _():
        tri = tri_ref[...]

        def block(j, tie_seen):
            kc = sc_ref[rows(j), :]
            eq = kc == thr
            tie_rank = jnp.dot(tri, eq.astype(jnp.bfloat16), preferred_element_type=f32) + tie_seen
            sel = (kc > thr) | (eq & (tie_rank <= n_tie))
            return sel, tie_seen + jnp.sum(eq.astype(f32), axis=0, keepdims=True)

        def body(j, tie_seen):
            sel, tie_seen = block(j, tie_seen)
            neg_ref[rows(j), :] = jnp.where(sel, 0.0, -jnp.inf)
            return tie_seen

        tie_seen = lax.fori_loop(0, i, body, jnp.zeros((1, BLOCK), f32))
        sel, _ = block(i, tie_seen)
        neg_ref[rows(i), :] = jnp.where(sel & causal_diag, 0.0, -jnp.inf)

    q_all = q_ref[0]

    def logits(j):
        lg = lax.dot_general(ckv_ref[0, rows(j), :], q_all, _NT, preferred_element_type=f32)
        return lg + jnp.concatenate([neg_ref[rows(j), :]] * A_HEADS, axis=1)

    def far_logits(j):
        lg = logits(j)
        lg_ref[rows(j), :] = lg
        return jnp.max(lg, axis=0, keepdims=True)

    def near_logits(j):
        lg = logits(j) + bias_ref[j - i + 1]
        lg_ref[rows(j), :] = lg
        return jnp.max(lg, axis=0, keepdims=True)

    n_far = jnp.maximum(i - 1, 0)
    m_run = fold_blocks(0, n_far, far_logits, jnp.full((1, A_HEADS * BLOCK), -jnp.inf, f32), jnp.maximum)
    m_run = fold_blocks(n_far, i + 1, near_logits, m_run, jnp.maximum)

    o_ref[...] = jnp.zeros_like(o_ref)

    def pv_rows(r):
        p = jnp.exp(lg_ref[r, :] - m_run).astype(jnp.bfloat16)
        o_ref[...] += jnp.dot(ckvt_ref[0, :, r], p, preferred_element_type=f32)

    def pv_group(k, carry):
        span = DSA_UNROLL * BLOCK
        pv_rows(pl.ds(pl.multiple_of(k * span, span), span))
        return carry

    n_grp = lax.shift_right_logical(i + 1, DSA_UNROLL.bit_length() - 1)
    lax.fori_loop(0, n_grp, pv_group, 0)
    j = DSA_UNROLL * n_grp
    left = i + 1 - j
    size = DSA_UNROLL // 2
    while size:
        @pl.when((left & size) != 0)
        def _(j=j, size=size):
            pv_rows(pl.ds(pl.multiple_of(j * BLOCK, size * BLOCK), size * BLOCK))
        j = j + (left & size)
        size //= 2

    inv_l = 1.0 / o_ref[A_LAT:A_LAT + 1, :]
    for pair in range(A_HEADS // 2):
        halves = []
        for h in (2 * pair, 2 * pair + 1):
            cols = slice(h * BLOCK, (h + 1) * BLOCK)
            y_t = jnp.dot(wuvt_ref[h], o_ref[:A_LAT, cols].astype(jnp.bfloat16),
                          preferred_element_type=f32)
            halves.append(y_t * inv_l[:, cols])
        two = jnp.concatenate(halves, axis=0)
        ya_ref[:, 2 * pair * A_HEAD_DIM:(2 * pair + 2) * A_HEAD_DIM] = two.T.astype(ya_ref.dtype)


def _dsa_attention(qidx_blocks, kidx, widx_l, q_blocks, ckv, ckv_tx, bias_near, wuv, bsz, seq):
    n_blk = seq // BLOCK
    k_sel = min(TOPK_MAX, seq // 4)
    search_chunk = min(512, seq)
    tri = jnp.asarray(np.tril(np.ones((BLOCK, BLOCK), np.float32)), jnp.bfloat16)
    kern = functools.partial(_dsa_kernel, k_sel=float(k_sel), search_chunk=search_chunk)
    blk = lambda b, i: (b * n_blk + i, 0, 0)
    return pl.pallas_call(
        kern,
        grid=(bsz, n_blk),
        in_specs=[
            pl.BlockSpec((1, IDX_HEADS * BLOCK, IDX_DIM), blk),
            pl.BlockSpec((1, seq, IDX_DIM), lambda b, i: (b, 0, 0)),
            pl.BlockSpec((1, 1, IDX_HEADS * BLOCK), blk),
            pl.BlockSpec((1, A_HEADS * BLOCK, A_LAT), blk),
            pl.BlockSpec((1, seq, A_LAT), lambda b, i: (b, 0, 0)),
            pl.BlockSpec((1, A_LAT + 8, seq), lambda b, i: (b, 0, 0)),
            pl.BlockSpec((2, BLOCK, A_HEADS * BLOCK), lambda b, i: (0, 0, 0)),
            pl.BlockSpec((A_HEADS, A_HEAD_DIM, A_LAT), lambda b, i: (0, 0, 0)),
            pl.BlockSpec((BLOCK, BLOCK), lambda b, i: (0, 0)),
        ],
        out_specs=pl.BlockSpec((BLOCK, A_HEADS * A_HEAD_DIM), lambda b, i: (b * n_blk + i, 0)),
        out_shape=jax.ShapeDtypeStruct((bsz * seq, A_HEADS * A_HEAD_DIM), jnp.bfloat16),
        scratch_shapes=[pltpu.VMEM((seq, BLOCK), jnp.float32),
                        pltpu.VMEM((seq, BLOCK), jnp.float32),
                        pltpu.VMEM((seq, A_HEADS * BLOCK), jnp.float32),
                        pltpu.VMEM((A_LAT + 8, A_HEADS * BLOCK), jnp.float32),
                        pltpu.VMEM((seq // 256, 32, 8, BLOCK), jnp.int32),
                        pltpu.VMEM((8, BLOCK), jnp.float32)],
        compiler_params=_cparams("arbitrary", "arbitrary"),
        name="dsa_attention",
    )(qidx_blocks, kidx, widx_l, q_blocks, ckv, ckv_tx, bias_near, wuv, tri)


SWA_QB = 4
SWA_ONES = 16


def _swa_kernel(sink_ref, q_ref, kp_ref, kc_ref, vp_ref, vc_ref, bias_ref, yb_ref, *, qb):
    i = pl.program_id(1)
    f32 = jnp.float32
    grp = B_HEADS // B_KV_HEADS
    cols = grp * BLOCK
    s_loc = lax.broadcasted_iota(jnp.int32, (BLOCK, BLOCK), 0)
    t_loc = lax.broadcasted_iota(jnp.int32, (BLOCK, BLOCK), 1)
    neg_cur = jnp.concatenate([jnp.where(s_loc <= t_loc, 0.0, -jnp.inf)] * grp, axis=1)
    in_prev = s_loc > t_loc
    head_of_col = lax.broadcasted_iota(jnp.int32, (1, cols), 1) >> (BLOCK.bit_length() - 1)
    scale = B_HEAD_DIM ** -0.5
    for g in range(B_KV_HEADS):
        sink = jnp.zeros((1, cols), f32)
        for hh in range(grp):
            sink = jnp.where(head_of_col == hh, sink_ref[g * grp + hh], sink)
        bias_p = bias_ref[:BLOCK, g * cols:(g + 1) * cols]
        bias_c = bias_ref[BLOCK:, g * cols:(g + 1) * cols]
        for r in range(qb):
            has_prev = jnp.zeros_like(s_loc) + jnp.minimum(i * qb + r, 1)
            neg_prev = jnp.concatenate([jnp.where(in_prev & (has_prev > 0), 0.0, -jnp.inf)] * grp, axis=1)
            q = q_ref[r, g * grp:(g + 1) * grp].reshape(cols, B_HEAD_DIM)
            if r == 0:
                kp, vp = kp_ref[g], vp_ref[g, 0]
            else:
                kp, vp = kc_ref[g, (r - 1) * BLOCK:r * BLOCK], vc_ref[g, 0, :, (r - 1) * BLOCK:r * BLOCK]
            kc, vc = kc_ref[g, r * BLOCK:(r + 1) * BLOCK], vc_ref[g, 0, :, r * BLOCK:(r + 1) * BLOCK]
            lp = lax.dot_general(kp, q, _NT, preferred_element_type=f32) * scale + bias_p + neg_prev
            lc = lax.dot_general(kc, q, _NT, preferred_element_type=f32) * scale + bias_c + neg_cur
            m = jnp.maximum(jnp.max(jnp.maximum(lp, lc), axis=0, keepdims=True), sink)
            pp = jnp.exp(lp - m).astype(jnp.bfloat16)
            pc = jnp.exp(lc - m).astype(jnp.bfloat16)
            ox = (jnp.dot(vp, pp, preferred_element_type=f32)
                  + jnp.dot(vc, pc, preferred_element_type=f32))
            den = ox[B_HEAD_DIM:B_HEAD_DIM + 1, :] + jnp.exp(sink - m)
            o = ox[:B_HEAD_DIM, :] / den
            for pair in range(grp // 2):
                two = jnp.concatenate([o[:, (2 * pair) * BLOCK:(2 * pair + 1) * BLOCK],
                                       o[:, (2 * pair + 1) * BLOCK:(2 * pair + 2) * BLOCK]], axis=0)
                h0 = g * grp + 2 * pair
                yb_ref[r * BLOCK:(r + 1) * BLOCK, h0 * B_HEAD_DIM:(h0 + 2) * B_HEAD_DIM] = (
                    two.T.astype(yb_ref.dtype))


def _swa_attention(sinks, q_blocks, kb, vbt, bias_b, bsz, seq):
    n_blk = seq // BLOCK
    qb = SWA_QB if n_blk % SWA_QB == 0 else 1
    n_step = n_blk // qb
    cur = lambda b, i: (0, b * n_step + i, 0)
    prev = lambda b, i: (0, b * n_blk + jnp.maximum(i * qb - 1, 0), 0)
    vrows = B_HEAD_DIM + SWA_ONES
    return pl.pallas_call(
        functools.partial(_swa_kernel, qb=qb),
        grid=(bsz, n_step),
        in_specs=[
            pl.BlockSpec(memory_space=pltpu.SMEM),
            pl.BlockSpec((qb, B_HEADS, BLOCK, B_HEAD_DIM), lambda b, i: (b * n_step + i, 0, 0, 0)),
            pl.BlockSpec((B_KV_HEADS, BLOCK, B_HEAD_DIM), prev),
            pl.BlockSpec((B_KV_HEADS, qb * BLOCK, B_HEAD_DIM), cur),
            pl.BlockSpec((B_KV_HEADS, 1, vrows, BLOCK), lambda b, i: (0, b, 0, jnp.maximum(i * qb - 1, 0))),
            pl.BlockSpec((B_KV_HEADS, 1, vrows, qb * BLOCK), lambda b, i: (0, b, 0, i)),
            pl.BlockSpec((2 * BLOCK, B_HEADS * BLOCK), lambda b, i: (0, 0)),
        ],
        out_specs=pl.BlockSpec((qb * BLOCK, B_HEADS * B_HEAD_DIM), lambda b, i: (b * n_step + i, 0)),
        out_shape=jax.ShapeDtypeStruct((bsz * seq, B_HEADS * B_HEAD_DIM), jnp.bfloat16),
        compiler_params=_cparams("arbitrary", "arbitrary"),
        name="swa_attention",
    )(sinks, q_blocks, kb, kb, vbt, vbt, bias_b)


def _layer_norm(h, g, b):
    mu = jnp.mean(h, axis=-1, keepdims=True)
    d = h - mu
    var = jnp.mean(d * d, axis=-1, keepdims=True)
    return d * lax.rsqrt(var + LN_EPS) * g + b


def _merge_kernel(ya_ref, yb_ref, ga_ref, gb_ref, x_ref, wa_ref, wb_ref, wo_ref, g1_ref, b1_ref,
                  wrh_ref, wrl_ref, br_ref, tri_ref, x1_ref, rt_ref, cnt_ref, carry_ref):
    f32 = jnp.float32
    pa = jnp.dot(ya_ref[...], wa_ref[...], preferred_element_type=f32)
    pb = jnp.dot(yb_ref[...], wb_ref[...], preferred_element_type=f32)
    merged = ga_ref[...] * pa + gb_ref[...] * pb
    h = ALPHA * x_ref[...] + jnp.dot(merged.astype(jnp.bfloat16), wo_ref[...],
                                     preferred_element_type=f32)
    x1 = _layer_norm(h, g1_ref[...], b1_ref[...])
    _store_tile_rows(x1_ref, x1)

    x_hi = x1.astype(jnp.bfloat16)
    x_lo = (x1 - x_hi.astype(f32)).astype(jnp.bfloat16)
    lg = (jnp.dot(x_hi, wrh_ref[...], preferred_element_type=f32)
          + jnp.dot(x_hi, wrl_ref[...], preferred_element_type=f32)
          + jnp.dot(x_lo, wrh_ref[...], preferred_element_type=f32)) + br_ref[...]
    lane_i = lax.broadcasted_iota(jnp.int32, lg.shape, 1)
    lane = lane_i.astype(f32)
    big = jnp.float32(1 << 20)
    is_g = lane_i < N_GROUPS
    gl = jnp.where(is_g, lg, -jnp.inf)
    gmax = jnp.max(gl, axis=1, keepdims=True)
    g_sel = jnp.min(jnp.where(gl == gmax, lane, big), axis=1, keepdims=True)
    g_w = 1.0 / jnp.sum(jnp.where(is_g, jnp.exp(gl - gmax), 0.0), axis=1, keepdims=True)
    e_id = lane_i - N_GROUPS
    e_grp = (e_id >> 3).astype(f32)
    in_grp = (e_id >= 0) & (e_id < N_EXPERTS) & (e_grp == g_sel)
    el = jnp.where(in_grp, lg, -jnp.inf)
    emax = jnp.max(el, axis=1, keepdims=True)
    ee = jnp.where(in_grp, jnp.exp(el - emax), 0.0)
    ep = ee / jnp.sum(ee, axis=1, keepdims=True)
    epm = jnp.where(in_grp, ep, -1.0)
    p1 = jnp.max(epm, axis=1, keepdims=True)
    i1 = jnp.min(jnp.where(epm == p1, lane, big), axis=1, keepdims=True)
    epm2 = jnp.where(lane == i1, -1.0, epm)
    p2 = jnp.max(epm2, axis=1, keepdims=True)
    i2 = jnp.min(jnp.where(epm2 == p2, lane, big), axis=1, keepdims=True)
    psum = p1 + p2
    w1 = g_w * p1 / psum
    w2 = g_w * p2 / psum
    @pl.when(pl.program_id(0) == 0)
    def _():
        carry_ref[...] = jnp.zeros_like(carry_ref)

    oh1 = lane + N_GROUPS == i1
    oh2 = lane + N_GROUPS == i2
    oh = jnp.where(oh1 | oh2, 1.0, 0.0)
    prefix = jnp.dot(tri_ref[...], oh.astype(jnp.bfloat16), preferred_element_type=f32) + carry_ref[...]
    rank1 = jnp.sum(jnp.where(oh1, prefix, 0.0), axis=1, keepdims=True)
    rank2 = jnp.sum(jnp.where(oh2, prefix, 0.0), axis=1, keepdims=True)
    carry_ref[...] += jnp.sum(oh, axis=0, keepdims=True)
    cnt_ref[...] = carry_ref[...]
    rec = (i1 - N_GROUPS, i2 - N_GROUPS, w1, w2, rank1, rank2)
    rt = jnp.zeros(lg.shape, f32)
    for k, v in enumerate(rec):
        rt = jnp.where(lane_i == k, v, rt)
    rt_ref[...] = rt


def _route_weights_kernel(wg_ref, wr_ref, hi_ref, lo_ref):
    w = jnp.concatenate([wg_ref[...], wr_ref[...],
                         jnp.zeros((D_MODEL, LANES - N_GROUPS - N_EXPERTS), jnp.float32)], axis=1)
    hi = w.astype(jnp.bfloat16)
    hi_ref[...] = hi
    lo_ref[...] = (w - hi.astype(jnp.float32)).astype(jnp.bfloat16)


def _route_weights(w_group, w_router):
    full = lambda c: pl.BlockSpec((D_MODEL, c), lambda: (0, 0))
    return pl.pallas_call(
        _route_weights_kernel,
        in_specs=[full(N_GROUPS), full(N_EXPERTS)],
        out_specs=(full(LANES), full(LANES)),
        out_shape=(jax.ShapeDtypeStruct((D_MODEL, LANES), jnp.bfloat16),) * 2,
        name="route_weights",
    )(w_group, w_router)


def _merge_route(ya, yb, ga, gb, x2, wa, wb, wo, g1, b1, wr_hi, wr_lo, br, tm):
    n = x2.shape[0]
    row = lambda w: pl.BlockSpec((tm, w), lambda i: (i, 0))
    full = lambda r, c: pl.BlockSpec((r, c), lambda i: (0, 0))
    aw, bw = A_HEADS * A_HEAD_DIM, B_HEADS * B_HEAD_DIM
    tri = jnp.asarray(np.tril(np.ones((tm, tm), np.float32), -1), jnp.bfloat16)
    return pl.pallas_call(
        _merge_kernel,
        grid=(n // tm,),
        in_specs=[row(aw), row(bw), row(D_MODEL), row(D_MODEL), row(D_MODEL),
                  full(aw, D_MODEL), full(bw, D_MODEL), full(D_MODEL, D_MODEL),
                  full(1, D_MODEL), full(1, D_MODEL), full(D_MODEL, LANES), full(D_MODEL, LANES),
                  full(1, LANES), full(tm, tm)],
        out_specs=(pl.BlockSpec((tm * ROW_TILE, LANES), lambda i: (i, 0)), row(LANES), full(1, LANES)),
        out_shape=(jax.ShapeDtypeStruct((n * ROW_TILE, LANES), jnp.float32),
                   jax.ShapeDtypeStruct((n, LANES), jnp.float32),
                   jax.ShapeDtypeStruct((1, LANES), jnp.float32)),
        scratch_shapes=[pltpu.VMEM((1, LANES), jnp.float32)],
        compiler_params=_cparams("arbitrary"),
        name="merge_route",
    )(ya, yb, ga, gb, x2, wa, wb, wo, g1, b1, wr_hi, wr_lo, br, tri)


ROW_TILE = D_MODEL // LANES


def _store_tile_rows(ref, val):
    rows = val.shape[0]
    for j in range(ROW_TILE):
        ref[pl.ds(j, rows, stride=ROW_TILE), :] = val[:, j * LANES:(j + 1) * LANES]


def _load_tile_rows(ref, rows):
    return jnp.concatenate([ref[pl.ds(j, rows, stride=ROW_TILE), :] for j in range(ROW_TILE)], axis=1)


DMA_UNROLL = 8


def _dispatch_kernel(pos_ref, x_ref, xs_hbm, sem, *, tm):
    def issue(k, carry):
        for u in range(DMA_UNROLL):
            r = k * DMA_UNROLL + u
            src = x_ref.at[pl.ds(pl.multiple_of(r * ROW_TILE, ROW_TILE), ROW_TILE)]
            for slot in range(2):
                p = pos_ref[0, slot, r]
                dst = xs_hbm.at[pl.ds(pl.multiple_of(p * ROW_TILE, ROW_TILE), ROW_TILE)]
                pltpu.make_async_copy(src, dst, sem).start(priority=slot)
        return carry

    lax.fori_loop(0, tm // DMA_UNROLL, issue, 0)
    for slot in range(2):
        pltpu.make_async_copy(x_ref, xs_hbm.at[pl.ds(0, tm * ROW_TILE)], sem).wait()


def _moe_dispatch(pos, x1r, n_rows, tm):
    n = x1r.shape[0] // ROW_TILE
    return pl.pallas_call(
        functools.partial(_dispatch_kernel, tm=tm),
        grid=(n // tm,),
        in_specs=[pl.BlockSpec((1, 8, tm), lambda i: (i, 0, 0), memory_space=pltpu.SMEM),
                  pl.BlockSpec((tm * ROW_TILE, LANES), lambda i: (i, 0))],
        out_specs=pl.BlockSpec(memory_space=pl.ANY),
        out_shape=jax.ShapeDtypeStruct((n_rows * ROW_TILE, LANES), jnp.float32),
        scratch_shapes=[pltpu.SemaphoreType.DMA],
        compiler_params=_cparams("arbitrary"),
        name="moe_dispatch",
    )(pos, x1r)


def _expert_kernel(tile_ref, exp_ref, lo_ref, hi_ref, nit_ref, xs_ref, wg_ref, wu_ref, wd_ref, ys_ref,
                   wgb_ref, wub_ref, wdb_ref, *, tm):
    f32 = jnp.float32
    w = pl.program_id(0)
    prev = jnp.maximum(w - 1, 0)

    @pl.when(jnp.logical_or(w == 0, exp_ref[prev] != exp_ref[w]))
    def _():
        wgb_ref[...] = wg_ref[0].astype(jnp.bfloat16)
        wub_ref[...] = wu_ref[0].astype(jnp.bfloat16)
        wdb_ref[...] = wd_ref[0].astype(jnp.bfloat16)

    @pl.when(w < nit_ref[0])
    def _():
        xb = _load_tile_rows(xs_ref, tm).astype(jnp.bfloat16)
        g = jnp.dot(xb, wgb_ref[...], preferred_element_type=f32)
        u = jnp.dot(xb, wub_ref[...], preferred_element_type=f32)
        hmid = (g * jax.nn.sigmoid(g) * u).astype(jnp.bfloat16)
        y = jnp.dot(hmid, wdb_ref[...], preferred_element_type=f32)
        row = tile_ref[w] * tm + lax.broadcasted_iota(jnp.int32, (tm, 1), 0)
        mine = (row >= lo_ref[w]) & (row < hi_ref[w])
        first_visit = jnp.logical_or(w == 0, tile_ref[prev] != tile_ref[w])

        @pl.when(first_visit)
        def _():
            _store_tile_rows(ys_ref, jnp.where(mine, y, 0.0))

        @pl.when(jnp.logical_not(first_visit))
        def _():
            _store_tile_rows(ys_ref, jnp.where(mine, y, _load_tile_rows(ys_ref, tm)))


def _moe_experts(items, xs, wg, wu, wd, tm):
    n_items = items[0].shape[0]
    grid_spec = pltpu.PrefetchScalarGridSpec(
        num_scalar_prefetch=5,
        grid=(n_items,),
        in_specs=[
            pl.BlockSpec((tm * ROW_TILE, LANES), lambda w, t, e, lo, hi, n: (t[w], 0)),
            pl.BlockSpec((1, D_MODEL, D_EXPERT), lambda w, t, e, lo, hi, n: (e[w], 0, 0)),
            pl.BlockSpec((1, D_MODEL, D_EXPERT), lambda w, t, e, lo, hi, n: (e[w], 0, 0)),
            pl.BlockSpec((1, D_EXPERT, D_MODEL), lambda w, t, e, lo, hi, n: (e[w], 0, 0)),
        ],
        out_specs=pl.BlockSpec((tm * ROW_TILE, LANES), lambda w, t, e, lo, hi, n: (t[w], 0)),
        scratch_shapes=[pltpu.VMEM((D_MODEL, D_EXPERT), jnp.bfloat16),
                        pltpu.VMEM((D_MODEL, D_EXPERT), jnp.bfloat16),
                        pltpu.VMEM((D_EXPERT, D_MODEL), jnp.bfloat16)],
    )
    return pl.pallas_call(
        functools.partial(_expert_kernel, tm=tm),
        grid_spec=grid_spec,
        out_shape=jax.ShapeDtypeStruct(xs.shape, jnp.float32),
        compiler_params=_cparams("arbitrary"),
        name="moe_experts",
    )(*items, xs, wg, wu, wd)


def _combine_kernel(pos_ref, rt_ref, x_ref, ys_hbm, g2_ref, b2_ref, out_ref, buf_a, buf_b, sem, *, tm):
    bufs = (buf_a, buf_b)
    th = tm // 2
    half = lambda ref, h: ref.at[pl.ds(h * th * ROW_TILE, th * ROW_TILE)]

    def issue_half(h):
        def issue(k, carry):
            for u in range(DMA_UNROLL):
                r = h * th + k * DMA_UNROLL + u
                for slot in range(2):
                    p = pos_ref[0, slot, r]
                    src = ys_hbm.at[pl.ds(pl.multiple_of(p * ROW_TILE, ROW_TILE), ROW_TILE)]
                    dst = bufs[slot].at[pl.ds(pl.multiple_of(r * ROW_TILE, ROW_TILE), ROW_TILE)]
                    pltpu.make_async_copy(src, dst, sem.at[h]).start(priority=slot)
            return carry
        lax.fori_loop(0, th // DMA_UNROLL, issue, 0)

    def finish_half(h):
        for slot in range(2):
            pltpu.make_async_copy(ys_hbm.at[pl.ds(0, th * ROW_TILE)], half(bufs[slot], h), sem.at[h]).wait()
        rt = rt_ref[h * th:(h + 1) * th, :]
        ffn = (rt[:, 2:3] * _load_tile_rows(half(buf_a, h), th)
               + rt[:, 3:4] * _load_tile_rows(half(buf_b, h), th))
        out_ref[h * th:(h + 1) * th, :] = _layer_norm(
            ALPHA * _load_tile_rows(half(x_ref, h), th) + ffn, g2_ref[...], b2_ref[...])

    issue_half(0)
    issue_half(1)
    finish_half(0)
    finish_half(1)


def _moe_combine(pos, rt, x1r, ys, g2, b2, tm):
    n = rt.shape[0]
    return pl.pallas_call(
        functools.partial(_combine_kernel, tm=tm),
        grid=(n // tm,),
        in_specs=[pl.BlockSpec((1, 8, tm), lambda i: (i, 0, 0), memory_space=pltpu.SMEM),
                  pl.BlockSpec((tm, LANES), lambda i: (i, 0)),
                  pl.BlockSpec((tm * ROW_TILE, LANES), lambda i: (i, 0)),
                  pl.BlockSpec(memory_space=pl.ANY),
                  pl.BlockSpec((1, D_MODEL), lambda i: (0, 0)),
                  pl.BlockSpec((1, D_MODEL), lambda i: (0, 0))],
        out_specs=pl.BlockSpec((tm, D_MODEL), lambda i: (i, 0)),
        out_shape=jax.ShapeDtypeStruct((n, D_MODEL), jnp.float32),
        scratch_shapes=[pltpu.VMEM((tm * ROW_TILE, LANES), jnp.float32),
                        pltpu.VMEM((tm * ROW_TILE, LANES), jnp.float32),
                        pltpu.SemaphoreType.DMA((2,))],
        compiler_params=_cparams("arbitrary"),
        name="moe_combine",
    )(pos, rt, x1r, ys, g2, b2)


MOE_TM = 512


def _pos_kernel(start_ref, rt_ref, pos_ref):
    rt = rt_ref[...]
    lane = lax.broadcasted_iota(jnp.int32, rt.shape, 1)
    lane_f = lane.astype(jnp.float32)
    start = start_ref[...]
    tile = jnp.zeros(rt.shape, jnp.float32)
    for slot in range(2):
        first_row = jnp.sum(jnp.where(lane_f == rt[:, slot:slot + 1], start, 0.0), axis=1, keepdims=True)
        tile = jnp.where(lane == slot, first_row + rt[:, 4 + slot:5 + slot], tile)
    tm = pos_ref.shape[2]
    for r in range(rt.shape[0] // BLOCK):
        t = tile[r * BLOCK:(r + 1) * BLOCK].T
        c = (r * BLOCK) % tm
        pos_ref[(r * BLOCK) // tm, :, c:c + BLOCK] = t[:8].astype(jnp.int32)


def _moe_pos(start, rt, tm):
    n = rt.shape[0]
    per_step = _pick_tile(n // tm, 4)
    return pl.pallas_call(
        _pos_kernel,
        grid=(n // (tm * per_step),),
        in_specs=[pl.BlockSpec((1, LANES), lambda i: (0, 0)),
                  pl.BlockSpec((tm * per_step, LANES), lambda i: (i, 0))],
        out_specs=pl.BlockSpec((per_step, 8, tm), lambda i: (i, 0, 0)),
        out_shape=jax.ShapeDtypeStruct((n // tm, 8, tm), jnp.int32),
        compiler_params=_cparams("arbitrary"),
        name="moe_pos",
    )(start, rt)


def _moe_plan(counts, n_tiles):
    i32 = jnp.int32
    cnt = counts[0, :N_EXPERTS].astype(i32)
    end = jnp.cumsum(cnt)
    start = end - cnt
    start_lanes = jnp.pad(start.astype(jnp.float32), (0, LANES - N_EXPERTS)).reshape(1, LANES)
    first_t = start // MOE_TM
    items_e = jnp.where(cnt > 0, (end - 1) // MOE_TM - first_t + 1, 0)
    item_end = jnp.cumsum(items_e)
    n_items = item_end[-1]
    w = jnp.minimum(jnp.arange(n_tiles + N_EXPERTS - 1, dtype=i32), n_items - 1)
    e_w = jnp.minimum(jnp.sum((item_end[None, :] <= w[:, None]).astype(i32), axis=1), N_EXPERTS - 1)
    t_w = first_t[e_w] + (w - (item_end - items_e)[e_w])
    lo_w = jnp.maximum(start[e_w], t_w * MOE_TM)
    hi_w = jnp.minimum(end[e_w], (t_w + 1) * MOE_TM)
    return start_lanes, (t_w.astype(i32), e_w, lo_w.astype(i32), hi_w.astype(i32),
                         n_items.reshape(1).astype(i32))


def _pick_tile(n, pref):
    t = min(pref, n)
    while n % t:
        t //= 2
    return t


def kernel(x, w_in, kv_norm_g, w_uv, w_branch_a, sinks, w_branch_b, w_out, rel_bias, ln1_g, ln1_b,
           w_group, b_group, w_router, b_router, w_gate, w_up, w_down, ln2_g, ln2_b):
    bsz, seq, d = x.shape
    assert d == D_MODEL and seq % BLOCK == 0 and w_in.shape[0] == DEPTH == 1
    n = bsz * seq
    n_blk = seq // BLOCK
    bf, f32 = jnp.bfloat16, jnp.float32
    x2 = x.reshape(n, d)

    r = np.arange(BLOCK, dtype=np.int32)
    d_prev = r[None, :] + BLOCK - r[:, None]
    d_cur = r[None, :] - r[:, None]
    bkt_a = jnp.asarray(_t5_bucket_np(np.stack([d_prev, d_cur])))
    s2 = np.arange(2 * BLOCK, dtype=np.int32)
    bkt_b = jnp.asarray(_t5_bucket_np(r[None, :] + BLOCK - s2[:, None])[None])
    assert int(_t5_bucket_np(np.arange(BLOCK + 1, max(seq, BLOCK + 2))).min()) == N_BUCKETS - 1
    tab_t = rel_bias.astype(f32).T
    bias_a = _bias_tiles(tab_t, bkt_a, 0, A_HEADS, True, minus_far=True)
    bias_b = _bias_tiles(tab_t, bkt_b, A_HEADS, B_HEADS, True)[0]

    w0 = w_in[0]
    w_pack = jnp.concatenate(
        [w0[:, :_RAW_KW_END], jnp.zeros((d, _C_QB - _RAW_KW_END), w0.dtype), w0[:, _RAW_KW_END:]],
        axis=1).astype(bf)
    (q_blocks, ckv, ckv_tx, qidx_blocks, kidx, widx, qb_blocks, kb, vbx, ga, gb) = _input_projection(
        x2, w_pack, kv_norm_g[0].reshape(1, A_LAT).astype(f32), _pick_tile(seq, 512), seq)

    ckv3 = ckv.reshape(bsz, seq, A_LAT)
    widx_l = jnp.swapaxes(widx.reshape(bsz * n_blk, BLOCK, IDX_HEADS), 1, 2).reshape(
        bsz * n_blk, 1, IDX_HEADS * BLOCK)
    ya = _dsa_attention(qidx_blocks.reshape(bsz * n_blk, IDX_HEADS * BLOCK, IDX_DIM),
                        kidx.reshape(bsz, seq, IDX_DIM), widx_l,
                        q_blocks.reshape(bsz * n_blk, A_HEADS * BLOCK, A_LAT),
                        ckv3, ckv_tx, bias_a, jnp.swapaxes(w_uv[0], 1, 2).astype(bf), bsz, seq)

    yb = _swa_attention(sinks[0].astype(f32), qb_blocks, kb, vbx, bias_b, bsz, seq)

    wr_hi, wr_lo = _route_weights(w_group[0].astype(f32), w_router[0].astype(f32))
    b_route = jnp.concatenate(
        [b_group[0], b_router[0], jnp.zeros((LANES - N_GROUPS - N_EXPERTS,), f32)]).reshape(1, LANES).astype(f32)
    x1r, rt, counts = _merge_route(
        ya, yb, ga, gb, x2, w_branch_a[0].astype(bf), w_branch_b[0].astype(bf), w_out[0].astype(bf),
        ln1_g[0].reshape(1, d).astype(f32), ln1_b[0].reshape(1, d).astype(f32), wr_hi, wr_lo, b_route,
        _pick_tile(n, 512))

    assert (2 * n) % MOE_TM == 0
    start_lanes, items = _moe_plan(counts, 2 * n // MOE_TM)
    tm_io = _pick_tile(n, 512)
    pos_blocks = _moe_pos(start_lanes, rt, tm_io)
    xs = _moe_dispatch(pos_blocks, x1r, 2 * n, tm_io)
    ys = _moe_experts(items, xs, w_gate[0], w_up[0], w_down[0], MOE_TM)
    out = _moe_combine(pos_blocks, rt, x1r, ys, ln2_g[0].reshape(1, d).astype(f32),
                       ln2_b[0].reshape(1, d).astype(f32), tm_io)
    return out.reshape(bsz, seq, d)
```

```python
import functools
import math

import numpy as np
import jax
import jax.numpy as jnp
from jax import lax
from jax.experimental import pallas as pl
from jax.experimental.pallas import tpu as pltpu

D_MODEL = 1024
A_HEADS = 8
A_LAT = 128
A_HEAD_DIM = 64
IDX_HEADS = 8
IDX_DIM = 64
TOPK_MAX = 256
B_HEADS = 8
B_KV_HEADS = 2
B_HEAD_DIM = 64
WINDOW = 128
BLOCK = 128
N_BUCKETS = 32
MAX_DISTANCE = 128
N_GROUPS = 4
EXPERTS_PER_GROUP = 8
N_EXPERTS = 32
D_EXPERT = 256
DEPTH = 1
ALPHA = (2 * DEPTH) ** 0.25
LN_EPS = 1e-5
RMS_EPS = 1e-6

LANES = 128
INT_MIN = -(2 ** 31)
VMEM_LIMIT = 48 * 1024 * 1024
PROJ_VMEM_LIMIT = 56 * 1024 * 1024

_NT = (((1,), (1,)), ((), ()))


def _cparams(*sem):
    return pltpu.CompilerParams(dimension_semantics=sem, vmem_limit_bytes=VMEM_LIMIT)


def _t5_bucket_np(dist):
    f32 = np.float32
    n = np.maximum(dist, 0)
    max_exact = N_BUCKETS // 2
    nf = np.maximum(n, 1).astype(f32)
    large = max_exact + (np.log(nf / f32(max_exact)) / f32(math.log(MAX_DISTANCE / max_exact))
                         * f32(N_BUCKETS - max_exact)).astype(np.int32)
    large = np.minimum(large, N_BUCKETS - 1)
    return np.where(n < max_exact, n, large).astype(np.int32)


def _bias_kernel(tab_ref, bkt_ref, out_ref, *, head0, minus_far):
    h = pl.program_id(0) + head0
    bkt = bkt_ref[...]
    far = tab_ref[h, N_BUCKETS - 1]
    acc = jnp.full(bkt.shape, far, jnp.float32)
    for b in range(N_BUCKETS - 1):
        acc = jnp.where(bkt == b, tab_ref[h, b], acc)
    out_ref[...] = acc - far if minus_far else acc


def _bias_tiles(tab_t, bkt, head0, n_heads, lane_major, minus_far=False):
    g, r, c = bkt.shape
    if lane_major:
        out_shape = jax.ShapeDtypeStruct((g, r, n_heads * c), jnp.float32)
        out_spec = pl.BlockSpec((g, r, c), lambda h: (0, 0, h))
    else:
        out_shape = jax.ShapeDtypeStruct((n_heads * g, r, c), jnp.float32)
        out_spec = pl.BlockSpec((g, r, c), lambda h: (h, 0, 0))
    return pl.pallas_call(
        functools.partial(_bias_kernel, head0=head0, minus_far=minus_far),
        grid=(n_heads,),
        in_specs=[pl.BlockSpec(memory_space=pltpu.SMEM),
                  pl.BlockSpec((g, r, c), lambda h: (0, 0, 0))],
        out_specs=out_spec,
        out_shape=out_shape,
        compiler_params=_cparams("arbitrary"),
        name="bias_tiles",
    )(tab_t, bkt)


_C_QLAT = 0
_C_CKV = _C_QLAT + A_HEADS * A_LAT
_C_QIDX = _C_CKV + A_LAT
_C_KW = _C_QIDX + IDX_HEADS * IDX_DIM
_C_QB = _C_KW + LANES
_C_KB = _C_QB + B_HEADS * B_HEAD_DIM
_C_VB = _C_KB + B_KV_HEADS * B_HEAD_DIM
_C_GA = _C_VB + B_KV_HEADS * B_HEAD_DIM
_C_GB = _C_GA + D_MODEL
_C_END = _C_GB + D_MODEL
_RAW_KW_END = _C_KW + IDX_DIM + IDX_HEADS


def _proj_kernel(x_ref, wraw_ref, g_ref, q_ref, ckv_ref, ckvt_ref, qidx_ref, kidx_ref, widx_ref,
                 qb_ref, kb_ref, vb_ref, ga_ref, gb_ref, w_ref):
    xb = x_ref[...].astype(jnp.bfloat16)
    tm = xb.shape[0]

    @pl.when(pl.program_id(0) == 0)
    def _():
        w_ref[:, :_C_KW] = wraw_ref[:, :_C_KW]
        w_ref[:, _C_KW:_C_QB] = jnp.concatenate(
            [wraw_ref[:, _C_KW:_RAW_KW_END], jnp.zeros((D_MODEL, _C_QB - _RAW_KW_END), jnp.bfloat16)], axis=1)
        for dst in range(_C_QB, _C_END, LANES):
            src = dst - (_C_QB - _RAW_KW_END)
            w_ref[:, dst:dst + LANES] = wraw_ref[:, src:src + LANES]

    def seg(lo, hi):
        return jnp.dot(xb, w_ref[:, lo:hi], preferred_element_type=jnp.float32)

    for h in range(A_HEADS):
        qh = (seg(_C_QLAT + h * A_LAT, _C_QLAT + (h + 1) * A_LAT) * (A_LAT ** -0.5)).astype(jnp.bfloat16)
        for r in range(tm // BLOCK):
            q_ref[r, h] = qh[r * BLOCK:(r + 1) * BLOCK]
    c = seg(_C_CKV, _C_QIDX)
    ms = jnp.mean(c * c, axis=-1, keepdims=True)
    cn = c * lax.rsqrt(ms + RMS_EPS) * g_ref[...]
    ckv_ref[...] = cn.astype(jnp.bfloat16)
    for r in range(tm // BLOCK):
        ckvt_ref[0, :A_LAT, r * BLOCK:(r + 1) * BLOCK] = cn[r * BLOCK:(r + 1) * BLOCK].T.astype(jnp.bfloat16)
    ckvt_ref[0, A_LAT:, :] = jnp.ones((8, tm), jnp.bfloat16)
    qif = seg(_C_QIDX, _C_KW).astype(jnp.bfloat16)
    for h in range(IDX_HEADS):
        for r in range(tm // BLOCK):
            qidx_ref[r, h] = qif[r * BLOCK:(r + 1) * BLOCK, h * IDX_DIM:(h + 1) * IDX_DIM]
    kw = seg(_C_KW, _C_QB)
    kidx_ref[...] = kw[:, :IDX_DIM].astype(jnp.bfloat16)
    widx_ref[...] = kw[:, IDX_DIM:IDX_DIM + IDX_HEADS]
    qbf = seg(_C_QB, _C_KB).astype(jnp.bfloat16)
    for h in range(B_HEADS):
        for r in range(tm // BLOCK):
            qb_ref[r, h] = qbf[r * BLOCK:(r + 1) * BLOCK, h * B_HEAD_DIM:(h + 1) * B_HEAD_DIM]
    kbf = seg(_C_KB, _C_VB).astype(jnp.bfloat16)
    vbf = seg(_C_VB, _C_GA)
    for g in range(B_KV_HEADS):
        kb_ref[g] = kbf[:, g * B_HEAD_DIM:(g + 1) * B_HEAD_DIM]
        vb_ref[g, 0, B_HEAD_DIM:, :] = jnp.ones((SWA_ONES, tm), jnp.bfloat16)
    for r in range(tm // BLOCK):
        vt = vbf[r * BLOCK:(r + 1) * BLOCK].T.astype(jnp.bfloat16)
        for g in range(B_KV_HEADS):
            vb_ref[g, 0, :B_HEAD_DIM, r * BLOCK:(r + 1) * BLOCK] = vt[g * B_HEAD_DIM:(g + 1) * B_HEAD_DIM]
    ga_ref[...] = jax.nn.sigmoid(seg(_C_GA, _C_GB))
    gb_ref[...] = jax.nn.sigmoid(seg(_C_GB, _C_END))


def _input_projection(x2, w_raw, kv_g, tm, seq):
    n = x2.shape[0]
    assert seq % tm == 0 and w_raw.shape == (D_MODEL, _C_END - (_C_QB - _RAW_KW_END))
    bf, f32 = jnp.bfloat16, jnp.float32
    row = lambda w: pl.BlockSpec((tm, w), lambda i: (i, 0))
    out_shape = (
        jax.ShapeDtypeStruct((n // BLOCK, A_HEADS, BLOCK, A_LAT), bf),
        jax.ShapeDtypeStruct((n, A_LAT), bf),
        jax.ShapeDtypeStruct((n // seq, A_LAT + 8, seq), bf),
        jax.ShapeDtypeStruct((n // BLOCK, IDX_HEADS, BLOCK, IDX_DIM), bf),
        jax.ShapeDtypeStruct((n, IDX_DIM), bf),
        jax.ShapeDtypeStruct((n, IDX_HEADS), f32),
        jax.ShapeDtypeStruct((n // BLOCK, B_HEADS, BLOCK, B_HEAD_DIM), bf),
        jax.ShapeDtypeStruct((B_KV_HEADS, n, B_HEAD_DIM), bf),
        jax.ShapeDtypeStruct((B_KV_HEADS, n // seq, B_HEAD_DIM + SWA_ONES, seq), bf),
        jax.ShapeDtypeStruct((n, D_MODEL), f32),
        jax.ShapeDtypeStruct((n, D_MODEL), f32),
    )
    out_specs = (
        pl.BlockSpec((tm // BLOCK, A_HEADS, BLOCK, A_LAT), lambda i: (i, 0, 0, 0)),
        row(A_LAT),
        pl.BlockSpec((1, A_LAT + 8, tm), lambda i: (i // (seq // tm), 0, i % (seq // tm))),
        pl.BlockSpec((tm // BLOCK, IDX_HEADS, BLOCK, IDX_DIM), lambda i: (i, 0, 0, 0)),
        row(IDX_DIM), row(IDX_HEADS),
        pl.BlockSpec((tm // BLOCK, B_HEADS, BLOCK, B_HEAD_DIM), lambda i: (i, 0, 0, 0)),
        pl.BlockSpec((B_KV_HEADS, tm, B_HEAD_DIM), lambda i: (0, i, 0)),
        pl.BlockSpec((B_KV_HEADS, 1, B_HEAD_DIM + SWA_ONES, tm),
                     lambda i: (0, i // (seq // tm), 0, i % (seq // tm))),
        row(D_MODEL), row(D_MODEL),
    )
    return pl.pallas_call(
        _proj_kernel,
        grid=(n // tm,),
        in_specs=[row(D_MODEL),
                  pl.BlockSpec(w_raw.shape, lambda i: (0, 0)),
                  pl.BlockSpec((1, A_LAT), lambda i: (0, 0))],
        out_specs=out_specs,
        out_shape=out_shape,
        scratch_shapes=[pltpu.VMEM((D_MODEL, _C_END), bf)],
        compiler_params=pltpu.CompilerParams(dimension_semantics=("arbitrary",),
                                             vmem_limit_bytes=PROJ_VMEM_LIMIT),
        name="input_projection",
    )(x2, w_raw, kv_g)


DSA_UNROLL = 8


def _bit_transpose32(words):
    w = list(words)
    j, m = 16, 0x0000FFFF
    while j:
        mask = jnp.int32(m - (1 << 32) if m >= (1 << 31) else m)
        k = 0
        while k < 32:
            t = (w[k] ^ lax.shift_right_logical(w[k + j], jnp.int32(j))) & mask
            w[k] = w[k] ^ t
            w[k + j] = w[k + j] ^ lax.shift_left(t, jnp.int32(j))
            k = (k + j + 1) & ~j
        j >>= 1
        m = (m ^ (m << j)) & 0xFFFFFFFF
    return w


def _dsa_kernel(qidx_ref, kidx_ref, widx_ref, q_ref, ckv_ref, ckvt_ref, bias_ref, wuvt_ref,
                tri_ref, ya_ref, sc_ref, neg_ref, lg_ref, o_ref, planes_ref, sel_ref, *, k_sel,
                search_chunk):
    i = pl.program_id(1)
    f32 = jnp.float32
    s_loc = lax.broadcasted_iota(jnp.int32, (BLOCK, BLOCK), 0)
    t_loc = lax.broadcasted_iota(jnp.int32, (BLOCK, BLOCK), 1)
    causal_diag = s_loc <= t_loc
    idx_scale = IDX_DIM ** -0.5
    w_scale = IDX_HEADS ** -0.5

    def rows(j):
        return pl.ds(pl.multiple_of(j * BLOCK, BLOCK), BLOCK)

    sc_ref[...] = jnp.full(sc_ref.shape, -jnp.inf, f32)
    wts = (widx_ref[0] * w_scale) * idx_scale
    qi = qidx_ref[0]

    def score_keys(j):
        s = lax.dot_general(kidx_ref[0, rows(j), :], qi, _NT, preferred_element_type=f32)
        r = jnp.maximum(s, 0.0) * wts
        acc = r[:, :BLOCK]
        for h in range(1, IDX_HEADS):
            acc = acc + r[:, h * BLOCK:(h + 1) * BLOCK]
        return acc

    def fold_blocks(lo, hi, fn, init, combine):
        n = hi - lo
        n_grp = lax.shift_right_logical(n, DSA_UNROLL.bit_length() - 1)

        def piece(acc, j, size):
            vals = [fn(j + u) for u in range(size)]
            while len(vals) > 1:
                vals = [combine(vals[a], vals[a + 1]) for a in range(0, len(vals), 2)]
            return combine(acc, vals[0])

        acc = lax.fori_loop(0, n_grp, lambda k, a: piece(a, lo + DSA_UNROLL * k, DSA_UNROLL), init)
        j = lo + DSA_UNROLL * n_grp
        left = hi - j
        size = DSA_UNROLL // 2
        while size:
            acc = lax.cond((left & size) != 0, functools.partial(piece, acc, j, size), lambda a=acc: a)
            j = j + (left & size)
            size //= 2
        return acc

    def far_scores(j):
        sc_ref[rows(j), :] = score_keys(j)
        return jnp.int32(0)

    fold_blocks(0, i, far_scores, jnp.int32(0), lambda a, b: a)
    sc_ref[rows(i), :] = jnp.where(causal_diag, score_keys(i), -jnp.inf)

    n_chunks = ((i + 1) * BLOCK + search_chunk - 1) // search_chunk
    n_acc = 64

    def count(cmp, cand):
        def body(c, cnt):
            base = pl.multiple_of(c * search_chunk, search_chunk)
            for r in range(search_chunk // n_acc):
                cnt = cnt + cmp(sc_ref[pl.ds(base + r * n_acc, n_acc), :], cand).astype(f32)
            return cnt
        cnt = lax.fori_loop(0, n_chunks, body, jnp.zeros((n_acc, BLOCK), f32))
        return jnp.sum(cnt, axis=0, keepdims=True)

    def ordered_to_float(u):
        k = u ^ INT_MIN
        return lax.bitcast_convert_type(k ^ ((k >> 31) & 0x7FFFFFFF), f32)

    n_grp_keys = 32 * 8
    n_groups = ((i + 1) * BLOCK + n_grp_keys - 1) // n_grp_keys
    max_groups = planes_ref.shape[0]

    def build_planes(g, carry):
        words = []
        for v in range(32):
            s = sc_ref[pl.ds(pl.multiple_of(g * n_grp_keys + v * 8, 8), 8), :]
            bits = lax.bitcast_convert_type(s, jnp.int32)
            words.append(bits ^ ((bits >> 31) & 0x7FFFFFFF) ^ INT_MIN)
        planes = _bit_transpose32(words)
        for p in range(32):
            planes_ref[g, p] = planes[p]
        return carry

    lax.fori_loop(0, n_groups, build_planes, 0)

    def popcount_rows(words):
        cnt = lax.population_count(words[0])
        for wd in words[1:]:
            cnt = cnt + lax.population_count(wd)
        return jnp.sum(cnt.astype(f32), axis=0, keepdims=True)

    def select_two_bits(n, carry):
        alive, k_left, thr_u = carry
        grp = [jnp.minimum(g, n_groups - 1) for g in range(max_groups)]
        hi1 = [alive[g] & planes_ref[grp[g], 2 * n] for g in range(max_groups)]
        hi0 = [alive[g] ^ hi1[g] for g in range(max_groups)]
        lo = [planes_ref[grp[g], 2 * n + 1] for g in range(max_groups)]
        c11 = [hi1[g] & lo[g] for g in range(max_groups)]
        c10 = [hi1[g] ^ c11[g] for g in range(max_groups)]
        c01 = [hi0[g] & lo[g] for g in range(max_groups)]
        c00 = [hi0[g] ^ c01[g] for g in range(max_groups)]
        s1 = popcount_rows(c11)
        s2 = s1 + popcount_rows(c10)
        s3 = s2 + popcount_rows(c01)
        t11 = s1 >= k_left
        t10 = jnp.logical_and(jnp.logical_not(t11), s2 >= k_left)
        t01 = jnp.logical_and(jnp.logical_not(t11 | t10), s3 >= k_left)
        w11, w10, w01 = (jnp.broadcast_to(t, (8, BLOCK)) for t in (t11, t10, t01))
        alive = [jnp.where(w11, c11[g], jnp.where(w10, c10[g], jnp.where(w01, c01[g], c00[g])))
                 for g in range(max_groups)]
        k_left = k_left - jnp.where(t11, 0.0, jnp.where(t10, s1, jnp.where(t01, s2, s3)))
        b_hi = 31 - 2 * n
        bits = (jnp.where(t11 | t10, lax.shift_left(jnp.int32(1), b_hi), 0)
                | jnp.where(t11 | t01, lax.shift_left(jnp.int32(1), b_hi - 1), 0))
        return alive, k_left, thr_u | bits

    alive0 = [jnp.where(g < n_groups, jnp.full((8, BLOCK), -1, jnp.int32), 0) for g in range(max_groups)]
    _, _, thr_u = lax.fori_loop(
        0, 16, select_two_bits, (alive0, jnp.full((1, BLOCK), k_sel, f32), jnp.zeros((1, BLOCK), jnp.int32)))
    thr = ordered_to_float(thr_u)
    c_ge = count(jnp.greater_equal, thr)
    c_gt = count(jnp.greater, thr)
    sel_ref[0:1, :] = thr
    sel_ref[1:2, :] = c_ge
    sel_ref[2:3, :] = c_gt
    consistent = jnp.min(((c_ge >= k_sel) & (c_gt < k_sel)).astype(f32)) > 0.0

    @pl.when(jnp.logical_not(consistent))
    def _():
        def search_bit(b, prefix_u):
            cand_u = prefix_u | lax.shift_left(jnp.int32(1), 31 - b)
            cnt = count(jnp.greater_equal, ordered_to_float(cand_u))
            return jnp.where(cnt >= k_sel, cand_u, prefix_u)

        t = ordered_to_float(lax.fori_loop(0, 32, search_bit, jnp.zeros((1, BLOCK), jnp.int32)))
        t = jnp.where(t != t, -jnp.inf, t)
        sel_ref[0:1, :] = t
        sel_ref[1:2, :] = count(jnp.greater_equal, t)
        sel_ref[2:3, :] = count(jnp.greater, t)

    thr = sel_ref[0:1, :]
    c_ge = sel_ref[1:2, :]
    c_gt = sel_ref[2:3, :]
    n_tie = k_sel - c_gt
    any_tie = jnp.max(((c_ge > k_sel) & (thr > -jnp.inf)).astype(f32)) > 0.0

    @pl.when(jnp.logical_not(any_tie))
    def _():
        def body(j, carry):
            neg_ref[rows(j), :] = jnp.where(sc_ref[rows(j), :] >= thr, 0.0, -jnp.inf)
            return carry
        lax.fori_loop(0, i, body, 0)
        neg_ref[rows(i), :] = jnp.where((sc_ref[rows(i), :] >= thr) & causal_diag, 0.0, -jnp.inf)

    @pl.when(any_tie)
    def _():
        tri = tri_ref[...]

        def block(j, tie_seen):
            kc = sc_ref[rows(j), :]
            eq = kc == thr
            tie_rank = jnp.dot(tri, eq.astype(jnp.bfloat16), preferred_element_type=f32) + tie_seen
            sel = (kc > thr) | (eq & (tie_rank <= n_tie))
            return sel, tie_seen + jnp.sum(eq.astype(f32), axis=0, keepdims=True)

        def body(j, tie_seen):
            sel, tie_seen = block(j, tie_seen)
            neg_ref[rows(j), :] = jnp.where(sel, 0.0, -jnp.inf)
            return tie_seen

        tie_seen = lax.fori_loop(0, i, body, jnp.zeros((1, BLOCK), f32))
        sel, _ = block(i, tie_seen)
        neg_ref[rows(i), :] = jnp.where(sel & causal_diag, 0.0, -jnp.inf)

    q_all = q_ref[0]

    def logits(j):
        lg = lax.dot_general(ckv_ref[0, rows(j), :], q_all, _NT, preferred_element_type=f32)
        return lg + jnp.concatenate([neg_ref[rows(j), :]] * A_HEADS, axis=1)

    def far_logits(j):
        lg = logits(j)
        lg_ref[rows(j), :] = lg
        return jnp.max(lg, axis=0, keepdims=True)

    def near_logits(j):
        lg = logits(j) + bias_ref[j - i + 1]
        lg_ref[rows(j), :] = lg
        return jnp.max(lg, axis=0, keepdims=True)

    n_far = jnp.maximum(i - 1, 0)
    m_run = fold_blocks(0, n_far, far_logits, jnp.full((1, A_HEADS * BLOCK), -jnp.inf, f32), jnp.maximum)
    m_run = fold_blocks(n_far, i + 1, near_logits, m_run, jnp.maximum)

    o_ref[...] = jnp.zeros_like(o_ref)

    def pv_rows(r):
        p = jnp.exp(lg_ref[r, :] - m_run).astype(jnp.bfloat16)
        o_ref[...] += jnp.dot(ckvt_ref[0, :, r], p, preferred_element_type=f32)

    def pv_group(k, carry):
        span = DSA_UNROLL * BLOCK
        pv_rows(pl.ds(pl.multiple_of(k * span, span), span))
        return carry

    n_grp = lax.shift_right_logical(i + 1, DSA_UNROLL.bit_length() - 1)
    lax.fori_loop(0, n_grp, pv_group, 0)
    j = DSA_UNROLL * n_grp
    left = i + 1 - j
    size = DSA_UNROLL // 2
    while size:
        @pl.when((left & size) != 0)
        def _(j=j, size=size):
            pv_rows(pl.ds(pl.multiple_of(j * BLOCK, size * BLOCK), size * BLOCK))
        j = j + (left & size)
        size //= 2

    inv_l = 1.0 / o_ref[A_LAT:A_LAT + 1, :]
    for pair in range(A_HEADS // 2):
        halves = []
        for h in (2 * pair, 2 * pair + 1):
            cols = slice(h * BLOCK, (h + 1) * BLOCK)
            y_t = jnp.dot(wuvt_ref[h], o_ref[:A_LAT, cols].astype(jnp.bfloat16),
                          preferred_element_type=f32)
            halves.append(y_t * inv_l[:, cols])
        two = jnp.concatenate(halves, axis=0)
        ya_ref[:, 2 * pair * A_HEAD_DIM:(2 * pair + 2) * A_HEAD_DIM] = two.T.astype(ya_ref.dtype)


def _dsa_attention(qidx_blocks, kidx, widx_l, q_blocks, ckv, ckv_tx, bias_near, wuv, bsz, seq):
    n_blk = seq // BLOCK
    k_sel = min(TOPK_MAX, seq // 4)
    search_chunk = min(512, seq)
    tri = jnp.asarray(np.tril(np.ones((BLOCK, BLOCK), np.float32)), jnp.bfloat16)
    kern = functools.partial(_dsa_kernel, k_sel=float(k_sel), search_chunk=search_chunk)
    blk = lambda b, i: (b * n_blk + i, 0, 0)
    return pl.pallas_call(
        kern,
        grid=(bsz, n_blk),
        in_specs=[
            pl.BlockSpec((1, IDX_HEADS * BLOCK, IDX_DIM), blk),
            pl.BlockSpec((1, seq, IDX_DIM), lambda b, i: (b, 0, 0)),
            pl.BlockSpec((1, 1, IDX_HEADS * BLOCK), blk),
            pl.BlockSpec((1, A_HEADS * BLOCK, A_LAT), blk),
            pl.BlockSpec((1, seq, A_LAT), lambda b, i: (b, 0, 0)),
            pl.BlockSpec((1, A_LAT + 8, seq), lambda b, i: (b, 0, 0)),
            pl.BlockSpec((2, BLOCK, A_HEADS * BLOCK), lambda b, i: (0, 0, 0)),
            pl.BlockSpec((A_HEADS, A_HEAD_DIM, A_LAT), lambda b, i: (0, 0, 0)),
            pl.BlockSpec((BLOCK, BLOCK), lambda b, i: (0, 0)),
        ],
        out_specs=pl.BlockSpec((BLOCK, A_HEADS * A_HEAD_DIM), lambda b, i: (b * n_blk + i, 0)),
        out_shape=jax.ShapeDtypeStruct((bsz * seq, A_HEADS * A_HEAD_DIM), jnp.bfloat16),
        scratch_shapes=[pltpu.VMEM((seq, BLOCK), jnp.float32),
                        pltpu.VMEM((seq, BLOCK), jnp.float32),
                        pltpu.VMEM((seq, A_HEADS * BLOCK), jnp.float32),
                        pltpu.VMEM((A_LAT + 8, A_HEADS * BLOCK), jnp.float32),
                        pltpu.VMEM((seq // 256, 32, 8, BLOCK), jnp.int32),
                        pltpu.VMEM((8, BLOCK), jnp.float32)],
        compiler_params=_cparams("arbitrary", "arbitrary"),
        name="dsa_attention",
    )(qidx_blocks, kidx, widx_l, q_blocks, ckv, ckv_tx, bias_near, wuv, tri)


SWA_QB = 4
SWA_ONES = 16


def _swa_kernel(sink_ref, q_ref, kp_ref, kc_ref, vp_ref, vc_ref, bias_ref, yb_ref, *, qb):
    i = pl.program_id(1)
    f32 = jnp.float32
    grp = B_HEADS // B_KV_HEADS
    cols = grp * BLOCK
    s_loc = lax.broadcasted_iota(jnp.int32, (BLOCK, BLOCK), 0)
    t_loc = lax.broadcasted_iota(jnp.int32, (BLOCK, BLOCK), 1)
    neg_cur = jnp.concatenate([jnp.where(s_loc <= t_loc, 0.0, -jnp.inf)] * grp, axis=1)
    in_prev = s_loc > t_loc
    head_of_col = lax.broadcasted_iota(jnp.int32, (1, cols), 1) >> (BLOCK.bit_length() - 1)
    scale = B_HEAD_DIM ** -0.5
    for g in range(B_KV_HEADS):
        sink = jnp.zeros((1, cols), f32)
        for hh in range(grp):
            sink = jnp.where(head_of_col == hh, sink_ref[g * grp + hh], sink)
        bias_p = bias_ref[:BLOCK, g * cols:(g + 1) * cols]
        bias_c = bias_ref[BLOCK:, g * cols:(g + 1) * cols]
        for r in range(qb):
            has_prev = jnp.zeros_like(s_loc) + jnp.minimum(i * qb + r, 1)
            neg_prev = jnp.concatenate([jnp.where(in_prev & (has_prev > 0), 0.0, -jnp.inf)] * grp, axis=1)
            q = q_ref[r, g * grp:(g + 1) * grp].reshape(cols, B_HEAD_DIM)
            if r == 0:
                kp, vp = kp_ref[g], vp_ref[g, 0]
            else:
                kp, vp = kc_ref[g, (r - 1) * BLOCK:r * BLOCK], vc_ref[g, 0, :, (r - 1) * BLOCK:r * BLOCK]
            kc, vc = kc_ref[g, r * BLOCK:(r + 1) * BLOCK], vc_ref[g, 0, :, r * BLOCK:(r + 1) * BLOCK]
            lp = lax.dot_general(kp, q, _NT, preferred_element_type=f32) * scale + bias_p + neg_prev
            lc = lax.dot_general(kc, q, _NT, preferred_element_type=f32) * scale + bias_c + neg_cur
            m = jnp.maximum(jnp.max(jnp.maximum(lp, lc), axis=0, keepdims=True), sink)
            pp = jnp.exp(lp - m).astype(jnp.bfloat16)
            pc = jnp.exp(lc - m).astype(jnp.bfloat16)
            ox = (jnp.dot(vp, pp, preferred_element_type=f32)
                  + jnp.dot(vc, pc, preferred_element_type=f32))
            den = ox[B_HEAD_DIM:B_HEAD_DIM + 1, :] + jnp.exp(sink - m)
            o = ox[:B_HEAD_DIM, :] / den
            for pair in range(grp // 2):
                two = jnp.concatenate([o[:, (2 * pair) * BLOCK:(2 * pair + 1) * BLOCK],
                                       o[:, (2 * pair + 1) * BLOCK:(2 * pair + 2) * BLOCK]], axis=0)
                h0 = g * grp + 2 * pair
                yb_ref[r * BLOCK:(r + 1) * BLOCK, h0 * B_HEAD_DIM:(h0 + 2) * B_HEAD_DIM] = (
                    two.T.astype(yb_ref.dtype))


def _swa_attention(sinks, q_blocks, kb, vbt, bias_b, bsz, seq):
    n_blk = seq // BLOCK
    qb = SWA_QB if n_blk % SWA_QB == 0 else 1
    n_step = n_blk // qb
    cur = lambda b, i: (0, b * n_step + i, 0)
    prev = lambda b, i: (0, b * n_blk + jnp.maximum(i * qb - 1, 0), 0)
    vrows = B_HEAD_DIM + SWA_ONES
    return pl.pallas_call(
        functools.partial(_swa_kernel, qb=qb),
        grid=(bsz, n_step),
        in_specs=[
            pl.BlockSpec(memory_space=pltpu.SMEM),
            pl.BlockSpec((qb, B_HEADS, BLOCK, B_HEAD_DIM), lambda b, i: (b * n_step + i, 0, 0, 0)),
            pl.BlockSpec((B_KV_HEADS, BLOCK, B_HEAD_DIM), prev),
            pl.BlockSpec((B_KV_HEADS, qb * BLOCK, B_HEAD_DIM), cur),
            pl.BlockSpec((B_KV_HEADS, 1, vrows, BLOCK), lambda b, i: (0, b, 0, jnp.maximum(i * qb - 1, 0))),
            pl.BlockSpec((B_KV_HEADS, 1, vrows, qb * BLOCK), lambda b, i: (0, b, 0, i)),
            pl.BlockSpec((2 * BLOCK, B_HEADS * BLOCK), lambda b, i: (0, 0)),
        ],
        out_specs=pl.BlockSpec((qb * BLOCK, B_HEADS * B_HEAD_DIM), lambda b, i: (b * n_step + i, 0)),
        out_shape=jax.ShapeDtypeStruct((bsz * seq, B_HEADS * B_HEAD_DIM), jnp.bfloat16),
        compiler_params=_cparams("arbitrary", "arbitrary"),
        name="swa_attention",
    )(sinks, q_blocks, kb, kb, vbt, vbt, bias_b)


def _layer_norm(h, g, b):
    mu = jnp.mean(h, axis=-1, keepdims=True)
    d = h - mu
    var = jnp.mean(d * d, axis=-1, keepdims=True)
    return d * lax.rsqrt(var + LN_EPS) * g + b


def _merge_kernel(ya_ref, yb_ref, ga_ref, gb_ref, x_ref, wa_ref, wb_ref, wo_ref, g1_ref, b1_ref,
                  wrh_ref, wrl_ref, br_ref, tri_ref, x1_ref, rt_ref, cnt_ref, carry_ref):
    f32 = jnp.float32
    pa = jnp.dot(ya_ref[...], wa_ref[...], preferred_element_type=f32)
    pb = jnp.dot(yb_ref[...], wb_ref[...], preferred_element_type=f32)
    merged = ga_ref[...] * pa + gb_ref[...] * pb
    h = ALPHA * x_ref[...] + jnp.dot(merged.astype(jnp.bfloat16), wo_ref[...],
                                     preferred_element_type=f32)
    x1 = _layer_norm(h, g1_ref[...], b1_ref[...])
    _store_tile_rows(x1_ref, x1)

    x_hi = x1.astype(jnp.bfloat16)
    x_lo = (x1 - x_hi.astype(f32)).astype(jnp.bfloat16)
    lg = (jnp.dot(x_hi, wrh_ref[...], preferred_element_type=f32)
          + jnp.dot(x_hi, wrl_ref[...], preferred_element_type=f32)
          + jnp.dot(x_lo, wrh_ref[...], preferred_element_type=f32)) + br_ref[...]
    lane_i = lax.broadcasted_iota(jnp.int32, lg.shape, 1)
    lane = lane_i.astype(f32)
    big = jnp.float32(1 << 20)
    is_g = lane_i < N_GROUPS
    gl = jnp.where(is_g, lg, -jnp.inf)
    gmax = jnp.max(gl, axis=1, keepdims=True)
    g_sel = jnp.min(jnp.where(gl == gmax, lane, big), axis=1, keepdims=True)
    g_w = 1.0 / jnp.sum(jnp.where(is_g, jnp.exp(gl - gmax), 0.0), axis=1, keepdims=True)
    e_id = lane_i - N_GROUPS
    e_grp = (e_id >> 3).astype(f32)
    in_grp = (e_id >= 0) & (e_id < N_EXPERTS) & (e_grp == g_sel)
    el = jnp.where(in_grp, lg, -jnp.inf)
    emax = jnp.max(el, axis=1, keepdims=True)
    ee = jnp.where(in_grp, jnp.exp(el - emax), 0.0)
    ep = ee / jnp.sum(ee, axis=1, keepdims=True)
    epm = jnp.where(in_grp, ep, -1.0)
    p1 = jnp.max(epm, axis=1, keepdims=True)
    i1 = jnp.min(jnp.where(epm == p1, lane, big), axis=1, keepdims=True)
    epm2 = jnp.where(lane == i1, -1.0, epm)
    p2 = jnp.max(epm2, axis=1, keepdims=True)
    i2 = jnp.min(jnp.where(epm2 == p2, lane, big), axis=1, keepdims=True)
    psum = p1 + p2
    w1 = g_w * p1 / psum
    w2 = g_w * p2 / psum
    @pl.when(pl.program_id(0) == 0)
    def _():
        carry_ref[...] = jnp.zeros_like(carry_ref)

    oh1 = lane + N_GROUPS == i1
    oh2 = lane + N_GROUPS == i2
    oh = jnp.where(oh1 | oh2, 1.0, 0.0)
    prefix = jnp.dot(tri_ref[...], oh.astype(jnp.bfloat16), preferred_element_type=f32) + carry_ref[...]
    rank1 = jnp.sum(jnp.where(oh1, prefix, 0.0), axis=1, keepdims=True)
    rank2 = jnp.sum(jnp.where(oh2, prefix, 0.0), axis=1, keepdims=True)
    carry_ref[...] += jnp.sum(oh, axis=0, keepdims=True)
    cnt_ref[...] = carry_ref[...]
    rec = (i1 - N_GROUPS, i2 - N_GROUPS, w1, w2, rank1, rank2)
    rt = jnp.zeros(lg.shape, f32)
    for k, v in enumerate(rec):
        rt = jnp.where(lane_i == k, v, rt)
    rt_ref[...] = rt


def _route_weights_kernel(wg_ref, wr_ref, hi_ref, lo_ref):
    w = jnp.concatenate([wg_ref[...], wr_ref[...],
                         jnp.zeros((D_MODEL, LANES - N_GROUPS - N_EXPERTS), jnp.float32)], axis=1)
    hi = w.astype(jnp.bfloat16)
    hi_ref[...] = hi
    lo_ref[...] = (w - hi.astype(jnp.float32)).astype(jnp.bfloat16)


def _route_weights(w_group, w_router):
    full = lambda c: pl.BlockSpec((D_MODEL, c), lambda: (0, 0))
    return pl.pallas_call(
        _route_weights_kernel,
        in_specs=[full(N_GROUPS), full(N_EXPERTS)],
        out_specs=(full(LANES), full(LANES)),
        out_shape=(jax.ShapeDtypeStruct((D_MODEL, LANES), jnp.bfloat16),) * 2,
        name="route_weights",
    )(w_group, w_router)


def _merge_route(ya, yb, ga, gb, x2, wa, wb, wo, g1, b1, wr_hi, wr_lo, br, tm):
    n = x2.shape[0]
    row = lambda w: pl.BlockSpec((tm, w), lambda i: (i, 0))
    full = lambda r, c: pl.BlockSpec((r, c), lambda i: (0, 0))
    aw, bw = A_HEADS * A_HEAD_DIM, B_HEADS * B_HEAD_DIM
    tri = jnp.asarray(np.tril(np.ones((tm, tm), np.float32), -1), jnp.bfloat16)
    return pl.pallas_call(
        _merge_kernel,
        grid=(n // tm,),
        in_specs=[row(aw), row(bw), row(D_MODEL), row(D_MODEL), row(D_MODEL),
                  full(aw, D_MODEL), full(bw, D_MODEL), full(D_MODEL, D_MODEL),
                  full(1, D_MODEL), full(1, D_MODEL), full(D_MODEL, LANES), full(D_MODEL, LANES),
                  full(1, LANES), full(tm, tm)],
        out_specs=(pl.BlockSpec((tm * ROW_TILE, LANES), lambda i: (i, 0)), row(LANES), full(1, LANES)),
        out_shape=(jax.ShapeDtypeStruct((n * ROW_TILE, LANES), jnp.float32),
                   jax.ShapeDtypeStruct((n, LANES), jnp.float32),
                   jax.ShapeDtypeStruct((1, LANES), jnp.float32)),
        scratch_shapes=[pltpu.VMEM((1, LANES), jnp.float32)],
        compiler_params=_cparams("arbitrary"),
        name="merge_route",
    )(ya, yb, ga, gb, x2, wa, wb, wo, g1, b1, wr_hi, wr_lo, br, tri)


ROW_TILE = D_MODEL // LANES


def _store_tile_rows(ref, val):
    rows = val.shape[0]
    for j in range(ROW_TILE):
        ref[pl.ds(j, rows, stride=ROW_TILE), :] = val[:, j * LANES:(j + 1) * LANES]


def _load_tile_rows(ref, rows):
    return jnp.concatenate([ref[pl.ds(j, rows, stride=ROW_TILE), :] for j in range(ROW_TILE)], axis=1)


DMA_UNROLL = 8


def _dispatch_kernel(pos_ref, x_ref, xs_hbm, sem, *, tm):
    def issue(k, carry):
        for u in range(DMA_UNROLL):
            r = k * DMA_UNROLL + u
            src = x_ref.at[pl.ds(pl.multiple_of(r * ROW_TILE, ROW_TILE), ROW_TILE)]
            for slot in range(2):
                p = pos_ref[0, slot, r]
                dst = xs_hbm.at[pl.ds(pl.multiple_of(p * ROW_TILE, ROW_TILE), ROW_TILE)]
                pltpu.make_async_copy(src, dst, sem).start(priority=slot)
        return carry

    lax.fori_loop(0, tm // DMA_UNROLL, issue, 0)
    for slot in range(2):
        pltpu.make_async_copy(x_ref, xs_hbm.at[pl.ds(0, tm * ROW_TILE)], sem).wait()


def _moe_dispatch(pos, x1r, n_rows, tm):
    n = x1r.shape[0] // ROW_TILE
    return pl.pallas_call(
        functools.partial(_dispatch_kernel, tm=tm),
        grid=(n // tm,),
        in_specs=[pl.BlockSpec((1, 8, tm), lambda i: (i, 0, 0), memory_space=pltpu.SMEM),
                  pl.BlockSpec((tm * ROW_TILE, LANES), lambda i: (i, 0))],
        out_specs=pl.BlockSpec(memory_space=pl.ANY),
        out_shape=jax.ShapeDtypeStruct((n_rows * ROW_TILE, LANES), jnp.float32),
        scratch_shapes=[pltpu.SemaphoreType.DMA],
        compiler_params=_cparams("arbitrary"),
        name="moe_dispatch",
    )(pos, x1r)


def _expert_kernel(tile_ref, exp_ref, lo_ref, hi_ref, nit_ref, xs_ref, wg_ref, wu_ref, wd_ref, ys_ref,
                   wgb_ref, wub_ref, wdb_ref, *, tm):
    f32 = jnp.float32
    w = pl.program_id(0)
    prev = jnp.maximum(w - 1, 0)

    @pl.when(jnp.logical_or(w == 0, exp_ref[prev] != exp_ref[w]))
    def _():
        wgb_ref[...] = wg_ref[0].astype(jnp.bfloat16)
        wub_ref[...] = wu_ref[0].astype(jnp.bfloat16)
        wdb_ref[...] = wd_ref[0].astype(jnp.bfloat16)

    @pl.when(w < nit_ref[0])
    def _():
        xb = _load_tile_rows(xs_ref, tm).astype(jnp.bfloat16)
        g = jnp.dot(xb, wgb_ref[...], preferred_element_type=f32)
        u = jnp.dot(xb, wub_ref[...], preferred_element_type=f32)
        hmid = (g * jax.nn.sigmoid(g) * u).astype(jnp.bfloat16)
        y = jnp.dot(hmid, wdb_ref[...], preferred_element_type=f32)
        row = tile_ref[w] * tm + lax.broadcasted_iota(jnp.int32, (tm, 1), 0)
        mine = (row >= lo_ref[w]) & (row < hi_ref[w])
        first_visit = jnp.logical_or(w == 0, tile_ref[prev] != tile_ref[w])

        @pl.when(first_visit)
        def _():
            _store_tile_rows(ys_ref, jnp.where(mine, y, 0.0))

        @pl.when(jnp.logical_not(first_visit))
        def _():
            _store_tile_rows(ys_ref, jnp.where(mine, y, _load_tile_rows(ys_ref, tm)))


def _moe_experts(items, xs, wg, wu, wd, tm):
    n_items = items[0].shape[0]
    grid_spec = pltpu.PrefetchScalarGridSpec(
        num_scalar_prefetch=5,
        grid=(n_items,),
        in_specs=[
            pl.BlockSpec((tm * ROW_TILE, LANES), lambda w, t, e, lo, hi, n: (t[w], 0)),
            pl.BlockSpec((1, D_MODEL, D_EXPERT), lambda w, t, e, lo, hi, n: (e[w], 0, 0)),
            pl.BlockSpec((1, D_MODEL, D_EXPERT), lambda w, t, e, lo, hi, n: (e[w], 0, 0)),
            pl.BlockSpec((1, D_EXPERT, D_MODEL), lambda w, t, e, lo, hi, n: (e[w], 0, 0)),
        ],
        out_specs=pl.BlockSpec((tm * ROW_TILE, LANES), lambda w, t, e, lo, hi, n: (t[w], 0)),
        scratch_shapes=[pltpu.VMEM((D_MODEL, D_EXPERT), jnp.bfloat16),
                        pltpu.VMEM((D_MODEL, D_EXPERT), jnp.bfloat16),
                        pltpu.VMEM((D_EXPERT, D_MODEL), jnp.bfloat16)],
    )
    return pl.pallas_call(
        functools.partial(_expert_kernel, tm=tm),
        grid_spec=grid_spec,
        out_shape=jax.ShapeDtypeStruct(xs.shape, jnp.float32),
        compiler_params=_cparams("arbitrary"),
        name="moe_experts",
    )(*items, xs, wg, wu, wd)


def _combine_kernel(pos_ref, rt_ref, x_ref, ys_hbm, g2_ref, b2_ref, out_ref, buf_a, buf_b, sem, *, tm):
    bufs = (buf_a, buf_b)
    th = tm // 2
    half = lambda ref, h: ref.at[pl.ds(h * th * ROW_TILE, th * ROW_TILE)]

    def issue_half(h):
        def issue(k, carry):
            for u in range(DMA_UNROLL):
                r = h * th + k * DMA_UNROLL + u
                for slot in range(2):
                    p = pos_ref[0, slot, r]
                    src = ys_hbm.at[pl.ds(pl.multiple_of(p * ROW_TILE, ROW_TILE), ROW_TILE)]
                    dst = bufs[slot].at[pl.ds(pl.multiple_of(r * ROW_TILE, ROW_TILE), ROW_TILE)]
                    pltpu.make_async_copy(src, dst, sem.at[h]).start(priority=slot)
            return carry
        lax.fori_loop(0, th // DMA_UNROLL, issue, 0)

    def finish_half(h):
        for slot in range(2):
            pltpu.make_async_copy(ys_hbm.at[pl.ds(0, th * ROW_TILE)], half(bufs[slot], h), sem.at[h]).wait()
        rt = rt_ref[h * th:(h + 1) * th, :]
        ffn = (rt[:, 2:3] * _load_tile_rows(half(buf_a, h), th)
               + rt[:, 3:4] * _load_tile_rows(half(buf_b, h), th))
        out_ref[h * th:(h + 1) * th, :] = _layer_norm(
            ALPHA * _load_tile_rows(half(x_ref, h), th) + ffn, g2_ref[...], b2_ref[...])

    issue_half(0)
    issue_half(1)
    finish_half(0)
    finish_half(1)


def _moe_combine(pos, rt, x1r, ys, g2, b2, tm):
    n = rt.shape[0]
    return pl.pallas_call(
        functools.partial(_combine_kernel, tm=tm),
        grid=(n // tm,),
        in_specs=[pl.BlockSpec((1, 8, tm), lambda i: (i, 0, 0), memory_space=pltpu.SMEM),
                  pl.BlockSpec((tm, LANES), lambda i: (i, 0)),
                  pl.BlockSpec((tm * ROW_TILE, LANES), lambda i: (i, 0)),
                  pl.BlockSpec(memory_space=pl.ANY),
                  pl.BlockSpec((1, D_MODEL), lambda i: (0, 0)),
                  pl.BlockSpec((1, D_MODEL), lambda i: (0, 0))],
        out_specs=pl.BlockSpec((tm, D_MODEL), lambda i: (i, 0)),
        out_shape=jax.ShapeDtypeStruct((n, D_MODEL), jnp.float32),
        scratch_shapes=[pltpu.VMEM((tm * ROW_TILE, LANES), jnp.float32),
                        pltpu.VMEM((tm * ROW_TILE, LANES), jnp.float32),
                        pltpu.SemaphoreType.DMA((2,))],
        compiler_params=_cparams("arbitrary"),
        name="moe_combine",
    )(pos, rt, x1r, ys, g2, b2)


MOE_TM = 512


def _pos_kernel(start_ref, rt_ref, pos_ref):
    rt = rt_ref[...]
    lane = lax.broadcasted_iota(jnp.int32, rt.shape, 1)
    lane_f = lane.astype(jnp.float32)
    start = start_ref[...]
    tile = jnp.zeros(rt.shape, jnp.float32)
    for slot in range(2):
        first_row = jnp.sum(jnp.where(lane_f == rt[:, slot:slot + 1], start, 0.0), axis=1, keepdims=True)
        tile = jnp.where(lane == slot, first_row + rt[:, 4 + slot:5 + slot], tile)
    tm = pos_ref.shape[2]
    for r in range(rt.shape[0] // BLOCK):
        t = tile[r * BLOCK:(r + 1) * BLOCK].T
        c = (r * BLOCK) % tm
        pos_ref[(r * BLOCK) // tm, :, c:c + BLOCK] = t[:8].astype(jnp.int32)


def _moe_pos(start, rt, tm):
    n = rt.shape[0]
    per_step = _pick_tile(n // tm, 4)
    return pl.pallas_call(
        _pos_kernel,
        grid=(n // (tm * per_step),),
        in_specs=[pl.BlockSpec((1, LANES), lambda i: (0, 0)),
                  pl.BlockSpec((tm * per_step, LANES), lambda i: (i, 0))],
        out_specs=pl.BlockSpec((per_step, 8, tm), lambda i: (i, 0, 0)),
        out_shape=jax.ShapeDtypeStruct((n // tm, 8, tm), jnp.int32),
        compiler_params=_cparams("arbitrary"),
        name="moe_pos",
    )(start, rt)


def _moe_plan(counts, n_tiles):
    i32 = jnp.int32
    cnt = counts[0, :N_EXPERTS].astype(i32)
    end = jnp.cumsum(cnt)
    start = end - cnt
    start_lanes = jnp.pad(start.astype(jnp.float32), (0, LANES - N_EXPERTS)).reshape(1, LANES)
    first_t = start // MOE_TM
    items_e = jnp.where(cnt > 0, (end - 1) // MOE_TM - first_t + 1, 0)
    item_end = jnp.cumsum(items_e)
    n_items = item_end[-1]
    w = jnp.minimum(jnp.arange(n_tiles + N_EXPERTS - 1, dtype=i32), n_items - 1)
    e_w = jnp.minimum(jnp.sum((item_end[None, :] <= w[:, None]).astype(i32), axis=1), N_EXPERTS - 1)
    t_w = first_t[e_w] + (w - (item_end - items_e)[e_w])
    lo_w = jnp.maximum(start[e_w], t_w * MOE_TM)
    hi_w = jnp.minimum(end[e_w], (t_w + 1) * MOE_TM)
    return start_lanes, (t_w.astype(i32), e_w, lo_w.astype(i32), hi_w.astype(i32),
                         n_items.reshape(1).astype(i32))


def _pick_tile(n, pref):
    t = min(pref, n)
    while n % t:
        t //= 2
    return t


def kernel(x, w_in, kv_norm_g, w_uv, w_branch_a, sinks, w_branch_b, w_out, rel_bias, ln1_g, ln1_b,
           w_group, b_group, w_router, b_router, w_gate, w_up, w_down, ln2_g, ln2_b):
    bsz, seq, d = x.shape
    assert d == D_MODEL and seq % BLOCK == 0 and w_in.shape[0] == DEPTH == 1
    n = bsz * seq
    n_blk = seq // BLOCK
    bf, f32 = jnp.bfloat16, jnp.float32
    x2 = x.reshape(n, d)

    r = np.arange(BLOCK, dtype=np.int32)
    d_prev = r[None, :] + BLOCK - r[:, None]
    d_cur = r[None, :] - r[:, None]
    bkt_a = jnp.asarray(_t5_bucket_np(np.stack([d_prev, d_cur])))
    s2 = np.arange(2 * BLOCK, dtype=np.int32)
    bkt_b = jnp.asarray(_t5_bucket_np(r[None, :] + BLOCK - s2[:, None])[None])
    assert int(_t5_bucket_np(np.arange(BLOCK + 1, max(seq, BLOCK + 2))).min()) == N_BUCKETS - 1
    tab_t = rel_bias.astype(f32).T
    bias_a = _bias_tiles(tab_t, bkt_a, 0, A_HEADS, True, minus_far=True)
    bias_b = _bias_tiles(tab_t, bkt_b, A_HEADS, B_HEADS, True)[0]

    (q_blocks, ckv, ckv_tx, qidx_blocks, kidx, widx, qb_blocks, kb, vbx, ga, gb) = _input_projection(
        x2, w_in[0].astype(bf), kv_norm_g[0].reshape(1, A_LAT).astype(f32), _pick_tile(seq, 512), seq)

    ckv3 = ckv.reshape(bsz, seq, A_LAT)
    widx_l = jnp.swapaxes(widx.reshape(bsz * n_blk, BLOCK, IDX_HEADS), 1, 2).reshape(
        bsz * n_blk, 1, IDX_HEADS * BLOCK)
    ya = _dsa_attention(qidx_blocks.reshape(bsz * n_blk, IDX_HEADS * BLOCK, IDX_DIM),
                        kidx.reshape(bsz, seq, IDX_DIM), widx_l,
                        q_blocks.reshape(bsz * n_blk, A_HEADS * BLOCK, A_LAT),
                        ckv3, ckv_tx, bias_a, jnp.swapaxes(w_uv[0], 1, 2).astype(bf), bsz, seq)

    yb = _swa_attention(sinks[0].astype(f32), qb_blocks, kb, vbx, bias_b, bsz, seq)

    wr_hi, wr_lo = _route_weights(w_group[0].astype(f32), w_router[0].astype(f32))
    b_route = jnp.concatenate(
        [b_group[0], b_router[0], jnp.zeros((LANES - N_GROUPS - N_EXPERTS,), f32)]).reshape(1, LANES).astype(f32)
    x1r, rt, counts = _merge_route(
        ya, yb, ga, gb, x2, w_branch_a[0].astype(bf), w_branch_b[0].astype(bf), w_out[0].astype(bf),
        ln1_g[0].reshape(1, d).astype(f32), ln1_b[0].reshape(1, d).astype(f32), wr_hi, wr_lo, b_route,
        _pick_tile(n, 512))

    assert (2 * n) % MOE_TM == 0
    start_lanes, items = _moe_plan(counts, 2 * n // MOE_TM)
    tm_io = _pick_tile(n, 512)
    pos_blocks = _moe_pos(start_lanes, rt, tm_io)
    xs = _moe_dispatch(pos_blocks, x1r, 2 * n, tm_io)
    ys = _moe_experts(items, xs, w_gate[0], w_up[0], w_down[0], MOE_TM)
    out = _moe_combine(pos_blocks, rt, x1r, ys, ln2_g[0].reshape(1, d).astype(f32),
                       ln2_b[0].reshape(1, d).astype(f32), tm_io)
    return out.reshape(bsz, seq, d)
```

```python
import functools
import math

import numpy as np
import jax
import jax.numpy as jnp
from jax import lax
from jax.experimental import pallas as pl
from jax.experimental.pallas import tpu as pltpu

D_MODEL = 1024
A_HEADS = 8
A_LAT = 128
A_HEAD_DIM = 64
IDX_HEADS = 8
IDX_DIM = 64
TOPK_MAX = 256
B_HEADS = 8
B_KV_HEADS = 2
B_HEAD_DIM = 64
WINDOW = 128
BLOCK = 128
N_BUCKETS = 32
MAX_DISTANCE = 128
N_GROUPS = 4
EXPERTS_PER_GROUP = 8
N_EXPERTS = 32
D_EXPERT = 256
DEPTH = 1
ALPHA = (2 * DEPTH) ** 0.25
LN_EPS = 1e-5
RMS_EPS = 1e-6

LANES = 128
INT_MIN = -(2 ** 31)
VMEM_LIMIT = 48 * 1024 * 1024
PROJ_VMEM_LIMIT = 56 * 1024 * 1024

_NT = (((1,), (1,)), ((), ()))


def _cparams(*sem):
    return pltpu.CompilerParams(dimension_semantics=sem, vmem_limit_bytes=VMEM_LIMIT)


def _t5_bucket_np(dist):
    f32 = np.float32
    n = np.maximum(dist, 0)
    max_exact = N_BUCKETS // 2
    nf = np.maximum(n, 1).astype(f32)
    large = max_exact + (np.log(nf / f32(max_exact)) / f32(math.log(MAX_DISTANCE / max_exact))
                         * f32(N_BUCKETS - max_exact)).astype(np.int32)
    large = np.minimum(large, N_BUCKETS - 1)
    return np.where(n < max_exact, n, large).astype(np.int32)


def _bias_kernel(tab_ref, bkt_ref, out_ref, *, head0, minus_far):
    h = pl.program_id(0) + head0
    bkt = bkt_ref[...]
    far = tab_ref[h, N_BUCKETS - 1]
    acc = jnp.full(bkt.shape, far, jnp.float32)
    for b in range(N_BUCKETS - 1):
        acc = jnp.where(bkt == b, tab_ref[h, b], acc)
    out_ref[...] = acc - far if minus_far else acc


def _bias_tiles(tab_t, bkt, head0, n_heads, lane_major, minus_far=False):
    g, r, c = bkt.shape
    if lane_major:
        out_shape = jax.ShapeDtypeStruct((g, r, n_heads * c), jnp.float32)
        out_spec = pl.BlockSpec((g, r, c), lambda h: (0, 0, h))
    else:
        out_shape = jax.ShapeDtypeStruct((n_heads * g, r, c), jnp.float32)
        out_spec = pl.BlockSpec((g, r, c), lambda h: (h, 0, 0))
    return pl.pallas_call(
        functools.partial(_bias_kernel, head0=head0, minus_far=minus_far),
        grid=(n_heads,),
        in_specs=[pl.BlockSpec(memory_space=pltpu.SMEM),
                  pl.BlockSpec((g, r, c), lambda h: (0, 0, 0))],
        out_specs=out_spec,
        out_shape=out_shape,
        compiler_params=_cparams("arbitrary"),
        name="bias_tiles",
    )(tab_t, bkt)


_C_QLAT = 0
_C_CKV = _C_QLAT + A_HEADS * A_LAT
_C_QIDX = _C_CKV + A_LAT
_C_KW = _C_QIDX + IDX_HEADS * IDX_DIM
_C_QB = _C_KW + LANES
_C_KB = _C_QB + B_HEADS * B_HEAD_DIM
_C_VB = _C_KB + B_KV_HEADS * B_HEAD_DIM
_C_GA = _C_VB + B_KV_HEADS * B_HEAD_DIM
_C_GB = _C_GA + D_MODEL
_C_END = _C_GB + D_MODEL
_RAW_KW_END = _C_KW + IDX_DIM + IDX_HEADS


def _proj_kernel(x_ref, wraw_ref, g_ref, q_ref, ckv_ref, ckvt_ref, qidx_ref, kidx_ref, widx_ref,
                 qb_ref, kb_ref, vb_ref, ga_ref, gb_ref, w_ref):
    xb = x_ref[...].astype(jnp.bfloat16)
    tm = xb.shape[0]

    @pl.when(pl.program_id(0) == 0)
    def _():
        w_ref[:, :_C_KW] = wraw_ref[:, :_C_KW]
        w_ref[:, _C_KW:_C_QB] = jnp.concatenate(
            [wraw_ref[:, _C_KW:_RAW_KW_END], jnp.zeros((D_MODEL, _C_QB - _RAW_KW_END), jnp.bfloat16)], axis=1)
        for dst in range(_C_QB, _C_END, LANES):
            src = dst - (_C_QB - _RAW_KW_END)
            w_ref[:, dst:dst + LANES] = wraw_ref[:, src:src + LANES]

    def seg(lo, hi):
        return jnp.dot(xb, w_ref[:, lo:hi], preferred_element_type=jnp.float32)

    for h in range(A_HEADS):
        qh = (seg(_C_QLAT + h * A_LAT, _C_QLAT + (h + 1) * A_LAT) * (A_LAT ** -0.5)).astype(jnp.bfloat16)
        for r in range(tm // BLOCK):
            q_ref[r, h] = qh[r * BLOCK:(r + 1) * BLOCK]
    c = seg(_C_CKV, _C_QIDX)
    ms = jnp.mean(c * c, axis=-1, keepdims=True)
    cn = c * lax.rsqrt(ms + RMS_EPS) * g_ref[...]
    ckv_ref[...] = cn.astype(jnp.bfloat16)
    for r in range(tm // BLOCK):
        ckvt_ref[0, :A_LAT, r * BLOCK:(r + 1) * BLOCK] = cn[r * BLOCK:(r + 1) * BLOCK].T.astype(jnp.bfloat16)
    ckvt_ref[0, A_LAT:, :] = jnp.ones((8, tm), jnp.bfloat16)
    qif = seg(_C_QIDX, _C_KW).astype(jnp.bfloat16)
    for h in range(IDX_HEADS):
        for r in range(tm // BLOCK):
            qidx_ref[r, h] = qif[r * BLOCK:(r + 1) * BLOCK, h * IDX_DIM:(h + 1) * IDX_DIM]
    kw = seg(_C_KW, _C_QB)
    kidx_ref[...] = kw[:, :IDX_DIM].astype(jnp.bfloat16)
    for r in range(tm // BLOCK):
        kwt = kw[r * BLOCK:(r + 1) * BLOCK].T
        for h in range(IDX_HEADS):
            widx_ref[r, :, h * BLOCK:(h + 1) * BLOCK] = kwt[IDX_DIM + h:IDX_DIM + h + 1, :]
    qbf = seg(_C_QB, _C_KB).astype(jnp.bfloat16)
    for h in range(B_HEADS):
        for r in range(tm // BLOCK):
            qb_ref[r, h] = qbf[r * BLOCK:(r + 1) * BLOCK, h * B_HEAD_DIM:(h + 1) * B_HEAD_DIM]
    kbf = seg(_C_KB, _C_VB).astype(jnp.bfloat16)
    vbf = seg(_C_VB, _C_GA)
    for g in range(B_KV_HEADS):
        kb_ref[g] = kbf[:, g * B_HEAD_DIM:(g + 1) * B_HEAD_DIM]
        vb_ref[g, 0, B_HEAD_DIM:, :] = jnp.ones((SWA_ONES, tm), jnp.bfloat16)
    for r in range(tm // BLOCK):
        vt = vbf[r * BLOCK:(r + 1) * BLOCK].T.astype(jnp.bfloat16)
        for g in range(B_KV_HEADS):
            vb_ref[g, 0, :B_HEAD_DIM, r * BLOCK:(r + 1) * BLOCK] = vt[g * B_HEAD_DIM:(g + 1) * B_HEAD_DIM]
    ga_ref[...] = jax.nn.sigmoid(seg(_C_GA, _C_GB))
    gb_ref[...] = jax.nn.sigmoid(seg(_C_GB, _C_END))


def _input_projection(x2, w_raw, kv_g, tm, seq):
    n = x2.shape[0]
    assert seq % tm == 0 and w_raw.shape == (D_MODEL, _C_END - (_C_QB - _RAW_KW_END))
    bf, f32 = jnp.bfloat16, jnp.float32
    row = lambda w: pl.BlockSpec((tm, w), lambda i: (i, 0))
    out_shape = (
        jax.ShapeDtypeStruct((n // BLOCK, A_HEADS, BLOCK, A_LAT), bf),
        jax.ShapeDtypeStruct((n, A_LAT), bf),
        jax.ShapeDtypeStruct((n // seq, A_LAT + 8, seq), bf),
        jax.ShapeDtypeStruct((n // BLOCK, IDX_HEADS, BLOCK, IDX_DIM), bf),
        jax.ShapeDtypeStruct((n, IDX_DIM), bf),
        jax.ShapeDtypeStruct((n // BLOCK, 1, IDX_HEADS * BLOCK), f32),
        jax.ShapeDtypeStruct((n // BLOCK, B_HEADS, BLOCK, B_HEAD_DIM), bf),
        jax.ShapeDtypeStruct((B_KV_HEADS, n, B_HEAD_DIM), bf),
        jax.ShapeDtypeStruct((B_KV_HEADS, n // seq, B_HEAD_DIM + SWA_ONES, seq), bf),
        jax.ShapeDtypeStruct((n, D_MODEL), f32),
        jax.ShapeDtypeStruct((n, D_MODEL), f32),
    )
    out_specs = (
        pl.BlockSpec((tm // BLOCK, A_HEADS, BLOCK, A_LAT), lambda i: (i, 0, 0, 0)),
        row(A_LAT),
        pl.BlockSpec((1, A_LAT + 8, tm), lambda i: (i // (seq // tm), 0, i % (seq // tm))),
        pl.BlockSpec((tm // BLOCK, IDX_HEADS, BLOCK, IDX_DIM), lambda i: (i, 0, 0, 0)),
        row(IDX_DIM), pl.BlockSpec((tm // BLOCK, 1, IDX_HEADS * BLOCK), lambda i: (i, 0, 0)),
        pl.BlockSpec((tm // BLOCK, B_HEADS, BLOCK, B_HEAD_DIM), lambda i: (i, 0, 0, 0)),
        pl.BlockSpec((B_KV_HEADS, tm, B_HEAD_DIM), lambda i: (0, i, 0)),
        pl.BlockSpec((B_KV_HEADS, 1, B_HEAD_DIM + SWA_ONES, tm),
                     lambda i: (0, i // (seq // tm), 0, i % (seq // tm))),
        row(D_MODEL), row(D_MODEL),
    )
    return pl.pallas_call(
        _proj_kernel,
        grid=(n // tm,),
        in_specs=[row(D_MODEL),
                  pl.BlockSpec(w_raw.shape, lambda i: (0, 0)),
                  pl.BlockSpec((1, A_LAT), lambda i: (0, 0))],
        out_specs=out_specs,
        out_shape=out_shape,
        scratch_shapes=[pltpu.VMEM((D_MODEL, _C_END), bf)],
        compiler_params=pltpu.CompilerParams(dimension_semantics=("arbitrary",),
                                             vmem_limit_bytes=PROJ_VMEM_LIMIT),
        name="input_projection",
    )(x2, w_raw, kv_g)


DSA_UNROLL = 8


def _bit_transpose32(words):
    w = list(words)
    j, m = 16, 0x0000FFFF
    while j:
        mask = jnp.int32(m - (1 << 32) if m >= (1 << 31) else m)
        k = 0
        while k < 32:
            t = (w[k] ^ lax.shift_right_logical(w[k + j], jnp.int32(j))) & mask
            w[k] = w[k] ^ t
            w[k + j] = w[k + j] ^ lax.shift_left(t, jnp.int32(j))
            k = (k + j + 1) & ~j
        j >>= 1
        m = (m ^ (m << j)) & 0xFFFFFFFF
    return w


def _dsa_kernel(qidx_ref, kidx_ref, widx_ref, q_ref, ckv_ref, ckvt_ref, bias_ref, wuvt_ref,
                tri_ref, ya_ref, sc_ref, neg_ref, lg_ref, o_ref, planes_ref, sel_ref, *, k_sel,
                search_chunk):
    i = pl.program_id(1)
    f32 = jnp.float32
    s_loc = lax.broadcasted_iota(jnp.int32, (BLOCK, BLOCK), 0)
    t_loc = lax.broadcasted_iota(jnp.int32, (BLOCK, BLOCK), 1)
    causal_diag = s_loc <= t_loc
    idx_scale = IDX_DIM ** -0.5
    w_scale = IDX_HEADS ** -0.5

    def rows(j):
        return pl.ds(pl.multiple_of(j * BLOCK, BLOCK), BLOCK)

    sc_ref[...] = jnp.full(sc_ref.shape, -jnp.inf, f32)
    wts = (widx_ref[0] * w_scale) * idx_scale
    qi = qidx_ref[0]

    def score_keys(j):
        s = lax.dot_general(kidx_ref[0, rows(j), :], qi, _NT, preferred_element_type=f32)
        r = jnp.maximum(s, 0.0) * wts
        acc = r[:, :BLOCK]
        for h in range(1, IDX_HEADS):
            acc = acc + r[:, h * BLOCK:(h + 1) * BLOCK]
        return acc

    def fold_blocks(lo, hi, fn, init, combine):
        n = hi - lo
        n_grp = lax.shift_right_logical(n, DSA_UNROLL.bit_length() - 1)

        def piece(acc, j, size):
            vals = [fn(j + u) for u in range(size)]
            while len(vals) > 1:
                vals = [combine(vals[a], vals[a + 1]) for a in range(0, len(vals), 2)]
            return combine(acc, vals[0])

        acc = lax.fori_loop(0, n_grp, lambda k, a: piece(a, lo + DSA_UNROLL * k, DSA_UNROLL), init)
        j = lo + DSA_UNROLL * n_grp
        left = hi - j
        size = DSA_UNROLL // 2
        while size:
            acc = lax.cond((left & size) != 0, functools.partial(piece, acc, j, size), lambda a=acc: a)
            j = j + (left & size)
            size //= 2
        return acc

    def block_scores(j):
        visible = (j * BLOCK + s_loc) <= (i * BLOCK + t_loc)
        sc_ref[rows(j), :] = jnp.where(visible, score_keys(j), -jnp.inf)
        return jnp.int32(0)

    fold_blocks(0, i + 1, block_scores, jnp.int32(0), lambda a, b: a)

    n_chunks = ((i + 1) * BLOCK + search_chunk - 1) // search_chunk
    n_acc = 64

    def count(cmp, cand):
        def body(c, cnt):
            base = pl.multiple_of(c * search_chunk, search_chunk)
            for r in range(search_chunk // n_acc):
                cnt = cnt + cmp(sc_ref[pl.ds(base + r * n_acc, n_acc), :], cand).astype(f32)
            return cnt
        cnt = lax.fori_loop(0, n_chunks, body, jnp.zeros((n_acc, BLOCK), f32))
        return jnp.sum(cnt, axis=0, keepdims=True)

    def ordered_to_float(u):
        k = u ^ INT_MIN
        return lax.bitcast_convert_type(k ^ ((k >> 31) & 0x7FFFFFFF), f32)

    n_grp_keys = 32 * 8
    n_groups = ((i + 1) * BLOCK + n_grp_keys - 1) // n_grp_keys
    max_groups = planes_ref.shape[0]

    def build_planes(g, carry):
        words = []
        for v in range(32):
            s = sc_ref[pl.ds(pl.multiple_of(g * n_grp_keys + v * 8, 8), 8), :]
            bits = lax.bitcast_convert_type(s, jnp.int32)
            words.append(bits ^ ((bits >> 31) & 0x7FFFFFFF) ^ INT_MIN)
        planes = _bit_transpose32(words)
        for p in range(32):
            planes_ref[g, p] = planes[p]
        return carry

    lax.fori_loop(0, n_groups, build_planes, 0)

    def popcount_rows(words):
        cnt = lax.population_count(words[0])
        for wd in words[1:]:
            cnt = cnt + lax.population_count(wd)
        return jnp.sum(cnt.astype(f32), axis=0, keepdims=True)

    def select_two_bits(n, carry):
        alive, k_left, thr_u = carry
        grp = [jnp.minimum(g, n_groups - 1) for g in range(max_groups)]
        hi1 = [alive[g] & planes_ref[grp[g], 2 * n] for g in range(max_groups)]
        hi0 = [alive[g] ^ hi1[g] for g in range(max_groups)]
        lo = [planes_ref[grp[g], 2 * n + 1] for g in range(max_groups)]
        c11 = [hi1[g] & lo[g] for g in range(max_groups)]
        c10 = [hi1[g] ^ c11[g] for g in range(max_groups)]
        c01 = [hi0[g] & lo[g] for g in range(max_groups)]
        c00 = [hi0[g] ^ c01[g] for g in range(max_groups)]
        s1 = popcount_rows(c11)
        s2 = s1 + popcount_rows(c10)
        s3 = s2 + popcount_rows(c01)
        t11 = s1 >= k_left
        t10 = jnp.logical_and(jnp.logical_not(t11), s2 >= k_left)
        t01 = jnp.logical_and(jnp.logical_not(t11 | t10), s3 >= k_left)
        w11, w10, w01 = (jnp.broadcast_to(t, (8, BLOCK)) for t in (t11, t10, t01))
        alive = [jnp.where(w11, c11[g], jnp.where(w10, c10[g], jnp.where(w01, c01[g], c00[g])))
                 for g in range(max_groups)]
        k_left = k_left - jnp.where(t11, 0.0, jnp.where(t10, s1, jnp.where(t01, s2, s3)))
        b_hi = 31 - 2 * n
        bits = (jnp.where(t11 | t10, lax.shift_left(jnp.int32(1), b_hi), 0)
                | jnp.where(t11 | t01, lax.shift_left(jnp.int32(1), b_hi - 1), 0))
        return alive, k_left, thr_u | bits

    alive0 = [jnp.where(g < n_groups, jnp.full((8, BLOCK), -1, jnp.int32), 0) for g in range(max_groups)]
    _, _, thr_u = lax.fori_loop(
        0, 16, select_two_bits, (alive0, jnp.full((1, BLOCK), k_sel, f32), jnp.zeros((1, BLOCK), jnp.int32)))
    thr = ordered_to_float(thr_u)
    c_ge = count(jnp.greater_equal, thr)
    c_gt = count(jnp.greater, thr)
    sel_ref[0:1, :] = thr
    sel_ref[1:2, :] = c_ge
    sel_ref[2:3, :] = c_gt
    consistent = jnp.min(((c_ge >= k_sel) & (c_gt < k_sel)).astype(f32)) > 0.0

    @pl.when(jnp.logical_not(consistent))
    def _():
        def search_bit(b, prefix_u):
            cand_u = prefix_u | lax.shift_left(jnp.int32(1), 31 - b)
            cnt = count(jnp.greater_equal, ordered_to_float(cand_u))
            return jnp.where(cnt >= k_sel, cand_u, prefix_u)

        t = ordered_to_float(lax.fori_loop(0, 32, search_bit, jnp.zeros((1, BLOCK), jnp.int32)))
        t = jnp.where(t != t, -jnp.inf, t)
        sel_ref[0:1, :] = t
        sel_ref[1:2, :] = count(jnp.greater_equal, t)
        sel_ref[2:3, :] = count(jnp.greater, t)

    thr = sel_ref[0:1, :]
    c_ge = sel_ref[1:2, :]
    c_gt = sel_ref[2:3, :]
    n_tie = k_sel - c_gt
    any_tie = jnp.max(((c_ge > k_sel) & (thr > -jnp.inf)).astype(f32)) > 0.0

    @pl.when(jnp.logical_not(any_tie))
    def _():
        def body(j, carry):
            neg_ref[rows(j), :] = jnp.where(sc_ref[rows(j), :] >= thr, 0.0, -jnp.inf)
            return carry
        lax.fori_loop(0, i, body, 0)
        neg_ref[rows(i), :] = jnp.where((sc_ref[rows(i), :] >= thr) & causal_diag, 0.0, -jnp.inf)

    @pl.when(any_tie)
    def _():
        tri = tri_ref[...]

        def block(j, tie_seen):
            kc = sc_ref[rows(j), :]
            eq = kc == thr
            tie_rank = jnp.dot(tri, eq.astype(jnp.bfloat16), preferred_element_type=f32) + tie_seen
            sel = (kc > thr) | (eq & (tie_rank <= n_tie))
            return sel, tie_seen + jnp.sum(eq.astype(f32), axis=0, keepdims=True)

        def body(j, tie_seen):
            sel, tie_seen = block(j, tie_seen)
            neg_ref[rows(j), :] = jnp.where(sel, 0.0, -jnp.inf)
            return tie_seen

        tie_seen = lax.fori_loop(0, i, body, jnp.zeros((1, BLOCK), f32))
        sel, _ = block(i, tie_seen)
        neg_ref[rows(i), :] = jnp.where(sel & causal_diag, 0.0, -jnp.inf)

    q_all = q_ref[0]

    def logits(j):
        lg = lax.dot_general(ckv_ref[0, rows(j), :], q_all, _NT, preferred_element_type=f32)
        return lg + jnp.concatenate([neg_ref[rows(j), :]] * A_HEADS, axis=1)

    def far_logits(j):
        lg = logits(j)
        lg_ref[rows(j), :] = lg
        return jnp.max(lg, axis=0, keepdims=True)

    def near_logits(j):
        lg = logits(j) + bias_ref[j - i + 1]
        lg_ref[rows(j), :] = lg
        return jnp.max(lg, axis=0, keepdims=True)

    n_far = jnp.maximum(i - 1, 0)
    m_run = fold_blocks(0, n_far, far_logits, jnp.full((1, A_HEADS * BLOCK), -jnp.inf, f32), jnp.maximum)
    m_run = fold_blocks(n_far, i + 1, near_logits, m_run, jnp.maximum)

    o_ref[...] = jnp.zeros_like(o_ref)

    def pv_rows(r):
        p = jnp.exp(lg_ref[r, :] - m_run).astype(jnp.bfloat16)
        o_ref[...] += jnp.dot(ckvt_ref[0, :, r], p, preferred_element_type=f32)

    def pv_group(k, carry):
        span = DSA_UNROLL * BLOCK
        pv_rows(pl.ds(pl.multiple_of(k * span, span), span))
        return carry

    n_grp = lax.shift_right_logical(i + 1, DSA_UNROLL.bit_length() - 1)
    lax.fori_loop(0, n_grp, pv_group, 0)
    j = DSA_UNROLL * n_grp
    left = i + 1 - j
    size = DSA_UNROLL // 2
    while size:
        @pl.when((left & size) != 0)
        def _(j=j, size=size):
            pv_rows(pl.ds(pl.multiple_of(j * BLOCK, size * BLOCK), size * BLOCK))
        j = j + (left & size)
        size //= 2

    inv_l = 1.0 / o_ref[A_LAT:A_LAT + 1, :]
    for pair in range(A_HEADS // 2):
        halves = []
        for h in (2 * pair, 2 * pair + 1):
            cols = slice(h * BLOCK, (h + 1) * BLOCK)
            y_t = jnp.dot(wuvt_ref[h], o_ref[:A_LAT, cols].astype(jnp.bfloat16),
                          preferred_element_type=f32)
            halves.append(y_t * inv_l[:, cols])
        two = jnp.concatenate(halves, axis=0)
        ya_ref[:, 2 * pair * A_HEAD_DIM:(2 * pair + 2) * A_HEAD_DIM] = two.T.astype(ya_ref.dtype)


def _dsa_attention(qidx_blocks, kidx, widx_l, q_blocks, ckv, ckv_tx, bias_near, wuv, bsz, seq):
    n_blk = seq // BLOCK
    k_sel = min(TOPK_MAX, seq // 4)
    search_chunk = min(512, seq)
    tri = jnp.asarray(np.tril(np.ones((BLOCK, BLOCK), np.float32)), jnp.bfloat16)
    kern = functools.partial(_dsa_kernel, k_sel=float(k_sel), search_chunk=search_chunk)
    blk = lambda b, i: (b * n_blk + i, 0, 0)
    return pl.pallas_call(
        kern,
        grid=(bsz, n_blk),
        in_specs=[
            pl.BlockSpec((1, IDX_HEADS * BLOCK, IDX_DIM), blk),
            pl.BlockSpec((1, seq, IDX_DIM), lambda b, i: (b, 0, 0)),
            pl.BlockSpec((1, 1, IDX_HEADS * BLOCK), blk),
            pl.BlockSpec((1, A_HEADS * BLOCK, A_LAT), blk),
            pl.BlockSpec((1, seq, A_LAT), lambda b, i: (b, 0, 0)),
            pl.BlockSpec((1, A_LAT + 8, seq), lambda b, i: (b, 0, 0)),
            pl.BlockSpec((2, BLOCK, A_HEADS * BLOCK), lambda b, i: (0, 0, 0)),
            pl.BlockSpec((A_HEADS, A_HEAD_DIM, A_LAT), lambda b, i: (0, 0, 0)),
            pl.BlockSpec((BLOCK, BLOCK), lambda b, i: (0, 0)),
        ],
        out_specs=pl.BlockSpec((BLOCK, A_HEADS * A_HEAD_DIM), lambda b, i: (b * n_blk + i, 0)),
        out_shape=jax.ShapeDtypeStruct((bsz * seq, A_HEADS * A_HEAD_DIM), jnp.bfloat16),
        scratch_shapes=[pltpu.VMEM((seq, BLOCK), jnp.float32),
                        pltpu.VMEM((seq, BLOCK), jnp.float32),
                        pltpu.VMEM((seq, A_HEADS * BLOCK), jnp.float32),
                        pltpu.VMEM((A_LAT + 8, A_HEADS * BLOCK), jnp.float32),
                        pltpu.VMEM((seq // 256, 32, 8, BLOCK), jnp.int32),
                        pltpu.VMEM((8, BLOCK), jnp.float32)],
        compiler_params=_cparams("arbitrary", "arbitrary"),
        name="dsa_attention",
    )(qidx_blocks, kidx, widx_l, q_blocks, ckv, ckv_tx, bias_near, wuv, tri)


SWA_QB = 4
SWA_ONES = 16


def _swa_kernel(sink_ref, q_ref, kp_ref, kc_ref, vp_ref, vc_ref, bias_ref, yb_ref, *, qb):
    i = pl.program_id(1)
    f32 = jnp.float32
    grp = B_HEADS // B_KV_HEADS
    cols = grp * BLOCK
    s_loc = lax.broadcasted_iota(jnp.int32, (BLOCK, BLOCK), 0)
    t_loc = lax.broadcasted_iota(jnp.int32, (BLOCK, BLOCK), 1)
    neg_cur = jnp.concatenate([jnp.where(s_loc <= t_loc, 0.0, -jnp.inf)] * grp, axis=1)
    in_prev = s_loc > t_loc
    head_of_col = lax.broadcasted_iota(jnp.int32, (1, cols), 1) >> (BLOCK.bit_length() - 1)
    scale = B_HEAD_DIM ** -0.5
    for g in range(B_KV_HEADS):
        sink = jnp.zeros((1, cols), f32)
        for hh in range(grp):
            sink = jnp.where(head_of_col == hh, sink_ref[g * grp + hh], sink)
        bias_p = bias_ref[:BLOCK, g * cols:(g + 1) * cols]
        bias_c = bias_ref[BLOCK:, g * cols:(g + 1) * cols]
        for r in range(qb):
            has_prev = jnp.zeros_like(s_loc) + jnp.minimum(i * qb + r, 1)
            neg_prev = jnp.concatenate([jnp.where(in_prev & (has_prev > 0), 0.0, -jnp.inf)] * grp, axis=1)
            q = q_ref[r, g * grp:(g + 1) * grp].reshape(cols, B_HEAD_DIM)
            if r == 0:
                kp, vp = kp_ref[g], vp_ref[g, 0]
            else:
                kp, vp = kc_ref[g, (r - 1) * BLOCK:r * BLOCK], vc_ref[g, 0, :, (r - 1) * BLOCK:r * BLOCK]
            kc, vc = kc_ref[g, r * BLOCK:(r + 1) * BLOCK], vc_ref[g, 0, :, r * BLOCK:(r + 1) * BLOCK]
            lp = lax.dot_general(kp, q, _NT, preferred_element_type=f32) * scale + bias_p + neg_prev
            lc = lax.dot_general(kc, q, _NT, preferred_element_type=f32) * scale + bias_c + neg_cur
            m = jnp.maximum(jnp.max(jnp.maximum(lp, lc), axis=0, keepdims=True), sink)
            pp = jnp.exp(lp - m).astype(jnp.bfloat16)
            pc = jnp.exp(lc - m).astype(jnp.bfloat16)
            ox = (jnp.dot(vp, pp, preferred_element_type=f32)
                  + jnp.dot(vc, pc, preferred_element_type=f32))
            den = ox[B_HEAD_DIM:B_HEAD_DIM + 1, :] + jnp.exp(sink - m)
            o = ox[:B_HEAD_DIM, :] / den
            for pair in range(grp // 2):
                two = jnp.concatenate([o[:, (2 * pair) * BLOCK:(2 * pair + 1) * BLOCK],
                                       o[:, (2 * pair + 1) * BLOCK:(2 * pair + 2) * BLOCK]], axis=0)
                h0 = g * grp + 2 * pair
                yb_ref[r * BLOCK:(r + 1) * BLOCK, h0 * B_HEAD_DIM:(h0 + 2) * B_HEAD_DIM] = (
                    two.T.astype(yb_ref.dtype))


def _swa_attention(sinks, q_blocks, kb, vbt, bias_b, bsz, seq):
    n_blk = seq // BLOCK
    qb = SWA_QB if n_blk % SWA_QB == 0 else 1
    n_step = n_blk // qb
    cur = lambda b, i: (0, b * n_step + i, 0)
    prev = lambda b, i: (0, b * n_blk + jnp.maximum(i * qb - 1, 0), 0)
    vrows = B_HEAD_DIM + SWA_ONES
    return pl.pallas_call(
        functools.partial(_swa_kernel, qb=qb),
        grid=(bsz, n_step),
        in_specs=[
            pl.BlockSpec(memory_space=pltpu.SMEM),
            pl.BlockSpec((qb, B_HEADS, BLOCK, B_HEAD_DIM), lambda b, i: (b * n_step + i, 0, 0, 0)),
            pl.BlockSpec((B_KV_HEADS, BLOCK, B_HEAD_DIM), prev),
            pl.BlockSpec((B_KV_HEADS, qb * BLOCK, B_HEAD_DIM), cur),
            pl.BlockSpec((B_KV_HEADS, 1, vrows, BLOCK), lambda b, i: (0, b, 0, jnp.maximum(i * qb - 1, 0))),
            pl.BlockSpec((B_KV_HEADS, 1, vrows, qb * BLOCK), lambda b, i: (0, b, 0, i)),
            pl.BlockSpec((2 * BLOCK, B_HEADS * BLOCK), lambda b, i: (0, 0)),
        ],
        out_specs=pl.BlockSpec((qb * BLOCK, B_HEADS * B_HEAD_DIM), lambda b, i: (b * n_step + i, 0)),
        out_shape=jax.ShapeDtypeStruct((bsz * seq, B_HEADS * B_HEAD_DIM), jnp.bfloat16),
        compiler_params=_cparams("arbitrary", "arbitrary"),
        name="swa_attention",
    )(sinks, q_blocks, kb, kb, vbt, vbt, bias_b)


def _layer_norm(h, g, b):
    mu = jnp.mean(h, axis=-1, keepdims=True)
    d = h - mu
    var = jnp.mean(d * d, axis=-1, keepdims=True)
    return d * lax.rsqrt(var + LN_EPS) * g + b


def _merge_kernel(ya_ref, yb_ref, ga_ref, gb_ref, x_ref, wa_ref, wb_ref, wo_ref, g1_ref, b1_ref,
                  wrh_ref, wrl_ref, br_ref, tri_ref, x1_ref, rt_ref, cnt_ref, carry_ref):
    f32 = jnp.float32
    pa = jnp.dot(ya_ref[...], wa_ref[...], preferred_element_type=f32)
    pb = jnp.dot(yb_ref[...], wb_ref[...], preferred_element_type=f32)
    merged = ga_ref[...] * pa + gb_ref[...] * pb
    h = ALPHA * x_ref[...] + jnp.dot(merged.astype(jnp.bfloat16), wo_ref[...],
                                     preferred_element_type=f32)
    x1 = _layer_norm(h, g1_ref[...], b1_ref[...])
    _store_tile_rows(x1_ref, x1)

    x_hi = x1.astype(jnp.bfloat16)
    x_lo = (x1 - x_hi.astype(f32)).astype(jnp.bfloat16)
    lg = (jnp.dot(x_hi, wrh_ref[...], preferred_element_type=f32)
          + jnp.dot(x_hi, wrl_ref[...], preferred_element_type=f32)
          + jnp.dot(x_lo, wrh_ref[...], preferred_element_type=f32)) + br_ref[...]
    lane_i = lax.broadcasted_iota(jnp.int32, lg.shape, 1)
    lane = lane_i.astype(f32)
    big = jnp.float32(1 << 20)
    is_g = lane_i < N_GROUPS
    gl = jnp.where(is_g, lg, -jnp.inf)
    gmax = jnp.max(gl, axis=1, keepdims=True)
    g_sel = jnp.min(jnp.where(gl == gmax, lane, big), axis=1, keepdims=True)
    g_w = 1.0 / jnp.sum(jnp.where(is_g, jnp.exp(gl - gmax), 0.0), axis=1, keepdims=True)
    e_id = lane_i - N_GROUPS
    e_grp = (e_id >> 3).astype(f32)
    in_grp = (e_id >= 0) & (e_id < N_EXPERTS) & (e_grp == g_sel)
    el = jnp.where(in_grp, lg, -jnp.inf)
    emax = jnp.max(el, axis=1, keepdims=True)
    ee = jnp.where(in_grp, jnp.exp(el - emax), 0.0)
    ep = ee / jnp.sum(ee, axis=1, keepdims=True)
    epm = jnp.where(in_grp, ep, -1.0)
    p1 = jnp.max(epm, axis=1, keepdims=True)
    i1 = jnp.min(jnp.where(epm == p1, lane, big), axis=1, keepdims=True)
    epm2 = jnp.where(lane == i1, -1.0, epm)
    p2 = jnp.max(epm2, axis=1, keepdims=True)
    i2 = jnp.min(jnp.where(epm2 == p2, lane, big), axis=1, keepdims=True)
    psum = p1 + p2
    w1 = g_w * p1 / psum
    w2 = g_w * p2 / psum
    @pl.when(pl.program_id(0) == 0)
    def _():
        carry_ref[...] = jnp.zeros_like(carry_ref)

    oh1 = lane + N_GROUPS == i1
    oh2 = lane + N_GROUPS == i2
    oh = jnp.where(oh1 | oh2, 1.0, 0.0)
    prefix = jnp.dot(tri_ref[...], oh.astype(jnp.bfloat16), preferred_element_type=f32) + carry_ref[...]
    rank1 = jnp.sum(jnp.where(oh1, prefix, 0.0), axis=1, keepdims=True)
    rank2 = jnp.sum(jnp.where(oh2, prefix, 0.0), axis=1, keepdims=True)
    carry_ref[...] += jnp.sum(oh, axis=0, keepdims=True)
    cnt_ref[...] = carry_ref[...]
    rec = (i1 - N_GROUPS, i2 - N_GROUPS, w1, w2, rank1, rank2)
    rt = jnp.zeros(lg.shape, f32)
    for k, v in enumerate(rec):
        rt = jnp.where(lane_i == k, v, rt)
    rt_ref[...] = rt


def _route_weights_kernel(wg_ref, wr_ref, hi_ref, lo_ref):
    w = jnp.concatenate([wg_ref[...], wr_ref[...],
                         jnp.zeros((D_MODEL, LANES - N_GROUPS - N_EXPERTS), jnp.float32)], axis=1)
    hi = w.astype(jnp.bfloat16)
    hi_ref[...] = hi
    lo_ref[...] = (w - hi.astype(jnp.float32)).astype(jnp.bfloat16)


def _route_weights(w_group, w_router):
    full = lambda c: pl.BlockSpec((D_MODEL, c), lambda: (0, 0))
    return pl.pallas_call(
        _route_weights_kernel,
        in_specs=[full(N_GROUPS), full(N_EXPERTS)],
        out_specs=(full(LANES), full(LANES)),
        out_shape=(jax.ShapeDtypeStruct((D_MODEL, LANES), jnp.bfloat16),) * 2,
        name="route_weights",
    )(w_group, w_router)


def _merge_route(ya, yb, ga, gb, x2, wa, wb, wo, g1, b1, wr_hi, wr_lo, br, tm):
    n = x2.shape[0]
    row = lambda w: pl.BlockSpec((tm, w), lambda i: (i, 0))
    full = lambda r, c: pl.BlockSpec((r, c), lambda i: (0, 0))
    aw, bw = A_HEADS * A_HEAD_DIM, B_HEADS * B_HEAD_DIM
    tri = jnp.asarray(np.tril(np.ones((tm, tm), np.float32), -1), jnp.bfloat16)
    return pl.pallas_call(
        _merge_kernel,
        grid=(n // tm,),
        in_specs=[row(aw), row(bw), row(D_MODEL), row(D_MODEL), row(D_MODEL),
                  full(aw, D_MODEL), full(bw, D_MODEL), full(D_MODEL, D_MODEL),
                  full(1, D_MODEL), full(1, D_MODEL), full(D_MODEL, LANES), full(D_MODEL, LANES),
                  full(1, LANES), full(tm, tm)],
        out_specs=(pl.BlockSpec((tm * ROW_TILE, LANES), lambda i: (i, 0)), row(LANES), full(1, LANES)),
        out_shape=(jax.ShapeDtypeStruct((n * ROW_TILE, LANES), jnp.float32),
                   jax.ShapeDtypeStruct((n, LANES), jnp.float32),
                   jax.ShapeDtypeStruct((1, LANES), jnp.float32)),
        scratch_shapes=[pltpu.VMEM((1, LANES), jnp.float32)],
        compiler_params=_cparams("arbitrary"),
        name="merge_route",
    )(ya, yb, ga, gb, x2, wa, wb, wo, g1, b1, wr_hi, wr_lo, br, tri)


ROW_TILE = D_MODEL // LANES


def _store_tile_rows(ref, val):
    rows = val.shape[0]
    for j in range(ROW_TILE):
        ref[pl.ds(j, rows, stride=ROW_TILE), :] = val[:, j * LANES:(j + 1) * LANES]


def _load_tile_rows(ref, rows):
    return jnp.concatenate([ref[pl.ds(j, rows, stride=ROW_TILE), :] for j in range(ROW_TILE)], axis=1)


DMA_UNROLL = 8


def _dispatch_kernel(pos_ref, x_ref, xs_hbm, sem, *, tm):
    def issue(k, carry):
        for u in range(DMA_UNROLL):
            r = k * DMA_UNROLL + u
            src = x_ref.at[pl.ds(pl.multiple_of(r * ROW_TILE, ROW_TILE), ROW_TILE)]
            for slot in range(2):
                p = pos_ref[0, slot, r]
                dst = xs_hbm.at[pl.ds(pl.multiple_of(p * ROW_TILE, ROW_TILE), ROW_TILE)]
                pltpu.make_async_copy(src, dst, sem).start(priority=slot)
        return carry

    lax.fori_loop(0, tm // DMA_UNROLL, issue, 0)
    for slot in range(2):
        pltpu.make_async_copy(x_ref, xs_hbm.at[pl.ds(0, tm * ROW_TILE)], sem).wait()


def _moe_dispatch(pos, x1r, n_rows, tm):
    n = x1r.shape[0] // ROW_TILE
    return pl.pallas_call(
        functools.partial(_dispatch_kernel, tm=tm),
        grid=(n // tm,),
        in_specs=[pl.BlockSpec((1, 8, tm), lambda i: (i, 0, 0), memory_space=pltpu.SMEM),
                  pl.BlockSpec((tm * ROW_TILE, LANES), lambda i: (i, 0))],
        out_specs=pl.BlockSpec(memory_space=pl.ANY),
        out_shape=jax.ShapeDtypeStruct((n_rows * ROW_TILE, LANES), jnp.float32),
        scratch_shapes=[pltpu.SemaphoreType.DMA],
        compiler_params=_cparams("arbitrary"),
        name="moe_dispatch",
    )(pos, x1r)


def _expert_kernel(tile_ref, exp_ref, lo_ref, hi_ref, nit_ref, xs_ref, wg_ref, wu_ref, wd_ref, ys_ref,
                   wgb_ref, wub_ref, wdb_ref, xb_ref, *, tm):
    f32 = jnp.float32
    w = pl.program_id(0)
    prev = jnp.maximum(w - 1, 0)

    @pl.when(jnp.logical_or(w == 0, exp_ref[prev] != exp_ref[w]))
    def _():
        wgb_ref[...] = wg_ref[0].astype(jnp.bfloat16)
        wub_ref[...] = wu_ref[0].astype(jnp.bfloat16)
        wdb_ref[...] = wd_ref[0].astype(jnp.bfloat16)

    first_visit = jnp.logical_or(w == 0, tile_ref[prev] != tile_ref[w])

    @pl.when(jnp.logical_and(w < nit_ref[0], first_visit))
    def _():
        xb_ref[...] = _load_tile_rows(xs_ref, tm).astype(jnp.bfloat16)

    @pl.when(w < nit_ref[0])
    def _():
        xb = xb_ref[...]
        g = jnp.dot(xb, wgb_ref[...], preferred_element_type=f32)
        u = jnp.dot(xb, wub_ref[...], preferred_element_type=f32)
        hmid = (g * jax.nn.sigmoid(g) * u).astype(jnp.bfloat16)
        y = jnp.dot(hmid, wdb_ref[...], preferred_element_type=f32)
        row = tile_ref[w] * tm + lax.broadcasted_iota(jnp.int32, (tm, 1), 0)
        mine = (row >= lo_ref[w]) & (row < hi_ref[w])

        @pl.when(first_visit)
        def _():
            _store_tile_rows(ys_ref, jnp.where(mine, y, 0.0))

        @pl.when(jnp.logical_not(first_visit))
        def _():
            _store_tile_rows(ys_ref, jnp.where(mine, y, _load_tile_rows(ys_ref, tm)))


def _moe_experts(items, xs, wg, wu, wd, tm):
    n_items = items[0].shape[0]
    grid_spec = pltpu.PrefetchScalarGridSpec(
        num_scalar_prefetch=5,
        grid=(n_items,),
        in_specs=[
            pl.BlockSpec((tm * ROW_TILE, LANES), lambda w, t, e, lo, hi, n: (t[w], 0)),
            pl.BlockSpec((1, D_MODEL, D_EXPERT), lambda w, t, e, lo, hi, n: (e[w], 0, 0)),
            pl.BlockSpec((1, D_MODEL, D_EXPERT), lambda w, t, e, lo, hi, n: (e[w], 0, 0)),
            pl.BlockSpec((1, D_EXPERT, D_MODEL), lambda w, t, e, lo, hi, n: (e[w], 0, 0)),
        ],
        out_specs=pl.BlockSpec((tm * ROW_TILE, LANES), lambda w, t, e, lo, hi, n: (t[w], 0)),
        scratch_shapes=[pltpu.VMEM((D_MODEL, D_EXPERT), jnp.bfloat16),
                        pltpu.VMEM((D_MODEL, D_EXPERT), jnp.bfloat16),
                        pltpu.VMEM((D_EXPERT, D_MODEL), jnp.bfloat16),
                        pltpu.VMEM((tm, D_MODEL), jnp.bfloat16)],
    )
    return pl.pallas_call(
        functools.partial(_expert_kernel, tm=tm),
        grid_spec=grid_spec,
        out_shape=jax.ShapeDtypeStruct(xs.shape, jnp.float32),
        compiler_params=_cparams("arbitrary"),
        name="moe_experts",
    )(*items, xs, wg, wu, wd)


def _combine_kernel(pos_ref, rt_ref, x_ref, ys_hbm, g2_ref, b2_ref, out_ref, buf_a, buf_b, sem, *, tm):
    bufs = (buf_a, buf_b)
    th = tm // 2
    half = lambda ref, h: ref.at[pl.ds(h * th * ROW_TILE, th * ROW_TILE)]

    def issue_half(h):
        def issue(k, carry):
            for u in range(DMA_UNROLL):
                r = h * th + k * DMA_UNROLL + u
                for slot in range(2):
                    p = pos_ref[0, slot, r]
                    src = ys_hbm.at[pl.ds(pl.multiple_of(p * ROW_TILE, ROW_TILE), ROW_TILE)]
                    dst = bufs[slot].at[pl.ds(pl.multiple_of(r * ROW_TILE, ROW_TILE), ROW_TILE)]
                    pltpu.make_async_copy(src, dst, sem.at[h]).start(priority=slot)
            return carry
        lax.fori_loop(0, th // DMA_UNROLL, issue, 0)

    def finish_half(h):
        for slot in range(2):
            pltpu.make_async_copy(ys_hbm.at[pl.ds(0, th * ROW_TILE)], half(bufs[slot], h), sem.at[h]).wait()
        rt = rt_ref[h * th:(h + 1) * th, :]
        ffn = (rt[:, 2:3] * _load_tile_rows(half(buf_a, h), th)
               + rt[:, 3:4] * _load_tile_rows(half(buf_b, h), th))
        out_ref[h * th:(h + 1) * th, :] = _layer_norm(
            ALPHA * _load_tile_rows(half(x_ref, h), th) + ffn, g2_ref[...], b2_ref[...])

    issue_half(0)
    issue_half(1)
    finish_half(0)
    finish_half(1)


def _moe_combine(pos, rt, x1r, ys, g2, b2, tm):
    n = rt.shape[0]
    return pl.pallas_call(
        functools.partial(_combine_kernel, tm=tm),
        grid=(n // tm,),
        in_specs=[pl.BlockSpec((1, 8, tm), lambda i: (i, 0, 0), memory_space=pltpu.SMEM),
                  pl.BlockSpec((tm, LANES), lambda i: (i, 0)),
                  pl.BlockSpec((tm * ROW_TILE, LANES), lambda i: (i, 0)),
                  pl.BlockSpec(memory_space=pl.ANY),
                  pl.BlockSpec((1, D_MODEL), lambda i: (0, 0)),
                  pl.BlockSpec((1, D_MODEL), lambda i: (0, 0))],
        out_specs=pl.BlockSpec((tm, D_MODEL), lambda i: (i, 0)),
        out_shape=jax.ShapeDtypeStruct((n, D_MODEL), jnp.float32),
        scratch_shapes=[pltpu.VMEM((tm * ROW_TILE, LANES), jnp.float32),
                        pltpu.VMEM((tm * ROW_TILE, LANES), jnp.float32),
                        pltpu.SemaphoreType.DMA((2,))],
        compiler_params=_cparams("arbitrary"),
        name="moe_combine",
    )(pos, rt, x1r, ys, g2, b2)


MOE_TM = 512


def _pos_kernel(start_ref, rt_ref, pos_ref):
    rt = rt_ref[...]
    lane = lax.broadcasted_iota(jnp.int32, rt.shape, 1)
    lane_f = lane.astype(jnp.float32)
    start = start_ref[...]
    tile = jnp.zeros(rt.shape, jnp.float32)
    for slot in range(2):
        first_row = jnp.sum(jnp.where(lane_f == rt[:, slot:slot + 1], start, 0.0), axis=1, keepdims=True)
        tile = jnp.where(lane == slot, first_row + rt[:, 4 + slot:5 + slot], tile)
    tm = pos_ref.shape[2]
    for r in range(rt.shape[0] // BLOCK):
        t = tile[r * BLOCK:(r + 1) * BLOCK].T
        c = (r * BLOCK) % tm
        pos_ref[(r * BLOCK) // tm, :, c:c + BLOCK] = t[:8].astype(jnp.int32)


def _moe_pos(start, rt, tm):
    n = rt.shape[0]
    per_step = _pick_tile(n // tm, 4)
    return pl.pallas_call(
        _pos_kernel,
        grid=(n // (tm * per_step),),
        in_specs=[pl.BlockSpec((1, LANES), lambda i: (0, 0)),
                  pl.BlockSpec((tm * per_step, LANES), lambda i: (i, 0))],
        out_specs=pl.BlockSpec((per_step, 8, tm), lambda i: (i, 0, 0)),
        out_shape=jax.ShapeDtypeStruct((n // tm, 8, tm), jnp.int32),
        compiler_params=_cparams("arbitrary"),
        name="moe_pos",
    )(start, rt)


def _moe_plan(counts, n_tiles):
    i32 = jnp.int32
    cnt = counts[0, :N_EXPERTS].astype(i32)
    end = jnp.cumsum(cnt)
    start = end - cnt
    start_lanes = jnp.pad(start.astype(jnp.float32), (0, LANES - N_EXPERTS)).reshape(1, LANES)
    first_t = start // MOE_TM
    items_e = jnp.where(cnt > 0, (end - 1) // MOE_TM - first_t + 1, 0)
    item_end = jnp.cumsum(items_e)
    n_items = item_end[-1]
    w = jnp.minimum(jnp.arange(n_tiles + N_EXPERTS - 1, dtype=i32), n_items - 1)
    e_w = jnp.minimum(jnp.sum((item_end[None, :] <= w[:, None]).astype(i32), axis=1), N_EXPERTS - 1)
    t_w = first_t[e_w] + (w - (item_end - items_e)[e_w])
    lo_w = jnp.maximum(start[e_w], t_w * MOE_TM)
    hi_w = jnp.minimum(end[e_w], (t_w + 1) * MOE_TM)
    return start_lanes, (t_w.astype(i32), e_w, lo_w.astype(i32), hi_w.astype(i32),
                         n_items.reshape(1).astype(i32))


def _pick_tile(n, pref):
    t = min(pref, n)
    while n % t:
        t //= 2
    return t


def kernel(x, w_in, kv_norm_g, w_uv, w_branch_a, sinks, w_branch_b, w_out, rel_bias, ln1_g, ln1_b,
           w_group, b_group, w_router, b_router, w_gate, w_up, w_down, ln2_g, ln2_b):
    bsz, seq, d = x.shape
    assert d == D_MODEL and seq % BLOCK == 0 and w_in.shape[0] == DEPTH == 1
    n = bsz * seq
    n_blk = seq // BLOCK
    bf, f32 = jnp.bfloat16, jnp.float32
    x2 = x.reshape(n, d)

    r = np.arange(BLOCK, dtype=np.int32)
    d_prev = r[None, :] + BLOCK - r[:, None]
    d_cur = r[None, :] - r[:, None]
    bkt_a = jnp.asarray(_t5_bucket_np(np.stack([d_prev, d_cur])))
    s2 = np.arange(2 * BLOCK, dtype=np.int32)
    bkt_b = jnp.asarray(_t5_bucket_np(r[None, :] + BLOCK - s2[:, None])[None])
    assert int(_t5_bucket_np(np.arange(BLOCK + 1, max(seq, BLOCK + 2))).min()) == N_BUCKETS - 1
    tab_t = rel_bias.astype(f32).T
    bias_a = _bias_tiles(tab_t, bkt_a, 0, A_HEADS, True, minus_far=True)
    bias_b = _bias_tiles(tab_t, bkt_b, A_HEADS, B_HEADS, True)[0]

    (q_blocks, ckv, ckv_tx, qidx_blocks, kidx, widx_l, qb_blocks, kb, vbx, ga, gb) = _input_projection(
        x2, w_in[0].astype(bf), kv_norm_g[0].reshape(1, A_LAT).astype(f32), _pick_tile(seq, 512), seq)

    ckv3 = ckv.reshape(bsz, seq, A_LAT)
    ya = _dsa_attention(qidx_blocks.reshape(bsz * n_blk, IDX_HEADS * BLOCK, IDX_DIM),
                        kidx.reshape(bsz, seq, IDX_DIM), widx_l,
                        q_blocks.reshape(bsz * n_blk, A_HEADS * BLOCK, A_LAT),
                        ckv3, ckv_tx, bias_a, jnp.swapaxes(w_uv[0], 1, 2).astype(bf), bsz, seq)

    yb = _swa_attention(sinks[0].astype(f32), qb_blocks, kb, vbx, bias_b, bsz, seq)

    wr_hi, wr_lo = _route_weights(w_group[0].astype(f32), w_router[0].astype(f32))
    b_route = jnp.concatenate(
        [b_group[0], b_router[0], jnp.zeros((LANES - N_GROUPS - N_EXPERTS,), f32)]).reshape(1, LANES).astype(f32)
    x1r, rt, counts = _merge_route(
        ya, yb, ga, gb, x2, w_branch_a[0].astype(bf), w_branch_b[0].astype(bf), w_out[0].astype(bf),
        ln1_g[0].reshape(1, d).astype(f32), ln1_b[0].reshape(1, d).astype(f32), wr_hi, wr_lo, b_route,
        _pick_tile(n, 512))

    assert (2 * n) % MOE_TM == 0
    start_lanes, items = _moe_plan(counts, 2 * n // MOE_TM)
    tm_io = _pick_tile(n, 512)
    pos_blocks = _moe_pos(start_lanes, rt, tm_io)
    xs = _moe_dispatch(pos_blocks, x1r, 2 * n, tm_io)
    ys = _moe_experts(items, xs, w_gate[0], w_up[0], w_down[0], MOE_TM)
    out = _moe_combine(pos_blocks, rt, x1r, ys, ln2_g[0].reshape(1, d).astype(f32),
                       ln2_b[0].reshape(1, d).astype(f32), tm_io)
    return out.reshape(bsz, seq, d)
```

```python
import functools
import math

import numpy as np
import jax
import jax.numpy as jnp
from jax import lax
from jax.experimental import pallas as pl
from jax.experimental.pallas import tpu as pltpu

D_MODEL = 1024
A_HEADS = 8
A_LAT = 128
A_HEAD_DIM = 64
IDX_HEADS = 8
IDX_DIM = 64
TOPK_MAX = 256
B_HEADS = 8
B_KV_HEADS = 2
B_HEAD_DIM = 64
WINDOW = 128
BLOCK = 128
N_BUCKETS = 32
MAX_DISTANCE = 128
N_GROUPS = 4
EXPERTS_PER_GROUP = 8
N_EXPERTS = 32
D_EXPERT = 256
DEPTH = 1
ALPHA = (2 * DEPTH) ** 0.25
LN_EPS = 1e-5
RMS_EPS = 1e-6

LANES = 128
INT_MIN = -(2 ** 31)
VMEM_LIMIT = 48 * 1024 * 1024
PROJ_VMEM_LIMIT = 56 * 1024 * 1024

_NT = (((1,), (1,)), ((), ()))


def _cparams(*sem):
    return pltpu.CompilerParams(dimension_semantics=sem, vmem_limit_bytes=VMEM_LIMIT)


def _t5_bucket_np(dist):
    f32 = np.float32
    n = np.maximum(dist, 0)
    max_exact = N_BUCKETS // 2
    nf = np.maximum(n, 1).astype(f32)
    large = max_exact + (np.log(nf / f32(max_exact)) / f32(math.log(MAX_DISTANCE / max_exact))
                         * f32(N_BUCKETS - max_exact)).astype(np.int32)
    large = np.minimum(large, N_BUCKETS - 1)
    return np.where(n < max_exact, n, large).astype(np.int32)


def _bias_kernel(tab_ref, bkt_ref, out_ref, *, head0, minus_far):
    h = pl.program_id(0) + head0
    bkt = bkt_ref[...]
    far = tab_ref[h, N_BUCKETS - 1]
    acc = jnp.full(bkt.shape, far, jnp.float32)
    for b in range(N_BUCKETS - 1):
        acc = jnp.where(bkt == b, tab_ref[h, b], acc)
    out_ref[...] = acc - far if minus_far else acc


def _bias_tiles(tab_t, bkt, head0, n_heads, minus_far=False):
    g, r, c = bkt.shape
    return pl.pallas_call(
        functools.partial(_bias_kernel, head0=head0, minus_far=minus_far),
        grid=(n_heads,),
        in_specs=[pl.BlockSpec(memory_space=pltpu.SMEM),
                  pl.BlockSpec((g, r, c), lambda h: (0, 0, 0))],
        out_specs=pl.BlockSpec((g, r, c), lambda h: (0, 0, h)),
        out_shape=jax.ShapeDtypeStruct((g, r, n_heads * c), jnp.float32),
        compiler_params=_cparams("arbitrary"),
        name="bias_tiles",
    )(tab_t, bkt)


_C_QLAT = 0
_C_CKV = _C_QLAT + A_HEADS * A_LAT
_C_QIDX = _C_CKV + A_LAT
_C_KW = _C_QIDX + IDX_HEADS * IDX_DIM
_C_QB = _C_KW + LANES
_C_KB = _C_QB + B_HEADS * B_HEAD_DIM
_C_VB = _C_KB + B_KV_HEADS * B_HEAD_DIM
_C_GA = _C_VB + B_KV_HEADS * B_HEAD_DIM
_C_GB = _C_GA + D_MODEL
_C_END = _C_GB + D_MODEL
_RAW_KW_END = _C_KW + IDX_DIM + IDX_HEADS


def _proj_kernel(x_ref, wraw_ref, g_ref, q_ref, ckv_ref, ckvt_ref, qidx_ref, kidx_ref, widx_ref,
                 qb_ref, kb_ref, vb_ref, ga_ref, gb_ref, w_ref):
    xb = x_ref[...].astype(jnp.bfloat16)
    tm = xb.shape[0]

    @pl.when(pl.program_id(0) == 0)
    def _():
        w_ref[:, :_C_KW] = wraw_ref[:, :_C_KW]
        w_ref[:, _C_KW:_C_QB] = jnp.concatenate(
            [wraw_ref[:, _C_KW:_RAW_KW_END], jnp.zeros((D_MODEL, _C_QB - _RAW_KW_END), jnp.bfloat16)], axis=1)
        for dst in range(_C_QB, _C_END, LANES):
            src = dst - (_C_QB - _RAW_KW_END)
            w_ref[:, dst:dst + LANES] = wraw_ref[:, src:src + LANES]

    def seg(lo, hi):
        return jnp.dot(xb, w_ref[:, lo:hi], preferred_element_type=jnp.float32)

    for h in range(A_HEADS):
        qh = (seg(_C_QLAT + h * A_LAT, _C_QLAT + (h + 1) * A_LAT) * (A_LAT ** -0.5)).astype(jnp.bfloat16)
        for r in range(tm // BLOCK):
            q_ref[r, h] = qh[r * BLOCK:(r + 1) * BLOCK]
    c = seg(_C_CKV, _C_QIDX)
    ms = jnp.mean(c * c, axis=-1, keepdims=True)
    cn = c * lax.rsqrt(ms + RMS_EPS) * g_ref[...]
    ckv_ref[...] = cn.astype(jnp.bfloat16)
    for r in range(tm // BLOCK):
        ckvt_ref[0, :A_LAT, r * BLOCK:(r + 1) * BLOCK] = cn[r * BLOCK:(r + 1) * BLOCK].T.astype(jnp.bfloat16)
    ckvt_ref[0, A_LAT:, :] = jnp.ones((8, tm), jnp.bfloat16)
    qif = seg(_C_QIDX, _C_KW).astype(jnp.bfloat16)
    for h in range(IDX_HEADS):
        for r in range(tm // BLOCK):
            qidx_ref[r, h] = qif[r * BLOCK:(r + 1) * BLOCK, h * IDX_DIM:(h + 1) * IDX_DIM]
    kw = seg(_C_KW, _C_QB)
    kidx_ref[...] = kw[:, :IDX_DIM].astype(jnp.bfloat16)
    for r in range(tm // BLOCK):
        kwt = kw[r * BLOCK:(r + 1) * BLOCK].T
        for h in range(IDX_HEADS):
            widx_ref[r, :, h * BLOCK:(h + 1) * BLOCK] = kwt[IDX_DIM + h:IDX_DIM + h + 1, :]
    qbf = seg(_C_QB, _C_KB).astype(jnp.bfloat16)
    for h in range(B_HEADS):
        for r in range(tm // BLOCK):
            qb_ref[r, h] = qbf[r * BLOCK:(r + 1) * BLOCK, h * B_HEAD_DIM:(h + 1) * B_HEAD_DIM]
    kbf = seg(_C_KB, _C_VB).astype(jnp.bfloat16)
    vbf = seg(_C_VB, _C_GA)
    for g in range(B_KV_HEADS):
        kb_ref[g] = kbf[:, g * B_HEAD_DIM:(g + 1) * B_HEAD_DIM]
        vb_ref[g, 0, B_HEAD_DIM:, :] = jnp.ones((SWA_ONES, tm), jnp.bfloat16)
    for r in range(tm // BLOCK):
        vt = vbf[r * BLOCK:(r + 1) * BLOCK].T.astype(jnp.bfloat16)
        for g in range(B_KV_HEADS):
            vb_ref[g, 0, :B_HEAD_DIM, r * BLOCK:(r + 1) * BLOCK] = vt[g * B_HEAD_DIM:(g + 1) * B_HEAD_DIM]
    ga_ref[...] = jax.nn.sigmoid(seg(_C_GA, _C_GB))
    gb_ref[...] = jax.nn.sigmoid(seg(_C_GB, _C_END))


def _input_projection(x2, w_raw, kv_g, tm, seq):
    n = x2.shape[0]
    assert seq % tm == 0 and w_raw.shape == (D_MODEL, _C_END - (_C_QB - _RAW_KW_END))
    bf, f32 = jnp.bfloat16, jnp.float32
    row = lambda w: pl.BlockSpec((tm, w), lambda i: (i, 0))
    out_shape = (
        jax.ShapeDtypeStruct((n // BLOCK, A_HEADS, BLOCK, A_LAT), bf),
        jax.ShapeDtypeStruct((n, A_LAT), bf),
        jax.ShapeDtypeStruct((n // seq, A_LAT + 8, seq), bf),
        jax.ShapeDtypeStruct((n // BLOCK, IDX_HEADS, BLOCK, IDX_DIM), bf),
        jax.ShapeDtypeStruct((n, IDX_DIM), bf),
        jax.ShapeDtypeStruct((n // BLOCK, 1, IDX_HEADS * BLOCK), f32),
        jax.ShapeDtypeStruct((n // BLOCK, B_HEADS, BLOCK, B_HEAD_DIM), bf),
        jax.ShapeDtypeStruct((B_KV_HEADS, n, B_HEAD_DIM), bf),
        jax.ShapeDtypeStruct((B_KV_HEADS, n // seq, B_HEAD_DIM + SWA_ONES, seq), bf),
        jax.ShapeDtypeStruct((n, D_MODEL), f32),
        jax.ShapeDtypeStruct((n, D_MODEL), f32),
    )
    out_specs = (
        pl.BlockSpec((tm // BLOCK, A_HEADS, BLOCK, A_LAT), lambda i: (i, 0, 0, 0)),
        row(A_LAT),
        pl.BlockSpec((1, A_LAT + 8, tm), lambda i: (i // (seq // tm), 0, i % (seq // tm))),
        pl.BlockSpec((tm // BLOCK, IDX_HEADS, BLOCK, IDX_DIM), lambda i: (i, 0, 0, 0)),
        row(IDX_DIM), pl.BlockSpec((tm // BLOCK, 1, IDX_HEADS * BLOCK), lambda i: (i, 0, 0)),
        pl.BlockSpec((tm // BLOCK, B_HEADS, BLOCK, B_HEAD_DIM), lambda i: (i, 0, 0, 0)),
        pl.BlockSpec((B_KV_HEADS, tm, B_HEAD_DIM), lambda i: (0, i, 0)),
        pl.BlockSpec((B_KV_HEADS, 1, B_HEAD_DIM + SWA_ONES, tm),
                     lambda i: (0, i // (seq // tm), 0, i % (seq // tm))),
        row(D_MODEL), row(D_MODEL),
    )
    return pl.pallas_call(
        _proj_kernel,
        grid=(n // tm,),
        in_specs=[row(D_MODEL),
                  pl.BlockSpec(w_raw.shape, lambda i: (0, 0)),
                  pl.BlockSpec((1, A_LAT), lambda i: (0, 0))],
        out_specs=out_specs,
        out_shape=out_shape,
        scratch_shapes=[pltpu.VMEM((D_MODEL, _C_END), bf)],
        compiler_params=pltpu.CompilerParams(dimension_semantics=("arbitrary",),
                                             vmem_limit_bytes=PROJ_VMEM_LIMIT),
        name="input_projection",
    )(x2, w_raw, kv_g)


DSA_UNROLL = 8


def _bit_transpose32(words):
    w = list(words)
    j, m = 16, 0x0000FFFF
    while j:
        mask = jnp.int32(m - (1 << 32) if m >= (1 << 31) else m)
        k = 0
        while k < 32:
            t = (w[k] ^ lax.shift_right_logical(w[k + j], jnp.int32(j))) & mask
            w[k] = w[k] ^ t
            w[k + j] = w[k + j] ^ lax.shift_left(t, jnp.int32(j))
            k = (k + j + 1) & ~j
        j >>= 1
        m = (m ^ (m << j)) & 0xFFFFFFFF
    return w


def _dsa_kernel(qidx_ref, kidx_ref, widx_ref, q_ref, ckv_ref, ckvt_ref, bias_ref, wuvt_ref,
                tri_ref, ya_ref, sc_ref, neg_ref, lg_ref, o_ref, planes_ref, sel_ref, *, k_sel,
                search_chunk):
    i = pl.program_id(1)
    f32 = jnp.float32
    s_loc = lax.broadcasted_iota(jnp.int32, (BLOCK, BLOCK), 0)
    t_loc = lax.broadcasted_iota(jnp.int32, (BLOCK, BLOCK), 1)
    causal_diag = s_loc <= t_loc
    idx_scale = IDX_DIM ** -0.5
    w_scale = IDX_HEADS ** -0.5

    def rows(j):
        return pl.ds(pl.multiple_of(j * BLOCK, BLOCK), BLOCK)

    sc_ref[...] = jnp.full(sc_ref.shape, -jnp.inf, f32)
    wts = (widx_ref[0] * w_scale) * idx_scale
    qi = qidx_ref[0]

    def score_keys(j):
        s = lax.dot_general(kidx_ref[0, rows(j), :], qi, _NT, preferred_element_type=f32)
        r = jnp.maximum(s, 0.0) * wts
        acc = r[:, :BLOCK]
        for h in range(1, IDX_HEADS):
            acc = acc + r[:, h * BLOCK:(h + 1) * BLOCK]
        return acc

    def fold_blocks(lo, hi, fn, init, combine):
        n = hi - lo
        n_grp = lax.shift_right_logical(n, DSA_UNROLL.bit_length() - 1)

        def piece(acc, j, size):
            vals = [fn(j + u) for u in range(size)]
            while len(vals) > 1:
                vals = [combine(vals[a], vals[a + 1]) for a in range(0, len(vals), 2)]
            return combine(acc, vals[0])

        acc = lax.fori_loop(0, n_grp, lambda k, a: piece(a, lo + DSA_UNROLL * k, DSA_UNROLL), init)
        j = lo + DSA_UNROLL * n_grp
        left = hi - j
        size = DSA_UNROLL // 2
        while size:
            acc = lax.cond((left & size) != 0, functools.partial(piece, acc, j, size), lambda a=acc: a)
            j = j + (left & size)
            size //= 2
        return acc

    def block_scores(j):
        visible = (j * BLOCK + s_loc) <= (i * BLOCK + t_loc)
        sc_ref[rows(j), :] = jnp.where(visible, score_keys(j), -jnp.inf)
        return jnp.int32(0)

    fold_blocks(0, i + 1, block_scores, jnp.int32(0), lambda a, b: a)

    n_chunks = ((i + 1) * BLOCK + search_chunk - 1) // search_chunk
    n_acc = 64

    def count(cmp, cand):
        def body(c, cnt):
            base = pl.multiple_of(c * search_chunk, search_chunk)
            for r in range(search_chunk // n_acc):
                cnt = cnt + cmp(sc_ref[pl.ds(base + r * n_acc, n_acc), :], cand).astype(f32)
            return cnt
        cnt = lax.fori_loop(0, n_chunks, body, jnp.zeros((n_acc, BLOCK), f32))
        return jnp.sum(cnt, axis=0, keepdims=True)

    def ordered_to_float(u):
        k = u ^ INT_MIN
        return lax.bitcast_convert_type(k ^ ((k >> 31) & 0x7FFFFFFF), f32)

    n_grp_keys = 32 * 8
    n_groups = ((i + 1) * BLOCK + n_grp_keys - 1) // n_grp_keys
    max_groups = planes_ref.shape[0]

    def build_planes(g, carry):
        words = []
        for v in range(32):
            s = sc_ref[pl.ds(pl.multiple_of(g * n_grp_keys + v * 8, 8), 8), :]
            bits = lax.bitcast_convert_type(s, jnp.int32)
            words.append(bits ^ ((bits >> 31) & 0x7FFFFFFF) ^ INT_MIN)
        planes = _bit_transpose32(words)
        for p in range(32):
            planes_ref[g, p] = planes[p]
        return carry

    lax.fori_loop(0, n_groups, build_planes, 0)

    def popcount_rows(words):
        cnt = lax.population_count(words[0])
        for wd in words[1:]:
            cnt = cnt + lax.population_count(wd)
        return jnp.sum(cnt.astype(f32), axis=0, keepdims=True)

    def select_two_bits(n, carry):
        alive, k_left, thr_u = carry
        grp = [jnp.minimum(g, n_groups - 1) for g in range(max_groups)]
        hi1 = [alive[g] & planes_ref[grp[g], 2 * n] for g in range(max_groups)]
        hi0 = [alive[g] ^ hi1[g] for g in range(max_groups)]
        lo = [planes_ref[grp[g], 2 * n + 1] for g in range(max_groups)]
        c11 = [hi1[g] & lo[g] for g in range(max_groups)]
        c10 = [hi1[g] ^ c11[g] for g in range(max_groups)]
        c01 = [hi0[g] & lo[g] for g in range(max_groups)]
        c00 = [hi0[g] ^ c01[g] for g in range(max_groups)]
        s1 = popcount_rows(c11)
        s2 = s1 + popcount_rows(c10)
        s3 = s2 + popcount_rows(c01)
        t11 = s1 >= k_left
        t10 = jnp.logical_and(jnp.logical_not(t11), s2 >= k_left)
        t01 = jnp.logical_and(jnp.logical_not(t11 | t10), s3 >= k_left)
        w11, w10, w01 = (jnp.broadcast_to(t, (8, BLOCK)) for t in (t11, t10, t01))
        alive = [jnp.where(w11, c11[g], jnp.where(w10, c10[g], jnp.where(w01, c01[g], c00[g])))
                 for g in range(max_groups)]
        k_left = k_left - jnp.where(t11, 0.0, jnp.where(t10, s1, jnp.where(t01, s2, s3)))
        b_hi = 31 - 2 * n
        bits = (jnp.where(t11 | t10, lax.shift_left(jnp.int32(1), b_hi), 0)
                | jnp.where(t11 | t01, lax.shift_left(jnp.int32(1), b_hi - 1), 0))
        return alive, k_left, thr_u | bits

    alive0 = [jnp.where(g < n_groups, jnp.full((8, BLOCK), -1, jnp.int32), 0) for g in range(max_groups)]
    _, _, thr_u = lax.fori_loop(
        0, 16, select_two_bits, (alive0, jnp.full((1, BLOCK), k_sel, f32), jnp.zeros((1, BLOCK), jnp.int32)))
    thr = ordered_to_float(thr_u)
    c_ge = count(jnp.greater_equal, thr)
    c_gt = count(jnp.greater, thr)
    sel_ref[0:1, :] = thr
    sel_ref[1:2, :] = c_ge
    sel_ref[2:3, :] = c_gt
    consistent = jnp.min(((c_ge >= k_sel) & (c_gt < k_sel)).astype(f32)) > 0.0

    @pl.when(jnp.logical_not(consistent))
    def _():
        def search_bit(b, prefix_u):
            cand_u = prefix_u | lax.shift_left(jnp.int32(1), 31 - b)
            cnt = count(jnp.greater_equal, ordered_to_float(cand_u))
            return jnp.where(cnt >= k_sel, cand_u, prefix_u)

        t = ordered_to_float(lax.fori_loop(0, 32, search_bit, jnp.zeros((1, BLOCK), jnp.int32)))
        t = jnp.where(t != t, -jnp.inf, t)
        sel_ref[0:1, :] = t
        sel_ref[1:2, :] = count(jnp.greater_equal, t)
        sel_ref[2:3, :] = count(jnp.greater, t)

    thr = sel_ref[0:1, :]
    c_ge = sel_ref[1:2, :]
    c_gt = sel_ref[2:3, :]
    n_tie = k_sel - c_gt
    any_tie = jnp.max(((c_ge > k_sel) & (thr > -jnp.inf)).astype(f32)) > 0.0

    @pl.when(jnp.logical_not(any_tie))
    def _():
        def body(j, carry):
            neg_ref[rows(j), :] = jnp.where(sc_ref[rows(j), :] >= thr, 0.0, -jnp.inf)
            return carry
        lax.fori_loop(0, i, body, 0)
        neg_ref[rows(i), :] = jnp.where((sc_ref[rows(i), :] >= thr) & causal_diag, 0.0, -jnp.inf)

    @pl.when(any_tie)
    def _():
        tri = tri_ref[...]

        def block(j, tie_seen):
            kc = sc_ref[rows(j), :]
            eq = kc == thr
            tie_rank = jnp.dot(tri, eq.astype(jnp.bfloat16), preferred_element_type=f32) + tie_seen
            sel = (kc > thr) | (eq & (tie_rank <= n_tie))
            return sel, tie_seen + jnp.sum(eq.astype(f32), axis=0, keepdims=True)

        def body(j, tie_seen):
            sel, tie_seen = block(j, tie_seen)
            neg_ref[rows(j), :] = jnp.where(sel, 0.0, -jnp.inf)
            return tie_seen

        tie_seen = lax.fori_loop(0, i, body, jnp.zeros((1, BLOCK), f32))
        sel, _ = block(i, tie_seen)
        neg_ref[rows(i), :] = jnp.where(sel & causal_diag, 0.0, -jnp.inf)

    q_all = q_ref[0]

    def logits(j):
        lg = lax.dot_general(ckv_ref[0, rows(j), :], q_all, _NT, preferred_element_type=f32)
        return lg + jnp.concatenate([neg_ref[rows(j), :]] * A_HEADS, axis=1)

    def far_logits(j):
        lg = logits(j)
        lg_ref[rows(j), :] = lg
        return jnp.max(lg, axis=0, keepdims=True)

    def near_logits(j):
        lg = logits(j) + bias_ref[j - i + 1]
        lg_ref[rows(j), :] = lg
        return jnp.max(lg, axis=0, keepdims=True)

    n_far = jnp.maximum(i - 1, 0)
    m_run = fold_blocks(0, n_far, far_logits, jnp.full((1, A_HEADS * BLOCK), -jnp.inf, f32), jnp.maximum)
    m_run = fold_blocks(n_far, i + 1, near_logits, m_run, jnp.maximum)

    o_ref[...] = jnp.zeros_like(o_ref)

    def pv_rows(r):
        p = jnp.exp(lg_ref[r, :] - m_run).astype(jnp.bfloat16)
        o_ref[...] += jnp.dot(ckvt_ref[0, :, r], p, preferred_element_type=f32)

    def pv_group(k, carry):
        span = DSA_UNROLL * BLOCK
        pv_rows(pl.ds(pl.multiple_of(k * span, span), span))
        return carry

    n_grp = lax.shift_right_logical(i + 1, DSA_UNROLL.bit_length() - 1)
    lax.fori_loop(0, n_grp, pv_group, 0)
    j = DSA_UNROLL * n_grp
    left = i + 1 - j
    size = DSA_UNROLL // 2
    while size:
        @pl.when((left & size) != 0)
        def _(j=j, size=size):
            pv_rows(pl.ds(pl.multiple_of(j * BLOCK, size * BLOCK), size * BLOCK))
        j = j + (left & size)
        size //= 2

    inv_l = 1.0 / o_ref[A_LAT:A_LAT + 1, :]
    for pair in range(A_HEADS // 2):
        halves = []
        for h in (2 * pair, 2 * pair + 1):
            cols = slice(h * BLOCK, (h + 1) * BLOCK)
            y_t = jnp.dot(wuvt_ref[h], o_ref[:A_LAT, cols].astype(jnp.bfloat16),
                          preferred_element_type=f32)
            halves.append(y_t * inv_l[:, cols])
        two = jnp.concatenate(halves, axis=0)
        ya_ref[:, 2 * pair * A_HEAD_DIM:(2 * pair + 2) * A_HEAD_DIM] = two.T.astype(ya_ref.dtype)


def _dsa_attention(qidx_blocks, kidx, widx_l, q_blocks, ckv, ckv_tx, bias_near, wuv, bsz, seq):
    n_blk = seq // BLOCK
    k_sel = min(TOPK_MAX, seq // 4)
    search_chunk = min(512, seq)
    tri = jnp.asarray(np.tril(np.ones((BLOCK, BLOCK), np.float32)), jnp.bfloat16)
    kern = functools.partial(_dsa_kernel, k_sel=float(k_sel), search_chunk=search_chunk)
    blk = lambda b, i: (b * n_blk + i, 0, 0)
    return pl.pallas_call(
        kern,
        grid=(bsz, n_blk),
        in_specs=[
            pl.BlockSpec((1, IDX_HEADS * BLOCK, IDX_DIM), blk),
            pl.BlockSpec((1, seq, IDX_DIM), lambda b, i: (b, 0, 0)),
            pl.BlockSpec((1, 1, IDX_HEADS * BLOCK), blk),
            pl.BlockSpec((1, A_HEADS * BLOCK, A_LAT), blk),
            pl.BlockSpec((1, seq, A_LAT), lambda b, i: (b, 0, 0)),
            pl.BlockSpec((1, A_LAT + 8, seq), lambda b, i: (b, 0, 0)),
            pl.BlockSpec((2, BLOCK, A_HEADS * BLOCK), lambda b, i: (0, 0, 0)),
            pl.BlockSpec((A_HEADS, A_HEAD_DIM, A_LAT), lambda b, i: (0, 0, 0)),
            pl.BlockSpec((BLOCK, BLOCK), lambda b, i: (0, 0)),
        ],
        out_specs=pl.BlockSpec((BLOCK, A_HEADS * A_HEAD_DIM), lambda b, i: (b * n_blk + i, 0)),
        out_shape=jax.ShapeDtypeStruct((bsz * seq, A_HEADS * A_HEAD_DIM), jnp.bfloat16),
        scratch_shapes=[pltpu.VMEM((seq, BLOCK), jnp.float32),
                        pltpu.VMEM((seq, BLOCK), jnp.float32),
                        pltpu.VMEM((seq, A_HEADS * BLOCK), jnp.float32),
                        pltpu.VMEM((A_LAT + 8, A_HEADS * BLOCK), jnp.float32),
                        pltpu.VMEM((seq // 256, 32, 8, BLOCK), jnp.int32),
                        pltpu.VMEM((8, BLOCK), jnp.float32)],
        compiler_params=_cparams("arbitrary", "arbitrary"),
        name="dsa_attention",
    )(qidx_blocks, kidx, widx_l, q_blocks, ckv, ckv_tx, bias_near, wuv, tri)


SWA_QB = 4
SWA_ONES = 16


def _swa_kernel(sink_ref, q_ref, kp_ref, kc_ref, vp_ref, vc_ref, bias_ref, yb_ref, *, qb):
    i = pl.program_id(1)
    f32 = jnp.float32
    grp = B_HEADS // B_KV_HEADS
    cols = grp * BLOCK
    s_loc = lax.broadcasted_iota(jnp.int32, (BLOCK, BLOCK), 0)
    t_loc = lax.broadcasted_iota(jnp.int32, (BLOCK, BLOCK), 1)
    neg_cur = jnp.concatenate([jnp.where(s_loc <= t_loc, 0.0, -jnp.inf)] * grp, axis=1)
    in_prev = s_loc > t_loc
    head_of_col = lax.broadcasted_iota(jnp.int32, (1, cols), 1) >> (BLOCK.bit_length() - 1)
    scale = B_HEAD_DIM ** -0.5
    for g in range(B_KV_HEADS):
        sink = jnp.zeros((1, cols), f32)
        for hh in range(grp):
            sink = jnp.where(head_of_col == hh, sink_ref[g * grp + hh], sink)
        bias_p = bias_ref[:BLOCK, g * cols:(g + 1) * cols]
        bias_c = bias_ref[BLOCK:, g * cols:(g + 1) * cols]
        for r in range(qb):
            has_prev = jnp.zeros_like(s_loc) + jnp.minimum(i * qb + r, 1)
            neg_prev = jnp.concatenate([jnp.where(in_prev & (has_prev > 0), 0.0, -jnp.inf)] * grp, axis=1)
            q = q_ref[r, g * grp:(g + 1) * grp].reshape(cols, B_HEAD_DIM)
            if r == 0:
                kp, vp = kp_ref[g], vp_ref[g, 0]
            else:
                kp, vp = kc_ref[g, (r - 1) * BLOCK:r * BLOCK], vc_ref[g, 0, :, (r - 1) * BLOCK:r * BLOCK]
            kc, vc = kc_ref[g, r * BLOCK:(r + 1) * BLOCK], vc_ref[g, 0, :, r * BLOCK:(r + 1) * BLOCK]
            lp = lax.dot_general(kp, q, _NT, preferred_element_type=f32) * scale + bias_p + neg_prev
            lc = lax.dot_general(kc, q, _NT, preferred_element_type=f32) * scale + bias_c + neg_cur
            m = jnp.maximum(jnp.max(jnp.maximum(lp, lc), axis=0, keepdims=True), sink)
            pp = jnp.exp(lp - m).astype(jnp.bfloat16)
            pc = jnp.exp(lc - m).astype(jnp.bfloat16)
            ox = (jnp.dot(vp, pp, preferred_element_type=f32)
                  + jnp.dot(vc, pc, preferred_element_type=f32))
            den = ox[B_HEAD_DIM:B_HEAD_DIM + 1, :] + jnp.exp(sink - m)
            o = ox[:B_HEAD_DIM, :] / den
            for pair in range(grp // 2):
                two = jnp.concatenate([o[:, (2 * pair) * BLOCK:(2 * pair + 1) * BLOCK],
                                       o[:, (2 * pair + 1) * BLOCK:(2 * pair + 2) * BLOCK]], axis=0)
                h0 = g * grp + 2 * pair
                yb_ref[r * BLOCK:(r + 1) * BLOCK, h0 * B_HEAD_DIM:(h0 + 2) * B_HEAD_DIM] = (
                    two.T.astype(yb_ref.dtype))


def _swa_attention(sinks, q_blocks, kb, vbt, bias_b, bsz, seq):
    n_blk = seq // BLOCK
    qb = SWA_QB if n_blk % SWA_QB == 0 else 1
    n_step = n_blk // qb
    cur = lambda b, i: (0, b * n_step + i, 0)
    prev = lambda b, i: (0, b * n_blk + jnp.maximum(i * qb - 1, 0), 0)
    vrows = B_HEAD_DIM + SWA_ONES
    return pl.pallas_call(
        functools.partial(_swa_kernel, qb=qb),
        grid=(bsz, n_step),
        in_specs=[
            pl.BlockSpec(memory_space=pltpu.SMEM),
            pl.BlockSpec((qb, B_HEADS, BLOCK, B_HEAD_DIM), lambda b, i: (b * n_step + i, 0, 0, 0)),
            pl.BlockSpec((B_KV_HEADS, BLOCK, B_HEAD_DIM), prev),
            pl.BlockSpec((B_KV_HEADS, qb * BLOCK, B_HEAD_DIM), cur),
            pl.BlockSpec((B_KV_HEADS, 1, vrows, BLOCK), lambda b, i: (0, b, 0, jnp.maximum(i * qb - 1, 0))),
            pl.BlockSpec((B_KV_HEADS, 1, vrows, qb * BLOCK), lambda b, i: (0, b, 0, i)),
            pl.BlockSpec((2 * BLOCK, B_HEADS * BLOCK), lambda b, i: (0, 0)),
        ],
        out_specs=pl.BlockSpec((qb * BLOCK, B_HEADS * B_HEAD_DIM), lambda b, i: (b * n_step + i, 0)),
        out_shape=jax.ShapeDtypeStruct((bsz * seq, B_HEADS * B_HEAD_DIM), jnp.bfloat16),
        compiler_params=_cparams("arbitrary", "arbitrary"),
        name="swa_attention",
    )(sinks, q_blocks, kb, kb, vbt, vbt, bias_b)


def _layer_norm(h, g, b):
    mu = jnp.mean(h, axis=-1, keepdims=True)
    d = h - mu
    var = jnp.mean(d * d, axis=-1, keepdims=True)
    return d * lax.rsqrt(var + LN_EPS) * g + b


MERGE_CHUNKS = 1


def _merge_kernel(ya_ref, yb_ref, ga_ref, gb_ref, x_ref, wa_ref, wb_ref, wo_ref, g1_ref, b1_ref,
                  wrh_ref, wrl_ref, br_ref, tri_ref, x1_ref, rt_ref, cnt_ref, carry_ref):
    @pl.when(pl.program_id(0) == 0)
    def _():
        carry_ref[...] = jnp.zeros_like(carry_ref)

    rc = tri_ref.shape[0]
    for c in range(ya_ref.shape[0] // rc):
        _merge_rows(slice(c * rc, (c + 1) * rc), ya_ref, yb_ref, ga_ref, gb_ref, x_ref, wa_ref, wb_ref,
                    wo_ref, g1_ref, b1_ref, wrh_ref, wrl_ref, br_ref, tri_ref,
                    x1_ref.at[pl.ds(c * rc * ROW_TILE, rc * ROW_TILE)], rt_ref, carry_ref)
    cnt_ref[...] = carry_ref[...]


def _merge_rows(r, ya_ref, yb_ref, ga_ref, gb_ref, x_ref, wa_ref, wb_ref, wo_ref, g1_ref, b1_ref,
                wrh_ref, wrl_ref, br_ref, tri_ref, x1_ref, rt_ref, carry_ref):
    f32 = jnp.float32
    pa = jnp.dot(ya_ref[r, :], wa_ref[...], preferred_element_type=f32)
    pb = jnp.dot(yb_ref[r, :], wb_ref[...], preferred_element_type=f32)
    merged = ga_ref[r, :] * pa + gb_ref[r, :] * pb
    h = ALPHA * x_ref[r, :] + jnp.dot(merged.astype(jnp.bfloat16), wo_ref[...],
                                      preferred_element_type=f32)
    x1 = _layer_norm(h, g1_ref[...], b1_ref[...])
    _store_tile_rows(x1_ref, x1)

    x_hi = x1.astype(jnp.bfloat16)
    x_lo = (x1 - x_hi.astype(f32)).astype(jnp.bfloat16)
    lg = (jnp.dot(x_hi, wrh_ref[...], preferred_element_type=f32)
          + jnp.dot(x_hi, wrl_ref[...], preferred_element_type=f32)
          + jnp.dot(x_lo, wrh_ref[...], preferred_element_type=f32)) + br_ref[...]
    lane_i = lax.broadcasted_iota(jnp.int32, lg.shape, 1)
    lane = lane_i.astype(f32)
    big = jnp.float32(1 << 20)
    is_g = lane_i < N_GROUPS
    gl = jnp.where(is_g, lg, -jnp.inf)
    gmax = jnp.max(gl, axis=1, keepdims=True)
    g_sel = jnp.min(jnp.where(gl == gmax, lane, big), axis=1, keepdims=True)
    g_w = 1.0 / jnp.sum(jnp.where(is_g, jnp.exp(gl - gmax), 0.0), axis=1, keepdims=True)
    e_id = lane_i - N_GROUPS
    e_grp = (e_id >> 3).astype(f32)
    in_grp = (e_id >= 0) & (e_id < N_EXPERTS) & (e_grp == g_sel)
    el = jnp.where(in_grp, lg, -jnp.inf)
    emax = jnp.max(el, axis=1, keepdims=True)
    ee = jnp.where(in_grp, jnp.exp(el - emax), 0.0)
    ep = ee / jnp.sum(ee, axis=1, keepdims=True)
    epm = jnp.where(in_grp, ep, -1.0)
    p1 = jnp.max(epm, axis=1, keepdims=True)
    i1 = jnp.min(jnp.where(epm == p1, lane, big), axis=1, keepdims=True)
    epm2 = jnp.where(lane == i1, -1.0, epm)
    p2 = jnp.max(epm2, axis=1, keepdims=True)
    i2 = jnp.min(jnp.where(epm2 == p2, lane, big), axis=1, keepdims=True)
    psum = p1 + p2
    w1 = g_w * p1 / psum
    w2 = g_w * p2 / psum
    oh1 = lane + N_GROUPS == i1
    oh2 = lane + N_GROUPS == i2
    oh = jnp.where(oh1 | oh2, 1.0, 0.0)
    prefix = jnp.dot(tri_ref[...], oh.astype(jnp.bfloat16), preferred_element_type=f32) + carry_ref[...]
    rank1 = jnp.sum(jnp.where(oh1, prefix, 0.0), axis=1, keepdims=True)
    rank2 = jnp.sum(jnp.where(oh2, prefix, 0.0), axis=1, keepdims=True)
    carry_ref[...] += jnp.sum(oh, axis=0, keepdims=True)
    rec = (i1 - N_GROUPS, i2 - N_GROUPS, w1, w2, rank1, rank2)
    rt = jnp.zeros(lg.shape, f32)
    for k, v in enumerate(rec):
        rt = jnp.where(lane_i == k, v, rt)
    rt_ref[r, :] = rt


def _route_weights_kernel(wg_ref, wr_ref, hi_ref, lo_ref):
    w = jnp.concatenate([wg_ref[...], wr_ref[...],
                         jnp.zeros((D_MODEL, LANES - N_GROUPS - N_EXPERTS), jnp.float32)], axis=1)
    hi = w.astype(jnp.bfloat16)
    hi_ref[...] = hi
    lo_ref[...] = (w - hi.astype(jnp.float32)).astype(jnp.bfloat16)


def _route_weights(w_group, w_router):
    full = lambda c: pl.BlockSpec((D_MODEL, c), lambda: (0, 0))
    return pl.pallas_call(
        _route_weights_kernel,
        in_specs=[full(N_GROUPS), full(N_EXPERTS)],
        out_specs=(full(LANES), full(LANES)),
        out_shape=(jax.ShapeDtypeStruct((D_MODEL, LANES), jnp.bfloat16),) * 2,
        name="route_weights",
    )(w_group, w_router)


def _merge_route(ya, yb, ga, gb, x2, wa, wb, wo, g1, b1, wr_hi, wr_lo, br, tm):
    n = x2.shape[0]
    row = lambda w: pl.BlockSpec((tm, w), lambda i: (i, 0))
    full = lambda r, c: pl.BlockSpec((r, c), lambda i: (0, 0))
    aw, bw = A_HEADS * A_HEAD_DIM, B_HEADS * B_HEAD_DIM
    rc = tm // MERGE_CHUNKS
    tri = jnp.asarray(np.tril(np.ones((rc, rc), np.float32), -1), jnp.bfloat16)
    return pl.pallas_call(
        _merge_kernel,
        grid=(n // tm,),
        in_specs=[row(aw), row(bw), row(D_MODEL), row(D_MODEL), row(D_MODEL),
                  full(aw, D_MODEL), full(bw, D_MODEL), full(D_MODEL, D_MODEL),
                  full(1, D_MODEL), full(1, D_MODEL), full(D_MODEL, LANES), full(D_MODEL, LANES),
                  full(1, LANES), full(rc, rc)],
        out_specs=(pl.BlockSpec((tm * ROW_TILE, LANES), lambda i: (i, 0)), row(LANES), full(1, LANES)),
        out_shape=(jax.ShapeDtypeStruct((n * ROW_TILE, LANES), jnp.float32),
                   jax.ShapeDtypeStruct((n, LANES), jnp.float32),
                   jax.ShapeDtypeStruct((1, LANES), jnp.float32)),
        scratch_shapes=[pltpu.VMEM((1, LANES), jnp.float32)],
        compiler_params=_cparams("arbitrary"),
        name="merge_route",
    )(ya, yb, ga, gb, x2, wa, wb, wo, g1, b1, wr_hi, wr_lo, br, tri)


ROW_TILE = D_MODEL // LANES


def _store_tile_rows(ref, val):
    rows = val.shape[0]
    for j in range(ROW_TILE):
        ref[pl.ds(j, rows, stride=ROW_TILE), :] = val[:, j * LANES:(j + 1) * LANES]


def _load_tile_rows(ref, rows):
    return jnp.concatenate([ref[pl.ds(j, rows, stride=ROW_TILE), :] for j in range(ROW_TILE)], axis=1)


DMA_UNROLL = 8


def _dispatch_kernel(pos_ref, x_ref, xs_hbm, sem, *, tm):
    def issue(k, carry):
        for u in range(DMA_UNROLL):
            r = k * DMA_UNROLL + u
            src = x_ref.at[pl.ds(pl.multiple_of(r * ROW_TILE, ROW_TILE), ROW_TILE)]
            for slot in range(2):
                p = pos_ref[0, slot, r]
                dst = xs_hbm.at[pl.ds(pl.multiple_of(p * ROW_TILE, ROW_TILE), ROW_TILE)]
                pltpu.make_async_copy(src, dst, sem).start(priority=slot)
        return carry

    lax.fori_loop(0, tm // DMA_UNROLL, issue, 0)
    for slot in range(2):
        pltpu.make_async_copy(x_ref, xs_hbm.at[pl.ds(0, tm * ROW_TILE)], sem).wait()


def _moe_dispatch(pos, x1r, n_rows, tm):
    n = x1r.shape[0] // ROW_TILE
    return pl.pallas_call(
        functools.partial(_dispatch_kernel, tm=tm),
        grid=(n // tm,),
        in_specs=[pl.BlockSpec((1, 8, tm), lambda i: (i, 0, 0), memory_space=pltpu.SMEM),
                  pl.BlockSpec((tm * ROW_TILE, LANES), lambda i: (i, 0))],
        out_specs=pl.BlockSpec(memory_space=pl.ANY),
        out_shape=jax.ShapeDtypeStruct((n_rows * ROW_TILE, LANES), jnp.float32),
        scratch_shapes=[pltpu.SemaphoreType.DMA],
        compiler_params=_cparams("arbitrary"),
        name="moe_dispatch",
    )(pos, x1r)


def _expert_kernel(tile_ref, exp_ref, lo_ref, hi_ref, nit_ref, xs_ref, wg_ref, wu_ref, wd_ref, ys_ref,
                   wgb_ref, wub_ref, wdb_ref, *, tm):
    f32 = jnp.float32
    w = pl.program_id(0)
    prev = jnp.maximum(w - 1, 0)

    @pl.when(jnp.logical_or(w == 0, exp_ref[prev] != exp_ref[w]))
    def _():
        wgb_ref[...] = wg_ref[0].astype(jnp.bfloat16)
        wub_ref[...] = wu_ref[0].astype(jnp.bfloat16)
        wdb_ref[...] = wd_ref[0].astype(jnp.bfloat16)

    first_visit = jnp.logical_or(w == 0, tile_ref[prev] != tile_ref[w])

    @pl.when(w < nit_ref[0])
    def _():
        xb = _load_tile_rows(xs_ref, tm).astype(jnp.bfloat16)
        g = jnp.dot(xb, wgb_ref[...], preferred_element_type=f32)
        u = jnp.dot(xb, wub_ref[...], preferred_element_type=f32)
        hmid = (g * jax.nn.sigmoid(g) * u).astype(jnp.bfloat16)
        y = jnp.dot(hmid, wdb_ref[...], preferred_element_type=f32)
        row = tile_ref[w] * tm + lax.broadcasted_iota(jnp.int32, (tm, 1), 0)
        mine = (row >= lo_ref[w]) & (row < hi_ref[w])

        @pl.when(first_visit)
        def _():
            _store_tile_rows(ys_ref, jnp.where(mine, y, 0.0))

        @pl.when(jnp.logical_not(first_visit))
        def _():
            _store_tile_rows(ys_ref, jnp.where(mine, y, _load_tile_rows(ys_ref, tm)))


def _moe_experts(items, xs, wg, wu, wd, tm):
    n_items = items[0].shape[0]
    grid_spec = pltpu.PrefetchScalarGridSpec(
        num_scalar_prefetch=5,
        grid=(n_items,),
        in_specs=[
            pl.BlockSpec((tm * ROW_TILE, LANES), lambda w, t, e, lo, hi, n: (t[w], 0)),
            pl.BlockSpec((1, D_MODEL, D_EXPERT), lambda w, t, e, lo, hi, n: (e[w], 0, 0)),
            pl.BlockSpec((1, D_MODEL, D_EXPERT), lambda w, t, e, lo, hi, n: (e[w], 0, 0)),
            pl.BlockSpec((1, D_EXPERT, D_MODEL), lambda w, t, e, lo, hi, n: (e[w], 0, 0)),
        ],
        out_specs=pl.BlockSpec((tm * ROW_TILE, LANES), lambda w, t, e, lo, hi, n: (t[w], 0)),
        scratch_shapes=[pltpu.VMEM((D_MODEL, D_EXPERT), jnp.bfloat16),
                        pltpu.VMEM((D_MODEL, D_EXPERT), jnp.bfloat16),
                        pltpu.VMEM((D_EXPERT, D_MODEL), jnp.bfloat16)],
    )
    return pl.pallas_call(
        functools.partial(_expert_kernel, tm=tm),
        grid_spec=grid_spec,
        out_shape=jax.ShapeDtypeStruct(xs.shape, jnp.float32),
        compiler_params=_cparams("arbitrary"),
        name="moe_experts",
    )(*items, xs, wg, wu, wd)


def _combine_kernel(pos_ref, rt_ref, x_ref, ys_hbm, g2_ref, b2_ref, out_ref, buf_a, buf_b, sem, *, tm):
    bufs = (buf_a, buf_b)
    th = tm // 2
    half = lambda ref, h: ref.at[pl.ds(h * th * ROW_TILE, th * ROW_TILE)]

    def issue_half(h):
        def issue(k, carry):
            for u in range(DMA_UNROLL):
                r = h * th + k * DMA_UNROLL + u
                for slot in range(2):
                    p = pos_ref[0, slot, r]
                    src = ys_hbm.at[pl.ds(pl.multiple_of(p * ROW_TILE, ROW_TILE), ROW_TILE)]
                    dst = bufs[slot].at[pl.ds(pl.multiple_of(r * ROW_TILE, ROW_TILE), ROW_TILE)]
                    pltpu.make_async_copy(src, dst, sem.at[h]).start(priority=slot)
            return carry
        lax.fori_loop(0, th // DMA_UNROLL, issue, 0)

    def finish_half(h):
        for slot in range(2):
            pltpu.make_async_copy(ys_hbm.at[pl.ds(0, th * ROW_TILE)], half(bufs[slot], h), sem.at[h]).wait()
        rt = rt_ref[h * th:(h + 1) * th, :]
        ffn = (rt[:, 2:3] * _load_tile_rows(half(buf_a, h), th)
               + rt[:, 3:4] * _load_tile_rows(half(buf_b, h), th))
        out_ref[h * th:(h + 1) * th, :] = _layer_norm(
            ALPHA * _load_tile_rows(half(x_ref, h), th) + ffn, g2_ref[...], b2_ref[...])

    issue_half(0)
    issue_half(1)
    finish_half(0)
    finish_half(1)


def _moe_combine(pos, rt, x1r, ys, g2, b2, tm):
    n = rt.shape[0]
    return pl.pallas_call(
        functools.partial(_combine_kernel, tm=tm),
        grid=(n // tm,),
        in_specs=[pl.BlockSpec((1, 8, tm), lambda i: (i, 0, 0), memory_space=pltpu.SMEM),
                  pl.BlockSpec((tm, LANES), lambda i: (i, 0)),
                  pl.BlockSpec((tm * ROW_TILE, LANES), lambda i: (i, 0)),
                  pl.BlockSpec(memory_space=pl.ANY),
                  pl.BlockSpec((1, D_MODEL), lambda i: (0, 0)),
                  pl.BlockSpec((1, D_MODEL), lambda i: (0, 0))],
        out_specs=pl.BlockSpec((tm, D_MODEL), lambda i: (i, 0)),
        out_shape=jax.ShapeDtypeStruct((n, D_MODEL), jnp.float32),
        scratch_shapes=[pltpu.VMEM((tm * ROW_TILE, LANES), jnp.float32),
                        pltpu.VMEM((tm * ROW_TILE, LANES), jnp.float32),
                        pltpu.SemaphoreType.DMA((2,))],
        compiler_params=_cparams("arbitrary"),
        name="moe_combine",
    )(pos, rt, x1r, ys, g2, b2)


MOE_TM = 512


def _pos_kernel(start_ref, rt_ref, pos_ref):
    rt = rt_ref[...]
    lane = lax.broadcasted_iota(jnp.int32, rt.shape, 1)
    lane_f = lane.astype(jnp.float32)
    start = start_ref[...]
    tile = jnp.zeros(rt.shape, jnp.float32)
    for slot in range(2):
        first_row = jnp.sum(jnp.where(lane_f == rt[:, slot:slot + 1], start, 0.0), axis=1, keepdims=True)
        tile = jnp.where(lane == slot, first_row + rt[:, 4 + slot:5 + slot], tile)
    tm = pos_ref.shape[2]
    for r in range(rt.shape[0] // BLOCK):
        t = tile[r * BLOCK:(r + 1) * BLOCK].T
        c = (r * BLOCK) % tm
        pos_ref[(r * BLOCK) // tm, :, c:c + BLOCK] = t[:8].astype(jnp.int32)


def _moe_pos(start, rt, tm):
    n = rt.shape[0]
    per_step = _pick_tile(n // tm, 4)
    return pl.pallas_call(
        _pos_kernel,
        grid=(n // (tm * per_step),),
        in_specs=[pl.BlockSpec((1, LANES), lambda i: (0, 0)),
                  pl.BlockSpec((tm * per_step, LANES), lambda i: (i, 0))],
        out_specs=pl.BlockSpec((per_step, 8, tm), lambda i: (i, 0, 0)),
        out_shape=jax.ShapeDtypeStruct((n // tm, 8, tm), jnp.int32),
        compiler_params=_cparams("arbitrary"),
        name="moe_pos",
    )(start, rt)


def _moe_plan(counts, n_tiles):
    i32 = jnp.int32
    cnt = counts[0, :N_EXPERTS].astype(i32)
    end = jnp.cumsum(cnt)
    start = end - cnt
    start_lanes = jnp.pad(start.astype(jnp.float32), (0, LANES - N_EXPERTS)).reshape(1, LANES)
    first_t = start // MOE_TM
    items_e = jnp.where(cnt > 0, (end - 1) // MOE_TM - first_t + 1, 0)
    item_end = jnp.cumsum(items_e)
    n_items = item_end[-1]
    w = jnp.minimum(jnp.arange(n_tiles + N_EXPERTS - 1, dtype=i32), n_items - 1)
    e_w = jnp.minimum(jnp.sum((item_end[None, :] <= w[:, None]).astype(i32), axis=1), N_EXPERTS - 1)
    t_w = first_t[e_w] + (w - (item_end - items_e)[e_w])
    lo_w = jnp.maximum(start[e_w], t_w * MOE_TM)
    hi_w = jnp.minimum(end[e_w], (t_w + 1) * MOE_TM)
    return start_lanes, (t_w.astype(i32), e_w, lo_w.astype(i32), hi_w.astype(i32),
                         n_items.reshape(1).astype(i32))


def _pick_tile(n, pref):
    t = min(pref, n)
    while n % t:
        t //= 2
    return t


def kernel(x, w_in, kv_norm_g, w_uv, w_branch_a, sinks, w_branch_b, w_out, rel_bias, ln1_g, ln1_b,
           w_group, b_group, w_router, b_router, w_gate, w_up, w_down, ln2_g, ln2_b):
    bsz, seq, d = x.shape
    assert d == D_MODEL and seq % BLOCK == 0 and w_in.shape[0] == DEPTH == 1
    n = bsz * seq
    n_blk = seq // BLOCK
    bf, f32 = jnp.bfloat16, jnp.float32
    x2 = x.reshape(n, d)

    r = np.arange(BLOCK, dtype=np.int32)
    d_prev = r[None, :] + BLOCK - r[:, None]
    d_cur = r[None, :] - r[:, None]
    bkt_a = jnp.asarray(_t5_bucket_np(np.stack([d_prev, d_cur])))
    s2 = np.arange(2 * BLOCK, dtype=np.int32)
    bkt_b = jnp.asarray(_t5_bucket_np(r[None, :] + BLOCK - s2[:, None])[None])
    assert int(_t5_bucket_np(np.arange(BLOCK + 1, max(seq, BLOCK + 2))).min()) == N_BUCKETS - 1
    tab_t = rel_bias.astype(f32).T
    bias_a = _bias_tiles(tab_t, bkt_a, 0, A_HEADS, minus_far=True)
    bias_b = _bias_tiles(tab_t, bkt_b, A_HEADS, B_HEADS)[0]

    (q_blocks, ckv, ckv_tx, qidx_blocks, kidx, widx_l, qb_blocks, kb, vbx, ga, gb) = _input_projection(
        x2, w_in[0].astype(bf), kv_norm_g[0].reshape(1, A_LAT).astype(f32), _pick_tile(seq, 512), seq)

    ckv3 = ckv.reshape(bsz, seq, A_LAT)
    ya = _dsa_attention(qidx_blocks.reshape(bsz * n_blk, IDX_HEADS * BLOCK, IDX_DIM),
                        kidx.reshape(bsz, seq, IDX_DIM), widx_l,
                        q_blocks.reshape(bsz * n_blk, A_HEADS * BLOCK, A_LAT),
                        ckv3, ckv_tx, bias_a, jnp.swapaxes(w_uv[0], 1, 2).astype(bf), bsz, seq)

    yb = _swa_attention(sinks[0].astype(f32), qb_blocks, kb, vbx, bias_b, bsz, seq)

    wr_hi, wr_lo = _route_weights(w_group[0].astype(f32), w_router[0].astype(f32))
    b_route = jnp.concatenate(
        [b_group[0], b_router[0], jnp.zeros((LANES - N_GROUPS - N_EXPERTS,), f32)]).reshape(1, LANES).astype(f32)
    x1r, rt, counts = _merge_route(
        ya, yb, ga, gb, x2, w_branch_a[0].astype(bf), w_branch_b[0].astype(bf), w_out[0].astype(bf),
        ln1_g[0].reshape(1, d).astype(f32), ln1_b[0].reshape(1, d).astype(f32), wr_hi, wr_lo, b_route,
        _pick_tile(n, 512))

    assert (2 * n) % MOE_TM == 0
    start_lanes, items = _moe_plan(counts, 2 * n // MOE_TM)
    tm_io = _pick_tile(n, 512)
    pos_blocks = _moe_pos(start_lanes, rt, tm_io)
    xs = _moe_dispatch(pos_blocks, x1r, 2 * n, tm_io)
    ys = _moe_experts(items, xs, w_gate[0], w_up[0], w_down[0], MOE_TM)
    out = _moe_combine(pos_blocks, rt, x1r, ys, ln2_g[0].reshape(1, d).astype(f32),
                       ln2_b[0].reshape(1, d).astype(f32), tm_io)
    return out.reshape(bsz, seq, d)
```
